```python
import jax, jax.numpy as jnp
from jax import lax
import numpy as np

D_MODEL = 2048
BATCH = 1
SEQ = 8192
DEPTH = 1

D_A = D_MODEL // 2
D_B = D_MODEL // 2
A_HEADS = 8
A_EXPAND = D_A // A_HEADS
A_DV = D_A // A_HEADS
A_CHUNK = 64
B_HEADS = 16
B_KV_HEADS = 4
B_GROUP = B_HEADS // B_KV_HEADS
B_HEAD_DIM = D_B // B_HEADS
KV_DIM = B_KV_HEADS * B_HEAD_DIM
CMP_BLOCK = 32
CMP_STRIDE = 16
CMP_HIDDEN = 256
SLC_BLOCK = 64
SLC_TOP_N = 16
SLC_WEIGHTS = (1.0, 2.0, 2.0, 2.0, 1.0)
WINDOW = 512
Q_BLOCK = 128
ROPE_THETA = 500000.0
ROPE_DIM = B_HEAD_DIM // 4
D_FF = ((8 * D_MODEL // 3 + 255) // 256) * 256
EPS = 1e-6
NEG_INF = -1e30
FORCE_BONUS = 1e4
IN_SPLITS = (D_A, D_A, D_A, D_A, D_B, KV_DIM, KV_DIM, KV_DIM, KV_DIM, KV_DIM, KV_DIM, 3 * B_HEADS, D_MODEL, D_MODEL)
IN_COLS = 4 * D_A + D_B + 6 * KV_DIM + 3 * B_HEADS + 2 * D_MODEL

kernel_name = "hgrn2_nsa_parallel_hybrid_block"


def rms_norm(x, g):
    xf = x.astype(jnp.float32)
    y = xf * lax.rsqrt(jnp.mean(xf * xf, axis=-1, keepdims=True) + EPS)
    return (y * g.astype(jnp.float32)).astype(x.dtype)


def rope_partial(x, pos):
    half = ROPE_DIM // 2
    inv = jnp.float32(ROPE_THETA) ** (-jnp.arange(half, dtype=jnp.float32) * 2.0 / ROPE_DIM)
    ang = pos.astype(jnp.float32)[:, None] * inv[None, :]
    shape = (1, ang.shape[0]) + (1,) * (x.ndim - 3) + (half,)
    cos = jnp.cos(ang).reshape(shape).astype(x.dtype)
    sin = jnp.sin(ang).reshape(shape).astype(x.dtype)
    x1, x2, rest = x[..., :half], x[..., half:ROPE_DIM], x[..., ROPE_DIM:]
    return jnp.concatenate([x1 * cos - x2 * sin, x2 * cos + x1 * sin, rest], axis=-1)


def masked_softmax(s, mask):
    s = jnp.where(mask, s, NEG_INF)
    return jax.nn.softmax(s, axis=-1) * mask


def hgrn2_mixer(q, f_logit, inp, g, lb):
    b_, t_, _ = q.shape
    n_c = t_ // A_CHUNK
    f = lb + (1.0 - lb) * jax.nn.sigmoid(f_logit.astype(jnp.float32))
    log_f = jnp.log(f)
    k = 1.0 - f

    def to_chunks(a, d):
        return a.reshape(b_, n_c, A_CHUNK, A_HEADS, d).transpose(1, 0, 3, 2, 4)

    qc = to_chunks(q.astype(jnp.float32), A_EXPAND)
    kc = to_chunks(k, A_EXPAND)
    lfc = to_chunks(log_f, A_EXPAND)
    ic = to_chunks(inp.astype(jnp.float32), A_DV)
    tri = jnp.tril(jnp.ones((A_CHUNK, A_CHUNK), dtype=bool))

    def step(S, chunk):
        qq, kk, lf, ii = chunk
        bcum = jnp.cumsum(lf, axis=2)
        o_inter = jnp.einsum('bhte,bhev->bhtv', qq * jnp.exp(bcum), S)
        diff = bcum[:, :, :, None, :] - bcum[:, :, None, :, :]
        decay = jnp.where(tri[None, None, :, :, None], jnp.exp(jnp.minimum(diff, 0.0)), 0.0)
        att = jnp.einsum('bhtse,bhte,bhse->bhts', decay, qq, kk)
        o_intra = jnp.einsum('bhts,bhsv->bhtv', att, ii)
        b_end = bcum[:, :, -1, :]
        S_new = jnp.exp(b_end)[..., None] * S + jnp.einsum(
            'bhse,bhsv->bhev', kk * jnp.exp(b_end[:, :, None, :] - bcum), ii)
        return S_new, o_inter + o_intra

    S0 = jnp.zeros((b_, A_HEADS, A_EXPAND, A_DV), jnp.float32)
    _, o = lax.scan(step, S0, (qc, kc, lfc, ic))
    return o.transpose(1, 0, 3, 2, 4).reshape(b_, t_, A_HEADS, A_DV)


def compress(x, pe, w1, b1, w2, b2):
    b_, t_ = x.shape[0], x.shape[1]
    n_cmp = (t_ - CMP_BLOCK) // CMP_STRIDE + 1
    idx = jnp.arange(n_cmp)[:, None] * CMP_STRIDE + jnp.arange(CMP_BLOCK)[None, :]
    blocks = x[:, idx] + pe[None, None, :, None, :]
    flat = blocks.transpose(0, 1, 3, 2, 4).reshape(b_, n_cmp, B_KV_HEADS, CMP_BLOCK * B_HEAD_DIM)
    h = jax.nn.gelu(flat @ w1 + b1)
    return h @ w2 + b2, idx[:, -1]


def nsa_mixer(q, kc, vc, ks, vs, kw, vw, gate_logits, pe_k, w1k, b1k, w2k, b2k, pe_v, w1v, b1v, w2v, b2v):
    b_, t_, _ = q.shape
    pos = jnp.arange(t_)
    scale = B_HEAD_DIM ** -0.5
    qh = rope_partial(q.reshape(b_, t_, B_KV_HEADS, B_GROUP, B_HEAD_DIM), pos)
    gates = jax.nn.sigmoid(gate_logits.astype(jnp.float32)).reshape(b_, t_, B_KV_HEADS, B_GROUP, 3).astype(q.dtype)

    def heads(a):
        return a.reshape(b_, t_, B_KV_HEADS, B_HEAD_DIM)

    k_cmp, end_pos = compress(heads(kc), pe_k, w1k, b1k, w2k, b2k)
    v_cmp, _ = compress(heads(vc), pe_v, w1v, b1v, w2v, b2v)
    k_cmp = rope_partial(k_cmp, end_pos).transpose(0, 2, 1, 3)
    v_cmp = v_cmp.transpose(0, 2, 1, 3)
    n_cmp = k_cmp.shape[2]

    n_slc = t_ // SLC_BLOCK
    k_sel = min(SLC_TOP_N, n_slc)
    ks_blocks = rope_partial(heads(ks), pos).transpose(0, 2, 1, 3).reshape(b_, B_KV_HEADS, n_slc, SLC_BLOCK, B_HEAD_DIM)
    vs_blocks = heads(vs).transpose(0, 2, 1, 3).reshape(b_, B_KV_HEADS, n_slc, SLC_BLOCK, B_HEAD_DIM)
    gather_blocks = jax.vmap(jax.vmap(lambda blk, ix: blk[ix]))

    pad = ((0, 0), (0, 0), (WINDOW, 0), (0, 0))
    kw_pad = jnp.pad(rope_partial(heads(kw), pos).transpose(0, 2, 1, 3), pad)
    vw_pad = jnp.pad(heads(vw).transpose(0, 2, 1, 3), pad)

    blk_ids = jnp.arange(n_slc)
    right_pad = 4 * n_slc + 1 - (n_cmp + 1)
    ratio = SLC_BLOCK // CMP_STRIDE

    def block_fn(c):
        q0 = c * Q_BLOCK
        t = q0 + jnp.arange(Q_BLOCK)
        qc_ = lax.dynamic_slice_in_dim(qh, q0, Q_BLOCK, axis=1).astype(jnp.float32)
        gc = lax.dynamic_slice_in_dim(gates, q0, Q_BLOCK, axis=1)

        s_c = jnp.einsum('btkgd,bknd->bkgtn', qc_, k_cmp.astype(jnp.float32)) * scale
        p_c = masked_softmax(s_c, end_pos[None, :] <= t[:, None])
        o_cmp = jnp.einsum('bkgtn,bknd->btkgd', p_c.astype(v_cmp.dtype), v_cmp)

        p_sum = jnp.pad(p_c.sum(axis=2), ((0, 0), (0, 0), (0, 0), (1, right_pad)))
        p_slc = sum(w * p_sum[..., o: o + ratio * n_slc: ratio] for o, w in enumerate(SLC_WEIGHTS))
        cur = t // SLC_BLOCK
        valid = blk_ids[None, :] * SLC_BLOCK <= t[:, None]
        forced = (blk_ids[None, :] == 0) | (blk_ids[None, :] == cur[:, None]) | (blk_ids[None, :] == cur[:, None] - 1)
        score = jnp.where(forced, FORCE_BONUS, jnp.where(valid, p_slc, NEG_INF))
        _, sel = lax.top_k(score, k_sel)
        sel_flat = sel.reshape(b_, B_KV_HEADS, Q_BLOCK * k_sel)
        kg = gather_blocks(ks_blocks, sel_flat).reshape(b_, B_KV_HEADS, Q_BLOCK, k_sel * SLC_BLOCK, B_HEAD_DIM)
        vg = gather_blocks(vs_blocks, sel_flat).reshape(b_, B_KV_HEADS, Q_BLOCK, k_sel * SLC_BLOCK, B_HEAD_DIM)
        pos_sel = (sel[..., None] * SLC_BLOCK + jnp.arange(SLC_BLOCK)).reshape(b_, B_KV_HEADS, Q_BLOCK, k_sel * SLC_BLOCK)
        s_s = jnp.einsum('btkgd,bktnd->bkgtn', qc_, kg.astype(jnp.float32)) * scale
        p_s = masked_softmax(s_s, (pos_sel <= t[None, None, :, None])[:, :, None])
        o_slc = jnp.einsum('bkgtn,bktnd->btkgd', p_s.astype(vg.dtype), vg)

        kwc = lax.dynamic_slice_in_dim(kw_pad, q0, WINDOW + Q_BLOCK, axis=2)
        vwc = lax.dynamic_slice_in_dim(vw_pad, q0, WINDOW + Q_BLOCK, axis=2)
        kpos = q0 - WINDOW + jnp.arange(WINDOW + Q_BLOCK)
        m_w = (kpos[None, :] <= t[:, None]) & (t[:, None] - kpos[None, :] < WINDOW) & (kpos[None, :] >= 0)
        s_w = jnp.einsum('btkgd,bknd->bkgtn', qc_, kwc.astype(jnp.float32)) * scale
        p_w = masked_softmax(s_w, m_w)
        o_win = jnp.einsum('bkgtn,bknd->btkgd', p_w.astype(vwc.dtype), vwc)

        o = gc[..., 0:1] * o_cmp + gc[..., 1:2] * o_slc + gc[..., 2:3] * o_win
        return o.reshape(b_, Q_BLOCK, D_B)

    out = lax.map(block_fn, jnp.arange(t_ // Q_BLOCK))
    return out.transpose(1, 0, 2, 3).reshape(b_, t_, D_B)


def hybrid_mixer(xn, w_in, lb, a_norm_g, pe_k, w1k, b1k, w2k, b2k, pe_v, w1v, b1v, w2v, b2v,
                 w_proj_a, w_proj_b, w_out):
    proj = xn @ w_in
    offsets = [int(o) for o in np.cumsum(IN_SPLITS)[:-1]]
    (qa, fa, ia, ga, qb, kc, vc, ks, vs, kw, vw, nsa_g, merge_a, merge_b) = jnp.split(proj, offsets, axis=-1)
    b_, t_, _ = xn.shape

    oa = hgrn2_mixer(qa, fa, ia, ga, lb)
    oa = oa * lax.rsqrt(jnp.mean(oa * oa, axis=-1, keepdims=True) + EPS)
    oa = (oa.reshape(b_, t_, D_A) * a_norm_g.astype(jnp.float32)).astype(xn.dtype) * jax.nn.silu(ga)
    ya = oa @ w_proj_a

    ob = nsa_mixer(qb, kc, vc, ks, vs, kw, vw, nsa_g, pe_k, w1k, b1k, w2k, b2k, pe_v, w1v, b1v, w2v, b2v)
    yb = ob @ w_proj_b

    merged = jax.nn.sigmoid(merge_a) * ya + jax.nn.sigmoid(merge_b) * yb
    return merged @ w_out


def swiglu(h, w_gate, w_up, w_down):
    return (jax.nn.silu(h @ w_gate) * (h @ w_up)) @ w_down


def setup_inputs(seed: int = 0) -> dict:
    key = jax.random.key(seed)
    ks = jax.random.split(key, 32)
    L = DEPTH

    def nrm(k, shape, scale):
        return jax.random.normal(k, shape, jnp.float32) * scale

    def gain(k, shape):
        return 1.0 + 0.1 * jax.random.normal(k, shape, jnp.float32)

    flat = CMP_BLOCK * B_HEAD_DIM
    return {
        "x": nrm(ks[0], (BATCH, SEQ, D_MODEL), 1.0),
        "pre_mix_g": gain(ks[1], (L, D_MODEL)),
        "w_in": nrm(ks[2], (L, D_MODEL, IN_COLS), D_MODEL ** -0.5),
        "lb_table": nrm(ks[3], (L + 1, D_A), 0.5),
        "a_norm_g": gain(ks[4], (L, D_A)),
        "cmp_pe_k": nrm(ks[5], (L, CMP_BLOCK, B_HEAD_DIM), 0.1),
        "cmp_w1_k": nrm(ks[6], (L, flat, CMP_HIDDEN), flat ** -0.5),
        "cmp_b1_k": nrm(ks[7], (L, CMP_HIDDEN), 0.01),
        "cmp_w2_k": nrm(ks[8], (L, CMP_HIDDEN, B_HEAD_DIM), CMP_HIDDEN ** -0.5),
        "cmp_b2_k": nrm(ks[9], (L, B_HEAD_DIM), 0.01),
        "cmp_pe_v": nrm(ks[10], (L, CMP_BLOCK, B_HEAD_DIM), 0.1),
        "cmp_w1_v": nrm(ks[11], (L, flat, CMP_HIDDEN), flat ** -0.5),
        "cmp_b1_v": nrm(ks[12], (L, CMP_HIDDEN), 0.01),
        "cmp_w2_v": nrm(ks[13], (L, CMP_HIDDEN, B_HEAD_DIM), CMP_HIDDEN ** -0.5),
        "cmp_b2_v": nrm(ks[14], (L, B_HEAD_DIM), 0.01),
        "w_proj_a": nrm(ks[15], (L, D_A, D_MODEL), D_A ** -0.5),
        "w_proj_b": nrm(ks[16], (L, D_B, D_MODEL), D_B ** -0.5),
        "w_out": nrm(ks[17], (L, D_MODEL, D_MODEL), D_MODEL ** -0.5),
        "post_mix_g": gain(ks[18], (L, D_MODEL)),
        "pre_ffn_g": gain(ks[19], (L, D_MODEL)),
        "w_gate": nrm(ks[20], (L, D_MODEL, D_FF), D_MODEL ** -0.5),
        "w_up": nrm(ks[21], (L, D_MODEL, D_FF), D_MODEL ** -0.5),
        "w_down": nrm(ks[22], (L, D_FF, D_MODEL), D_FF ** -0.5),
        "post_ffn_g": gain(ks[23], (L, D_MODEL)),
    }


def reference(x, pre_mix_g, w_in, lb_table, a_norm_g, cmp_pe_k, cmp_w1_k, cmp_b1_k, cmp_w2_k, cmp_b2_k,
              cmp_pe_v, cmp_w1_v, cmp_b1_v, cmp_w2_v, cmp_b2_v, w_proj_a, w_proj_b, w_out, post_mix_g,
              pre_ffn_g, w_gate, w_up, w_down, post_ffn_g):
    lb_all = jnp.cumsum(jax.nn.softmax(lb_table.astype(jnp.float32), axis=0), axis=0)
    h = x
    for l in range(DEPTH):
        mix = hybrid_mixer(rms_norm(h, pre_mix_g[l]), w_in[l], lb_all[l], a_norm_g[l],
                           cmp_pe_k[l], cmp_w1_k[l], cmp_b1_k[l], cmp_w2_k[l], cmp_b2_k[l],
                           cmp_pe_v[l], cmp_w1_v[l], cmp_b1_v[l], cmp_w2_v[l], cmp_b2_v[l],
                           w_proj_a[l], w_proj_b[l], w_out[l])
        h = h + rms_norm(mix, post_mix_g[l])
        ff = swiglu(rms_norm(h, pre_ffn_g[l]), w_gate[l], w_up[l], w_down[l])
        h = h + rms_norm(ff, post_ffn_g[l])
    return h
```

```python
import functools

import numpy as np
import jax
import jax.numpy as jnp
from jax import lax
from jax.experimental import pallas as pl
from jax.experimental.pallas import tpu as pltpu

F32 = jnp.float32
BF16 = jnp.bfloat16

D_MODEL = 2048
D_A = 1024
D_B = 1024
A_HEADS = 8
A_DK = 128
B_HEADS = 16
B_KV_HEADS = 4
B_GROUP = 4
HEAD_DIM = 64
KV_DIM = 256
CMP_BLOCK = 32
CMP_STRIDE = 16
CMP_HIDDEN = 256
SLC_BLOCK = 64
SLC_TOP_N = 16
SLC_WEIGHTS = (1.0, 2.0, 2.0, 2.0, 1.0)
WINDOW = 512
Q_TILE = 128
ROPE_THETA = 500000.0
ROPE_DIM = 16
D_FF = 5632
EPS = 1e-6
NEG_INF = -1e30
FORCE_BONUS = 1e4
N_BLK_LANES = 128
LANES = 128
VMEM_LIMIT = 56 * 1024 * 1024

NT_DIMS = (((1,), (1,)), ((), ()))
TN_DIMS = (((0,), (0,)), ((), ()))


def _cparams(n_axes, vmem=None):
    return pltpu.CompilerParams(dimension_semantics=("arbitrary",) * n_axes,
                                vmem_limit_bytes=vmem)


def _rmsnorm_kernel(x_ref, g_ref, o_ref):
    x = x_ref[...]
    y = x * lax.rsqrt(jnp.mean(x * x, axis=-1, keepdims=True) + EPS)
    o_ref[...] = (y * g_ref[...]).astype(o_ref.dtype)


def _rmsnorm_call(x2, g, tm=512):
    t, d = x2.shape
    return pl.pallas_call(
        _rmsnorm_kernel, grid=(t // tm,),
        in_specs=[pl.BlockSpec((tm, d), lambda i: (i, 0)), pl.BlockSpec((1, d), lambda i: (0, 0))],
        out_specs=pl.BlockSpec((tm, d), lambda i: (i, 0)),
        out_shape=jax.ShapeDtypeStruct((t, d), BF16), name="pre_mix_rmsnorm",
        compiler_params=_cparams(1))(x2, g)


def _proj_call(name, xn, w, *, tm, tn, epilogue, out_shape, out_specs, extra=(), extra_specs=()):
    t, k = xn.shape
    n = w.shape[1]
    n_extra = len(extra)

    def body(x_ref, w_ref, *rest):
        acc = jnp.dot(x_ref[...], w_ref[...], preferred_element_type=F32)
        epilogue(acc, rest[:n_extra], rest[n_extra:])

    return pl.pallas_call(
        body, grid=(t // tm, n // tn),
        in_specs=[pl.BlockSpec((tm, k), lambda i, j: (i, 0)),
                  pl.BlockSpec((k, tn), lambda i, j: (0, j)), *extra_specs],
        out_specs=out_specs, out_shape=out_shape, name=name,
        compiler_params=_cparams(2, VMEM_LIMIT))(xn, w, *extra)


def _rope_tile(a, tab_ref):
    return (a * tab_ref[0] + pltpu.roll(a, LANES - ROPE_DIM // 2, 1) * tab_ref[1]
            + pltpu.roll(a, ROPE_DIM // 2, 1) * tab_ref[2])


def _ep_plain(acc, extra, outs):
    outs[0][0] = acc.astype(outs[0].dtype)


def _ep_q(acc, extra, outs):
    for half in range(2):
        r = _rope_tile(acc[:, half * LANES:(half + 1) * LANES], extra[0])
        outs[0][2 * half] = r[:, :HEAD_DIM]
        outs[0][2 * half + 1] = r[:, HEAD_DIM:]


def _ep_cmp(acc, extra, outs):
    for h in range(B_KV_HEADS):
        outs[0][0, h] = acc[:, h * HEAD_DIM:(h + 1) * HEAD_DIM]


def _ep_kv(acc, extra, outs, *, tm):
    i = pl.program_id(0)
    j = pl.program_id(1)
    rowg = i * tm + lax.broadcasted_iota(jnp.int32, (tm, LANES), 0)
    lane = lax.broadcasted_iota(jnp.int32, (tm, LANES), 1)
    aux = jnp.where(j == 0, (rowg // SLC_BLOCK == lane).astype(F32), (lane == 0).astype(F32)).astype(BF16)
    for h in range(B_KV_HEADS):
        r = _rope_tile(acc[:, h * LANES:(h + 1) * LANES], extra[0])
        outs[0][0, h, :, 0:LANES] = r.astype(BF16)
        outs[0][0, h, :, LANES:2 * LANES] = aux


def _ep_gate(acc, extra, outs):
    for h in range(B_KV_HEADS):
        outs[0][h] = jax.nn.sigmoid(acc[:, h * LANES:(h + 1) * LANES])


def _ep_sigmoid(acc, extra, outs):
    outs[0][0] = jax.nn.sigmoid(acc).astype(outs[0].dtype)


def _split3(x):
    hi = x.astype(BF16)
    r1 = x - hi.astype(F32)
    mid = r1.astype(BF16)
    lo = (r1 - mid.astype(F32)).astype(BF16)
    return hi, mid, lo


def _hgrn_kernel(q_ref, f_ref, i_ref, g_ref, lb_ref, gn_ref, o_ref, st_ref, *, chunk, heads):
    c = pl.program_id(1)

    @pl.when(c == 0)
    def _():
        st_ref[...] = jnp.zeros_like(st_ref)

    row = lax.broadcasted_iota(jnp.int32, (chunk, chunk), 0)
    col = lax.broadcasted_iota(jnp.int32, (chunk, chunk), 1)
    tri = (col <= row).astype(BF16)
    rowv = lax.broadcasted_iota(jnp.int32, (chunk, A_DK), 0)

    for hb in range(heads):
        sl = slice(hb * A_DK, (hb + 1) * A_DK)
        q = q_ref[0, :, sl]
        ii = i_ref[0, :, sl]
        gg = g_ref[0, :, sl]
        lbv = lb_ref[:, sl]
        f = lbv + (1.0 - lbv) * jax.nn.sigmoid(f_ref[0, :, sl])
        lf = jnp.log(f)
        k = 1.0 - f
        b = sum(jnp.dot(tri, p, preferred_element_type=F32) for p in _split3(lf))
        b_end = b[chunk - 1:chunk, :]
        ii16 = ii.astype(BF16)

        st = st_ref[hb]
        o = lax.dot_general((q * jnp.exp(b)).astype(BF16), st.astype(BF16), NT_DIMS,
                            preferred_element_type=F32)
        kd = (k * jnp.exp(b_end - b)).astype(BF16)
        st_ref[hb] = st * jnp.exp(b_end) + lax.dot_general(ii16, kd, TN_DIMS,
                                                           preferred_element_type=F32)

        att = jnp.zeros((chunk, chunk), F32)
        m = 8
        while m < chunk:
            grp = chunk // (2 * m)
            b3 = b.reshape(grp, 2 * m, A_DK)
            refrow = jnp.broadcast_to(b3[:, m - 1:m, :], (grp, 2 * m, A_DK)).reshape(chunk, A_DK)
            second = (rowv % (2 * m)) >= m
            ql = jnp.where(second, q * jnp.exp(jnp.where(second, b - refrow, 0.0)), 0.0)
            kl = jnp.where(second, 0.0, k * jnp.exp(jnp.where(second, 0.0, refrow - b)))
            a = lax.dot_general(ql.astype(BF16), kl.astype(BF16), NT_DIMS, preferred_element_type=F32)
            if grp > 1:
                a = jnp.where(row // (2 * m) == col // (2 * m), a, 0.0)
            att = att + a
            m *= 2
        o = o + jnp.dot(att.astype(BF16), ii16, preferred_element_type=F32)

        for d in range(8):
            if d == 0:
                p = q * k
                i_d = ii
            else:
                valid = (rowv % 8) >= d
                w = jnp.exp(jnp.where(valid, b - pltpu.roll(b, d, 0), 0.0))
                p = jnp.where(valid, q * pltpu.roll(k, d, 0) * w, 0.0)
                i_d = pltpu.roll(ii, d, 0)
            o = o + jnp.sum(p, axis=-1, keepdims=True) * i_d

        o = o * lax.rsqrt(jnp.mean(o * o, axis=-1, keepdims=True) + EPS)
        o_ref[:, sl] = ((o * gn_ref[:, sl]) * (gg * jax.nn.sigmoid(gg))).astype(o_ref.dtype)


def _hgrn_call(proj4, lb, gn, *, chunk=64, heads=4):
    _, t, _ = proj4.shape
    w = heads * A_DK

    def spec(kind):
        return pl.BlockSpec((1, chunk, w), lambda h, c, kind=kind: (kind, c, h))

    vec = pl.BlockSpec((1, w), lambda h, c: (0, h))
    return pl.pallas_call(
        functools.partial(_hgrn_kernel, chunk=chunk, heads=heads),
        grid=(A_HEADS // heads, t // chunk),
        in_specs=[spec(0), spec(1), spec(2), spec(3), vec, vec],
        out_specs=pl.BlockSpec((chunk, w), lambda h, c: (c, h)),
        out_shape=jax.ShapeDtypeStruct((t, D_A), BF16),
        scratch_shapes=[pltpu.VMEM((heads, A_DK, A_DK), F32)], name="hgrn2_scan",
        compiler_params=_cparams(2))(proj4, proj4, proj4, proj4, lb, gn)


def _compress_kernel(x_ref, pe_ref, w1_ref, b1_ref, w2_ref, b2_ref, tab_ref, o_ref):
    kv = pl.program_id(0)
    half = CMP_BLOCK * HEAD_DIM // 2
    x = x_ref[0, 0]
    top = jnp.dot((x + pe_ref[0, 0:1, :]).astype(BF16), w1_ref[0, :half, :], preferred_element_type=F32)
    bot = jnp.dot((x + pe_ref[0, 1:2, :]).astype(BF16), w1_ref[0, half:, :], preferred_element_type=F32)
    n = x.shape[0]
    h = jax.nn.gelu(top + pltpu.roll(bot, n - 1, 0) + b1_ref[0])
    y = jnp.dot(h.astype(BF16), w2_ref[0], preferred_element_type=F32) + b2_ref[0]
    y = _rope_tile(y, tab_ref)
    hi = y.astype(BF16)
    o_ref[0, 0, :, 0:LANES] = hi
    o_ref[0, 0, :, LANES:2 * LANES] = jnp.where(kv == 0, (y - hi.astype(F32)).astype(BF16), hi)


def _compress_call(x16, pe2, w1, b1, w2p, b2p, tab):
    _, nh, n, wid = x16.shape
    hid = w1.shape[-1]
    return pl.pallas_call(
        _compress_kernel, grid=(2, nh),
        in_specs=[pl.BlockSpec((1, 1, n, wid), lambda kv, h: (kv, h, 0, 0)),
                  pl.BlockSpec((1, 2, wid), lambda kv, h: (kv, 0, 0)),
                  pl.BlockSpec((1, 2 * wid, hid), lambda kv, h: (kv, 0, 0)),
                  pl.BlockSpec((1, 1, hid), lambda kv, h: (kv, 0, 0)),
                  pl.BlockSpec((1, hid, LANES), lambda kv, h: (kv, 0, 0)),
                  pl.BlockSpec((1, 1, LANES), lambda kv, h: (kv, 0, 0)),
                  pl.BlockSpec((3, n, LANES), lambda kv, h: (kv, 0, 0))],
        out_specs=pl.BlockSpec((1, 1, n, 2 * LANES), lambda kv, h: (kv, h, 0, 0)),
        out_shape=jax.ShapeDtypeStruct((2, nh, n, 2 * LANES), BF16),
        name="compress_mlp", compiler_params=_cparams(2, VMEM_LIMIT))(x16, pe2, w1, b1, w2p, b2p, tab)


def _nsa_kernel(q_ref, kc_ref, vc_ref, wt_ref, gate_ref, ksv_ref, kwv_ref, o_ref, lhs_ref, *, key_tile):
    qt = pl.program_id(1)
    q0 = qt * Q_TILE
    rows = B_GROUP * Q_TILE
    n_cmp = kc_ref.shape[2]

    q = q_ref[...].reshape(rows, HEAD_DIM)
    qhi = q.astype(BF16)
    qlo = (q - qhi.astype(F32)).astype(BF16)
    zeros64 = jnp.zeros((rows, HEAD_DIM), BF16)
    t_col = q0 + lax.broadcasted_iota(jnp.int32, (rows, 1), 0) % Q_TILE

    lhs_ref[:, 0:HEAD_DIM] = qhi
    lhs_ref[:, HEAD_DIM:2 * HEAD_DIM] = zeros64
    lhs_ref[:, 2 * HEAD_DIM:3 * HEAD_DIM] = qhi
    lhs_ref[:, 3 * HEAD_DIM:] = zeros64
    s = lax.dot_general(lhs_ref[...], kc_ref[0, 0], NT_DIMS, preferred_element_type=F32)
    s = s + lax.dot_general(qlo, kc_ref[0, 0, :, 0:HEAD_DIM], NT_DIMS, preferred_element_type=F32)
    n_idx = lax.broadcasted_iota(jnp.int32, (rows, n_cmp), 1)
    vis = n_idx * CMP_STRIDE + (CMP_BLOCK - 1) <= t_col
    s = jnp.where(vis, s, NEG_INF)
    e = jnp.where(vis, jnp.exp(s - jnp.max(s, axis=-1, keepdims=True)), 0.0)
    l = jnp.sum(e, axis=-1, keepdims=True)
    p = e / jnp.where(l > 0.0, l, 1.0)
    o_cmp = jnp.dot(p.astype(BF16), vc_ref[0, 0, :, 0:LANES], preferred_element_type=F32)[:, :HEAD_DIM]

    psum = p[0:Q_TILE] + p[Q_TILE:2 * Q_TILE] + p[2 * Q_TILE:3 * Q_TILE] + p[3 * Q_TILE:]
    ps_hi = psum.astype(BF16)
    ps_lo = (psum - ps_hi.astype(F32)).astype(BF16)
    wt = wt_ref[...]
    pslc = (lax.dot_general(wt, ps_hi, NT_DIMS, preferred_element_type=F32)
            + lax.dot_general(wt, ps_lo, NT_DIMS, preferred_element_type=F32))
    blk = lax.broadcasted_iota(jnp.int32, (N_BLK_LANES, Q_TILE), 0)
    tok = q0 + lax.broadcasted_iota(jnp.int32, (N_BLK_LANES, Q_TILE), 1)
    cur = tok // SLC_BLOCK
    forced = (blk == 0) | (blk == cur) | (blk == cur - 1)
    score = jnp.where(forced, FORCE_BONUS, jnp.where(blk * SLC_BLOCK <= tok, pslc, NEG_INF))

    def pick(_, carry):
        sc, sel = carry
        best = jnp.max(sc, axis=0, keepdims=True)
        first = jnp.min(jnp.where(sc == best, blk, N_BLK_LANES), axis=0, keepdims=True)
        hit = blk == first
        return jnp.where(hit, -jnp.inf, sc), jnp.where(hit, 1.0, sel)

    _, sel_t = lax.fori_loop(0, SLC_TOP_N, pick, (score, jnp.zeros((N_BLK_LANES, Q_TILE), F32)))
    bias = jnp.where(sel_t.T > 0.0, 0.0, NEG_INF).astype(BF16)

    lhs_ref[:, 2 * HEAD_DIM:3 * HEAD_DIM] = zeros64
    slab = WINDOW + Q_TILE
    ws = pl.multiple_of(jnp.maximum(q0 - WINDOW, 0), Q_TILE)
    kw = kwv_ref[0, 0, pl.ds(ws, slab), :]
    s = lax.dot_general(lhs_ref[...], kw, NT_DIMS, preferred_element_type=F32)
    kpos = ws + lax.broadcasted_iota(jnp.int32, (rows, slab), 1)
    ok = (kpos <= t_col) & (t_col - kpos < WINDOW)
    s = jnp.where(ok, s, NEG_INF)
    e = jnp.exp(s - jnp.max(s, axis=-1, keepdims=True))
    acc = jnp.dot(e.astype(BF16), kw, preferred_element_type=F32)
    o_win = acc[:, HEAD_DIM:2 * HEAD_DIM] / acc[:, 2 * HEAD_DIM:2 * HEAD_DIM + 1]

    for g in range(B_GROUP):
        lhs_ref[g * Q_TILE:(g + 1) * Q_TILE, 2 * HEAD_DIM:] = bias

    def tile(kt, carry, causal):
        m_run, acc_run = carry
        start = pl.multiple_of(kt * key_tile, key_tile)
        kv = ksv_ref[0, 0, pl.ds(start, key_tile), :]
        st = lax.dot_general(lhs_ref[...], kv, NT_DIMS, preferred_element_type=F32)
        if causal:
            kp = start + lax.broadcasted_iota(jnp.int32, (rows, key_tile), 1)
            st = jnp.where(kp <= t_col, st, NEG_INF)
        m_new = jnp.maximum(m_run, jnp.max(st, axis=-1, keepdims=True))
        pt = jnp.exp(st - m_new)
        acc_new = acc_run * jnp.exp(m_run - m_new) + jnp.dot(pt.astype(BF16), kv,
                                                               preferred_element_type=F32)
        return m_new, acc_new

    n_full = q0 // key_tile
    carry = (jnp.full((rows, 1), NEG_INF, F32), jnp.zeros((rows, 2 * LANES), F32))
    carry = lax.fori_loop(0, n_full, functools.partial(tile, causal=False), carry)
    _, acc = tile(n_full, carry, True)
    o_slc = acc[:, HEAD_DIM:2 * HEAD_DIM] / jnp.sum(acc[:, 2 * HEAD_DIM:], axis=-1, keepdims=True)

    gates = gate_ref[0]
    for g in range(B_GROUP):
        r = slice(g * Q_TILE, (g + 1) * Q_TILE)
        og = (gates[:, 3 * g:3 * g + 1] * o_cmp[r] + gates[:, 3 * g + 1:3 * g + 2] * o_slc[r]
              + gates[:, 3 * g + 2:3 * g + 3] * o_win[r])
        o_ref[:, g * HEAD_DIM:(g + 1) * HEAD_DIM] = og.astype(o_ref.dtype)


def _nsa_call(q_hm, cmp_aug, wt, gates, kv_aug, *, key_tile=256):
    _, t, _ = q_hm.shape
    n_cmp = cmp_aug.shape[2]
    return pl.pallas_call(
        functools.partial(_nsa_kernel, key_tile=key_tile),
        grid=(B_KV_HEADS, t // Q_TILE),
        in_specs=[pl.BlockSpec((B_GROUP, Q_TILE, HEAD_DIM), lambda h, i: (h, i, 0)),
                  pl.BlockSpec((1, 1, n_cmp, 2 * LANES), lambda h, i: (0, h, 0, 0)),
                  pl.BlockSpec((1, 1, n_cmp, 2 * LANES), lambda h, i: (1, h, 0, 0)),
                  pl.BlockSpec((N_BLK_LANES, n_cmp), lambda h, i: (0, 0)),
                  pl.BlockSpec((1, Q_TILE, LANES), lambda h, i: (h, i, 0)),
                  pl.BlockSpec((1, 1, t, 2 * LANES), lambda h, i: (0, h, 0, 0)),
                  pl.BlockSpec((1, 1, t, 2 * LANES), lambda h, i: (1, h, 0, 0))],
        out_specs=pl.BlockSpec((Q_TILE, B_GROUP * HEAD_DIM), lambda h, i: (i, h)),
        out_shape=jax.ShapeDtypeStruct((t, D_B), BF16),
        scratch_shapes=[pltpu.VMEM((B_GROUP * Q_TILE, 2 * LANES), BF16)], name="nsa_attention",
        compiler_params=_cparams(2, VMEM_LIMIT))(q_hm, cmp_aug, cmp_aug, wt, gates, kv_aug, kv_aug)


def _post_kernel(oa_ref, ob_ref, sg_ref, x_ref, wa_ref, wb_ref, wo_ref, g_ref, o_ref):
    ya = jnp.dot(oa_ref[...], wa_ref[...], preferred_element_type=F32)
    yb = jnp.dot(ob_ref[...], wb_ref[...], preferred_element_type=F32)
    merged = sg_ref[0].astype(F32) * ya + sg_ref[1].astype(F32) * yb
    mix = jnp.dot(merged.astype(BF16), wo_ref[...], preferred_element_type=F32)
    y = mix * lax.rsqrt(jnp.mean(mix * mix, axis=-1, keepdims=True) + EPS)
    o_ref[...] = x_ref[...] + y * g_ref[...]


def _post_call(oa, ob, sg, x2, wa, wb, wo, g, tm=256):
    t, d = x2.shape
    const = dict(pipeline_mode=pl.Buffered(1))
    return pl.pallas_call(
        _post_kernel, grid=(t // tm,),
        in_specs=[pl.BlockSpec((tm, D_A), lambda i: (i, 0)),
                  pl.BlockSpec((tm, D_B), lambda i: (i, 0)),
                  pl.BlockSpec((2, tm, d), lambda i: (0, i, 0)),
                  pl.BlockSpec((tm, d), lambda i: (i, 0)),
                  pl.BlockSpec((D_A, d), lambda i: (0, 0), **const),
                  pl.BlockSpec((D_B, d), lambda i: (0, 0), **const),
                  pl.BlockSpec((d, d), lambda i: (0, 0), **const),
                  pl.BlockSpec((1, d), lambda i: (0, 0))],
        out_specs=pl.BlockSpec((tm, d), lambda i: (i, 0)),
        out_shape=jax.ShapeDtypeStruct((t, d), F32), name="mix_out_residual",
        compiler_params=_cparams(1, VMEM_LIMIT))(oa, ob, sg, x2, wa, wb, wo, g)


def _ffn_kernel(h_ref, g1_ref, wg_ref, wu_ref, wd_ref, g2_ref, o_ref, hn_ref, acc_ref):
    f = pl.program_id(1)

    @pl.when(f == 0)
    def _():
        h = h_ref[...]
        y = h * lax.rsqrt(jnp.mean(h * h, axis=-1, keepdims=True) + EPS)
        hn_ref[...] = (y * g1_ref[...]).astype(BF16)
        acc_ref[...] = jnp.zeros_like(acc_ref)

    hn = hn_ref[...]
    a = jnp.dot(hn, wg_ref[...], preferred_element_type=F32)
    u = jnp.dot(hn, wu_ref[...], preferred_element_type=F32)
    z = (a * jax.nn.sigmoid(a) * u).astype(BF16)
    acc_ref[...] += jnp.dot(z, wd_ref[...], preferred_element_type=F32)

    @pl.when(f == pl.num_programs(1) - 1)
    def _():
        ff = acc_ref[...]
        y = ff * lax.rsqrt(jnp.mean(ff * ff, axis=-1, keepdims=True) + EPS)
        o_ref[...] = h_ref[...] + y * g2_ref[...]


def _ffn_call(h1, g1, wg, wu, wd, g2, tm=512, tf=512):
    t, d = h1.shape
    dff = wg.shape[1]
    return pl.pallas_call(
        _ffn_kernel, grid=(t // tm, dff // tf),
        in_specs=[pl.BlockSpec((tm, d), lambda i, f: (i, 0)),
                  pl.BlockSpec((1, d), lambda i, f: (0, 0)),
                  pl.BlockSpec((d, tf), lambda i, f: (0, f)),
                  pl.BlockSpec((d, tf), lambda i, f: (0, f)),
                  pl.BlockSpec((tf, d), lambda i, f: (f, 0)),
                  pl.BlockSpec((1, d), lambda i, f: (0, 0))],
        out_specs=pl.BlockSpec((tm, d), lambda i, f: (i, 0)),
        out_shape=jax.ShapeDtypeStruct((t, d), F32),
        scratch_shapes=[pltpu.VMEM((tm, d), BF16), pltpu.VMEM((tm, d), F32)], name="swiglu_ffn",
        compiler_params=_cparams(2, VMEM_LIMIT))(h1, g1, wg, wu, wd, g2)


def _rope_tables(pos, period, scale=1.0):
    half = ROPE_DIM // 2
    inv = jnp.float32(ROPE_THETA) ** (-jnp.arange(half, dtype=F32) * 2.0 / ROPE_DIM)
    ang = pos.astype(F32)[:, None] * inv[None, :]
    cos, sin = jnp.cos(ang), jnp.sin(ang)
    r = np.arange(LANES) % period
    f = r % half
    lo = jnp.asarray((r < half)[None, :])
    hi = jnp.asarray(((r >= half) & (r < ROPE_DIM))[None, :])
    c = jnp.where(lo | hi, cos[:, f], 1.0)
    sa = jnp.where(lo, -sin[:, f], 0.0)
    sb = jnp.where(hi, sin[:, f], 0.0)
    return jnp.stack([c, sa, sb]) * scale


def _slc_weight_matrix(n_cmp_pad):
    ratio = SLC_BLOCK // CMP_STRIDE
    w = np.zeros((N_BLK_LANES, n_cmp_pad), np.float32)
    for j in range(N_BLK_LANES):
        for o, wv in enumerate(SLC_WEIGHTS):
            n = ratio * j + o - 1
            if 0 <= n < n_cmp_pad - 1:
                w[j, n] = wv
    return jnp.asarray(w, BF16)


def kernel(x, pre_mix_g, w_in, lb_table, a_norm_g, cmp_pe_k, cmp_w1_k, cmp_b1_k, cmp_w2_k, cmp_b2_k, cmp_pe_v, cmp_w1_v, cmp_b1_v, cmp_w2_v, cmp_b2_v, w_proj_a, w_proj_b, w_out, post_mix_g, pre_ffn_g, w_gate, w_up, w_down, post_ffn_g):
    bsz, t, d = x.shape
    assert bsz == 1 and d == D_MODEL and t % 256 == 0 and WINDOW + Q_TILE <= t <= N_BLK_LANES * SLC_BLOCK
    x2 = x.reshape(t, d)
    n_cmp_pad = t // CMP_STRIDE
    assert n_cmp_pad % LANES == 0

    lb = jnp.cumsum(jax.nn.softmax(lb_table.astype(F32), axis=0), axis=0)[0].reshape(1, D_A)
    w = w_in[0]
    o_q, o_kc, o_ks, o_g, o_m = 4 * D_A, 4 * D_A + D_B, 4 * D_A + D_B + 2 * KV_DIM, 4 * D_A + D_B + 6 * KV_DIM, 4 * D_A + D_B + 6 * KV_DIM + 3 * B_HEADS
    w_hgrn = w[:, :o_q].astype(BF16)
    w_q = w[:, o_q:o_kc].astype(BF16)
    w_cmp = w[:, o_kc:o_ks].astype(BF16)
    w_kv = (w[:, o_ks:o_g].reshape(d, 2, 2, B_KV_HEADS, HEAD_DIM).transpose(0, 1, 3, 2, 4)
            .reshape(d, 4 * KV_DIM).astype(BF16))
    w_gate3 = w[:, o_g:o_m].reshape(d, B_KV_HEADS, 3 * B_GROUP)
    w_g = jnp.pad(w_gate3, ((0, 0), (0, 0), (0, LANES - 3 * B_GROUP))).reshape(d, B_KV_HEADS * LANES).astype(BF16)
    w_m = w[:, o_m:].astype(BF16)

    pos = jnp.arange(t)
    tab_q = _rope_tables(pos, HEAD_DIM, HEAD_DIM ** -0.5)
    tab_kv = _rope_tables(pos, LANES)
    end_pos = jnp.arange(n_cmp_pad) * CMP_STRIDE + (CMP_BLOCK - 1)
    ident = jnp.stack([jnp.ones((n_cmp_pad, LANES), F32), jnp.zeros((n_cmp_pad, LANES), F32),
                       jnp.zeros((n_cmp_pad, LANES), F32)])
    tab_cmp = jnp.concatenate([_rope_tables(end_pos, LANES), ident], axis=0)

    xn = _rmsnorm_call(x2, pre_mix_g[0].reshape(1, d))

    tm = 1024 if t % 1024 == 0 else 256
    proj4 = _proj_call(
        "proj_hgrn", xn, w_hgrn, tm=tm, tn=512, epilogue=_ep_plain,
        out_shape=jax.ShapeDtypeStruct((4, t, D_A), F32),
        out_specs=pl.BlockSpec((1, tm, 512), lambda i, j: (j // 2, i, j % 2)))
    q_hm = _proj_call(
        "proj_q", xn, w_q, tm=tm, tn=B_GROUP * HEAD_DIM, epilogue=_ep_q,
        out_shape=jax.ShapeDtypeStruct((B_HEADS, t, HEAD_DIM), F32),
        out_specs=pl.BlockSpec((B_GROUP, tm, HEAD_DIM), lambda i, j: (j, i, 0)),
        extra=(tab_q,), extra_specs=(pl.BlockSpec((3, tm, LANES), lambda i, j: (0, i, 0)),))
    cmp_in = _proj_call(
        "proj_cmp", xn, w_cmp, tm=tm, tn=KV_DIM, epilogue=_ep_cmp,
        out_shape=jax.ShapeDtypeStruct((2, B_KV_HEADS, t, HEAD_DIM), F32),
        out_specs=pl.BlockSpec((1, B_KV_HEADS, tm, HEAD_DIM), lambda i, j: (j, 0, i, 0)))
    kv_aug = _proj_call(
        "proj_kv", xn, w_kv, tm=tm, tn=2 * KV_DIM, epilogue=functools.partial(_ep_kv, tm=tm),
        out_shape=jax.ShapeDtypeStruct((2, B_KV_HEADS, t, 2 * LANES), BF16),
        out_specs=pl.BlockSpec((1, B_KV_HEADS, tm, 2 * LANES), lambda i, j: (j, 0, i, 0)),
        extra=(tab_kv,), extra_specs=(pl.BlockSpec((3, tm, LANES), lambda i, j: (0, i, 0)),))
    gates = _proj_call(
        "proj_gate", xn, w_g, tm=tm, tn=B_KV_HEADS * LANES, epilogue=_ep_gate,
        out_shape=jax.ShapeDtypeStruct((B_KV_HEADS, t, LANES), F32),
        out_specs=pl.BlockSpec((B_KV_HEADS, tm, LANES), lambda i, j: (0, i, 0)))
    sg = _proj_call(
        "proj_merge", xn, w_m, tm=tm, tn=512, epilogue=_ep_sigmoid,
        out_shape=jax.ShapeDtypeStruct((2, t, d), BF16),
        out_specs=pl.BlockSpec((1, tm, 512), lambda i, j: (j // 4, i, j % 4)))

    oa = _hgrn_call(proj4, lb, a_norm_g[0].reshape(1, D_A))

    half = CMP_BLOCK // 2
    x16 = cmp_in.reshape(2, B_KV_HEADS, n_cmp_pad, CMP_STRIDE * HEAD_DIM)
    pe2 = jnp.stack([cmp_pe_k[0], cmp_pe_v[0]]).reshape(2, 2, half * HEAD_DIM)
    w1 = jnp.stack([cmp_w1_k[0], cmp_w1_v[0]]).astype(BF16)
    b1 = jnp.stack([cmp_b1_k[0], cmp_b1_v[0]]).reshape(2, 1, CMP_HIDDEN)
    w2p = jnp.pad(jnp.stack([cmp_w2_k[0], cmp_w2_v[0]]), ((0, 0), (0, 0), (0, LANES - HEAD_DIM))).astype(BF16)
    b2p = jnp.pad(jnp.stack([cmp_b2_k[0], cmp_b2_v[0]]), ((0, 0), (0, LANES - HEAD_DIM))).reshape(2, 1, LANES)
    cmp_aug = _compress_call(x16, pe2, w1, b1, w2p, b2p, tab_cmp)
    ob = _nsa_call(q_hm, cmp_aug, _slc_weight_matrix(n_cmp_pad), gates, kv_aug)

    h1 = _post_call(oa, ob, sg, x2, w_proj_a[0].astype(BF16), w_proj_b[0].astype(BF16),
                    w_out[0].astype(BF16), post_mix_g[0].reshape(1, d))
    out = _ffn_call(h1, pre_ffn_g[0].reshape(1, d), w_gate[0].astype(BF16), w_up[0].astype(BF16),
                    w_down[0].astype(BF16), post_ffn_g[0].reshape(1, d))
    return out.reshape(bsz, t, d)
```

```python
import functools

import numpy as np
import jax
import jax.numpy as jnp
from jax import lax
from jax.experimental import pallas as pl
from jax.experimental.pallas import tpu as pltpu

F32 = jnp.float32
BF16 = jnp.bfloat16

D_MODEL = 2048
D_A = 1024
D_B = 1024
A_HEADS = 8
A_DK = 128
B_HEADS = 16
B_KV_HEADS = 4
B_GROUP = 4
HEAD_DIM = 64
KV_DIM = 256
CMP_BLOCK = 32
CMP_STRIDE = 16
CMP_HIDDEN = 256
SLC_BLOCK = 64
SLC_TOP_N = 16
SLC_WEIGHTS = (1.0, 2.0, 2.0, 2.0, 1.0)
WINDOW = 512
Q_TILE = 128
ROPE_THETA = 500000.0
ROPE_DIM = 16
D_FF = 5632
EPS = 1e-6
NEG_INF = -1e30
FORCE_BONUS = 1e4
N_BLK_LANES = 128
LANES = 128
VMEM_LIMIT = 56 * 1024 * 1024

NT_DIMS = (((1,), (1,)), ((), ()))
TN_DIMS = (((0,), (0,)), ((), ()))


def _cparams(n_axes, vmem=None):
    return pltpu.CompilerParams(dimension_semantics=("arbitrary",) * n_axes,
                                vmem_limit_bytes=vmem)


def _rmsnorm_kernel(x_ref, g_ref, o_ref):
    x = x_ref[...]
    y = x * lax.rsqrt(jnp.mean(x * x, axis=-1, keepdims=True) + EPS)
    o_ref[...] = (y * g_ref[...]).astype(o_ref.dtype)


def _rmsnorm_call(x2, g, tm=512):
    t, d = x2.shape
    return pl.pallas_call(
        _rmsnorm_kernel, grid=(t // tm,),
        in_specs=[pl.BlockSpec((tm, d), lambda i: (i, 0)), pl.BlockSpec((1, d), lambda i: (0, 0))],
        out_specs=pl.BlockSpec((tm, d), lambda i: (i, 0)),
        out_shape=jax.ShapeDtypeStruct((t, d), BF16), name="pre_mix_rmsnorm",
        compiler_params=_cparams(1))(x2, g)


def _proj_call(name, xn, w, *, tm, tn, epilogue, out_shape, out_specs, extra=(), extra_specs=()):
    t, k = xn.shape
    n = w.shape[1]
    n_extra = len(extra)

    def body(x_ref, w_ref, *rest):
        acc = jnp.dot(x_ref[...], w_ref[...], preferred_element_type=F32)
        epilogue(acc, rest[:n_extra], rest[n_extra:])

    return pl.pallas_call(
        body, grid=(t // tm, n // tn),
        in_specs=[pl.BlockSpec((tm, k), lambda i, j: (i, 0)),
                  pl.BlockSpec((k, tn), lambda i, j: (0, j)), *extra_specs],
        out_specs=out_specs, out_shape=out_shape, name=name,
        compiler_params=_cparams(2, VMEM_LIMIT))(xn, w, *extra)


def _rope_tile(a, tab_ref):
    return (a * tab_ref[0] + pltpu.roll(a, LANES - ROPE_DIM // 2, 1) * tab_ref[1]
            + pltpu.roll(a, ROPE_DIM // 2, 1) * tab_ref[2])


def _ep_plain(acc, extra, outs):
    outs[0][0] = acc.astype(outs[0].dtype)


def _ep_q(acc, extra, outs):
    for half in range(2):
        r = _rope_tile(acc[:, half * LANES:(half + 1) * LANES], extra[0])
        outs[0][2 * half] = r[:, :HEAD_DIM]
        outs[0][2 * half + 1] = r[:, HEAD_DIM:]


def _ep_cmp(acc, extra, outs):
    for h in range(B_KV_HEADS):
        outs[0][0, h] = acc[:, h * HEAD_DIM:(h + 1) * HEAD_DIM]


def _ep_kv(acc, extra, outs, *, tm):
    i = pl.program_id(0)
    j = pl.program_id(1)
    rowg = i * tm + lax.broadcasted_iota(jnp.int32, (tm, LANES), 0)
    lane = lax.broadcasted_iota(jnp.int32, (tm, LANES), 1)
    aux = jnp.where(j == 0, (rowg // SLC_BLOCK == lane).astype(F32), (lane == 0).astype(F32)).astype(BF16)
    ones_col = (lane == HEAD_DIM).astype(F32)
    for h in range(B_KV_HEADS):
        r = _rope_tile(acc[:, h * LANES:(h + 1) * LANES], extra[0])
        outs[0][0, h, :, 0:LANES] = r.astype(BF16)
        outs[0][0, h, :, LANES:2 * LANES] = aux
        outs[1][0, h] = jnp.where(lane < HEAD_DIM, pltpu.roll(r, HEAD_DIM, 1), ones_col).astype(BF16)


def _ep_gate(acc, extra, outs):
    for h in range(B_KV_HEADS):
        outs[0][h] = jax.nn.sigmoid(acc[:, h * LANES:(h + 1) * LANES])


def _ep_sigmoid(acc, extra, outs):
    outs[0][0] = jax.nn.sigmoid(acc).astype(outs[0].dtype)


def _split3(x):
    hi = x.astype(BF16)
    r1 = x - hi.astype(F32)
    mid = r1.astype(BF16)
    lo = (r1 - mid.astype(F32)).astype(BF16)
    return hi, mid, lo


def _hgrn_kernel(q_ref, f_ref, i_ref, g_ref, lb_ref, gn_ref, o_ref, st_ref, *, chunk, heads):
    c = pl.program_id(1)

    @pl.when(c == 0)
    def _():
        st_ref[...] = jnp.zeros_like(st_ref)

    row = lax.broadcasted_iota(jnp.int32, (chunk, chunk), 0)
    col = lax.broadcasted_iota(jnp.int32, (chunk, chunk), 1)
    tri = (col <= row).astype(BF16)
    rowv = lax.broadcasted_iota(jnp.int32, (chunk, A_DK), 0)

    for hb in range(heads):
        sl = slice(hb * A_DK, (hb + 1) * A_DK)
        q = q_ref[0, :, sl]
        ii = i_ref[0, :, sl]
        gg = g_ref[0, :, sl]
        lbv = lb_ref[:, sl]
        f = lbv + (1.0 - lbv) * jax.nn.sigmoid(f_ref[0, :, sl])
        lf = jnp.log(f)
        k = 1.0 - f
        b = sum(jnp.dot(tri, p, preferred_element_type=F32) for p in _split3(lf))
        b_end = b[chunk - 1:chunk, :]
        ii16 = ii.astype(BF16)

        st = st_ref[hb]
        o = lax.dot_general((q * jnp.exp(b)).astype(BF16), st.astype(BF16), NT_DIMS,
                            preferred_element_type=F32)
        kd = (k * jnp.exp(b_end - b)).astype(BF16)
        st_ref[hb] = st * jnp.exp(b_end) + lax.dot_general(ii16, kd, TN_DIMS,
                                                           preferred_element_type=F32)

        att = jnp.zeros((chunk, chunk), F32)
        m = 8
        while m < chunk:
            grp = chunk // (2 * m)
            b3 = b.reshape(grp, 2 * m, A_DK)
            refrow = jnp.broadcast_to(b3[:, m - 1:m, :], (grp, 2 * m, A_DK)).reshape(chunk, A_DK)
            second = (rowv % (2 * m)) >= m
            ql = jnp.where(second, q * jnp.exp(jnp.where(second, b - refrow, 0.0)), 0.0)
            kl = jnp.where(second, 0.0, k * jnp.exp(jnp.where(second, 0.0, refrow - b)))
            a = lax.dot_general(ql.astype(BF16), kl.astype(BF16), NT_DIMS, preferred_element_type=F32)
            if grp > 1:
                a = jnp.where(row // (2 * m) == col // (2 * m), a, 0.0)
            att = att + a
            m *= 2
        o = o + jnp.dot(att.astype(BF16), ii16, preferred_element_type=F32)

        for d in range(8):
            if d == 0:
                p = q * k
                i_d = ii
            else:
                valid = (rowv % 8) >= d
                w = jnp.exp(jnp.where(valid, b - pltpu.roll(b, d, 0), 0.0))
                p = jnp.where(valid, q * pltpu.roll(k, d, 0) * w, 0.0)
                i_d = pltpu.roll(ii, d, 0)
            o = o + jnp.sum(p, axis=-1, keepdims=True) * i_d

        o = o * lax.rsqrt(jnp.mean(o * o, axis=-1, keepdims=True) + EPS)
        o_ref[:, sl] = ((o * gn_ref[:, sl]) * (gg * jax.nn.sigmoid(gg))).astype(o_ref.dtype)


def _hgrn_call(proj4, lb, gn, *, chunk=64, heads=4):
    _, t, _ = proj4.shape
    w = heads * A_DK

    def spec(kind):
        return pl.BlockSpec((1, chunk, w), lambda h, c, kind=kind: (kind, c, h))

    vec = pl.BlockSpec((1, w), lambda h, c: (0, h))
    return pl.pallas_call(
        functools.partial(_hgrn_kernel, chunk=chunk, heads=heads),
        grid=(A_HEADS // heads, t // chunk),
        in_specs=[spec(0), spec(1), spec(2), spec(3), vec, vec],
        out_specs=pl.BlockSpec((chunk, w), lambda h, c: (c, h)),
        out_shape=jax.ShapeDtypeStruct((t, D_A), BF16),
        scratch_shapes=[pltpu.VMEM((heads, A_DK, A_DK), F32)], name="hgrn2_scan",
        compiler_params=_cparams(2))(proj4, proj4, proj4, proj4, lb, gn)


def _compress_kernel(x_ref, pe_ref, w1_ref, b1_ref, w2_ref, b2_ref, tab_ref, o_ref):
    half = CMP_BLOCK * HEAD_DIM // 2
    x = x_ref[0, 0]
    top = jnp.dot((x + pe_ref[0, 0:1, :]).astype(BF16), w1_ref[0, :half, :], preferred_element_type=F32)
    bot = jnp.dot((x + pe_ref[0, 1:2, :]).astype(BF16), w1_ref[0, half:, :], preferred_element_type=F32)
    n = x.shape[0]
    h = jax.nn.gelu(top + pltpu.roll(bot, n - 1, 0) + b1_ref[0])
    y = jnp.dot(h.astype(BF16), w2_ref[0], preferred_element_type=F32) + b2_ref[0]
    y = _rope_tile(y, tab_ref)
    hi = y.astype(BF16)
    lo = (y - hi.astype(F32)).astype(BF16)
    o_ref[0, 0, :, 0:LANES] = (y + pltpu.roll(y, HEAD_DIM, 1)).astype(BF16)
    o_ref[0, 0, :, LANES:2 * LANES] = lo


def _compress_call(x16, pe2, w1, b1, w2p, b2p, tab):
    _, nh, n, wid = x16.shape
    hid = w1.shape[-1]
    return pl.pallas_call(
        _compress_kernel, grid=(2, nh),
        in_specs=[pl.BlockSpec((1, 1, n, wid), lambda kv, h: (kv, h, 0, 0)),
                  pl.BlockSpec((1, 2, wid), lambda kv, h: (kv, 0, 0)),
                  pl.BlockSpec((1, 2 * wid, hid), lambda kv, h: (kv, 0, 0)),
                  pl.BlockSpec((1, 1, hid), lambda kv, h: (kv, 0, 0)),
                  pl.BlockSpec((1, hid, LANES), lambda kv, h: (kv, 0, 0)),
                  pl.BlockSpec((1, 1, LANES), lambda kv, h: (kv, 0, 0)),
                  pl.BlockSpec((3, n, LANES), lambda kv, h: (kv, 0, 0))],
        out_specs=pl.BlockSpec((1, 1, n, 2 * LANES), lambda kv, h: (kv, h, 0, 0)),
        out_shape=jax.ShapeDtypeStruct((2, nh, n, 2 * LANES), BF16),
        name="compress_mlp", compiler_params=_cparams(2, VMEM_LIMIT))(x16, pe2, w1, b1, w2p, b2p, tab)


def _nsa_kernel(q_ref, kc_ref, vc_ref, wt_ref, gate_ref, ksv_ref, kwv_ref, vs_ref, vw_ref, o_ref, lhs_ref, *,
                key_tile, tiles_per_group):
    qt = pl.program_id(1)
    q0 = qt * Q_TILE
    rows = B_GROUP * Q_TILE
    n_cmp = kc_ref.shape[2]

    q = q_ref[...].reshape(rows, HEAD_DIM)
    qhi = q.astype(BF16)
    qlo = (q - qhi.astype(F32)).astype(BF16)
    zeros64 = jnp.zeros((rows, HEAD_DIM), BF16)
    t_col = q0 + lax.broadcasted_iota(jnp.int32, (rows, 1), 0) % Q_TILE

    lhs_ref[:, 0:HEAD_DIM] = qhi
    lhs_ref[:, HEAD_DIM:2 * HEAD_DIM] = qlo
    lhs_ref[:, 2 * HEAD_DIM:3 * HEAD_DIM] = qhi
    lhs_ref[:, 3 * HEAD_DIM:] = zeros64
    s = lax.dot_general(lhs_ref[...], kc_ref[0, 0], NT_DIMS, preferred_element_type=F32)
    n_idx = lax.broadcasted_iota(jnp.int32, (rows, n_cmp), 1)
    vis = n_idx <= (t_col - (CMP_BLOCK - 1)) // CMP_STRIDE
    s = jnp.where(vis, s, NEG_INF)
    e = jnp.where(vis, jnp.exp2(s - jnp.max(s, axis=-1, keepdims=True)), 0.0)
    l = jnp.sum(e, axis=-1, keepdims=True)
    p = e / jnp.where(l > 0.0, l, 1.0)
    o_cmp = jnp.dot(p.astype(BF16), vc_ref[0, 0, :, 0:LANES], preferred_element_type=F32)[:, :HEAD_DIM]

    psum = p[0:Q_TILE] + p[Q_TILE:2 * Q_TILE] + p[2 * Q_TILE:3 * Q_TILE] + p[3 * Q_TILE:]
    ps_hi = psum.astype(BF16)
    ps_lo = (psum - ps_hi.astype(F32)).astype(BF16)
    wt = wt_ref[...]
    pslc = (lax.dot_general(wt, ps_hi, NT_DIMS, preferred_element_type=F32)
            + lax.dot_general(wt, ps_lo, NT_DIMS, preferred_element_type=F32))
    blk = lax.broadcasted_iota(jnp.int32, (N_BLK_LANES, Q_TILE), 0)
    tok = q0 + lax.broadcasted_iota(jnp.int32, (N_BLK_LANES, Q_TILE), 1)
    cur = tok // SLC_BLOCK
    forced = (blk == 0) | (blk == cur) | (blk == cur - 1)
    score = jnp.where(forced, FORCE_BONUS, jnp.where(blk * SLC_BLOCK <= tok, pslc, NEG_INF))

    sel_t = jnp.zeros((N_BLK_LANES, Q_TILE), F32)
    for _ in range(SLC_TOP_N):
        best = jnp.max(score, axis=0, keepdims=True)
        first = jnp.min(jnp.where(score == best, blk, N_BLK_LANES), axis=0, keepdims=True)
        hit = blk == first
        score = jnp.where(hit, -jnp.inf, score)
        sel_t = jnp.where(hit, 1.0, sel_t)
    bias = jnp.where(sel_t.T > 0.0, 0.0, NEG_INF).astype(BF16)

    lhs_ref[:, HEAD_DIM:3 * HEAD_DIM] = jnp.zeros((rows, 2 * HEAD_DIM), BF16)
    slab = WINDOW + Q_TILE
    ws = pl.multiple_of(jnp.maximum(q0 - WINDOW, 0), Q_TILE)
    kw = kwv_ref[0, 0, pl.ds(ws, slab), :]
    s = lax.dot_general(lhs_ref[...], kw, NT_DIMS, preferred_element_type=F32)
    kpos = ws + lax.broadcasted_iota(jnp.int32, (rows, slab), 1)
    ok = lax.bitcast_convert_type(t_col - kpos, jnp.uint32) < jnp.uint32(WINDOW)
    s = jnp.where(ok, s, NEG_INF)
    e = jnp.exp2(s - jnp.max(s, axis=-1, keepdims=True))
    acc = jnp.dot(e.astype(BF16), vw_ref[0, 0, pl.ds(ws, slab), :], preferred_element_type=F32)
    o_win = acc[:, :HEAD_DIM] / acc[:, HEAD_DIM:HEAD_DIM + 1]

    for g in range(B_GROUP):
        lhs_ref[g * Q_TILE:(g + 1) * Q_TILE, 2 * HEAD_DIM:] = bias

    def tile(start, carry, causal):
        m_run, acc_run = carry
        kv = ksv_ref[0, 0, pl.ds(start, key_tile), :]
        st = lax.dot_general(lhs_ref[...], kv, NT_DIMS, preferred_element_type=F32)
        if causal:
            kp = start + lax.broadcasted_iota(jnp.int32, (rows, key_tile), 1)
            st = jnp.where(kp <= t_col, st, NEG_INF)
        m_new = jnp.maximum(m_run, jnp.max(st, axis=-1, keepdims=True))
        pt = jnp.exp2(st - m_new)
        pv = jnp.dot(pt.astype(BF16), vs_ref[0, 0, pl.ds(start, key_tile), :], preferred_element_type=F32)
        return m_new, acc_run * jnp.exp2(m_run - m_new) + pv

    def group(gi, carry, causal):
        for u in range(tiles_per_group):
            start = pl.multiple_of((gi * tiles_per_group + u) * key_tile, key_tile)
            carry = tile(start, carry, causal)
        return carry

    n_groups = q0 // (tiles_per_group * key_tile)
    carry = (jnp.full((rows, 1), NEG_INF, F32), jnp.zeros((rows, LANES), F32))
    carry = lax.fori_loop(0, n_groups, functools.partial(group, causal=False), carry)
    _, acc = group(n_groups, carry, True)
    o_slc = acc[:, :HEAD_DIM] / acc[:, HEAD_DIM:HEAD_DIM + 1]

    gates = gate_ref[0]
    for g in range(B_GROUP):
        r = slice(g * Q_TILE, (g + 1) * Q_TILE)
        og = (gates[:, 3 * g:3 * g + 1] * o_cmp[r] + gates[:, 3 * g + 1:3 * g + 2] * o_slc[r]
              + gates[:, 3 * g + 2:3 * g + 3] * o_win[r])
        o_ref[:, g * HEAD_DIM:(g + 1) * HEAD_DIM] = og.astype(o_ref.dtype)


def _nsa_call(q_hm, cmp_aug, wt, gates, kv_aug, v_aug, *, key_tile=512, tiles_per_group=2):
    _, t, _ = q_hm.shape
    n_cmp = cmp_aug.shape[2]
    assert t % (key_tile * tiles_per_group) == 0
    return pl.pallas_call(
        functools.partial(_nsa_kernel, key_tile=key_tile, tiles_per_group=tiles_per_group),
        grid=(B_KV_HEADS, t // Q_TILE),
        in_specs=[pl.BlockSpec((B_GROUP, Q_TILE, HEAD_DIM), lambda h, i: (h, i, 0)),
                  pl.BlockSpec((1, 1, n_cmp, 2 * LANES), lambda h, i: (0, h, 0, 0)),
                  pl.BlockSpec((1, 1, n_cmp, 2 * LANES), lambda h, i: (1, h, 0, 0)),
                  pl.BlockSpec((N_BLK_LANES, n_cmp), lambda h, i: (0, 0)),
                  pl.BlockSpec((1, Q_TILE, LANES), lambda h, i: (h, i, 0)),
                  pl.BlockSpec((1, 1, t, 2 * LANES), lambda h, i: (0, h, 0, 0)),
                  pl.BlockSpec((1, 1, t, 2 * LANES), lambda h, i: (1, h, 0, 0)),
                  pl.BlockSpec((1, 1, t, LANES), lambda h, i: (0, h, 0, 0)),
                  pl.BlockSpec((1, 1, t, LANES), lambda h, i: (1, h, 0, 0))],
        out_specs=pl.BlockSpec((Q_TILE, B_GROUP * HEAD_DIM), lambda h, i: (i, h)),
        out_shape=jax.ShapeDtypeStruct((t, D_B), BF16),
        scratch_shapes=[pltpu.VMEM((B_GROUP * Q_TILE, 2 * LANES), BF16)], name="nsa_attention",
        compiler_params=_cparams(2, VMEM_LIMIT))(q_hm, cmp_aug, cmp_aug, wt, gates, kv_aug, kv_aug, v_aug, v_aug)


def _post_kernel(oa_ref, ob_ref, sg_ref, x_ref, wa_ref, wb_ref, wo_ref, g_ref, o_ref):
    ya = jnp.dot(oa_ref[...], wa_ref[...], preferred_element_type=F32)
    yb = jnp.dot(ob_ref[...], wb_ref[...], preferred_element_type=F32)
    merged = sg_ref[0].astype(F32) * ya + sg_ref[1].astype(F32) * yb
    mix = jnp.dot(merged.astype(BF16), wo_ref[...], preferred_element_type=F32)
    y = mix * lax.rsqrt(jnp.mean(mix * mix, axis=-1, keepdims=True) + EPS)
    o_ref[...] = x_ref[...] + y * g_ref[...]


def _post_call(oa, ob, sg, x2, wa, wb, wo, g, tm=256):
    t, d = x2.shape
    const = dict(pipeline_mode=pl.Buffered(1))
    return pl.pallas_call(
        _post_kernel, grid=(t // tm,),
        in_specs=[pl.BlockSpec((tm, D_A), lambda i: (i, 0)),
                  pl.BlockSpec((tm, D_B), lambda i: (i, 0)),
                  pl.BlockSpec((2, tm, d), lambda i: (0, i, 0)),
                  pl.BlockSpec((tm, d), lambda i: (i, 0)),
                  pl.BlockSpec((D_A, d), lambda i: (0, 0), **const),
                  pl.BlockSpec((D_B, d), lambda i: (0, 0), **const),
                  pl.BlockSpec((d, d), lambda i: (0, 0), **const),
                  pl.BlockSpec((1, d), lambda i: (0, 0))],
        out_specs=pl.BlockSpec((tm, d), lambda i: (i, 0)),
        out_shape=jax.ShapeDtypeStruct((t, d), F32), name="mix_out_residual",
        compiler_params=_cparams(1, VMEM_LIMIT))(oa, ob, sg, x2, wa, wb, wo, g)


def _ffn_kernel(h_ref, g1_ref, wg_ref, wu_ref, wd_ref, g2_ref, o_ref, hn_ref, acc_ref):
    f = pl.program_id(1)

    @pl.when(f == 0)
    def _():
        h = h_ref[...]
        y = h * lax.rsqrt(jnp.mean(h * h, axis=-1, keepdims=True) + EPS)
        hn_ref[...] = (y * g1_ref[...]).astype(BF16)
        acc_ref[...] = jnp.zeros_like(acc_ref)

    hn = hn_ref[...]
    a = jnp.dot(hn, wg_ref[...], preferred_element_type=F32)
    u = jnp.dot(hn, wu_ref[...], preferred_element_type=F32)
    z = (a * jax.nn.sigmoid(a) * u).astype(BF16)
    acc_ref[...] += jnp.dot(z, wd_ref[...], preferred_element_type=F32)

    @pl.when(f == pl.num_programs(1) - 1)
    def _():
        ff = acc_ref[...]
        y = ff * lax.rsqrt(jnp.mean(ff * ff, axis=-1, keepdims=True) + EPS)
        o_ref[...] = h_ref[...] + y * g2_ref[...]


def _ffn_call(h1, g1, wg, wu, wd, g2, tm=512, tf=512):
    t, d = h1.shape
    dff = wg.shape[1]
    return pl.pallas_call(
        _ffn_kernel, grid=(t // tm, dff // tf),
        in_specs=[pl.BlockSpec((tm, d), lambda i, f: (i, 0)),
                  pl.BlockSpec((1, d), lambda i, f: (0, 0)),
                  pl.BlockSpec((d, tf), lambda i, f: (0, f)),
                  pl.BlockSpec((d, tf), lambda i, f: (0, f)),
                  pl.BlockSpec((tf, d), lambda i, f: (f, 0)),
                  pl.BlockSpec((1, d), lambda i, f: (0, 0))],
        out_specs=pl.BlockSpec((tm, d), lambda i, f: (i, 0)),
        out_shape=jax.ShapeDtypeStruct((t, d), F32),
        scratch_shapes=[pltpu.VMEM((tm, d), BF16), pltpu.VMEM((tm, d), F32)], name="swiglu_ffn",
        compiler_params=_cparams(2, VMEM_LIMIT))(h1, g1, wg, wu, wd, g2)


def _rope_tables(pos, period, scale=1.0):
    half = ROPE_DIM // 2
    inv = jnp.float32(ROPE_THETA) ** (-jnp.arange(half, dtype=F32) * 2.0 / ROPE_DIM)
    ang = pos.astype(F32)[:, None] * inv[None, :]
    cos, sin = jnp.cos(ang), jnp.sin(ang)
    r = np.arange(LANES) % period
    f = r % half
    lo = jnp.asarray((r < half)[None, :])
    hi = jnp.asarray(((r >= half) & (r < ROPE_DIM))[None, :])
    c = jnp.where(lo | hi, cos[:, f], 1.0)
    sa = jnp.where(lo, -sin[:, f], 0.0)
    sb = jnp.where(hi, sin[:, f], 0.0)
    return jnp.stack([c, sa, sb]) * scale


def _slc_weight_matrix(n_cmp_pad):
    ratio = SLC_BLOCK // CMP_STRIDE
    w = np.zeros((N_BLK_LANES, n_cmp_pad), np.float32)
    for j in range(N_BLK_LANES):
        for o, wv in enumerate(SLC_WEIGHTS):
            n = ratio * j + o - 1
            if 0 <= n < n_cmp_pad - 1:
                w[j, n] = wv
    return jnp.asarray(w, BF16)


def kernel(x, pre_mix_g, w_in, lb_table, a_norm_g, cmp_pe_k, cmp_w1_k, cmp_b1_k, cmp_w2_k, cmp_b2_k, cmp_pe_v, cmp_w1_v, cmp_b1_v, cmp_w2_v, cmp_b2_v, w_proj_a, w_proj_b, w_out, post_mix_g, pre_ffn_g, w_gate, w_up, w_down, post_ffn_g):
    bsz, t, d = x.shape
    assert bsz == 1 and d == D_MODEL and t % 256 == 0 and WINDOW + Q_TILE <= t <= N_BLK_LANES * SLC_BLOCK
    x2 = x.reshape(t, d)
    n_cmp_pad = t // CMP_STRIDE
    assert n_cmp_pad % LANES == 0

    lb = jnp.cumsum(jax.nn.softmax(lb_table.astype(F32), axis=0), axis=0)[0].reshape(1, D_A)
    w = w_in[0]
    o_q, o_kc, o_ks, o_g, o_m = 4 * D_A, 4 * D_A + D_B, 4 * D_A + D_B + 2 * KV_DIM, 4 * D_A + D_B + 6 * KV_DIM, 4 * D_A + D_B + 6 * KV_DIM + 3 * B_HEADS
    w_hgrn = w[:, :o_q].astype(BF16)
    w_q = w[:, o_q:o_kc].astype(BF16)
    w_cmp = w[:, o_kc:o_ks].astype(BF16)
    w_kv = (w[:, o_ks:o_g].reshape(d, 2, 2, B_KV_HEADS, HEAD_DIM).transpose(0, 1, 3, 2, 4)
            .reshape(d, 4 * KV_DIM).astype(BF16))
    w_gate3 = w[:, o_g:o_m].reshape(d, B_KV_HEADS, 3 * B_GROUP)
    w_g = jnp.pad(w_gate3, ((0, 0), (0, 0), (0, LANES - 3 * B_GROUP))).reshape(d, B_KV_HEADS * LANES).astype(BF16)
    w_m = w[:, o_m:].astype(BF16)

    pos = jnp.arange(t)
    tab_q = _rope_tables(pos, HEAD_DIM, HEAD_DIM ** -0.5 * np.log2(np.e))
    tab_kv = _rope_tables(pos, LANES)
    end_pos = jnp.arange(n_cmp_pad) * CMP_STRIDE + (CMP_BLOCK - 1)
    ident = jnp.stack([jnp.ones((n_cmp_pad, LANES), F32), jnp.zeros((n_cmp_pad, LANES), F32),
                       jnp.zeros((n_cmp_pad, LANES), F32)])
    tab_cmp = jnp.concatenate([_rope_tables(end_pos, LANES), ident], axis=0)

    xn = _rmsnorm_call(x2, pre_mix_g[0].reshape(1, d))

    tm = 1024 if t % 1024 == 0 else 256
    proj4 = _proj_call(
        "proj_hgrn", xn, w_hgrn, tm=tm, tn=512, epilogue=_ep_plain,
        out_shape=jax.ShapeDtypeStruct((4, t, D_A), F32),
        out_specs=pl.BlockSpec((1, tm, 512), lambda i, j: (j // 2, i, j % 2)))
    q_hm = _proj_call(
        "proj_q", xn, w_q, tm=tm, tn=B_GROUP * HEAD_DIM, epilogue=_ep_q,
        out_shape=jax.ShapeDtypeStruct((B_HEADS, t, HEAD_DIM), F32),
        out_specs=pl.BlockSpec((B_GROUP, tm, HEAD_DIM), lambda i, j: (j, i, 0)),
        extra=(tab_q,), extra_specs=(pl.BlockSpec((3, tm, LANES), lambda i, j: (0, i, 0)),))
    cmp_in = _proj_call(
        "proj_cmp", xn, w_cmp, tm=tm, tn=KV_DIM, epilogue=_ep_cmp,
        out_shape=jax.ShapeDtypeStruct((2, B_KV_HEADS, t, HEAD_DIM), F32),
        out_specs=pl.BlockSpec((1, B_KV_HEADS, tm, HEAD_DIM), lambda i, j: (j, 0, i, 0)))
    kv_aug, v_aug = _proj_call(
        "proj_kv", xn, w_kv, tm=tm, tn=2 * KV_DIM, epilogue=functools.partial(_ep_kv, tm=tm),
        out_shape=[jax.ShapeDtypeStruct((2, B_KV_HEADS, t, 2 * LANES), BF16),
                   jax.ShapeDtypeStruct((2, B_KV_HEADS, t, LANES), BF16)],
        out_specs=[pl.BlockSpec((1, B_KV_HEADS, tm, 2 * LANES), lambda i, j: (j, 0, i, 0)),
                   pl.BlockSpec((1, B_KV_HEADS, tm, LANES), lambda i, j: (j, 0, i, 0))],
        extra=(tab_kv,), extra_specs=(pl.BlockSpec((3, tm, LANES), lambda i, j: (0, i, 0)),))
    gates = _proj_call(
        "proj_gate", xn, w_g, tm=tm, tn=B_KV_HEADS * LANES, epilogue=_ep_gate,
        out_shape=jax.ShapeDtypeStruct((B_KV_HEADS, t, LANES), F32),
        out_specs=pl.BlockSpec((B_KV_HEADS, tm, LANES), lambda i, j: (0, i, 0)))
    sg = _proj_call(
        "proj_merge", xn, w_m, tm=tm, tn=512, epilogue=_ep_sigmoid,
        out_shape=jax.ShapeDtypeStruct((2, t, d), BF16),
        out_specs=pl.BlockSpec((1, tm, 512), lambda i, j: (j // 4, i, j % 4)))

    oa = _hgrn_call(proj4, lb, a_norm_g[0].reshape(1, D_A))

    half = CMP_BLOCK // 2
    x16 = cmp_in.reshape(2, B_KV_HEADS, n_cmp_pad, CMP_STRIDE * HEAD_DIM)
    pe2 = jnp.stack([cmp_pe_k[0], cmp_pe_v[0]]).reshape(2, 2, half * HEAD_DIM)
    w1 = jnp.stack([cmp_w1_k[0], cmp_w1_v[0]]).astype(BF16)
    b1 = jnp.stack([cmp_b1_k[0], cmp_b1_v[0]]).reshape(2, 1, CMP_HIDDEN)
    w2p = jnp.pad(jnp.stack([cmp_w2_k[0], cmp_w2_v[0]]), ((0, 0), (0, 0), (0, LANES - HEAD_DIM))).astype(BF16)
    b2p = jnp.pad(jnp.stack([cmp_b2_k[0], cmp_b2_v[0]]), ((0, 0), (0, LANES - HEAD_DIM))).reshape(2, 1, LANES)
    cmp_aug = _compress_call(x16, pe2, w1, b1, w2p, b2p, tab_cmp)
    ob = _nsa_call(q_hm, cmp_aug, _slc_weight_matrix(n_cmp_pad), gates, kv_aug, v_aug)

    h1 = _post_call(oa, ob, sg, x2, w_proj_a[0].astype(BF16), w_proj_b[0].astype(BF16),
                    w_out[0].astype(BF16), post_mix_g[0].reshape(1, d))
    out = _ffn_call(h1, pre_ffn_g[0].reshape(1, d), w_gate[0].astype(BF16), w_up[0].astype(BF16),
                    w_down[0].astype(BF16), post_ffn_g[0].reshape(1, d))
    return out.reshape(bsz, t, d)
```

```python
import functools

import numpy as np
import jax
import jax.numpy as jnp
from jax import lax
from jax.experimental import pallas as pl
from jax.experimental.pallas import tpu as pltpu

F32 = jnp.float32
BF16 = jnp.bfloat16

D_MODEL = 2048
D_A = 1024
D_B = 1024
A_HEADS = 8
A_DK = 128
B_HEADS = 16
B_KV_HEADS = 4
B_GROUP = 4
HEAD_DIM = 64
KV_DIM = 256
CMP_BLOCK = 32
CMP_STRIDE = 16
CMP_HIDDEN = 256
SLC_BLOCK = 64
SLC_TOP_N = 16
SLC_FORCED = 3
SLC_WEIGHTS = (1.0, 2.0, 2.0, 2.0, 1.0)
WINDOW = 512
Q_TILE = 256
KV_PER_SELECT_STEP = 2
ROPE_THETA = 500000.0
ROPE_DIM = 16
D_FF = 5632
EPS = 1e-6
NEG_INF = -1e30
N_BLK_LANES = 128
LANES = 128
VMEM_LIMIT = 56 * 1024 * 1024

NT_DIMS = (((1,), (1,)), ((), ()))
TN_DIMS = (((0,), (0,)), ((), ()))


def _cparams(n_axes, vmem=None):
    return pltpu.CompilerParams(dimension_semantics=("arbitrary",) * n_axes,
                                vmem_limit_bytes=vmem)


def _rmsnorm_kernel(x_ref, g_ref, o_ref):
    x = x_ref[...]
    y = x * lax.rsqrt(jnp.mean(x * x, axis=-1, keepdims=True) + EPS)
    o_ref[...] = (y * g_ref[...]).astype(o_ref.dtype)


def _rmsnorm_call(x2, g, tm=512):
    t, d = x2.shape
    return pl.pallas_call(
        _rmsnorm_kernel, grid=(t // tm,),
        in_specs=[pl.BlockSpec((tm, d), lambda i: (i, 0)), pl.BlockSpec((1, d), lambda i: (0, 0))],
        out_specs=pl.BlockSpec((tm, d), lambda i: (i, 0)),
        out_shape=jax.ShapeDtypeStruct((t, d), BF16), name="pre_mix_rmsnorm",
        compiler_params=_cparams(1))(x2, g)


def _proj_call(name, xn, w, *, tm, tn, epilogue, out_shape, out_specs, extra=(), extra_specs=()):
    t, k = xn.shape
    n = w.shape[1]
    n_extra = len(extra)

    def body(x_ref, w_ref, *rest):
        acc = jnp.dot(x_ref[...], w_ref[...], preferred_element_type=F32)
        epilogue(acc, rest[:n_extra], rest[n_extra:])

    return pl.pallas_call(
        body, grid=(t // tm, n // tn),
        in_specs=[pl.BlockSpec((tm, k), lambda i, j: (i, 0)),
                  pl.BlockSpec((k, tn), lambda i, j: (0, j)), *extra_specs],
        out_specs=out_specs, out_shape=out_shape, name=name,
        compiler_params=_cparams(2, VMEM_LIMIT))(xn, w, *extra)


def _rope_tile(a, tab_ref):
    return (a * tab_ref[0] + pltpu.roll(a, LANES - ROPE_DIM // 2, 1) * tab_ref[1]
            + pltpu.roll(a, ROPE_DIM // 2, 1) * tab_ref[2])


def _ep_plain(acc, extra, outs):
    outs[0][0] = acc.astype(outs[0].dtype)


def _ep_q(acc, extra, outs):
    for pair in range(acc.shape[1] // LANES):
        r = _rope_tile(acc[:, pair * LANES:(pair + 1) * LANES], extra[0])
        outs[0][2 * pair] = r[:, :HEAD_DIM]
        outs[0][2 * pair + 1] = r[:, HEAD_DIM:]


def _ep_cmp(acc, extra, outs):
    for kv in range(2):
        for h in range(B_KV_HEADS):
            c0 = kv * KV_DIM + h * HEAD_DIM
            outs[0][kv, h] = acc[:, c0:c0 + HEAD_DIM]


def _ep_kv(acc, extra, outs, *, tm):
    i = pl.program_id(0)
    j = pl.program_id(1)
    rowg = i * tm + lax.broadcasted_iota(jnp.int32, (tm, LANES), 0)
    lane = lax.broadcasted_iota(jnp.int32, (tm, LANES), 1)
    aux = jnp.where(j == 0, (rowg // SLC_BLOCK == lane).astype(F32), 0.0).astype(BF16)
    ones_col = (lane == HEAD_DIM).astype(F32)
    for h in range(B_KV_HEADS):
        r = _rope_tile(acc[:, h * LANES:(h + 1) * LANES], extra[0])
        outs[0][0, h, :, 0:LANES] = r.astype(BF16)
        outs[0][0, h, :, LANES:2 * LANES] = aux
        outs[1][0, h] = jnp.where(lane < HEAD_DIM, pltpu.roll(r, HEAD_DIM, 1), ones_col).astype(BF16)


def _ep_gate(acc, extra, outs):
    for h in range(B_KV_HEADS):
        outs[0][h] = jax.nn.sigmoid(acc[:, h * LANES:(h + 1) * LANES])


def _ep_sigmoid(acc, extra, outs):
    outs[0][0] = jax.nn.sigmoid(acc).astype(outs[0].dtype)


def _split3(x):
    hi = x.astype(BF16)
    r1 = x - hi.astype(F32)
    mid = r1.astype(BF16)
    lo = (r1 - mid.astype(F32)).astype(BF16)
    return hi, mid, lo


def _hgrn_kernel(q_ref, f_ref, i_ref, g_ref, lb_ref, gn_ref, o_ref, st_ref, *, chunk, heads):
    c = pl.program_id(1)

    @pl.when(c == 0)
    def _():
        st_ref[...] = jnp.zeros_like(st_ref)

    row = lax.broadcasted_iota(jnp.int32, (chunk, chunk), 0)
    col = lax.broadcasted_iota(jnp.int32, (chunk, chunk), 1)
    tri = (col <= row).astype(BF16)
    rowv = lax.broadcasted_iota(jnp.int32, (chunk, A_DK), 0)

    for hb in range(heads):
        sl = slice(hb * A_DK, (hb + 1) * A_DK)
        q = q_ref[0, :, sl]
        ii = i_ref[0, :, sl]
        gg = g_ref[0, :, sl]
        lbv = lb_ref[:, sl]
        f = lbv + (1.0 - lbv) * jax.nn.sigmoid(f_ref[0, :, sl])
        lf = jnp.log(f)
        k = 1.0 - f
        b = sum(jnp.dot(tri, p, preferred_element_type=F32) for p in _split3(lf))
        b_end = b[chunk - 1:chunk, :]
        ii16 = ii.astype(BF16)

        st = st_ref[hb]
        o = lax.dot_general((q * jnp.exp(b)).astype(BF16), st.astype(BF16), NT_DIMS,
                            preferred_element_type=F32)
        kd = (k * jnp.exp(b_end - b)).astype(BF16)
        st_ref[hb] = st * jnp.exp(b_end) + lax.dot_general(ii16, kd, TN_DIMS,
                                                           preferred_element_type=F32)

        att = jnp.zeros((chunk, chunk), F32)
        m = 8
        while m < chunk:
            grp = chunk // (2 * m)
            b3 = b.reshape(grp, 2 * m, A_DK)
            refrow = jnp.broadcast_to(b3[:, m - 1:m, :], (grp, 2 * m, A_DK)).reshape(chunk, A_DK)
            second = (rowv % (2 * m)) >= m
            ql = jnp.where(second, q * jnp.exp(jnp.where(second, b - refrow, 0.0)), 0.0)
            kl = jnp.where(second, 0.0, k * jnp.exp(jnp.where(second, 0.0, refrow - b)))
            a = lax.dot_general(ql.astype(BF16), kl.astype(BF16), NT_DIMS, preferred_element_type=F32)
            if grp > 1:
                a = jnp.where(row // (2 * m) == col // (2 * m), a, 0.0)
            att = att + a
            m *= 2
        o = o + jnp.dot(att.astype(BF16), ii16, preferred_element_type=F32)

        for d in range(8):
            if d == 0:
                p = q * k
                i_d = ii
            else:
                valid = (rowv % 8) >= d
                w = jnp.exp(jnp.where(valid, b - pltpu.roll(b, d, 0), 0.0))
                p = jnp.where(valid, q * pltpu.roll(k, d, 0) * w, 0.0)
                i_d = pltpu.roll(ii, d, 0)
            o = o + jnp.sum(p, axis=-1, keepdims=True) * i_d

        o = o * lax.rsqrt(jnp.mean(o * o, axis=-1, keepdims=True) + EPS)
        o_ref[:, sl] = ((o * gn_ref[:, sl]) * (gg * jax.nn.sigmoid(gg))).astype(o_ref.dtype)


def _hgrn_call(proj4, lb, gn, *, chunk=64, heads=8):
    _, t, _ = proj4.shape
    w = heads * A_DK

    def spec(kind):
        return pl.BlockSpec((1, chunk, w), lambda h, c, kind=kind: (kind, c, h))

    vec = pl.BlockSpec((1, w), lambda h, c: (0, h))
    return pl.pallas_call(
        functools.partial(_hgrn_kernel, chunk=chunk, heads=heads),
        grid=(A_HEADS // heads, t // chunk),
        in_specs=[spec(0), spec(1), spec(2), spec(3), vec, vec],
        out_specs=pl.BlockSpec((chunk, w), lambda h, c: (c, h)),
        out_shape=jax.ShapeDtypeStruct((t, D_A), BF16),
        scratch_shapes=[pltpu.VMEM((heads, A_DK, A_DK), F32)], name="hgrn2_scan",
        compiler_params=_cparams(2))(proj4, proj4, proj4, proj4, lb, gn)


def _compress_kernel(x_ref, pe_ref, w1_ref, b1_ref, w2_ref, b2_ref, tab_ref, o_ref):
    half = CMP_BLOCK * HEAD_DIM // 2
    x = x_ref[0, 0]
    top = jnp.dot((x + pe_ref[0, 0:1, :]).astype(BF16), w1_ref[0, :half, :], preferred_element_type=F32)
    bot = jnp.dot((x + pe_ref[0, 1:2, :]).astype(BF16), w1_ref[0, half:, :], preferred_element_type=F32)
    n = x.shape[0]
    h = jax.nn.gelu(top + pltpu.roll(bot, n - 1, 0) + b1_ref[0])
    y = jnp.dot(h.astype(BF16), w2_ref[0], preferred_element_type=F32) + b2_ref[0]
    y = _rope_tile(y, tab_ref)
    hi = y.astype(BF16)
    lo = (y - hi.astype(F32)).astype(BF16)
    o_ref[0, 0, :, 0:LANES] = (y + pltpu.roll(y, HEAD_DIM, 1)).astype(BF16)
    o_ref[0, 0, :, LANES:2 * LANES] = lo


def _compress_call(x16, pe2, w1, b1, w2p, b2p, tab):
    _, nh, n, wid = x16.shape
    hid = w1.shape[-1]
    return pl.pallas_call(
        _compress_kernel, grid=(2, nh),
        in_specs=[pl.BlockSpec((1, 1, n, wid), lambda kv, h: (kv, h, 0, 0)),
                  pl.BlockSpec((1, 2, wid), lambda kv, h: (kv, 0, 0)),
                  pl.BlockSpec((1, 2 * wid, hid), lambda kv, h: (kv, 0, 0)),
                  pl.BlockSpec((1, 1, hid), lambda kv, h: (kv, 0, 0)),
                  pl.BlockSpec((1, hid, LANES), lambda kv, h: (kv, 0, 0)),
                  pl.BlockSpec((1, 1, LANES), lambda kv, h: (kv, 0, 0)),
                  pl.BlockSpec((3, n, LANES), lambda kv, h: (kv, 0, 0))],
        out_specs=pl.BlockSpec((1, 1, n, 2 * LANES), lambda kv, h: (kv, h, 0, 0)),
        out_shape=jax.ShapeDtypeStruct((2, nh, n, 2 * LANES), BF16),
        name="compress_mlp", compiler_params=_cparams(2, VMEM_LIMIT))(x16, pe2, w1, b1, w2p, b2p, tab)


def _nsa_select_kernel(q_ref, kc_ref, vc_ref, wt_ref, ocmp_ref, bias_ref, lhs_ref):
    q0 = pl.program_id(1) * Q_TILE
    rows = B_GROUP * Q_TILE
    n_cmp = kc_ref.shape[2]
    t_col = q0 + lax.broadcasted_iota(jnp.int32, (rows, 1), 0) % Q_TILE
    n_idx = lax.broadcasted_iota(jnp.int32, (rows, n_cmp), 1)
    vis = n_idx <= (t_col - (CMP_BLOCK - 1)) // CMP_STRIDE
    blk = lax.broadcasted_iota(jnp.int32, (N_BLK_LANES, Q_TILE), 0)
    tok = q0 + lax.broadcasted_iota(jnp.int32, (N_BLK_LANES, Q_TILE), 1)
    cur = tok // SLC_BLOCK
    forced = (blk == 0) | (blk == cur) | (blk == cur - 1)
    candidate = (blk * SLC_BLOCK <= tok) & jnp.logical_not(forced)
    wt = wt_ref[...]
    zeros64 = jnp.zeros((rows, HEAD_DIM), BF16)

    for c in range(KV_PER_SELECT_STEP):
        q = q_ref[c * B_GROUP:(c + 1) * B_GROUP].reshape(rows, HEAD_DIM)
        qhi = q.astype(BF16)
        qlo = (q - qhi.astype(F32)).astype(BF16)
        lhs_ref[c, :, 0:HEAD_DIM] = qhi
        lhs_ref[c, :, HEAD_DIM:2 * HEAD_DIM] = qlo
        lhs_ref[c, :, 2 * HEAD_DIM:3 * HEAD_DIM] = qhi
        lhs_ref[c, :, 3 * HEAD_DIM:] = zeros64
        s = lax.dot_general(lhs_ref[c], kc_ref[0, c], NT_DIMS, preferred_element_type=F32)
        s = jnp.where(vis, s, NEG_INF)
        e = jnp.where(vis, jnp.exp2(s - jnp.max(s, axis=-1, keepdims=True)), 0.0)
        l = jnp.sum(e, axis=-1, keepdims=True)
        p = e * (1.0 / jnp.where(l > 0.0, l, 1.0))
        o_cmp = jnp.dot(p.astype(BF16), vc_ref[0, c, :, 0:LANES], preferred_element_type=F32)
        for g in range(B_GROUP):
            c0 = (c * B_GROUP + g) * HEAD_DIM
            ocmp_ref[:, c0:c0 + HEAD_DIM] = o_cmp[g * Q_TILE:(g + 1) * Q_TILE, :HEAD_DIM]

        psum = p[0:Q_TILE] + p[Q_TILE:2 * Q_TILE] + p[2 * Q_TILE:3 * Q_TILE] + p[3 * Q_TILE:]
        ps_hi = psum.astype(BF16)
        ps_lo = (psum - ps_hi.astype(F32)).astype(BF16)
        pslc = (lax.dot_general(wt, ps_hi, NT_DIMS, preferred_element_type=F32)
                + lax.dot_general(wt, ps_lo, NT_DIMS, preferred_element_type=F32))
        score = jnp.where(candidate, pslc, jnp.where(forced, -jnp.inf, NEG_INF))
        sel_t = forced.astype(F32)
        for _ in range(SLC_TOP_N - SLC_FORCED):
            best = jnp.max(score, axis=0, keepdims=True)
            first = jnp.min(jnp.where(score == best, blk, N_BLK_LANES), axis=0, keepdims=True)
            hit = blk == first
            score = jnp.where(hit, -jnp.inf, score)
            sel_t = jnp.where(hit, 1.0, sel_t)
        bias_ref[c] = jnp.where(sel_t.T > 0.0, 0.0, NEG_INF).astype(BF16)


def _nsa_select_call(q_hm, cmp_aug, wt):
    _, t, _ = q_hm.shape
    n_cmp = cmp_aug.shape[2]
    kvs = KV_PER_SELECT_STEP
    return pl.pallas_call(
        _nsa_select_kernel, grid=(B_KV_HEADS // kvs, t // Q_TILE),
        in_specs=[pl.BlockSpec((kvs * B_GROUP, Q_TILE, HEAD_DIM), lambda p, i: (p, i, 0)),
                  pl.BlockSpec((1, kvs, n_cmp, 2 * LANES), lambda p, i: (0, p, 0, 0)),
                  pl.BlockSpec((1, kvs, n_cmp, 2 * LANES), lambda p, i: (1, p, 0, 0)),
                  pl.BlockSpec((N_BLK_LANES, n_cmp), lambda p, i: (0, 0))],
        out_specs=[pl.BlockSpec((Q_TILE, kvs * B_GROUP * HEAD_DIM), lambda p, i: (i, p)),
                   pl.BlockSpec((kvs, Q_TILE, N_BLK_LANES), lambda p, i: (p, i, 0))],
        out_shape=[jax.ShapeDtypeStruct((t, D_B), F32),
                   jax.ShapeDtypeStruct((B_KV_HEADS, t, N_BLK_LANES), BF16)],
        scratch_shapes=[pltpu.VMEM((kvs, B_GROUP * Q_TILE, 2 * LANES), BF16)], name="nsa_select",
        compiler_params=_cparams(2, VMEM_LIMIT))(q_hm, cmp_aug, cmp_aug, wt)


def _nsa_attend_kernel(q_ref, bias_ref, ocmp_ref, gate_ref, ksv_ref, kwv_ref, vs_ref, vw_ref, o_ref,
                       lhs_w_ref, lhs_s_ref, *, key_tile, tiles_per_group):
    q0 = pl.program_id(1) * Q_TILE
    rows = B_GROUP * Q_TILE
    qhi = q_ref[...].reshape(rows, HEAD_DIM).astype(BF16)
    t_col = q0 + lax.broadcasted_iota(jnp.int32, (rows, 1), 0) % Q_TILE

    lhs_s_ref[:, 0:HEAD_DIM] = qhi
    lhs_s_ref[:, HEAD_DIM:LANES] = jnp.zeros((rows, LANES - HEAD_DIM), BF16)
    for g in range(B_GROUP):
        lhs_s_ref[g * Q_TILE:(g + 1) * Q_TILE, LANES:] = bias_ref[0]
    lhs_w_ref[:, 0:HEAD_DIM] = qhi
    lhs_w_ref[:, HEAD_DIM:] = jnp.zeros((rows, 2 * LANES - HEAD_DIM), BF16)

    def tile(ti, carry, causal):
        m_run, acc_run = carry
        start = pl.multiple_of(ti * key_tile, key_tile)
        st = lax.dot_general(lhs_s_ref[...], ksv_ref[0, 0, pl.ds(start, key_tile), :], NT_DIMS,
                             preferred_element_type=F32)
        if causal:
            kp = start + lax.broadcasted_iota(jnp.int32, (rows, key_tile), 1)
            st = jnp.where(kp <= t_col, st, NEG_INF)
        m_new = jnp.maximum(m_run, jnp.max(st, axis=-1, keepdims=True))
        pt = jnp.exp2(st - m_new)
        pv = jnp.dot(pt.astype(BF16), vs_ref[0, 0, pl.ds(start, key_tile), :], preferred_element_type=F32)
        return m_new, acc_run * jnp.exp2(m_run - m_new) + pv

    def group(gi, carry):
        for u in range(tiles_per_group):
            carry = tile(gi * tiles_per_group + u, carry, False)
        return carry

    n_full = q0 // key_tile
    n_groups = n_full // tiles_per_group
    carry = (jnp.full((rows, 1), NEG_INF, F32), jnp.zeros((rows, LANES), F32))
    carry = lax.fori_loop(0, n_groups, group, carry)
    carry = lax.fori_loop(n_groups * tiles_per_group, n_full, functools.partial(tile, causal=False), carry)
    _, acc = tile(n_full, carry, True)
    o_slc = acc[:, :HEAD_DIM] * (1.0 / acc[:, HEAD_DIM:HEAD_DIM + 1])

    slab = WINDOW + Q_TILE
    ws = pl.multiple_of(jnp.maximum(q0 - WINDOW, 0), Q_TILE)
    s = lax.dot_general(lhs_w_ref[...], kwv_ref[0, 0, pl.ds(ws, slab), :], NT_DIMS,
                        preferred_element_type=F32)
    kpos = ws + lax.broadcasted_iota(jnp.int32, (rows, slab), 1)
    ok = lax.bitcast_convert_type(t_col - kpos, jnp.uint32) < jnp.uint32(WINDOW)
    s = jnp.where(ok, s, NEG_INF)
    e = jnp.exp2(s - jnp.max(s, axis=-1, keepdims=True))
    acc = jnp.dot(e.astype(BF16), vw_ref[0, 0, pl.ds(ws, slab), :], preferred_element_type=F32)
    o_win = acc[:, :HEAD_DIM] * (1.0 / acc[:, HEAD_DIM:HEAD_DIM + 1])

    gates = gate_ref[0]
    for g in range(B_GROUP):
        r = slice(g * Q_TILE, (g + 1) * Q_TILE)
        og = (gates[:, 3 * g:3 * g + 1] * ocmp_ref[:, g * HEAD_DIM:(g + 1) * HEAD_DIM]
              + gates[:, 3 * g + 1:3 * g + 2] * o_slc[r] + gates[:, 3 * g + 2:3 * g + 3] * o_win[r])
        o_ref[:, g * HEAD_DIM:(g + 1) * HEAD_DIM] = og.astype(o_ref.dtype)


def _nsa_attend_call(q_hm, bias, ocmp, gates, kv_aug, v_aug, *, key_tile=512, tiles_per_group=2):
    _, t, _ = q_hm.shape
    assert t % (key_tile * tiles_per_group) == 0
    rows = B_GROUP * Q_TILE
    return pl.pallas_call(
        functools.partial(_nsa_attend_kernel, key_tile=key_tile, tiles_per_group=tiles_per_group),
        grid=(B_KV_HEADS, t // Q_TILE),
        in_specs=[pl.BlockSpec((B_GROUP, Q_TILE, HEAD_DIM), lambda h, i: (h, i, 0)),
                  pl.BlockSpec((1, Q_TILE, N_BLK_LANES), lambda h, i: (h, i, 0)),
                  pl.BlockSpec((Q_TILE, B_GROUP * HEAD_DIM), lambda h, i: (i, h)),
                  pl.BlockSpec((1, Q_TILE, LANES), lambda h, i: (h, i, 0)),
                  pl.BlockSpec((1, 1, t, 2 * LANES), lambda h, i: (0, h, 0, 0)),
                  pl.BlockSpec((1, 1, t, 2 * LANES), lambda h, i: (1, h, 0, 0)),
                  pl.BlockSpec((1, 1, t, LANES), lambda h, i: (0, h, 0, 0)),
                  pl.BlockSpec((1, 1, t, LANES), lambda h, i: (1, h, 0, 0))],
        out_specs=pl.BlockSpec((Q_TILE, B_GROUP * HEAD_DIM), lambda h, i: (i, h)),
        out_shape=jax.ShapeDtypeStruct((t, D_B), BF16),
        scratch_shapes=[pltpu.VMEM((rows, 2 * LANES), BF16), pltpu.VMEM((rows, 2 * LANES), BF16)],
        name="nsa_attend",
        compiler_params=_cparams(2, VMEM_LIMIT))(q_hm, bias, ocmp, gates, kv_aug, kv_aug, v_aug, v_aug)


def _post_kernel(oa_ref, ob_ref, sg_ref, x_ref, wa_ref, wb_ref, wo_ref, g_ref, o_ref):
    ya = jnp.dot(oa_ref[...], wa_ref[...], preferred_element_type=F32)
    yb = jnp.dot(ob_ref[...], wb_ref[...], preferred_element_type=F32)
    merged = sg_ref[0].astype(F32) * ya + sg_ref[1].astype(F32) * yb
    mix = jnp.dot(merged.astype(BF16), wo_ref[...], preferred_element_type=F32)
    y = mix * lax.rsqrt(jnp.mean(mix * mix, axis=-1, keepdims=True) + EPS)
    o_ref[...] = x_ref[...] + y * g_ref[...]


def _post_call(oa, ob, sg, x2, wa, wb, wo, g, tm=256):
    t, d = x2.shape
    const = dict(pipeline_mode=pl.Buffered(1))
    return pl.pallas_call(
        _post_kernel, grid=(t // tm,),
        in_specs=[pl.BlockSpec((tm, D_A), lambda i: (i, 0)),
                  pl.BlockSpec((tm, D_B), lambda i: (i, 0)),
                  pl.BlockSpec((2, tm, d), lambda i: (0, i, 0)),
                  pl.BlockSpec((tm, d), lambda i: (i, 0)),
                  pl.BlockSpec((D_A, d), lambda i: (0, 0), **const),
                  pl.BlockSpec((D_B, d), lambda i: (0, 0), **const),
                  pl.BlockSpec((d, d), lambda i: (0, 0), **const),
                  pl.BlockSpec((1, d), lambda i: (0, 0))],
        out_specs=pl.BlockSpec((tm, d), lambda i: (i, 0)),
        out_shape=jax.ShapeDtypeStruct((t, d), F32), name="mix_out_residual",
        compiler_params=_cparams(1, VMEM_LIMIT))(oa, ob, sg, x2, wa, wb, wo, g)


def _ffn_kernel(h_ref, g1_ref, wg_ref, wu_ref, wd_ref, g2_ref, o_ref, hn_ref, acc_ref):
    f = pl.program_id(1)

    @pl.when(f == 0)
    def _():
        h = h_ref[...]
        y = h * lax.rsqrt(jnp.mean(h * h, axis=-1, keepdims=True) + EPS)
        hn_ref[...] = (y * g1_ref[...]).astype(BF16)
        acc_ref[...] = jnp.zeros_like(acc_ref)

    hn = hn_ref[...]
    a = jnp.dot(hn, wg_ref[...], preferred_element_type=F32)
    u = jnp.dot(hn, wu_ref[...], preferred_element_type=F32)
    z = (a * jax.nn.sigmoid(a) * u).astype(BF16)
    acc_ref[...] += jnp.dot(z, wd_ref[...], preferred_element_type=F32)

    @pl.when(f == pl.num_programs(1) - 1)
    def _():
        ff = acc_ref[...]
        y = ff * lax.rsqrt(jnp.mean(ff * ff, axis=-1, keepdims=True) + EPS)
        o_ref[...] = h_ref[...] + y * g2_ref[...]


def _ffn_call(h1, g1, wg, wu, wd, g2, tm=512, tf=512):
    t, d = h1.shape
    dff = wg.shape[1]
    return pl.pallas_call(
        _ffn_kernel, grid=(t // tm, dff // tf),
        in_specs=[pl.BlockSpec((tm, d), lambda i, f: (i, 0)),
                  pl.BlockSpec((1, d), lambda i, f: (0, 0)),
                  pl.BlockSpec((d, tf), lambda i, f: (0, f)),
                  pl.BlockSpec((d, tf), lambda i, f: (0, f)),
                  pl.BlockSpec((tf, d), lambda i, f: (f, 0)),
                  pl.BlockSpec((1, d), lambda i, f: (0, 0))],
        out_specs=pl.BlockSpec((tm, d), lambda i, f: (i, 0)),
        out_shape=jax.ShapeDtypeStruct((t, d), F32),
        scratch_shapes=[pltpu.VMEM((tm, d), BF16), pltpu.VMEM((tm, d), F32)], name="swiglu_ffn",
        compiler_params=_cparams(2, VMEM_LIMIT))(h1, g1, wg, wu, wd, g2)


def _rope_tables(pos, period, scale=1.0):
    half = ROPE_DIM // 2
    inv = jnp.float32(ROPE_THETA) ** (-jnp.arange(half, dtype=F32) * 2.0 / ROPE_DIM)
    ang = pos.astype(F32)[:, None] * inv[None, :]
    cos, sin = jnp.cos(ang), jnp.sin(ang)
    r = np.arange(LANES) % period
    f = r % half
    lo = jnp.asarray((r < half)[None, :])
    hi = jnp.asarray(((r >= half) & (r < ROPE_DIM))[None, :])
    c = jnp.where(lo | hi, cos[:, f], 1.0)
    sa = jnp.where(lo, -sin[:, f], 0.0)
    sb = jnp.where(hi, sin[:, f], 0.0)
    return jnp.stack([c, sa, sb]) * scale


def _slc_weight_matrix(n_cmp_pad):
    ratio = SLC_BLOCK // CMP_STRIDE
    w = np.zeros((N_BLK_LANES, n_cmp_pad), np.float32)
    for j in range(N_BLK_LANES):
        for o, wv in enumerate(SLC_WEIGHTS):
            n = ratio * j + o - 1
            if 0 <= n < n_cmp_pad - 1:
                w[j, n] = wv
    return jnp.asarray(w, BF16)


def kernel(x, pre_mix_g, w_in, lb_table, a_norm_g, cmp_pe_k, cmp_w1_k, cmp_b1_k, cmp_w2_k, cmp_b2_k, cmp_pe_v, cmp_w1_v, cmp_b1_v, cmp_w2_v, cmp_b2_v, w_proj_a, w_proj_b, w_out, post_mix_g, pre_ffn_g, w_gate, w_up, w_down, post_ffn_g):
    bsz, t, d = x.shape
    assert bsz == 1 and d == D_MODEL and WINDOW + Q_TILE <= t <= N_BLK_LANES * SLC_BLOCK
    x2 = x.reshape(t, d)
    n_cmp_pad = t // CMP_STRIDE
    assert n_cmp_pad % LANES == 0

    lb = jnp.cumsum(jax.nn.softmax(lb_table.astype(F32), axis=0), axis=0)[0].reshape(1, D_A)
    w = w_in[0]
    o_q, o_kc, o_ks, o_g, o_m = 4 * D_A, 4 * D_A + D_B, 4 * D_A + D_B + 2 * KV_DIM, 4 * D_A + D_B + 6 * KV_DIM, 4 * D_A + D_B + 6 * KV_DIM + 3 * B_HEADS
    w_hgrn = w[:, :o_q].astype(BF16)
    w_q = w[:, o_q:o_kc].astype(BF16)
    w_cmp = w[:, o_kc:o_ks].astype(BF16)
    w_kv = (w[:, o_ks:o_g].reshape(d, 2, 2, B_KV_HEADS, HEAD_DIM).transpose(0, 1, 3, 2, 4)
            .reshape(d, 4 * KV_DIM).astype(BF16))
    w_gate3 = w[:, o_g:o_m].reshape(d, B_KV_HEADS, 3 * B_GROUP)
    w_g = jnp.pad(w_gate3, ((0, 0), (0, 0), (0, LANES - 3 * B_GROUP))).reshape(d, B_KV_HEADS * LANES).astype(BF16)
    w_m = w[:, o_m:].astype(BF16)

    pos = jnp.arange(t)
    tab_q = _rope_tables(pos, HEAD_DIM, HEAD_DIM ** -0.5 * np.log2(np.e))
    tab_kv = _rope_tables(pos, LANES)
    end_pos = jnp.arange(n_cmp_pad) * CMP_STRIDE + (CMP_BLOCK - 1)
    ident = jnp.stack([jnp.ones((n_cmp_pad, LANES), F32), jnp.zeros((n_cmp_pad, LANES), F32),
                       jnp.zeros((n_cmp_pad, LANES), F32)])
    tab_cmp = jnp.concatenate([_rope_tables(end_pos, LANES), ident], axis=0)

    xn = _rmsnorm_call(x2, pre_mix_g[0].reshape(1, d))

    tm = 1024 if t % 1024 == 0 else 256
    proj4 = _proj_call(
        "proj_hgrn", xn, w_hgrn, tm=tm, tn=512, epilogue=_ep_plain,
        out_shape=jax.ShapeDtypeStruct((4, t, D_A), F32),
        out_specs=pl.BlockSpec((1, tm, 512), lambda i, j: (j // 2, i, j % 2)))
    q_hm = _proj_call(
        "proj_q", xn, w_q, tm=tm, tn=512, epilogue=_ep_q,
        out_shape=jax.ShapeDtypeStruct((B_HEADS, t, HEAD_DIM), F32),
        out_specs=pl.BlockSpec((512 // HEAD_DIM, tm, HEAD_DIM), lambda i, j: (j, i, 0)),
        extra=(tab_q,), extra_specs=(pl.BlockSpec((3, tm, LANES), lambda i, j: (0, i, 0)),))
    cmp_in = _proj_call(
        "proj_cmp", xn, w_cmp, tm=tm, tn=2 * KV_DIM, epilogue=_ep_cmp,
        out_shape=jax.ShapeDtypeStruct((2, B_KV_HEADS, t, HEAD_DIM), F32),
        out_specs=pl.BlockSpec((2, B_KV_HEADS, tm, HEAD_DIM), lambda i, j: (0, 0, i, 0)))
    kv_aug, v_aug = _proj_call(
        "proj_kv", xn, w_kv, tm=tm, tn=2 * KV_DIM, epilogue=functools.partial(_ep_kv, tm=tm),
        out_shape=[jax.ShapeDtypeStruct((2, B_KV_HEADS, t, 2 * LANES), BF16),
                   jax.ShapeDtypeStruct((2, B_KV_HEADS, t, LANES), BF16)],
        out_specs=[pl.BlockSpec((1, B_KV_HEADS, tm, 2 * LANES), lambda i, j: (j, 0, i, 0)),
                   pl.BlockSpec((1, B_KV_HEADS, tm, LANES), lambda i, j: (j, 0, i, 0))],
        extra=(tab_kv,), extra_specs=(pl.BlockSpec((3, tm, LANES), lambda i, j: (0, i, 0)),))
    gates = _proj_call(
        "proj_gate", xn, w_g, tm=tm, tn=B_KV_HEADS * LANES, epilogue=_ep_gate,
        out_shape=jax.ShapeDtypeStruct((B_KV_HEADS, t, LANES), F32),
        out_specs=pl.BlockSpec((B_KV_HEADS, tm, LANES), lambda i, j: (0, i, 0)))
    sg = _proj_call(
        "proj_merge", xn, w_m, tm=tm, tn=512, epilogue=_ep_sigmoid,
        out_shape=jax.ShapeDtypeStruct((2, t, d), BF16),
        out_specs=pl.BlockSpec((1, tm, 512), lambda i, j: (j // 4, i, j % 4)))

    oa = _hgrn_call(proj4, lb, a_norm_g[0].reshape(1, D_A))

    half = CMP_BLOCK // 2
    x16 = cmp_in.reshape(2, B_KV_HEADS, n_cmp_pad, CMP_STRIDE * HEAD_DIM)
    pe2 = jnp.stack([cmp_pe_k[0], cmp_pe_v[0]]).reshape(2, 2, half * HEAD_DIM)
    w1 = jnp.stack([cmp_w1_k[0], cmp_w1_v[0]]).astype(BF16)
    b1 = jnp.stack([cmp_b1_k[0], cmp_b1_v[0]]).reshape(2, 1, CMP_HIDDEN)
    w2p = jnp.pad(jnp.stack([cmp_w2_k[0], cmp_w2_v[0]]), ((0, 0), (0, 0), (0, LANES - HEAD_DIM))).astype(BF16)
    b2p = jnp.pad(jnp.stack([cmp_b2_k[0], cmp_b2_v[0]]), ((0, 0), (0, LANES - HEAD_DIM))).reshape(2, 1, LANES)
    cmp_aug = _compress_call(x16, pe2, w1, b1, w2p, b2p, tab_cmp)
    ocmp, bias = _nsa_select_call(q_hm, cmp_aug, _slc_weight_matrix(n_cmp_pad))
    ob = _nsa_attend_call(q_hm, bias, ocmp, gates, kv_aug, v_aug)

    h1 = _post_call(oa, ob, sg, x2, w_proj_a[0].astype(BF16), w_proj_b[0].astype(BF16),
                    w_out[0].astype(BF16), post_mix_g[0].reshape(1, d))
    out = _ffn_call(h1, pre_ffn_g[0].reshape(1, d), w_gate[0].astype(BF16), w_up[0].astype(BF16),
                    w_down[0].astype(BF16), post_ffn_g[0].reshape(1, d))
    return out.reshape(bsz, t, d)
```

```python
import functools

import numpy as np
import jax
import jax.numpy as jnp
from jax import lax
from jax.experimental import pallas as pl
from jax.experimental.pallas import tpu as pltpu

F32 = jnp.float32
BF16 = jnp.bfloat16

D_MODEL = 2048
D_A = 1024
D_B = 1024
A_HEADS = 8
A_DK = 128
B_HEADS = 16
B_KV_HEADS = 4
B_GROUP = 4
HEAD_DIM = 64
KV_DIM = 256
CMP_BLOCK = 32
CMP_STRIDE = 16
CMP_HIDDEN = 256
SLC_BLOCK = 64
SLC_TOP_N = 16
SLC_FORCED = 3
SLC_WEIGHTS = (1.0, 2.0, 2.0, 2.0, 1.0)
WINDOW = 512
Q_TILE = 256
ATT_TILE = 512
KV_PER_SELECT_STEP = 2
ROPE_THETA = 500000.0
ROPE_DIM = 16
D_FF = 5632
EPS = 1e-6
LOG2_E = 1.4426950408889634
NEG_INF = -1e30
N_BLK_LANES = 128
LANES = 128
VMEM_LIMIT = 56 * 1024 * 1024

NT_DIMS = (((1,), (1,)), ((), ()))
TN_DIMS = (((0,), (0,)), ((), ()))


def _cparams(n_axes, vmem=None):
    return pltpu.CompilerParams(dimension_semantics=("arbitrary",) * n_axes,
                                vmem_limit_bytes=vmem)


def _rmsnorm_kernel(x_ref, g_ref, o_ref):
    x = x_ref[...]
    y = x * lax.rsqrt(jnp.mean(x * x, axis=-1, keepdims=True) + EPS)
    o_ref[...] = (y * g_ref[...]).astype(o_ref.dtype)


def _rmsnorm_call(x2, g, tm=512):
    t, d = x2.shape
    return pl.pallas_call(
        _rmsnorm_kernel, grid=(t // tm,),
        in_specs=[pl.BlockSpec((tm, d), lambda i: (i, 0)), pl.BlockSpec((1, d), lambda i: (0, 0))],
        out_specs=pl.BlockSpec((tm, d), lambda i: (i, 0)),
        out_shape=jax.ShapeDtypeStruct((t, d), BF16), name="pre_mix_rmsnorm",
        compiler_params=_cparams(1))(x2, g)


def _proj_call(name, xn, w, *, tm, tn, epilogue, out_shape, out_specs, extra=(), extra_specs=()):
    t, k = xn.shape
    n = w.shape[1]
    n_extra = len(extra)

    def body(x_ref, w_ref, *rest):
        acc = jnp.dot(x_ref[...], w_ref[...], preferred_element_type=F32)
        epilogue(acc, rest[:n_extra], rest[n_extra:])

    return pl.pallas_call(
        body, grid=(t // tm, n // tn),
        in_specs=[pl.BlockSpec((tm, k), lambda i, j: (i, 0)),
                  pl.BlockSpec((k, tn), lambda i, j: (0, j)), *extra_specs],
        out_specs=out_specs, out_shape=out_shape, name=name,
        compiler_params=_cparams(2, VMEM_LIMIT))(xn, w, *extra)


def _rope_tile(a, tab_ref):
    return (a * tab_ref[0] + pltpu.roll(a, LANES - ROPE_DIM // 2, 1) * tab_ref[1]
            + pltpu.roll(a, ROPE_DIM // 2, 1) * tab_ref[2])


def _ep_plain(acc, extra, outs):
    outs[0][0] = acc.astype(outs[0].dtype)


def _ep_q(acc, extra, outs):
    for pair in range(acc.shape[1] // LANES):
        r = _rope_tile(acc[:, pair * LANES:(pair + 1) * LANES], extra[0])
        outs[0][2 * pair] = r[:, :HEAD_DIM]
        outs[0][2 * pair + 1] = r[:, HEAD_DIM:]


def _ep_cmp(acc, extra, outs):
    for kv in range(2):
        for h in range(B_KV_HEADS):
            c0 = kv * KV_DIM + h * HEAD_DIM
            outs[0][kv, h] = acc[:, c0:c0 + HEAD_DIM]


def _ep_kv(acc, extra, outs, *, tm):
    i = pl.program_id(0)
    j = pl.program_id(1)
    rowg = i * tm + lax.broadcasted_iota(jnp.int32, (tm, LANES), 0)
    lane = lax.broadcasted_iota(jnp.int32, (tm, LANES), 1)
    aux = jnp.where(j == 0, (rowg // SLC_BLOCK == lane).astype(F32), 0.0).astype(BF16)
    ones_col = (lane == HEAD_DIM).astype(F32)
    for h in range(B_KV_HEADS):
        r = _rope_tile(acc[:, h * LANES:(h + 1) * LANES], extra[0])
        outs[0][0, h, :, 0:LANES] = r.astype(BF16)
        outs[0][0, h, :, LANES:2 * LANES] = aux
        outs[1][0, h] = jnp.where(lane < HEAD_DIM, pltpu.roll(r, HEAD_DIM, 1), ones_col).astype(BF16)


def _ep_gate(acc, extra, outs):
    for h in range(B_KV_HEADS):
        outs[0][h] = jax.nn.sigmoid(acc[:, h * LANES:(h + 1) * LANES])


def _ep_sigmoid(acc, extra, outs):
    outs[0][0] = jax.nn.sigmoid(acc).astype(outs[0].dtype)


def _split3(x):
    hi = x.astype(BF16)
    r1 = x - hi.astype(F32)
    mid = r1.astype(BF16)
    lo = (r1 - mid.astype(F32)).astype(BF16)
    return hi, mid, lo


def _hgrn_kernel(q_ref, f_ref, i_ref, g_ref, lb_ref, gn_ref, o_ref, st_ref, *, chunk, heads):
    c = pl.program_id(1)

    @pl.when(c == 0)
    def _():
        st_ref[...] = jnp.zeros_like(st_ref)

    row = lax.broadcasted_iota(jnp.int32, (chunk, chunk), 0)
    col = lax.broadcasted_iota(jnp.int32, (chunk, chunk), 1)
    tri = (col <= row).astype(BF16)
    rowv = lax.broadcasted_iota(jnp.int32, (chunk, A_DK), 0)

    for hb in range(heads):
        sl = slice(hb * A_DK, (hb + 1) * A_DK)
        q = q_ref[0, :, sl]
        ii = i_ref[0, :, sl]
        gg = g_ref[0, :, sl]
        lbv = lb_ref[:, sl]
        f = lbv + (1.0 - lbv) * jax.nn.sigmoid(f_ref[0, :, sl])
        lf = jnp.log(f)
        k = 1.0 - f
        b = sum(jnp.dot(tri, p, preferred_element_type=F32) for p in _split3(lf)) * LOG2_E
        b_end = b[chunk - 1:chunk, :]
        ii16 = ii.astype(BF16)

        st = st_ref[hb]
        o = lax.dot_general((q * jnp.exp2(b)).astype(BF16), st.astype(BF16), NT_DIMS,
                            preferred_element_type=F32)
        kd = (k * jnp.exp2(b_end - b)).astype(BF16)
        st_ref[hb] = st * jnp.exp2(b_end) + lax.dot_general(ii16, kd, TN_DIMS,
                                                            preferred_element_type=F32)

        att = jnp.where(col == row, jnp.sum(q * k, axis=-1, keepdims=True), 0.0)
        for d in range(1, 8):
            valid = (rowv % 8) >= d
            w = jnp.exp2(b - pltpu.roll(b, d, 0))
            p = jnp.where(valid, q * pltpu.roll(k, d, 0) * w, 0.0)
            att = att + jnp.where(col == row - d, jnp.sum(p, axis=-1, keepdims=True), 0.0)
        m = 8
        while m < chunk:
            grp = chunk // (2 * m)
            b3 = b.reshape(grp, 2 * m, A_DK)
            refrow = jnp.broadcast_to(b3[:, m - 1:m, :], (grp, 2 * m, A_DK)).reshape(chunk, A_DK)
            second = (rowv % (2 * m)) >= m
            ql = jnp.where(second, q * jnp.exp2(b - refrow), 0.0)
            kl = jnp.where(second, 0.0, k * jnp.exp2(refrow - b))
            a = lax.dot_general(ql.astype(BF16), kl.astype(BF16), NT_DIMS, preferred_element_type=F32)
            if grp > 1:
                a = jnp.where(row // (2 * m) == col // (2 * m), a, 0.0)
            att = att + a
            m *= 2
        o = o + jnp.dot(att.astype(BF16), ii16, preferred_element_type=F32)

        o = o * lax.rsqrt(jnp.mean(o * o, axis=-1, keepdims=True) + EPS)
        o_ref[:, sl] = ((o * gn_ref[:, sl]) * (gg * jax.nn.sigmoid(gg))).astype(o_ref.dtype)


def _hgrn_call(proj4, lb, gn, *, chunk=64, heads=8):
    _, t, _ = proj4.shape
    w = heads * A_DK

    def spec(kind):
        return pl.BlockSpec((1, chunk, w), lambda h, c, kind=kind: (kind, c, h))

    vec = pl.BlockSpec((1, w), lambda h, c: (0, h))
    return pl.pallas_call(
        functools.partial(_hgrn_kernel, chunk=chunk, heads=heads),
        grid=(A_HEADS // heads, t // chunk),
        in_specs=[spec(0), spec(1), spec(2), spec(3), vec, vec],
        out_specs=pl.BlockSpec((chunk, w), lambda h, c: (c, h)),
        out_shape=jax.ShapeDtypeStruct((t, D_A), BF16),
        scratch_shapes=[pltpu.VMEM((heads, A_DK, A_DK), F32)], name="hgrn2_scan",
        compiler_params=_cparams(2))(proj4, proj4, proj4, proj4, lb, gn)


def _compress_kernel(x_ref, pe_ref, w1_ref, b1_ref, w2_ref, b2_ref, tab_ref, o_ref):
    half = CMP_BLOCK * HEAD_DIM // 2
    x = x_ref[0, 0]
    top = jnp.dot((x + pe_ref[0, 0:1, :]).astype(BF16), w1_ref[0, :half, :], preferred_element_type=F32)
    bot = jnp.dot((x + pe_ref[0, 1:2, :]).astype(BF16), w1_ref[0, half:, :], preferred_element_type=F32)
    n = x.shape[0]
    h = jax.nn.gelu(top + pltpu.roll(bot, n - 1, 0) + b1_ref[0])
    y = jnp.dot(h.astype(BF16), w2_ref[0], preferred_element_type=F32) + b2_ref[0]
    y = _rope_tile(y, tab_ref)
    hi = y.astype(BF16)
    lo = (y - hi.astype(F32)).astype(BF16)
    o_ref[0, 0, :, 0:LANES] = (y + pltpu.roll(y, HEAD_DIM, 1)).astype(BF16)
    o_ref[0, 0, :, LANES:2 * LANES] = lo


def _compress_call(x16, pe2, w1, b1, w2p, b2p, tab):
    _, nh, n, wid = x16.shape
    hid = w1.shape[-1]
    return pl.pallas_call(
        _compress_kernel, grid=(2, nh),
        in_specs=[pl.BlockSpec((1, 1, n, wid), lambda kv, h: (kv, h, 0, 0)),
                  pl.BlockSpec((1, 2, wid), lambda kv, h: (kv, 0, 0)),
                  pl.BlockSpec((1, 2 * wid, hid), lambda kv, h: (kv, 0, 0)),
                  pl.BlockSpec((1, 1, hid), lambda kv, h: (kv, 0, 0)),
                  pl.BlockSpec((1, hid, LANES), lambda kv, h: (kv, 0, 0)),
                  pl.BlockSpec((1, 1, LANES), lambda kv, h: (kv, 0, 0)),
                  pl.BlockSpec((3, n, LANES), lambda kv, h: (kv, 0, 0))],
        out_specs=pl.BlockSpec((1, 1, n, 2 * LANES), lambda kv, h: (kv, h, 0, 0)),
        out_shape=jax.ShapeDtypeStruct((2, nh, n, 2 * LANES), BF16),
        name="compress_mlp", compiler_params=_cparams(2, VMEM_LIMIT))(x16, pe2, w1, b1, w2p, b2p, tab)


def _nsa_select_kernel(q_ref, kc_ref, vc_ref, wt_ref, ocmp_ref, bias_ref, lhs_ref):
    qt = pl.program_id(1)
    q0 = qt * Q_TILE
    rows = B_GROUP * Q_TILE
    n_cmp = kc_ref.shape[2]

    def chains(n_cols):
        t_col = q0 + lax.broadcasted_iota(jnp.int32, (rows, 1), 0) % Q_TILE
        n_idx = lax.broadcasted_iota(jnp.int32, (rows, n_cols), 1)
        vis = n_idx <= (t_col - (CMP_BLOCK - 1)) // CMP_STRIDE
        blk = lax.broadcasted_iota(jnp.int32, (N_BLK_LANES, Q_TILE), 0)
        tok = q0 + lax.broadcasted_iota(jnp.int32, (N_BLK_LANES, Q_TILE), 1)
        cur = tok // SLC_BLOCK
        forced = (blk == 0) | (blk == cur) | (blk == cur - 1)
        candidate = (blk * SLC_BLOCK <= tok) & jnp.logical_not(forced)
        wt = wt_ref[:, 0:n_cols]
        zeros64 = jnp.zeros((rows, HEAD_DIM), BF16)

        for c in range(KV_PER_SELECT_STEP):
            q = q_ref[c * B_GROUP:(c + 1) * B_GROUP].reshape(rows, HEAD_DIM)
            qhi = q.astype(BF16)
            qlo = (q - qhi.astype(F32)).astype(BF16)
            lhs_ref[c, :, 0:HEAD_DIM] = qhi
            lhs_ref[c, :, HEAD_DIM:2 * HEAD_DIM] = qlo
            lhs_ref[c, :, 2 * HEAD_DIM:3 * HEAD_DIM] = qhi
            lhs_ref[c, :, 3 * HEAD_DIM:] = zeros64
            s = lax.dot_general(lhs_ref[c], kc_ref[0, c, 0:n_cols, :], NT_DIMS, preferred_element_type=F32)
            s = jnp.where(vis, s, NEG_INF)
            e = jnp.where(vis, jnp.exp2(s - jnp.max(s, axis=-1, keepdims=True)), 0.0)
            l = jnp.sum(e, axis=-1, keepdims=True)
            p = e * (1.0 / jnp.where(l > 0.0, l, 1.0))
            o_cmp = jnp.dot(p.astype(BF16), vc_ref[0, c, 0:n_cols, 0:LANES], preferred_element_type=F32)
            for g in range(B_GROUP):
                c0 = (c * B_GROUP + g) * HEAD_DIM
                ocmp_ref[:, c0:c0 + HEAD_DIM] = o_cmp[g * Q_TILE:(g + 1) * Q_TILE, :HEAD_DIM]

            psum = p[0:Q_TILE] + p[Q_TILE:2 * Q_TILE] + p[2 * Q_TILE:3 * Q_TILE] + p[3 * Q_TILE:]
            ps_hi = psum.astype(BF16)
            ps_lo = (psum - ps_hi.astype(F32)).astype(BF16)
            pslc = (lax.dot_general(wt, ps_hi, NT_DIMS, preferred_element_type=F32)
                    + lax.dot_general(wt, ps_lo, NT_DIMS, preferred_element_type=F32))
            score = jnp.where(candidate, pslc, jnp.where(forced, -jnp.inf, NEG_INF))
            sel_t = forced.astype(F32)
            for _ in range(SLC_TOP_N - SLC_FORCED):
                best = jnp.max(score, axis=0, keepdims=True)
                first = jnp.min(jnp.where(score == best, blk, N_BLK_LANES), axis=0, keepdims=True)
                hit = blk == first
                score = jnp.where(hit, -jnp.inf, score)
                sel_t = jnp.where(hit, 1.0, sel_t)
            bias_ref[c] = jnp.where(sel_t.T > 0.0, 0.0, NEG_INF).astype(BF16)

    early = qt < pl.num_programs(1) // 2
    pl.when(early)(lambda: chains(n_cmp // 2))
    pl.when(jnp.logical_not(early))(lambda: chains(n_cmp))


def _nsa_select_call(q_hm, cmp_aug, wt):
    _, t, _ = q_hm.shape
    n_cmp = cmp_aug.shape[2]
    kvs = KV_PER_SELECT_STEP
    return pl.pallas_call(
        _nsa_select_kernel, grid=(B_KV_HEADS // kvs, t // Q_TILE),
        in_specs=[pl.BlockSpec((kvs * B_GROUP, Q_TILE, HEAD_DIM), lambda p, i: (p, i, 0)),
                  pl.BlockSpec((1, kvs, n_cmp, 2 * LANES), lambda p, i: (0, p, 0, 0)),
                  pl.BlockSpec((1, kvs, n_cmp, 2 * LANES), lambda p, i: (1, p, 0, 0)),
                  pl.BlockSpec((N_BLK_LANES, n_cmp), lambda p, i: (0, 0))],
        out_specs=[pl.BlockSpec((Q_TILE, kvs * B_GROUP * HEAD_DIM), lambda p, i: (i, p)),
                   pl.BlockSpec((kvs, Q_TILE, N_BLK_LANES), lambda p, i: (p, i, 0))],
        out_shape=[jax.ShapeDtypeStruct((t, D_B), F32),
                   jax.ShapeDtypeStruct((B_KV_HEADS, t, N_BLK_LANES), BF16)],
        scratch_shapes=[pltpu.VMEM((kvs, B_GROUP * Q_TILE, 2 * LANES), BF16)], name="nsa_select",
        compiler_params=_cparams(2, VMEM_LIMIT))(q_hm, cmp_aug, cmp_aug, wt)


def _nsa_attend_kernel(q_ref, bias_ref, ocmp_ref, gate_ref, ksv_ref, kwv_ref, vs_ref, vw_ref, o_ref,
                       lhs_w_ref, lhs_s_ref, *, key_tile, tiles_per_group):
    q0 = pl.program_id(1) * ATT_TILE
    rows = B_GROUP * ATT_TILE
    qhi = q_ref[...].reshape(rows, HEAD_DIM).astype(BF16)
    t_col = q0 + lax.broadcasted_iota(jnp.int32, (rows, 1), 0) % ATT_TILE

    lhs_s_ref[:, 0:HEAD_DIM] = qhi
    lhs_s_ref[:, HEAD_DIM:LANES] = jnp.zeros((rows, LANES - HEAD_DIM), BF16)
    for g in range(B_GROUP):
        lhs_s_ref[g * ATT_TILE:(g + 1) * ATT_TILE, LANES:] = bias_ref[0]
    lhs_w_ref[:, 0:HEAD_DIM] = qhi
    lhs_w_ref[:, HEAD_DIM:] = jnp.zeros((rows, 2 * LANES - HEAD_DIM), BF16)

    def tile(ti, carry, causal):
        m_run, acc_run = carry
        start = pl.multiple_of(ti * key_tile, key_tile)
        st = lax.dot_general(lhs_s_ref[...], ksv_ref[0, 0, pl.ds(start, key_tile), :], NT_DIMS,
                             preferred_element_type=F32)
        if causal:
            kp = start + lax.broadcasted_iota(jnp.int32, (rows, key_tile), 1)
            st = jnp.where(kp <= t_col, st, NEG_INF)
        m_new = jnp.maximum(m_run, jnp.max(st, axis=-1, keepdims=True))
        pt = jnp.exp2(st - m_new)
        pv = jnp.dot(pt.astype(BF16), vs_ref[0, 0, pl.ds(start, key_tile), :], preferred_element_type=F32)
        return m_new, acc_run * jnp.exp2(m_run - m_new) + pv

    def group(gi, carry):
        for u in range(tiles_per_group):
            carry = tile(gi * tiles_per_group + u, carry, False)
        return carry

    n_full = q0 // key_tile
    n_groups = n_full // tiles_per_group
    carry = (jnp.full((rows, 1), NEG_INF, F32), jnp.zeros((rows, LANES), F32))
    carry = lax.fori_loop(0, n_groups, group, carry)
    carry = lax.fori_loop(n_groups * tiles_per_group, n_full, functools.partial(tile, causal=False), carry)
    _, acc = tile(n_full, carry, True)
    o_slc = acc[:, :HEAD_DIM] * (1.0 / acc[:, HEAD_DIM:HEAD_DIM + 1])

    slab = WINDOW + ATT_TILE
    ws = pl.multiple_of(jnp.maximum(q0 - WINDOW, 0), ATT_TILE)
    s = lax.dot_general(lhs_w_ref[...], kwv_ref[0, 0, pl.ds(ws, slab), :], NT_DIMS,
                        preferred_element_type=F32)
    kpos = ws + lax.broadcasted_iota(jnp.int32, (rows, slab), 1)
    ok = lax.bitcast_convert_type(t_col - kpos, jnp.uint32) < jnp.uint32(WINDOW)
    s = jnp.where(ok, s, NEG_INF)
    e = jnp.exp2(s - jnp.max(s, axis=-1, keepdims=True))
    acc = jnp.dot(e.astype(BF16), vw_ref[0, 0, pl.ds(ws, slab), :], preferred_element_type=F32)
    o_win = acc[:, :HEAD_DIM] * (1.0 / acc[:, HEAD_DIM:HEAD_DIM + 1])

    gates = gate_ref[0]
    for g in range(B_GROUP):
        r = slice(g * ATT_TILE, (g + 1) * ATT_TILE)
        og = (gates[:, 3 * g:3 * g + 1] * ocmp_ref[:, g * HEAD_DIM:(g + 1) * HEAD_DIM]
              + gates[:, 3 * g + 1:3 * g + 2] * o_slc[r] + gates[:, 3 * g + 2:3 * g + 3] * o_win[r])
        o_ref[:, g * HEAD_DIM:(g + 1) * HEAD_DIM] = og.astype(o_ref.dtype)


def _nsa_attend_call(q_hm, bias, ocmp, gates, kv_aug, v_aug, *, key_tile=512, tiles_per_group=2):
    _, t, _ = q_hm.shape
    assert t % key_tile == 0 and key_tile % ATT_TILE == 0 and t % ATT_TILE == 0
    rows = B_GROUP * ATT_TILE
    return pl.pallas_call(
        functools.partial(_nsa_attend_kernel, key_tile=key_tile, tiles_per_group=tiles_per_group),
        grid=(B_KV_HEADS, t // ATT_TILE),
        in_specs=[pl.BlockSpec((B_GROUP, ATT_TILE, HEAD_DIM), lambda h, i: (h, i, 0)),
                  pl.BlockSpec((1, ATT_TILE, N_BLK_LANES), lambda h, i: (h, i, 0)),
                  pl.BlockSpec((ATT_TILE, B_GROUP * HEAD_DIM), lambda h, i: (i, h)),
                  pl.BlockSpec((1, ATT_TILE, LANES), lambda h, i: (h, i, 0)),
                  pl.BlockSpec((1, 1, t, 2 * LANES), lambda h, i: (0, h, 0, 0)),
                  pl.BlockSpec((1, 1, t, 2 * LANES), lambda h, i: (1, h, 0, 0)),
                  pl.BlockSpec((1, 1, t, LANES), lambda h, i: (0, h, 0, 0)),
                  pl.BlockSpec((1, 1, t, LANES), lambda h, i: (1, h, 0, 0))],
        out_specs=pl.BlockSpec((ATT_TILE, B_GROUP * HEAD_DIM), lambda h, i: (i, h)),
        out_shape=jax.ShapeDtypeStruct((t, D_B), BF16),
        scratch_shapes=[pltpu.VMEM((rows, 2 * LANES), BF16), pltpu.VMEM((rows, 2 * LANES), BF16)],
        name="nsa_attend",
        compiler_params=_cparams(2, VMEM_LIMIT))(q_hm, bias, ocmp, gates, kv_aug, kv_aug, v_aug, v_aug)


def _post_kernel(oa_ref, ob_ref, sg_ref, x_ref, wa_ref, wb_ref, wo_ref, g_ref, o_ref):
    ya = jnp.dot(oa_ref[...], wa_ref[...], preferred_element_type=F32)
    yb = jnp.dot(ob_ref[...], wb_ref[...], preferred_element_type=F32)
    merged = sg_ref[0].astype(F32) * ya + sg_ref[1].astype(F32) * yb
    mix = jnp.dot(merged.astype(BF16), wo_ref[...], preferred_element_type=F32)
    y = mix * lax.rsqrt(jnp.mean(mix * mix, axis=-1, keepdims=True) + EPS)
    o_ref[...] = x_ref[...] + y * g_ref[...]


def _post_call(oa, ob, sg, x2, wa, wb, wo, g, tm=256):
    t, d = x2.shape
    const = dict(pipeline_mode=pl.Buffered(1))
    return pl.pallas_call(
        _post_kernel, grid=(t // tm,),
        in_specs=[pl.BlockSpec((tm, D_A), lambda i: (i, 0)),
                  pl.BlockSpec((tm, D_B), lambda i: (i, 0)),
                  pl.BlockSpec((2, tm, d), lambda i: (0, i, 0)),
                  pl.BlockSpec((tm, d), lambda i: (i, 0)),
                  pl.BlockSpec((D_A, d), lambda i: (0, 0), **const),
                  pl.BlockSpec((D_B, d), lambda i: (0, 0), **const),
                  pl.BlockSpec((d, d), lambda i: (0, 0), **const),
                  pl.BlockSpec((1, d), lambda i: (0, 0))],
        out_specs=pl.BlockSpec((tm, d), lambda i: (i, 0)),
        out_shape=jax.ShapeDtypeStruct((t, d), F32), name="mix_out_residual",
        compiler_params=_cparams(1, VMEM_LIMIT))(oa, ob, sg, x2, wa, wb, wo, g)


def _ffn_kernel(h_ref, g1_ref, wg_ref, wu_ref, wd_ref, g2_ref, o_ref, hn_ref, acc_ref):
    f = pl.program_id(1)

    @pl.when(f == 0)
    def _():
        h = h_ref[...]
        y = h * lax.rsqrt(jnp.mean(h * h, axis=-1, keepdims=True) + EPS)
        hn_ref[...] = (y * g1_ref[...]).astype(BF16)
        acc_ref[...] = jnp.zeros_like(acc_ref)

    hn = hn_ref[...]
    a = jnp.dot(hn, wg_ref[...], preferred_element_type=F32)
    u = jnp.dot(hn, wu_ref[...], preferred_element_type=F32)
    z = (a * jax.nn.sigmoid(a) * u).astype(BF16)
    acc_ref[...] += jnp.dot(z, wd_ref[...], preferred_element_type=F32)

    @pl.when(f == pl.num_programs(1) - 1)
    def _():
        ff = acc_ref[...]
        y = ff * lax.rsqrt(jnp.mean(ff * ff, axis=-1, keepdims=True) + EPS)
        o_ref[...] = h_ref[...] + y * g2_ref[...]


def _ffn_call(h1, g1, wg, wu, wd, g2, tm=512, tf=512):
    t, d = h1.shape
    dff = wg.shape[1]
    return pl.pallas_call(
        _ffn_kernel, grid=(t // tm, dff // tf),
        in_specs=[pl.BlockSpec((tm, d), lambda i, f: (i, 0)),
                  pl.BlockSpec((1, d), lambda i, f: (0, 0)),
                  pl.BlockSpec((d, tf), lambda i, f: (0, f)),
                  pl.BlockSpec((d, tf), lambda i, f: (0, f)),
                  pl.BlockSpec((tf, d), lambda i, f: (f, 0)),
                  pl.BlockSpec((1, d), lambda i, f: (0, 0))],
        out_specs=pl.BlockSpec((tm, d), lambda i, f: (i, 0)),
        out_shape=jax.ShapeDtypeStruct((t, d), F32),
        scratch_shapes=[pltpu.VMEM((tm, d), BF16), pltpu.VMEM((tm, d), F32)], name="swiglu_ffn",
        compiler_params=_cparams(2, VMEM_LIMIT))(h1, g1, wg, wu, wd, g2)


def _rope_tables(pos, period, scale=1.0):
    half = ROPE_DIM // 2
    inv = jnp.float32(ROPE_THETA) ** (-jnp.arange(half, dtype=F32) * 2.0 / ROPE_DIM)
    ang = pos.astype(F32)[:, None] * inv[None, :]
    cos, sin = jnp.cos(ang), jnp.sin(ang)
    r = np.arange(LANES) % period
    f = r % half
    lo = jnp.asarray((r < half)[None, :])
    hi = jnp.asarray(((r >= half) & (r < ROPE_DIM))[None, :])
    c = jnp.where(lo | hi, cos[:, f], 1.0)
    sa = jnp.where(lo, -sin[:, f], 0.0)
    sb = jnp.where(hi, sin[:, f], 0.0)
    return jnp.stack([c, sa, sb]) * scale


def _slc_weight_matrix(n_cmp_pad):
    ratio = SLC_BLOCK // CMP_STRIDE
    w = np.zeros((N_BLK_LANES, n_cmp_pad), np.float32)
    for j in range(N_BLK_LANES):
        for o, wv in enumerate(SLC_WEIGHTS):
            n = ratio * j + o - 1
            if 0 <= n < n_cmp_pad - 1:
                w[j, n] = wv
    return jnp.asarray(w, BF16)


def kernel(x, pre_mix_g, w_in, lb_table, a_norm_g, cmp_pe_k, cmp_w1_k, cmp_b1_k, cmp_w2_k, cmp_b2_k, cmp_pe_v, cmp_w1_v, cmp_b1_v, cmp_w2_v, cmp_b2_v, w_proj_a, w_proj_b, w_out, post_mix_g, pre_ffn_g, w_gate, w_up, w_down, post_ffn_g):
    bsz, t, d = x.shape
    assert bsz == 1 and d == D_MODEL and WINDOW + ATT_TILE <= t <= N_BLK_LANES * SLC_BLOCK
    x2 = x.reshape(t, d)
    n_cmp_pad = t // CMP_STRIDE
    assert n_cmp_pad % LANES == 0

    lb = jnp.cumsum(jax.nn.softmax(lb_table.astype(F32), axis=0), axis=0)[0].reshape(1, D_A)
    w = w_in[0]
    o_q, o_kc, o_ks, o_g, o_m = 4 * D_A, 4 * D_A + D_B, 4 * D_A + D_B + 2 * KV_DIM, 4 * D_A + D_B + 6 * KV_DIM, 4 * D_A + D_B + 6 * KV_DIM + 3 * B_HEADS
    w_hgrn = w[:, :o_q].astype(BF16)
    w_q = w[:, o_q:o_kc].astype(BF16)
    w_cmp = w[:, o_kc:o_ks].astype(BF16)
    w_kv = (w[:, o_ks:o_g].reshape(d, 2, 2, B_KV_HEADS, HEAD_DIM).transpose(0, 1, 3, 2, 4)
            .reshape(d, 4 * KV_DIM).astype(BF16))
    w_gate3 = w[:, o_g:o_m].reshape(d, B_KV_HEADS, 3 * B_GROUP)
    w_g = jnp.pad(w_gate3, ((0, 0), (0, 0), (0, LANES - 3 * B_GROUP))).reshape(d, B_KV_HEADS * LANES).astype(BF16)
    w_m = w[:, o_m:].astype(BF16)

    pos = jnp.arange(t)
    tab_q = _rope_tables(pos, HEAD_DIM, HEAD_DIM ** -0.5 * LOG2_E)
    tab_kv = _rope_tables(pos, LANES)
    end_pos = jnp.arange(n_cmp_pad) * CMP_STRIDE + (CMP_BLOCK - 1)
    ident = jnp.stack([jnp.ones((n_cmp_pad, LANES), F32), jnp.zeros((n_cmp_pad, LANES), F32),
                       jnp.zeros((n_cmp_pad, LANES), F32)])
    tab_cmp = jnp.concatenate([_rope_tables(end_pos, LANES), ident], axis=0)

    xn = _rmsnorm_call(x2, pre_mix_g[0].reshape(1, d))

    tm = 1024 if t % 1024 == 0 else 256
    proj4 = _proj_call(
        "proj_hgrn", xn, w_hgrn, tm=tm, tn=512, epilogue=_ep_plain,
        out_shape=jax.ShapeDtypeStruct((4, t, D_A), F32),
        out_specs=pl.BlockSpec((1, tm, 512), lambda i, j: (j // 2, i, j % 2)))
    q_hm = _proj_call(
        "proj_q", xn, w_q, tm=tm, tn=512, epilogue=_ep_q,
        out_shape=jax.ShapeDtypeStruct((B_HEADS, t, HEAD_DIM), F32),
        out_specs=pl.BlockSpec((512 // HEAD_DIM, tm, HEAD_DIM), lambda i, j: (j, i, 0)),
        extra=(tab_q,), extra_specs=(pl.BlockSpec((3, tm, LANES), lambda i, j: (0, i, 0)),))
    cmp_in = _proj_call(
        "proj_cmp", xn, w_cmp, tm=tm, tn=2 * KV_DIM, epilogue=_ep_cmp,
        out_shape=jax.ShapeDtypeStruct((2, B_KV_HEADS, t, HEAD_DIM), F32),
        out_specs=pl.BlockSpec((2, B_KV_HEADS, tm, HEAD_DIM), lambda i, j: (0, 0, i, 0)))
    kv_aug, v_aug = _proj_call(
        "proj_kv", xn, w_kv, tm=tm, tn=2 * KV_DIM, epilogue=functools.partial(_ep_kv, tm=tm),
        out_shape=[jax.ShapeDtypeStruct((2, B_KV_HEADS, t, 2 * LANES), BF16),
                   jax.ShapeDtypeStruct((2, B_KV_HEADS, t, LANES), BF16)],
        out_specs=[pl.BlockSpec((1, B_KV_HEADS, tm, 2 * LANES), lambda i, j: (j, 0, i, 0)),
                   pl.BlockSpec((1, B_KV_HEADS, tm, LANES), lambda i, j: (j, 0, i, 0))],
        extra=(tab_kv,), extra_specs=(pl.BlockSpec((3, tm, LANES), lambda i, j: (0, i, 0)),))
    gates = _proj_call(
        "proj_gate", xn, w_g, tm=tm, tn=B_KV_HEADS * LANES, epilogue=_ep_gate,
        out_shape=jax.ShapeDtypeStruct((B_KV_HEADS, t, LANES), F32),
        out_specs=pl.BlockSpec((B_KV_HEADS, tm, LANES), lambda i, j: (0, i, 0)))
    sg = _proj_call(
        "proj_merge", xn, w_m, tm=tm, tn=512, epilogue=_ep_sigmoid,
        out_shape=jax.ShapeDtypeStruct((2, t, d), BF16),
        out_specs=pl.BlockSpec((1, tm, 512), lambda i, j: (j // 4, i, j % 4)))

    oa = _hgrn_call(proj4, lb, a_norm_g[0].reshape(1, D_A))

    half = CMP_BLOCK // 2
    x16 = cmp_in.reshape(2, B_KV_HEADS, n_cmp_pad, CMP_STRIDE * HEAD_DIM)
    pe2 = jnp.stack([cmp_pe_k[0], cmp_pe_v[0]]).reshape(2, 2, half * HEAD_DIM)
    w1 = jnp.stack([cmp_w1_k[0], cmp_w1_v[0]]).astype(BF16)
    b1 = jnp.stack([cmp_b1_k[0], cmp_b1_v[0]]).reshape(2, 1, CMP_HIDDEN)
    w2p = jnp.pad(jnp.stack([cmp_w2_k[0], cmp_w2_v[0]]), ((0, 0), (0, 0), (0, LANES - HEAD_DIM))).astype(BF16)
    b2p = jnp.pad(jnp.stack([cmp_b2_k[0], cmp_b2_v[0]]), ((0, 0), (0, LANES - HEAD_DIM))).reshape(2, 1, LANES)
    cmp_aug = _compress_call(x16, pe2, w1, b1, w2p, b2p, tab_cmp)
    ocmp, bias = _nsa_select_call(q_hm, cmp_aug, _slc_weight_matrix(n_cmp_pad))
    ob = _nsa_attend_call(q_hm, bias, ocmp, gates, kv_aug, v_aug)

    h1 = _post_call(oa, ob, sg, x2, w_proj_a[0].astype(BF16), w_proj_b[0].astype(BF16),
                    w_out[0].astype(BF16), post_mix_g[0].reshape(1, d))
    out = _ffn_call(h1, pre_ffn_g[0].reshape(1, d), w_gate[0].astype(BF16), w_up[0].astype(BF16),
                    w_down[0].astype(BF16), post_ffn_g[0].reshape(1, d))
    return out.reshape(bsz, t, d)
```

```python
import functools

import numpy as np
import jax
import jax.numpy as jnp
from jax import lax
from jax.experimental import pallas as pl
from jax.experimental.pallas import tpu as pltpu

F32 = jnp.float32
BF16 = jnp.bfloat16

D_MODEL = 2048
D_A = 1024
D_B = 1024
A_HEADS = 8
A_DK = 128
B_HEADS = 16
B_KV_HEADS = 4
B_GROUP = 4
HEAD_DIM = 64
KV_DIM = 256
CMP_BLOCK = 32
CMP_STRIDE = 16
CMP_HIDDEN = 256
SLC_BLOCK = 64
SLC_TOP_N = 16
SLC_FORCED = 3
SLC_WEIGHTS = (1.0, 2.0, 2.0, 2.0, 1.0)
WINDOW = 512
Q_TILE = 256
ATT_TILE = 256
KV_PER_ATTEND_STEP = 2
KV_PER_SELECT_STEP = 2
PROJ_SUBTILE = 256
ROPE_THETA = 500000.0
ROPE_DIM = 16
D_FF = 5632
EPS = 1e-6
LOG2_E = 1.4426950408889634
NEG_INF = -1e30
N_BLK_LANES = 128
LANES = 128
VMEM_LIMIT = 56 * 1024 * 1024

NT_DIMS = (((1,), (1,)), ((), ()))
TN_DIMS = (((0,), (0,)), ((), ()))


def _cparams(n_axes, vmem=None):
    return pltpu.CompilerParams(dimension_semantics=("arbitrary",) * n_axes,
                                vmem_limit_bytes=vmem)


def _rmsnorm_kernel(x_ref, g_ref, o_ref):
    x = x_ref[...]
    y = x * lax.rsqrt(jnp.mean(x * x, axis=-1, keepdims=True) + EPS)
    o_ref[...] = (y * g_ref[...]).astype(o_ref.dtype)


def _rmsnorm_call(x2, g, tm=512):
    t, d = x2.shape
    return pl.pallas_call(
        _rmsnorm_kernel, grid=(t // tm,),
        in_specs=[pl.BlockSpec((tm, d), lambda i: (i, 0)), pl.BlockSpec((1, d), lambda i: (0, 0))],
        out_specs=pl.BlockSpec((tm, d), lambda i: (i, 0)),
        out_shape=jax.ShapeDtypeStruct((t, d), BF16), name="pre_mix_rmsnorm",
        compiler_params=_cparams(1))(x2, g)


def _proj_call(name, xn, w, *, tm, tn, epilogue, out_shape, out_specs, extra=(), extra_specs=()):
    t, k = xn.shape
    n = w.shape[1]
    n_extra = len(extra)

    def body(x_ref, w_ref, *rest):
        x = x_ref[...]
        for c0 in range(0, tn, PROJ_SUBTILE):
            acc = jnp.dot(x, w_ref[:, c0:c0 + PROJ_SUBTILE], preferred_element_type=F32)
            epilogue(acc, c0, rest[:n_extra], rest[n_extra:])

    assert tn % PROJ_SUBTILE == 0
    return pl.pallas_call(
        body, grid=(t // tm, n // tn),
        in_specs=[pl.BlockSpec((tm, k), lambda i, j: (i, 0)),
                  pl.BlockSpec((k, tn), lambda i, j: (0, j)), *extra_specs],
        out_specs=out_specs, out_shape=out_shape, name=name,
        compiler_params=_cparams(2, VMEM_LIMIT))(xn, w, *extra)


def _rope_tile(a, tab_ref):
    return (a * tab_ref[0] + pltpu.roll(a, LANES - ROPE_DIM // 2, 1) * tab_ref[1]
            + pltpu.roll(a, ROPE_DIM // 2, 1) * tab_ref[2])


def _ep_plain(acc, c0, extra, outs):
    outs[0][0, :, c0:c0 + acc.shape[1]] = acc.astype(outs[0].dtype)


def _ep_q(acc, c0, extra, outs):
    for pair in range(acc.shape[1] // LANES):
        r = _rope_tile(acc[:, pair * LANES:(pair + 1) * LANES], extra[0])
        head = 2 * (c0 // LANES + pair)
        outs[0][head] = r[:, :HEAD_DIM]
        outs[0][head + 1] = r[:, HEAD_DIM:]


def _ep_cmp(acc, c0, extra, outs):
    for hh in range(acc.shape[1] // HEAD_DIM):
        head = c0 // HEAD_DIM + hh
        outs[0][head // B_KV_HEADS, head % B_KV_HEADS] = acc[:, hh * HEAD_DIM:(hh + 1) * HEAD_DIM]


def _ep_kv(acc, c0, extra, outs, *, tm):
    i = pl.program_id(0)
    j = pl.program_id(1)
    rowg = i * tm + lax.broadcasted_iota(jnp.int32, (tm, LANES), 0)
    lane = lax.broadcasted_iota(jnp.int32, (tm, LANES), 1)
    aux = jnp.where(j == 0, (rowg // SLC_BLOCK == lane).astype(F32), 0.0).astype(BF16)
    for hh in range(acc.shape[1] // LANES):
        h = c0 // LANES + hh
        r = _rope_tile(acc[:, hh * LANES:(hh + 1) * LANES], extra[0])
        outs[0][0, h, :, 0:LANES] = r.astype(BF16)
        outs[0][0, h, :, LANES:2 * LANES] = aux
        outs[1][0, h] = jnp.where(lane < HEAD_DIM, pltpu.roll(r, HEAD_DIM, 1), 1.0).astype(BF16)


def _ep_gate(acc, c0, extra, outs):
    for hh in range(acc.shape[1] // LANES):
        outs[0][c0 // LANES + hh] = jax.nn.sigmoid(acc[:, hh * LANES:(hh + 1) * LANES])


def _ep_sigmoid(acc, c0, extra, outs):
    outs[0][0, :, c0:c0 + acc.shape[1]] = jax.nn.sigmoid(acc).astype(outs[0].dtype)


def _split3(x):
    hi = x.astype(BF16)
    r1 = x - hi.astype(F32)
    mid = r1.astype(BF16)
    lo = (r1 - mid.astype(F32)).astype(BF16)
    return hi, mid, lo


def _hgrn_kernel(q_ref, f_ref, i_ref, g_ref, lb_ref, gn_ref, o_ref, st_ref, *, chunk, heads):
    c = pl.program_id(1)

    @pl.when(c == 0)
    def _():
        st_ref[...] = jnp.zeros_like(st_ref)

    row = lax.broadcasted_iota(jnp.int32, (chunk, chunk), 0)
    col = lax.broadcasted_iota(jnp.int32, (chunk, chunk), 1)
    tri = (col <= row).astype(BF16)
    rowv = lax.broadcasted_iota(jnp.int32, (chunk, A_DK), 0)

    for hb in range(heads):
        sl = slice(hb * A_DK, (hb + 1) * A_DK)
        q = q_ref[0, :, sl]
        ii = i_ref[0, :, sl]
        gg = g_ref[0, :, sl]
        lbv = lb_ref[:, sl]
        f = lbv + (1.0 - lbv) * jax.nn.sigmoid(f_ref[0, :, sl])
        lf = jnp.log(f)
        k = 1.0 - f
        b = sum(jnp.dot(tri, p, preferred_element_type=F32) for p in _split3(lf)) * LOG2_E
        b_end = b[chunk - 1:chunk, :]
        ii16 = ii.astype(BF16)

        st = st_ref[hb]
        o = lax.dot_general((q * jnp.exp2(b)).astype(BF16), st.astype(BF16), NT_DIMS,
                            preferred_element_type=F32)
        kd = (k * jnp.exp2(b_end - b)).astype(BF16)
        st_ref[hb] = st * jnp.exp2(b_end) + lax.dot_general(ii16, kd, TN_DIMS,
                                                            preferred_element_type=F32)

        att = jnp.where(col == row, jnp.sum(q * k, axis=-1, keepdims=True), 0.0)
        for d in range(1, 8):
            valid = (rowv % 8) >= d
            w = jnp.exp2(b - pltpu.roll(b, d, 0))
            p = jnp.where(valid, q * pltpu.roll(k, d, 0) * w, 0.0)
            att = att + jnp.where(col == row - d, jnp.sum(p, axis=-1, keepdims=True), 0.0)
        m = 8
        while m < chunk:
            grp = chunk // (2 * m)
            b3 = b.reshape(grp, 2 * m, A_DK)
            refrow = jnp.broadcast_to(b3[:, m - 1:m, :], (grp, 2 * m, A_DK)).reshape(chunk, A_DK)
            second = (rowv % (2 * m)) >= m
            ql = jnp.where(second, q * jnp.exp2(b - refrow), 0.0)
            kl = jnp.where(second, 0.0, k * jnp.exp2(refrow - b))
            a = lax.dot_general(ql.astype(BF16), kl.astype(BF16), NT_DIMS, preferred_element_type=F32)
            if grp > 1:
                a = jnp.where(row // (2 * m) == col // (2 * m), a, 0.0)
            att = att + a
            m *= 2
        o = o + jnp.dot(att.astype(BF16), ii16, preferred_element_type=F32)

        o = o * lax.rsqrt(jnp.mean(o * o, axis=-1, keepdims=True) + EPS)
        o_ref[:, sl] = ((o * gn_ref[:, sl]) * (gg * jax.nn.sigmoid(gg))).astype(o_ref.dtype)


def _hgrn_call(proj4, lb, gn, *, chunk=64, heads=8):
    _, t, _ = proj4.shape
    w = heads * A_DK

    def spec(kind):
        return pl.BlockSpec((1, chunk, w), lambda h, c, kind=kind: (kind, c, h))

    vec = pl.BlockSpec((1, w), lambda h, c: (0, h))
    return pl.pallas_call(
        functools.partial(_hgrn_kernel, chunk=chunk, heads=heads),
        grid=(A_HEADS // heads, t // chunk),
        in_specs=[spec(0), spec(1), spec(2), spec(3), vec, vec],
        out_specs=pl.BlockSpec((chunk, w), lambda h, c: (c, h)),
        out_shape=jax.ShapeDtypeStruct((t, D_A), BF16),
        scratch_shapes=[pltpu.VMEM((heads, A_DK, A_DK), F32)], name="hgrn2_scan",
        compiler_params=_cparams(2))(proj4, proj4, proj4, proj4, lb, gn)


def _compress_kernel(x_ref, pe_ref, w1_ref, b1_ref, w2_ref, b2_ref, tab_ref, o_ref):
    half = CMP_BLOCK * HEAD_DIM // 2
    x = x_ref[0, 0]
    top = jnp.dot((x + pe_ref[0, 0:1, :]).astype(BF16), w1_ref[0, :half, :], preferred_element_type=F32)
    bot = jnp.dot((x + pe_ref[0, 1:2, :]).astype(BF16), w1_ref[0, half:, :], preferred_element_type=F32)
    n = x.shape[0]
    h = jax.nn.gelu(top + pltpu.roll(bot, n - 1, 0) + b1_ref[0])
    y = jnp.dot(h.astype(BF16), w2_ref[0], preferred_element_type=F32) + b2_ref[0]
    y = _rope_tile(y, tab_ref)
    hi = y.astype(BF16)
    lo = (y - hi.astype(F32)).astype(BF16)
    o_ref[0, 0, :, 0:LANES] = (y + pltpu.roll(y, HEAD_DIM, 1)).astype(BF16)
    o_ref[0, 0, :, LANES:2 * LANES] = lo


def _compress_call(x16, pe2, w1, b1, w2p, b2p, tab):
    _, nh, n, wid = x16.shape
    hid = w1.shape[-1]
    return pl.pallas_call(
        _compress_kernel, grid=(2, nh),
        in_specs=[pl.BlockSpec((1, 1, n, wid), lambda kv, h: (kv, h, 0, 0)),
                  pl.BlockSpec((1, 2, wid), lambda kv, h: (kv, 0, 0)),
                  pl.BlockSpec((1, 2 * wid, hid), lambda kv, h: (kv, 0, 0)),
                  pl.BlockSpec((1, 1, hid), lambda kv, h: (kv, 0, 0)),
                  pl.BlockSpec((1, hid, LANES), lambda kv, h: (kv, 0, 0)),
                  pl.BlockSpec((1, 1, LANES), lambda kv, h: (kv, 0, 0)),
                  pl.BlockSpec((3, n, LANES), lambda kv, h: (kv, 0, 0))],
        out_specs=pl.BlockSpec((1, 1, n, 2 * LANES), lambda kv, h: (kv, h, 0, 0)),
        out_shape=jax.ShapeDtypeStruct((2, nh, n, 2 * LANES), BF16),
        name="compress_mlp", compiler_params=_cparams(2, VMEM_LIMIT))(x16, pe2, w1, b1, w2p, b2p, tab)


def _nsa_select_kernel(q_ref, kc_ref, vc_ref, wt_ref, ocmp_ref, bias_ref, lhs_ref):
    qt = pl.program_id(1)
    q0 = qt * Q_TILE
    rows = B_GROUP * Q_TILE
    n_cmp = kc_ref.shape[2]

    def chains(n_cols):
        t_col = q0 + lax.broadcasted_iota(jnp.int32, (rows, 1), 0) % Q_TILE
        n_idx = lax.broadcasted_iota(jnp.int32, (rows, n_cols), 1)
        vis = n_idx <= (t_col - (CMP_BLOCK - 1)) // CMP_STRIDE
        blk = lax.broadcasted_iota(jnp.int32, (N_BLK_LANES, Q_TILE), 0)
        tok = q0 + lax.broadcasted_iota(jnp.int32, (N_BLK_LANES, Q_TILE), 1)
        cur = tok // SLC_BLOCK
        forced = (blk == 0) | (blk == cur) | (blk == cur - 1)
        candidate = (blk * SLC_BLOCK <= tok) & jnp.logical_not(forced)
        wt = wt_ref[:, 0:n_cols]
        zeros64 = jnp.zeros((rows, HEAD_DIM), BF16)

        for c in range(KV_PER_SELECT_STEP):
            q = q_ref[c * B_GROUP:(c + 1) * B_GROUP].reshape(rows, HEAD_DIM)
            qhi = q.astype(BF16)
            qlo = (q - qhi.astype(F32)).astype(BF16)
            lhs_ref[c, :, 0:HEAD_DIM] = qhi
            lhs_ref[c, :, HEAD_DIM:2 * HEAD_DIM] = qlo
            lhs_ref[c, :, 2 * HEAD_DIM:3 * HEAD_DIM] = qhi
            lhs_ref[c, :, 3 * HEAD_DIM:] = zeros64
            s = lax.dot_general(lhs_ref[c], kc_ref[0, c, 0:n_cols, :], NT_DIMS, preferred_element_type=F32)
            s = jnp.where(vis, s, NEG_INF)
            e = jnp.exp2(s - jnp.max(s, axis=-1, keepdims=True))
            l = jnp.sum(e, axis=-1, keepdims=True)
            p = e * jnp.where(t_col >= CMP_BLOCK - 1, 1.0 / l, 0.0)
            o_cmp = jnp.dot(p.astype(BF16), vc_ref[0, c, 0:n_cols, 0:LANES], preferred_element_type=F32)
            for g in range(B_GROUP):
                c0 = (c * B_GROUP + g) * HEAD_DIM
                ocmp_ref[:, c0:c0 + HEAD_DIM] = o_cmp[g * Q_TILE:(g + 1) * Q_TILE, :HEAD_DIM]

            psum = p[0:Q_TILE] + p[Q_TILE:2 * Q_TILE] + p[2 * Q_TILE:3 * Q_TILE] + p[3 * Q_TILE:]
            ps_hi = psum.astype(BF16)
            ps_lo = (psum - ps_hi.astype(F32)).astype(BF16)
            pslc = (lax.dot_general(wt, ps_hi, NT_DIMS, preferred_element_type=F32)
                    + lax.dot_general(wt, ps_lo, NT_DIMS, preferred_element_type=F32))
            score = jnp.where(candidate, pslc, jnp.where(forced, -jnp.inf, NEG_INF))
            sel_t = forced.astype(F32)
            for _ in range(SLC_TOP_N - SLC_FORCED):
                best = jnp.max(score, axis=0, keepdims=True)
                first = jnp.min(jnp.where(score == best, blk, N_BLK_LANES), axis=0, keepdims=True)
                hit = blk == first
                score = jnp.where(hit, -jnp.inf, score)
                sel_t = jnp.where(hit, 1.0, sel_t)
            bias_ref[c] = jnp.where(sel_t.T > 0.0, 0.0, NEG_INF).astype(BF16)

    early = qt < pl.num_programs(1) // 2
    pl.when(early)(lambda: chains(n_cmp // 2))
    pl.when(jnp.logical_not(early))(lambda: chains(n_cmp))


def _nsa_select_call(q_hm, cmp_aug, wt):
    _, t, _ = q_hm.shape
    n_cmp = cmp_aug.shape[2]
    kvs = KV_PER_SELECT_STEP
    return pl.pallas_call(
        _nsa_select_kernel, grid=(B_KV_HEADS // kvs, t // Q_TILE),
        in_specs=[pl.BlockSpec((kvs * B_GROUP, Q_TILE, HEAD_DIM), lambda p, i: (p, i, 0)),
                  pl.BlockSpec((1, kvs, n_cmp, 2 * LANES), lambda p, i: (0, p, 0, 0)),
                  pl.BlockSpec((1, kvs, n_cmp, 2 * LANES), lambda p, i: (1, p, 0, 0)),
                  pl.BlockSpec((N_BLK_LANES, n_cmp), lambda p, i: (0, 0))],
        out_specs=[pl.BlockSpec((Q_TILE, kvs * B_GROUP * HEAD_DIM), lambda p, i: (i, p)),
                   pl.BlockSpec((kvs, Q_TILE, N_BLK_LANES), lambda p, i: (p, i, 0))],
        out_shape=[jax.ShapeDtypeStruct((t, D_B), F32),
                   jax.ShapeDtypeStruct((B_KV_HEADS, t, N_BLK_LANES), BF16)],
        scratch_shapes=[pltpu.VMEM((kvs, B_GROUP * Q_TILE, 2 * LANES), BF16)], name="nsa_select",
        compiler_params=_cparams(2, VMEM_LIMIT))(q_hm, cmp_aug, cmp_aug, wt)


def _nsa_attend_kernel(q_ref, bias_ref, ocmp_ref, gate_ref, gx_ref, ksv_ref, kwv_ref, vs_ref, vw_ref, o_ref,
                       lhs_w_ref, lhs_s_ref, m_ref, acc_ref, *, key_tile, tiles_per_group):
    q0 = pl.program_id(1) * ATT_TILE
    rows = B_GROUP * ATT_TILE
    chains = range(KV_PER_ATTEND_STEP)
    t_col = q0 + lax.broadcasted_iota(jnp.int32, (rows, 1), 0) % ATT_TILE

    for c in chains:
        qhi = q_ref[c * B_GROUP:(c + 1) * B_GROUP].reshape(rows, HEAD_DIM).astype(BF16)
        lhs_s_ref[c, :, 0:HEAD_DIM] = qhi
        lhs_s_ref[c, :, HEAD_DIM:LANES] = jnp.zeros((rows, LANES - HEAD_DIM), BF16)
        for g in range(B_GROUP):
            lhs_s_ref[c, g * ATT_TILE:(g + 1) * ATT_TILE, LANES:] = bias_ref[c]
        lhs_w_ref[c, :, 0:HEAD_DIM] = qhi
        lhs_w_ref[c, :, HEAD_DIM:] = jnp.zeros((rows, 2 * LANES - HEAD_DIM), BF16)

    for c in chains:
        m_ref[c] = jnp.full((rows, LANES), NEG_INF, F32)
        acc_ref[c] = jnp.zeros((rows, LANES), F32)

    def tile(ti, causal):
        start = pl.multiple_of(ti * key_tile, key_tile)
        for c in chains:
            st = lax.dot_general(lhs_s_ref[c], ksv_ref[0, c, pl.ds(start, key_tile), :], NT_DIMS,
                                 preferred_element_type=F32)
            if causal:
                kp = start + lax.broadcasted_iota(jnp.int32, (rows, key_tile), 1)
                st = jnp.where(kp <= t_col, st, NEG_INF)
            m_run = m_ref[c]
            m_new = jnp.maximum(m_run, jnp.max(st, axis=-1, keepdims=True))
            pt = jnp.exp2(st - jnp.concatenate([m_new] * (key_tile // LANES), axis=1))
            pv = jnp.dot(pt.astype(BF16), vs_ref[0, c, pl.ds(start, key_tile), :], preferred_element_type=F32)
            acc_ref[c] = acc_ref[c] * jnp.exp2(m_run - m_new) + pv
            m_ref[c] = m_new

    def group(gi, _):
        for u in range(tiles_per_group):
            tile(gi * tiles_per_group + u, False)
        return 0

    def single(ti, _):
        tile(ti, False)
        return 0

    n_full = q0 // key_tile
    n_groups = n_full // tiles_per_group
    lax.fori_loop(0, n_groups, group, 0)
    lax.fori_loop(n_groups * tiles_per_group, n_full, single, 0)
    tile(n_full, True)

    slab = WINDOW + ATT_TILE
    ws = pl.multiple_of(jnp.maximum(q0 - WINDOW, 0), ATT_TILE)
    kpos = ws + lax.broadcasted_iota(jnp.int32, (rows, slab), 1)
    in_window = lax.bitcast_convert_type(t_col - kpos, jnp.uint32) < jnp.uint32(WINDOW)
    lane = lax.broadcasted_iota(jnp.int32, (ATT_TILE, LANES), 1)
    per_branch = B_GROUP * HEAD_DIM

    def head_pair(acc, k):
        a_even = acc[2 * k * ATT_TILE:(2 * k + 1) * ATT_TILE]
        a_odd = acc[(2 * k + 1) * ATT_TILE:(2 * k + 2) * ATT_TILE]
        num = jnp.where(lane < HEAD_DIM, a_even, pltpu.roll(a_odd, HEAD_DIM, 1))
        den = jnp.where(lane < HEAD_DIM, pltpu.roll(a_even, HEAD_DIM, 1), a_odd)
        return num * (1.0 / den)

    for c in chains:
        acc_s = acc_ref[c]
        s = lax.dot_general(lhs_w_ref[c], kwv_ref[0, c, pl.ds(ws, slab), :], NT_DIMS,
                            preferred_element_type=F32)
        s = jnp.where(in_window, s, NEG_INF)
        e = jnp.exp2(s - jnp.max(s, axis=-1, keepdims=True))
        acc_w = jnp.dot(e.astype(BF16), vw_ref[0, c, pl.ds(ws, slab), :], preferred_element_type=F32)

        gates = gate_ref[c]
        g_hi = gates.astype(BF16)
        g_lo = (gates - g_hi.astype(F32)).astype(BF16)
        gx = (jnp.dot(g_hi, gx_ref[...], preferred_element_type=F32)
              + jnp.dot(g_lo, gx_ref[...], preferred_element_type=F32))
        for k in range(B_GROUP // 2):
            c0 = c * B_GROUP * HEAD_DIM + k * LANES
            g0 = k * LANES
            og = (gx[:, g0:g0 + LANES] * ocmp_ref[:, c0:c0 + LANES]
                  + gx[:, per_branch + g0:per_branch + g0 + LANES] * head_pair(acc_s, k)
                  + gx[:, 2 * per_branch + g0:2 * per_branch + g0 + LANES] * head_pair(acc_w, k))
            o_ref[:, c0:c0 + LANES] = og.astype(o_ref.dtype)


def _nsa_attend_call(q_hm, bias, ocmp, gates, kv_aug, v_aug, *, key_tile=512, tiles_per_group=2):
    _, t, _ = q_hm.shape
    assert t % key_tile == 0 and key_tile % ATT_TILE == 0 and t % ATT_TILE == 0
    rows = B_GROUP * ATT_TILE
    kvs = KV_PER_ATTEND_STEP
    wide = kvs * B_GROUP * HEAD_DIM
    resident = dict(pipeline_mode=pl.Buffered(1))
    return pl.pallas_call(
        functools.partial(_nsa_attend_kernel, key_tile=key_tile, tiles_per_group=tiles_per_group),
        grid=(B_KV_HEADS // kvs, t // ATT_TILE),
        in_specs=[pl.BlockSpec((kvs * B_GROUP, ATT_TILE, HEAD_DIM), lambda p, i: (p, i, 0)),
                  pl.BlockSpec((kvs, ATT_TILE, N_BLK_LANES), lambda p, i: (p, i, 0)),
                  pl.BlockSpec((ATT_TILE, wide), lambda p, i: (i, p)),
                  pl.BlockSpec((kvs, ATT_TILE, LANES), lambda p, i: (p, i, 0)),
                  pl.BlockSpec((LANES, 3 * B_GROUP * HEAD_DIM), lambda p, i: (0, 0)),
                  pl.BlockSpec((1, kvs, t, 2 * LANES), lambda p, i: (0, p, 0, 0), **resident),
                  pl.BlockSpec((1, kvs, t, 2 * LANES), lambda p, i: (1, p, 0, 0), **resident),
                  pl.BlockSpec((1, kvs, t, LANES), lambda p, i: (0, p, 0, 0), **resident),
                  pl.BlockSpec((1, kvs, t, LANES), lambda p, i: (1, p, 0, 0), **resident)],
        out_specs=pl.BlockSpec((ATT_TILE, wide), lambda p, i: (i, p)),
        out_shape=jax.ShapeDtypeStruct((t, D_B), BF16),
        scratch_shapes=[pltpu.VMEM((kvs, rows, 2 * LANES), BF16), pltpu.VMEM((kvs, rows, 2 * LANES), BF16),
                        pltpu.VMEM((kvs, rows, LANES), F32), pltpu.VMEM((kvs, rows, LANES), F32)],
        name="nsa_attend",
        compiler_params=_cparams(2, VMEM_LIMIT))(q_hm, bias, ocmp, gates, _gate_expand_matrix(),
                                                 kv_aug, kv_aug, v_aug, v_aug)


def _post_kernel(oa_ref, ob_ref, sg_ref, x_ref, wa_ref, wb_ref, wo_ref, g_ref, o_ref):
    ya = jnp.dot(oa_ref[...], wa_ref[...], preferred_element_type=F32)
    yb = jnp.dot(ob_ref[...], wb_ref[...], preferred_element_type=F32)
    merged = sg_ref[0].astype(F32) * ya + sg_ref[1].astype(F32) * yb
    mix = jnp.dot(merged.astype(BF16), wo_ref[...], preferred_element_type=F32)
    y = mix * lax.rsqrt(jnp.mean(mix * mix, axis=-1, keepdims=True) + EPS)
    o_ref[...] = x_ref[...] + y * g_ref[...]


def _post_call(oa, ob, sg, x2, wa, wb, wo, g, tm=256):
    t, d = x2.shape
    const = dict(pipeline_mode=pl.Buffered(1))
    return pl.pallas_call(
        _post_kernel, grid=(t // tm,),
        in_specs=[pl.BlockSpec((tm, D_A), lambda i: (i, 0)),
                  pl.BlockSpec((tm, D_B), lambda i: (i, 0)),
                  pl.BlockSpec((2, tm, d), lambda i: (0, i, 0)),
                  pl.BlockSpec((tm, d), lambda i: (i, 0)),
                  pl.BlockSpec((D_A, d), lambda i: (0, 0), **const),
                  pl.BlockSpec((D_B, d), lambda i: (0, 0), **const),
                  pl.BlockSpec((d, d), lambda i: (0, 0), **const),
                  pl.BlockSpec((1, d), lambda i: (0, 0))],
        out_specs=pl.BlockSpec((tm, d), lambda i: (i, 0)),
        out_shape=jax.ShapeDtypeStruct((t, d), F32), name="mix_out_residual",
        compiler_params=_cparams(1, VMEM_LIMIT))(oa, ob, sg, x2, wa, wb, wo, g)


def _ffn_kernel(h_ref, g1_ref, wg_ref, wu_ref, wd_ref, g2_ref, o_ref, hn_ref, acc_ref):
    f = pl.program_id(1)

    @pl.when(f == 0)
    def _():
        h = h_ref[...]
        y = h * lax.rsqrt(jnp.mean(h * h, axis=-1, keepdims=True) + EPS)
        hn_ref[...] = (y * g1_ref[...]).astype(BF16)
        acc_ref[...] = jnp.zeros_like(acc_ref)

    hn = hn_ref[...]
    a = jnp.dot(hn, wg_ref[...], preferred_element_type=F32)
    u = jnp.dot(hn, wu_ref[...], preferred_element_type=F32)
    z = (a * jax.nn.sigmoid(a) * u).astype(BF16)
    acc_ref[...] += jnp.dot(z, wd_ref[...], preferred_element_type=F32)

    @pl.when(f == pl.num_programs(1) - 1)
    def _():
        ff = acc_ref[...]
        y = ff * lax.rsqrt(jnp.mean(ff * ff, axis=-1, keepdims=True) + EPS)
        o_ref[...] = h_ref[...] + y * g2_ref[...]


def _ffn_call(h1, g1, wg, wu, wd, g2, tm=512, tf=512):
    t, d = h1.shape
    dff = wg.shape[1]
    return pl.pallas_call(
        _ffn_kernel, grid=(t // tm, dff // tf),
        in_specs=[pl.BlockSpec((tm, d), lambda i, f: (i, 0)),
                  pl.BlockSpec((1, d), lambda i, f: (0, 0)),
                  pl.BlockSpec((d, tf), lambda i, f: (0, f)),
                  pl.BlockSpec((d, tf), lambda i, f: (0, f)),
                  pl.BlockSpec((tf, d), lambda i, f: (f, 0)),
                  pl.BlockSpec((1, d), lambda i, f: (0, 0))],
        out_specs=pl.BlockSpec((tm, d), lambda i, f: (i, 0)),
        out_shape=jax.ShapeDtypeStruct((t, d), F32),
        scratch_shapes=[pltpu.VMEM((tm, d), BF16), pltpu.VMEM((tm, d), F32)], name="swiglu_ffn",
        compiler_params=_cparams(2, VMEM_LIMIT))(h1, g1, wg, wu, wd, g2)


def _rope_tables(pos, period, scale=1.0):
    half = ROPE_DIM // 2
    inv = jnp.float32(ROPE_THETA) ** (-jnp.arange(half, dtype=F32) * 2.0 / ROPE_DIM)
    ang = pos.astype(F32)[:, None] * inv[None, :]
    cos, sin = jnp.cos(ang), jnp.sin(ang)
    r = np.arange(LANES) % period
    f = r % half
    lo = jnp.asarray((r < half)[None, :])
    hi = jnp.asarray(((r >= half) & (r < ROPE_DIM))[None, :])
    c = jnp.where(lo | hi, cos[:, f], 1.0)
    sa = jnp.where(lo, -sin[:, f], 0.0)
    sb = jnp.where(hi, sin[:, f], 0.0)
    return jnp.stack([c, sa, sb]) * scale


def _gate_expand_matrix():
    e = np.zeros((LANES, 3 * B_GROUP * HEAD_DIM), np.float32)
    for g in range(B_GROUP):
        for b in range(3):
            c0 = (b * B_GROUP + g) * HEAD_DIM
            e[3 * g + b, c0:c0 + HEAD_DIM] = 1.0
    return jnp.asarray(e, BF16)


def _slc_weight_matrix(n_cmp_pad):
    ratio = SLC_BLOCK // CMP_STRIDE
    w = np.zeros((N_BLK_LANES, n_cmp_pad), np.float32)
    for j in range(N_BLK_LANES):
        for o, wv in enumerate(SLC_WEIGHTS):
            n = ratio * j + o - 1
            if 0 <= n < n_cmp_pad - 1:
                w[j, n] = wv
    return jnp.asarray(w, BF16)


def kernel(x, pre_mix_g, w_in, lb_table, a_norm_g, cmp_pe_k, cmp_w1_k, cmp_b1_k, cmp_w2_k, cmp_b2_k, cmp_pe_v, cmp_w1_v, cmp_b1_v, cmp_w2_v, cmp_b2_v, w_proj_a, w_proj_b, w_out, post_mix_g, pre_ffn_g, w_gate, w_up, w_down, post_ffn_g):
    bsz, t, d = x.shape
    assert bsz == 1 and d == D_MODEL and WINDOW + ATT_TILE <= t <= N_BLK_LANES * SLC_BLOCK
    x2 = x.reshape(t, d)
    n_cmp_pad = t // CMP_STRIDE
    assert n_cmp_pad % LANES == 0

    lb = jnp.cumsum(jax.nn.softmax(lb_table.astype(F32), axis=0), axis=0)[0].reshape(1, D_A)
    w = w_in[0]
    o_q, o_kc, o_ks, o_g, o_m = 4 * D_A, 4 * D_A + D_B, 4 * D_A + D_B + 2 * KV_DIM, 4 * D_A + D_B + 6 * KV_DIM, 4 * D_A + D_B + 6 * KV_DIM + 3 * B_HEADS
    w_hgrn = w[:, :o_q].astype(BF16)
    w_q = w[:, o_q:o_kc].astype(BF16)
    w_cmp = w[:, o_kc:o_ks].astype(BF16)
    w_kv = (w[:, o_ks:o_g].reshape(d, 2, 2, B_KV_HEADS, HEAD_DIM).transpose(0, 1, 3, 2, 4)
            .reshape(d, 4 * KV_DIM).astype(BF16))
    w_gate3 = w[:, o_g:o_m].reshape(d, B_KV_HEADS, 3 * B_GROUP)
    w_g = jnp.pad(w_gate3, ((0, 0), (0, 0), (0, LANES - 3 * B_GROUP))).reshape(d, B_KV_HEADS * LANES).astype(BF16)
    w_m = w[:, o_m:].astype(BF16)

    pos = jnp.arange(t)
    tab_q = _rope_tables(pos, HEAD_DIM, HEAD_DIM ** -0.5 * LOG2_E)
    tab_kv = _rope_tables(pos, LANES)
    end_pos = jnp.arange(n_cmp_pad) * CMP_STRIDE + (CMP_BLOCK - 1)
    ident = jnp.stack([jnp.ones((n_cmp_pad, LANES), F32), jnp.zeros((n_cmp_pad, LANES), F32),
                       jnp.zeros((n_cmp_pad, LANES), F32)])
    tab_cmp = jnp.concatenate([_rope_tables(end_pos, LANES), ident], axis=0)

    xn = _rmsnorm_call(x2, pre_mix_g[0].reshape(1, d))

    tm = 1024 if t % 1024 == 0 else 256
    proj4 = _proj_call(
        "proj_hgrn", xn, w_hgrn, tm=tm, tn=512, epilogue=_ep_plain,
        out_shape=jax.ShapeDtypeStruct((4, t, D_A), F32),
        out_specs=pl.BlockSpec((1, tm, 512), lambda i, j: (j // 2, i, j % 2)))
    q_hm = _proj_call(
        "proj_q", xn, w_q, tm=tm, tn=512, epilogue=_ep_q,
        out_shape=jax.ShapeDtypeStruct((B_HEADS, t, HEAD_DIM), F32),
        out_specs=pl.BlockSpec((512 // HEAD_DIM, tm, HEAD_DIM), lambda i, j: (j, i, 0)),
        extra=(tab_q,), extra_specs=(pl.BlockSpec((3, tm, LANES), lambda i, j: (0, i, 0)),))
    cmp_in = _proj_call(
        "proj_cmp", xn, w_cmp, tm=tm, tn=2 * KV_DIM, epilogue=_ep_cmp,
        out_shape=jax.ShapeDtypeStruct((2, B_KV_HEADS, t, HEAD_DIM), F32),
        out_specs=pl.BlockSpec((2, B_KV_HEADS, tm, HEAD_DIM), lambda i, j: (0, 0, i, 0)))
    kv_aug, v_aug = _proj_call(
        "proj_kv", xn, w_kv, tm=tm, tn=2 * KV_DIM, epilogue=functools.partial(_ep_kv, tm=tm),
        out_shape=[jax.ShapeDtypeStruct((2, B_KV_HEADS, t, 2 * LANES), BF16),
                   jax.ShapeDtypeStruct((2, B_KV_HEADS, t, LANES), BF16)],
        out_specs=[pl.BlockSpec((1, B_KV_HEADS, tm, 2 * LANES), lambda i, j: (j, 0, i, 0)),
                   pl.BlockSpec((1, B_KV_HEADS, tm, LANES), lambda i, j: (j, 0, i, 0))],
        extra=(tab_kv,), extra_specs=(pl.BlockSpec((3, tm, LANES), lambda i, j: (0, i, 0)),))
    gates = _proj_call(
        "proj_gate", xn, w_g, tm=tm, tn=B_KV_HEADS * LANES, epilogue=_ep_gate,
        out_shape=jax.ShapeDtypeStruct((B_KV_HEADS, t, LANES), F32),
        out_specs=pl.BlockSpec((B_KV_HEADS, tm, LANES), lambda i, j: (0, i, 0)))
    sg = _proj_call(
        "proj_merge", xn, w_m, tm=tm, tn=512, epilogue=_ep_sigmoid,
        out_shape=jax.ShapeDtypeStruct((2, t, d), BF16),
        out_specs=pl.BlockSpec((1, tm, 512), lambda i, j: (j // 4, i, j % 4)))

    oa = _hgrn_call(proj4, lb, a_norm_g[0].reshape(1, D_A))

    half = CMP_BLOCK // 2
    x16 = cmp_in.reshape(2, B_KV_HEADS, n_cmp_pad, CMP_STRIDE * HEAD_DIM)
    pe2 = jnp.stack([cmp_pe_k[0], cmp_pe_v[0]]).reshape(2, 2, half * HEAD_DIM)
    w1 = jnp.stack([cmp_w1_k[0], cmp_w1_v[0]]).astype(BF16)
    b1 = jnp.stack([cmp_b1_k[0], cmp_b1_v[0]]).reshape(2, 1, CMP_HIDDEN)
    w2p = jnp.pad(jnp.stack([cmp_w2_k[0], cmp_w2_v[0]]), ((0, 0), (0, 0), (0, LANES - HEAD_DIM))).astype(BF16)
    b2p = jnp.pad(jnp.stack([cmp_b2_k[0], cmp_b2_v[0]]), ((0, 0), (0, LANES - HEAD_DIM))).reshape(2, 1, LANES)
    cmp_aug = _compress_call(x16, pe2, w1, b1, w2p, b2p, tab_cmp)
    ocmp, bias = _nsa_select_call(q_hm, cmp_aug, _slc_weight_matrix(n_cmp_pad))
    ob = _nsa_attend_call(q_hm, bias, ocmp, gates, kv_aug, v_aug)

    h1 = _post_call(oa, ob, sg, x2, w_proj_a[0].astype(BF16), w_proj_b[0].astype(BF16),
                    w_out[0].astype(BF16), post_mix_g[0].reshape(1, d))
    out = _ffn_call(h1, pre_ffn_g[0].reshape(1, d), w_gate[0].astype(BF16), w_up[0].astype(BF16),
                    w_down[0].astype(BF16), post_ffn_g[0].reshape(1, d))
    return out.reshape(bsz, t, d)
```

```python
import functools

import numpy as np
import jax
import jax.numpy as jnp
from jax import lax
from jax.experimental import pallas as pl
from jax.experimental.pallas import tpu as pltpu

F32 = jnp.float32
BF16 = jnp.bfloat16

D_MODEL = 2048
D_A = 1024
D_B = 1024
A_HEADS = 8
A_DK = 128
B_HEADS = 16
B_KV_HEADS = 4
B_GROUP = 4
HEAD_DIM = 64
KV_DIM = 256
CMP_BLOCK = 32
CMP_STRIDE = 16
CMP_HIDDEN = 256
SLC_BLOCK = 64
SLC_TOP_N = 16
SLC_FORCED = 3
SLC_WEIGHTS = (1.0, 2.0, 2.0, 2.0, 1.0)
WINDOW = 512
Q_TILE = 256
ATT_TILE = 256
KV_PER_ATTEND_STEP = 2
KV_PER_SELECT_STEP = 2
PROJ_SUBTILE = 256
ROPE_THETA = 500000.0
ROPE_DIM = 16
D_FF = 5632
EPS = 1e-6
LOG2_E = 1.4426950408889634
NEG_INF = -1e30
N_BLK_LANES = 128
LANES = 128
VMEM_LIMIT = 56 * 1024 * 1024

NT_DIMS = (((1,), (1,)), ((), ()))
TN_DIMS = (((0,), (0,)), ((), ()))


def _cparams(n_axes, vmem=None):
    return pltpu.CompilerParams(dimension_semantics=("arbitrary",) * n_axes,
                                vmem_limit_bytes=vmem)


def _rmsnorm_kernel(x_ref, g_ref, o_ref):
    x = x_ref[...]
    y = x * lax.rsqrt(jnp.mean(x * x, axis=-1, keepdims=True) + EPS)
    o_ref[...] = (y * g_ref[...]).astype(o_ref.dtype)


def _rmsnorm_call(x2, g, tm=512):
    t, d = x2.shape
    return pl.pallas_call(
        _rmsnorm_kernel, grid=(t // tm,),
        in_specs=[pl.BlockSpec((tm, d), lambda i: (i, 0)), pl.BlockSpec((1, d), lambda i: (0, 0))],
        out_specs=pl.BlockSpec((tm, d), lambda i: (i, 0)),
        out_shape=jax.ShapeDtypeStruct((t, d), BF16), name="pre_mix_rmsnorm",
        compiler_params=_cparams(1))(x2, g)


def _proj_call(name, xn, w, *, tm, tn, epilogue, out_shape, out_specs, extra=(), extra_specs=()):
    t, k = xn.shape
    n = w.shape[1]
    n_extra = len(extra)

    def body(x_ref, w_ref, *rest):
        x = x_ref[...]
        for c0 in range(0, tn, PROJ_SUBTILE):
            acc = jnp.dot(x, w_ref[:, c0:c0 + PROJ_SUBTILE], preferred_element_type=F32)
            epilogue(acc, c0, rest[:n_extra], rest[n_extra:])

    assert tn % PROJ_SUBTILE == 0
    return pl.pallas_call(
        body, grid=(t // tm, n // tn),
        in_specs=[pl.BlockSpec((tm, k), lambda i, j: (i, 0)),
                  pl.BlockSpec((k, tn), lambda i, j: (0, j)), *extra_specs],
        out_specs=out_specs, out_shape=out_shape, name=name,
        compiler_params=_cparams(2, VMEM_LIMIT))(xn, w, *extra)


def _rope_tile(a, tab_ref):
    return (a * tab_ref[0] + pltpu.roll(a, LANES - ROPE_DIM // 2, 1) * tab_ref[1]
            + pltpu.roll(a, ROPE_DIM // 2, 1) * tab_ref[2])


def _ep_plain(acc, c0, extra, outs):
    outs[0][0, :, c0:c0 + acc.shape[1]] = acc.astype(outs[0].dtype)


def _ep_q(acc, c0, extra, outs):
    for pair in range(acc.shape[1] // LANES):
        outs[0][c0 // LANES + pair] = _rope_tile(acc[:, pair * LANES:(pair + 1) * LANES], extra[0])


def _ep_cmp(acc, c0, extra, outs):
    for hh in range(acc.shape[1] // HEAD_DIM):
        head = c0 // HEAD_DIM + hh
        outs[0][head // B_KV_HEADS, head % B_KV_HEADS] = acc[:, hh * HEAD_DIM:(hh + 1) * HEAD_DIM]


def _ep_kv(acc, c0, extra, outs, *, tm):
    i = pl.program_id(0)
    j = pl.program_id(1)
    rowg = i * tm + lax.broadcasted_iota(jnp.int32, (tm, LANES), 0)
    lane = lax.broadcasted_iota(jnp.int32, (tm, LANES), 1)
    aux = jnp.where(j == 0, (rowg // SLC_BLOCK == lane).astype(F32), 0.0).astype(BF16)
    for hh in range(acc.shape[1] // LANES):
        h = c0 // LANES + hh
        r = _rope_tile(acc[:, hh * LANES:(hh + 1) * LANES], extra[0])
        outs[0][0, h, :, 0:LANES] = r.astype(BF16)
        outs[0][0, h, :, LANES:2 * LANES] = aux
        outs[1][0, h] = jnp.where(lane < HEAD_DIM, pltpu.roll(r, HEAD_DIM, 1), 1.0).astype(BF16)


def _ep_gate(acc, c0, extra, outs):
    for hh in range(acc.shape[1] // LANES):
        outs[0][c0 // LANES + hh] = jax.nn.sigmoid(acc[:, hh * LANES:(hh + 1) * LANES])


def _ep_sigmoid(acc, c0, extra, outs):
    outs[0][0, :, c0:c0 + acc.shape[1]] = jax.nn.sigmoid(acc).astype(outs[0].dtype)


def _split3(x):
    hi = x.astype(BF16)
    r1 = x - hi.astype(F32)
    mid = r1.astype(BF16)
    lo = (r1 - mid.astype(F32)).astype(BF16)
    return hi, mid, lo


def _hgrn_kernel(q_ref, f_ref, i_ref, g_ref, lb_ref, gn_ref, o_ref, st_ref, *, chunk, heads):
    c = pl.program_id(1)

    @pl.when(c == 0)
    def _():
        st_ref[...] = jnp.zeros_like(st_ref)

    row = lax.broadcasted_iota(jnp.int32, (chunk, chunk), 0)
    col = lax.broadcasted_iota(jnp.int32, (chunk, chunk), 1)
    tri = (col <= row).astype(BF16)
    rowv = lax.broadcasted_iota(jnp.int32, (chunk, A_DK), 0)

    for hb in range(heads):
        sl = slice(hb * A_DK, (hb + 1) * A_DK)
        q = q_ref[0, :, sl]
        ii = i_ref[0, :, sl]
        gg = g_ref[0, :, sl]
        lbv = lb_ref[:, sl]
        f = lbv + (1.0 - lbv) * jax.nn.sigmoid(f_ref[0, :, sl])
        lf = jnp.log(f)
        k = 1.0 - f
        b = sum(jnp.dot(tri, p, preferred_element_type=F32) for p in _split3(lf)) * LOG2_E
        b_end = b[chunk - 1:chunk, :]
        ii16 = ii.astype(BF16)

        st = st_ref[hb]
        o = lax.dot_general((q * jnp.exp2(b)).astype(BF16), st.astype(BF16), NT_DIMS,
                            preferred_element_type=F32)
        kd = (k * jnp.exp2(b_end - b)).astype(BF16)
        st_ref[hb] = st * jnp.exp2(b_end) + lax.dot_general(ii16, kd, TN_DIMS,
                                                            preferred_element_type=F32)

        att = jnp.where(col == row, jnp.sum(q * k, axis=-1, keepdims=True), 0.0)
        for d in range(1, 8):
            valid = (rowv % 8) >= d
            w = jnp.exp2(b - pltpu.roll(b, d, 0))
            p = jnp.where(valid, q * pltpu.roll(k, d, 0) * w, 0.0)
            att = att + jnp.where(col == row - d, jnp.sum(p, axis=-1, keepdims=True), 0.0)
        m = 8
        while m < chunk:
            grp = chunk // (2 * m)
            b3 = b.reshape(grp, 2 * m, A_DK)
            refrow = jnp.broadcast_to(b3[:, m - 1:m, :], (grp, 2 * m, A_DK)).reshape(chunk, A_DK)
            second = (rowv % (2 * m)) >= m
            ql = jnp.where(second, q * jnp.exp2(b - refrow), 0.0)
            kl = jnp.where(second, 0.0, k * jnp.exp2(refrow - b))
            a = lax.dot_general(ql.astype(BF16), kl.astype(BF16), NT_DIMS, preferred_element_type=F32)
            if grp > 1:
                a = jnp.where(row // (2 * m) == col // (2 * m), a, 0.0)
            att = att + a
            m *= 2
        o = o + jnp.dot(att.astype(BF16), ii16, preferred_element_type=F32)

        o = o * lax.rsqrt(jnp.mean(o * o, axis=-1, keepdims=True) + EPS)
        o_ref[:, sl] = ((o * gn_ref[:, sl]) * (gg * jax.nn.sigmoid(gg))).astype(o_ref.dtype)


def _hgrn_call(proj4, lb, gn, *, chunk=128, heads=8):
    _, t, _ = proj4.shape
    w = heads * A_DK

    def spec(kind):
        return pl.BlockSpec((1, chunk, w), lambda h, c, kind=kind: (kind, c, h))

    vec = pl.BlockSpec((1, w), lambda h, c: (0, h))
    return pl.pallas_call(
        functools.partial(_hgrn_kernel, chunk=chunk, heads=heads),
        grid=(A_HEADS // heads, t // chunk),
        in_specs=[spec(0), spec(1), spec(2), spec(3), vec, vec],
        out_specs=pl.BlockSpec((chunk, w), lambda h, c: (c, h)),
        out_shape=jax.ShapeDtypeStruct((t, D_A), BF16),
        scratch_shapes=[pltpu.VMEM((heads, A_DK, A_DK), F32)], name="hgrn2_scan",
        compiler_params=_cparams(2))(proj4, proj4, proj4, proj4, lb, gn)


def _compress_kernel(x_ref, pe_ref, w1_ref, b1_ref, w2_ref, b2_ref, tab_ref, o_ref):
    half = CMP_BLOCK * HEAD_DIM // 2
    x = x_ref[0, 0]
    top = jnp.dot((x + pe_ref[0, 0:1, :]).astype(BF16), w1_ref[0, :half, :], preferred_element_type=F32)
    bot = jnp.dot((x + pe_ref[0, 1:2, :]).astype(BF16), w1_ref[0, half:, :], preferred_element_type=F32)
    n = x.shape[0]
    h = jax.nn.gelu(top + pltpu.roll(bot, n - 1, 0) + b1_ref[0])
    y = jnp.dot(h.astype(BF16), w2_ref[0], preferred_element_type=F32) + b2_ref[0]
    y = _rope_tile(y, tab_ref)
    hi = y.astype(BF16)
    lo = (y - hi.astype(F32)).astype(BF16)
    o_ref[0, 0, :, 0:LANES] = (y + pltpu.roll(y, HEAD_DIM, 1)).astype(BF16)
    o_ref[0, 0, :, LANES:2 * LANES] = lo


def _compress_call(x16, pe2, w1, b1, w2p, b2p, tab):
    _, nh, n, wid = x16.shape
    hid = w1.shape[-1]
    return pl.pallas_call(
        _compress_kernel, grid=(2, nh),
        in_specs=[pl.BlockSpec((1, 1, n, wid), lambda kv, h: (kv, h, 0, 0)),
                  pl.BlockSpec((1, 2, wid), lambda kv, h: (kv, 0, 0)),
                  pl.BlockSpec((1, 2 * wid, hid), lambda kv, h: (kv, 0, 0)),
                  pl.BlockSpec((1, 1, hid), lambda kv, h: (kv, 0, 0)),
                  pl.BlockSpec((1, hid, LANES), lambda kv, h: (kv, 0, 0)),
                  pl.BlockSpec((1, 1, LANES), lambda kv, h: (kv, 0, 0)),
                  pl.BlockSpec((3, n, LANES), lambda kv, h: (kv, 0, 0))],
        out_specs=pl.BlockSpec((1, 1, n, 2 * LANES), lambda kv, h: (kv, h, 0, 0)),
        out_shape=jax.ShapeDtypeStruct((2, nh, n, 2 * LANES), BF16),
        name="compress_mlp", compiler_params=_cparams(2, VMEM_LIMIT))(x16, pe2, w1, b1, w2p, b2p, tab)


def _nsa_select_kernel(q_ref, kc_ref, vc_ref, wt_ref, ocmp_ref, bias_ref, lhs_ref):
    qt = pl.program_id(1)
    q0 = qt * Q_TILE
    rows = B_GROUP * Q_TILE
    n_cmp = kc_ref.shape[2]

    def chains(n_cols, n_blk):
        t_col = q0 + lax.broadcasted_iota(jnp.int32, (rows, 1), 0) % Q_TILE
        n_idx = lax.broadcasted_iota(jnp.int32, (rows, n_cols), 1)
        vis = n_idx <= (t_col - (CMP_BLOCK - 1)) // CMP_STRIDE
        blk = lax.broadcasted_iota(jnp.int32, (n_blk, Q_TILE), 0)
        tok = q0 + lax.broadcasted_iota(jnp.int32, (n_blk, Q_TILE), 1)
        cur = tok // SLC_BLOCK
        forced = (blk == 0) | (blk == cur) | (blk == cur - 1)
        candidate = (blk * SLC_BLOCK <= tok) & jnp.logical_not(forced)
        wt = wt_ref[0:n_blk, 0:n_cols]
        zeros64 = jnp.zeros((rows, HEAD_DIM), BF16)
        never = jnp.zeros((N_BLK_LANES - n_blk, Q_TILE), F32)

        for c in range(KV_PER_SELECT_STEP):
            for pair in range(B_GROUP // 2):
                qp = q_ref[c * (B_GROUP // 2) + pair]
                hi = qp.astype(BF16)
                lo = (qp - hi.astype(F32)).astype(BF16)
                for half in range(2):
                    r = slice((2 * pair + half) * Q_TILE, (2 * pair + half + 1) * Q_TILE)
                    ln = slice(half * HEAD_DIM, (half + 1) * HEAD_DIM)
                    lhs_ref[c, r, 0:HEAD_DIM] = hi[:, ln]
                    lhs_ref[c, r, HEAD_DIM:2 * HEAD_DIM] = lo[:, ln]
                    lhs_ref[c, r, 2 * HEAD_DIM:3 * HEAD_DIM] = hi[:, ln]
            lhs_ref[c, :, 3 * HEAD_DIM:] = zeros64
            s = lax.dot_general(lhs_ref[c], kc_ref[0, c, 0:n_cols, :], NT_DIMS, preferred_element_type=F32)
            s = jnp.where(vis, s, NEG_INF)
            e = jnp.exp2(s - jnp.max(s, axis=-1, keepdims=True))
            l = jnp.sum(e, axis=-1, keepdims=True)
            p = e * jnp.where(t_col >= CMP_BLOCK - 1, 1.0 / l, 0.0)
            o_cmp = jnp.dot(p.astype(BF16), vc_ref[0, c, 0:n_cols, 0:LANES], preferred_element_type=F32)
            for g in range(B_GROUP):
                c0 = (c * B_GROUP + g) * HEAD_DIM
                ocmp_ref[:, c0:c0 + HEAD_DIM] = o_cmp[g * Q_TILE:(g + 1) * Q_TILE, :HEAD_DIM]

            psum = p[0:Q_TILE] + p[Q_TILE:2 * Q_TILE] + p[2 * Q_TILE:3 * Q_TILE] + p[3 * Q_TILE:]
            ps_hi = psum.astype(BF16)
            ps_lo = (psum - ps_hi.astype(F32)).astype(BF16)
            pslc = (lax.dot_general(wt, ps_hi, NT_DIMS, preferred_element_type=F32)
                    + lax.dot_general(wt, ps_lo, NT_DIMS, preferred_element_type=F32))
            score = jnp.where(candidate, pslc, jnp.where(forced, -jnp.inf, NEG_INF))
            sel_t = forced.astype(F32)
            for _ in range(SLC_TOP_N - SLC_FORCED):
                best = jnp.max(score, axis=0, keepdims=True)
                first = jnp.min(jnp.where(score == best, blk, N_BLK_LANES), axis=0, keepdims=True)
                hit = blk == first
                score = jnp.where(hit, -jnp.inf, score)
                sel_t = jnp.where(hit, 1.0, sel_t)
            if n_blk < N_BLK_LANES:
                sel_t = jnp.concatenate([sel_t, never], axis=0)
            bias_ref[c] = jnp.where(sel_t.T > 0.0, 0.0, NEG_INF).astype(BF16)

    ratio = SLC_BLOCK // CMP_STRIDE
    early = qt < pl.num_programs(1) // 2
    pl.when(early)(lambda: chains(n_cmp // 2, n_cmp // (2 * ratio)))
    pl.when(jnp.logical_not(early))(lambda: chains(n_cmp, N_BLK_LANES))


def _nsa_select_call(q_hm, cmp_aug, wt):
    _, t, _ = q_hm.shape
    n_cmp = cmp_aug.shape[2]
    kvs = KV_PER_SELECT_STEP
    return pl.pallas_call(
        _nsa_select_kernel, grid=(B_KV_HEADS // kvs, t // Q_TILE),
        in_specs=[pl.BlockSpec((kvs * B_GROUP // 2, Q_TILE, LANES), lambda p, i: (p, i, 0)),
                  pl.BlockSpec((1, kvs, n_cmp, 2 * LANES), lambda p, i: (0, p, 0, 0)),
                  pl.BlockSpec((1, kvs, n_cmp, 2 * LANES), lambda p, i: (1, p, 0, 0)),
                  pl.BlockSpec((N_BLK_LANES, n_cmp), lambda p, i: (0, 0))],
        out_specs=[pl.BlockSpec((Q_TILE, kvs * B_GROUP * HEAD_DIM), lambda p, i: (i, p)),
                   pl.BlockSpec((kvs, Q_TILE, N_BLK_LANES), lambda p, i: (p, i, 0))],
        out_shape=[jax.ShapeDtypeStruct((t, D_B), F32),
                   jax.ShapeDtypeStruct((B_KV_HEADS, t, N_BLK_LANES), BF16)],
        scratch_shapes=[pltpu.VMEM((kvs, B_GROUP * Q_TILE, 2 * LANES), BF16)], name="nsa_select",
        compiler_params=_cparams(2, VMEM_LIMIT))(q_hm, cmp_aug, cmp_aug, wt)


def _nsa_attend_kernel(q_ref, bias_ref, ocmp_ref, gate_ref, gx_ref, ksv_ref, kwv_ref, vs_ref, vw_ref, o_ref,
                       lhs_w_ref, lhs_s_ref, m_ref, acc_ref, *, key_tile, tiles_per_group):
    q0 = pl.program_id(1) * ATT_TILE
    rows = B_GROUP * ATT_TILE
    chains = range(KV_PER_ATTEND_STEP)
    t_col = q0 + lax.broadcasted_iota(jnp.int32, (rows, 1), 0) % ATT_TILE

    for c in chains:
        for g in range(B_GROUP):
            r = slice(g * ATT_TILE, (g + 1) * ATT_TILE)
            qp = q_ref[c * (B_GROUP // 2) + g // 2]
            qhi = qp[:, (g % 2) * HEAD_DIM:(g % 2 + 1) * HEAD_DIM].astype(BF16)
            lhs_s_ref[c, r, 0:HEAD_DIM] = qhi
            lhs_s_ref[c, r, LANES:] = bias_ref[c]
            lhs_w_ref[c, r, 0:HEAD_DIM] = qhi
        lhs_s_ref[c, :, HEAD_DIM:LANES] = jnp.zeros((rows, LANES - HEAD_DIM), BF16)
        lhs_w_ref[c, :, HEAD_DIM:] = jnp.zeros((rows, 2 * LANES - HEAD_DIM), BF16)

    for c in chains:
        m_ref[c] = jnp.full((rows, LANES), NEG_INF, F32)
        acc_ref[c] = jnp.zeros((rows, LANES), F32)

    def tile(ti, causal):
        start = pl.multiple_of(ti * key_tile, key_tile)
        for c in chains:
            st = lax.dot_general(lhs_s_ref[c], ksv_ref[0, c, pl.ds(start, key_tile), :], NT_DIMS,
                                 preferred_element_type=F32)
            if causal:
                kp = start + lax.broadcasted_iota(jnp.int32, (rows, key_tile), 1)
                st = jnp.where(kp <= t_col, st, NEG_INF)
            m_run = m_ref[c]
            m_new = jnp.maximum(m_run, jnp.max(st, axis=-1, keepdims=True))
            pt = jnp.exp2(st - jnp.concatenate([m_new] * (key_tile // LANES), axis=1))
            pv = jnp.dot(pt.astype(BF16), vs_ref[0, c, pl.ds(start, key_tile), :], preferred_element_type=F32)
            acc_ref[c] = acc_ref[c] * jnp.exp2(m_run - m_new) + pv
            m_ref[c] = m_new

    def group(gi, _):
        for u in range(tiles_per_group):
            tile(gi * tiles_per_group + u, False)
        return 0

    def single(ti, _):
        tile(ti, False)
        return 0

    n_full = q0 // key_tile
    n_groups = n_full // tiles_per_group
    lax.fori_loop(0, n_groups, group, 0)
    lax.fori_loop(n_groups * tiles_per_group, n_full, single, 0)
    tile(n_full, True)

    slab = WINDOW + ATT_TILE
    ws = pl.multiple_of(jnp.maximum(q0 - WINDOW, 0), ATT_TILE)
    kpos = ws + lax.broadcasted_iota(jnp.int32, (rows, slab), 1)
    in_window = lax.bitcast_convert_type(t_col - kpos, jnp.uint32) < jnp.uint32(WINDOW)
    lane = lax.broadcasted_iota(jnp.int32, (ATT_TILE, LANES), 1)
    per_branch = B_GROUP * HEAD_DIM

    def head_pair(acc, k):
        a_even = acc[2 * k * ATT_TILE:(2 * k + 1) * ATT_TILE]
        a_odd = acc[(2 * k + 1) * ATT_TILE:(2 * k + 2) * ATT_TILE]
        num = jnp.where(lane < HEAD_DIM, a_even, pltpu.roll(a_odd, HEAD_DIM, 1))
        den = jnp.where(lane < HEAD_DIM, pltpu.roll(a_even, HEAD_DIM, 1), a_odd)
        return num * (1.0 / den)

    for c in chains:
        acc_s = acc_ref[c]
        s = lax.dot_general(lhs_w_ref[c], kwv_ref[0, c, pl.ds(ws, slab), :], NT_DIMS,
                            preferred_element_type=F32)
        s = jnp.where(in_window, s, NEG_INF)
        e = jnp.exp2(s - jnp.max(s, axis=-1, keepdims=True))
        acc_w = jnp.dot(e.astype(BF16), vw_ref[0, c, pl.ds(ws, slab), :], preferred_element_type=F32)

        gates = gate_ref[c]
        g_hi = gates.astype(BF16)
        g_lo = (gates - g_hi.astype(F32)).astype(BF16)
        gx = (jnp.dot(g_hi, gx_ref[...], preferred_element_type=F32)
              + jnp.dot(g_lo, gx_ref[...], preferred_element_type=F32))
        for k in range(B_GROUP // 2):
            c0 = c * B_GROUP * HEAD_DIM + k * LANES
            g0 = k * LANES
            og = (gx[:, g0:g0 + LANES] * ocmp_ref[:, c0:c0 + LANES]
                  + gx[:, per_branch + g0:per_branch + g0 + LANES] * head_pair(acc_s, k)
                  + gx[:, 2 * per_branch + g0:2 * per_branch + g0 + LANES] * head_pair(acc_w, k))
            o_ref[:, c0:c0 + LANES] = og.astype(o_ref.dtype)


def _nsa_attend_call(q_hm, bias, ocmp, gates, kv_aug, v_aug, *, key_tile=512, tiles_per_group=2):
    _, t, _ = q_hm.shape
    assert t % key_tile == 0 and key_tile % ATT_TILE == 0 and t % ATT_TILE == 0
    rows = B_GROUP * ATT_TILE
    kvs = KV_PER_ATTEND_STEP
    wide = kvs * B_GROUP * HEAD_DIM
    resident = dict(pipeline_mode=pl.Buffered(1))
    return pl.pallas_call(
        functools.partial(_nsa_attend_kernel, key_tile=key_tile, tiles_per_group=tiles_per_group),
        grid=(B_KV_HEADS // kvs, t // ATT_TILE),
        in_specs=[pl.BlockSpec((kvs * B_GROUP // 2, ATT_TILE, LANES), lambda p, i: (p, i, 0)),
                  pl.BlockSpec((kvs, ATT_TILE, N_BLK_LANES), lambda p, i: (p, i, 0)),
                  pl.BlockSpec((ATT_TILE, wide), lambda p, i: (i, p)),
                  pl.BlockSpec((kvs, ATT_TILE, LANES), lambda p, i: (p, i, 0)),
                  pl.BlockSpec((LANES, 3 * B_GROUP * HEAD_DIM), lambda p, i: (0, 0)),
                  pl.BlockSpec((1, kvs, t, 2 * LANES), lambda p, i: (0, p, 0, 0), **resident),
                  pl.BlockSpec((1, kvs, t, 2 * LANES), lambda p, i: (1, p, 0, 0), **resident),
                  pl.BlockSpec((1, kvs, t, LANES), lambda p, i: (0, p, 0, 0), **resident),
                  pl.BlockSpec((1, kvs, t, LANES), lambda p, i: (1, p, 0, 0), **resident)],
        out_specs=pl.BlockSpec((ATT_TILE, wide), lambda p, i: (i, p)),
        out_shape=jax.ShapeDtypeStruct((t, D_B), BF16),
        scratch_shapes=[pltpu.VMEM((kvs, rows, 2 * LANES), BF16), pltpu.VMEM((kvs, rows, 2 * LANES), BF16),
                        pltpu.VMEM((kvs, rows, LANES), F32), pltpu.VMEM((kvs, rows, LANES), F32)],
        name="nsa_attend",
        compiler_params=_cparams(2, VMEM_LIMIT))(q_hm, bias, ocmp, gates, _gate_expand_matrix(),
                                                 kv_aug, kv_aug, v_aug, v_aug)


def _post_kernel(oa_ref, ob_ref, sg_ref, x_ref, wa_ref, wb_ref, wo_ref, g_ref, o_ref):
    ya = jnp.dot(oa_ref[...], wa_ref[...], preferred_element_type=F32)
    yb = jnp.dot(ob_ref[...], wb_ref[...], preferred_element_type=F32)
    merged = sg_ref[0].astype(F32) * ya + sg_ref[1].astype(F32) * yb
    mix = jnp.dot(merged.astype(BF16), wo_ref[...], preferred_element_type=F32)
    y = mix * lax.rsqrt(jnp.mean(mix * mix, axis=-1, keepdims=True) + EPS)
    o_ref[...] = x_ref[...] + y * g_ref[...]


def _post_call(oa, ob, sg, x2, wa, wb, wo, g, tm=256):
    t, d = x2.shape
    const = dict(pipeline_mode=pl.Buffered(1))
    return pl.pallas_call(
        _post_kernel, grid=(t // tm,),
        in_specs=[pl.BlockSpec((tm, D_A), lambda i: (i, 0)),
                  pl.BlockSpec((tm, D_B), lambda i: (i, 0)),
                  pl.BlockSpec((2, tm, d), lambda i: (0, i, 0)),
                  pl.BlockSpec((tm, d), lambda i: (i, 0)),
                  pl.BlockSpec((D_A, d), lambda i: (0, 0), **const),
                  pl.BlockSpec((D_B, d), lambda i: (0, 0), **const),
                  pl.BlockSpec((d, d), lambda i: (0, 0), **const),
                  pl.BlockSpec((1, d), lambda i: (0, 0))],
        out_specs=pl.BlockSpec((tm, d), lambda i: (i, 0)),
        out_shape=jax.ShapeDtypeStruct((t, d), F32), name="mix_out_residual",
        compiler_params=_cparams(1, VMEM_LIMIT))(oa, ob, sg, x2, wa, wb, wo, g)


def _ffn_kernel(h_ref, g1_ref, wg_ref, wu_ref, wd_ref, g2_ref, o_ref, hn_ref, acc_ref):
    f = pl.program_id(1)

    @pl.when(f == 0)
    def _():
        h = h_ref[...]
        y = h * lax.rsqrt(jnp.mean(h * h, axis=-1, keepdims=True) + EPS)
        hn_ref[...] = (y * g1_ref[...]).astype(BF16)
        acc_ref[...] = jnp.zeros_like(acc_ref)

    hn = hn_ref[...]
    a = jnp.dot(hn, wg_ref[...], preferred_element_type=F32)
    u = jnp.dot(hn, wu_ref[...], preferred_element_type=F32)
    z = (a * jax.nn.sigmoid(a) * u).astype(BF16)
    acc_ref[...] += jnp.dot(z, wd_ref[...], preferred_element_type=F32)

    @pl.when(f == pl.num_programs(1) - 1)
    def _():
        ff = acc_ref[...]
        y = ff * lax.rsqrt(jnp.mean(ff * ff, axis=-1, keepdims=True) + EPS)
        o_ref[...] = h_ref[...] + y * g2_ref[...]


def _ffn_call(h1, g1, wg, wu, wd, g2, tm=512, tf=512):
    t, d = h1.shape
    dff = wg.shape[1]
    return pl.pallas_call(
        _ffn_kernel, grid=(t // tm, dff // tf),
        in_specs=[pl.BlockSpec((tm, d), lambda i, f: (i, 0)),
                  pl.BlockSpec((1, d), lambda i, f: (0, 0)),
                  pl.BlockSpec((d, tf), lambda i, f: (0, f)),
                  pl.BlockSpec((d, tf), lambda i, f: (0, f)),
                  pl.BlockSpec((tf, d), lambda i, f: (f, 0)),
                  pl.BlockSpec((1, d), lambda i, f: (0, 0))],
        out_specs=pl.BlockSpec((tm, d), lambda i, f: (i, 0)),
        out_shape=jax.ShapeDtypeStruct((t, d), F32),
        scratch_shapes=[pltpu.VMEM((tm, d), BF16), pltpu.VMEM((tm, d), F32)], name="swiglu_ffn",
        compiler_params=_cparams(2, VMEM_LIMIT))(h1, g1, wg, wu, wd, g2)


def _rope_tables(pos, period, scale=1.0):
    half = ROPE_DIM // 2
    inv = jnp.float32(ROPE_THETA) ** (-jnp.arange(half, dtype=F32) * 2.0 / ROPE_DIM)
    ang = pos.astype(F32)[:, None] * inv[None, :]
    cos, sin = jnp.cos(ang), jnp.sin(ang)
    r = np.arange(LANES) % period
    f = r % half
    lo = jnp.asarray((r < half)[None, :])
    hi = jnp.asarray(((r >= half) & (r < ROPE_DIM))[None, :])
    c = jnp.where(lo | hi, cos[:, f], 1.0)
    sa = jnp.where(lo, -sin[:, f], 0.0)
    sb = jnp.where(hi, sin[:, f], 0.0)
    return jnp.stack([c, sa, sb]) * scale


def _gate_expand_matrix():
    e = np.zeros((LANES, 3 * B_GROUP * HEAD_DIM), np.float32)
    for g in range(B_GROUP):
        for b in range(3):
            c0 = (b * B_GROUP + g) * HEAD_DIM
            e[3 * g + b, c0:c0 + HEAD_DIM] = 1.0
    return jnp.asarray(e, BF16)


def _slc_weight_matrix(n_cmp_pad):
    ratio = SLC_BLOCK // CMP_STRIDE
    w = np.zeros((N_BLK_LANES, n_cmp_pad), np.float32)
    for j in range(N_BLK_LANES):
        for o, wv in enumerate(SLC_WEIGHTS):
            n = ratio * j + o - 1
            if 0 <= n < n_cmp_pad - 1:
                w[j, n] = wv
    return jnp.asarray(w, BF16)


def kernel(x, pre_mix_g, w_in, lb_table, a_norm_g, cmp_pe_k, cmp_w1_k, cmp_b1_k, cmp_w2_k, cmp_b2_k, cmp_pe_v, cmp_w1_v, cmp_b1_v, cmp_w2_v, cmp_b2_v, w_proj_a, w_proj_b, w_out, post_mix_g, pre_ffn_g, w_gate, w_up, w_down, post_ffn_g):
    bsz, t, d = x.shape
    assert bsz == 1 and d == D_MODEL and WINDOW + ATT_TILE <= t <= N_BLK_LANES * SLC_BLOCK
    x2 = x.reshape(t, d)
    n_cmp_pad = t // CMP_STRIDE
    assert n_cmp_pad % LANES == 0

    lb = jnp.cumsum(jax.nn.softmax(lb_table.astype(F32), axis=0), axis=0)[0].reshape(1, D_A)
    w = w_in[0]
    o_q, o_kc, o_ks, o_g, o_m = 4 * D_A, 4 * D_A + D_B, 4 * D_A + D_B + 2 * KV_DIM, 4 * D_A + D_B + 6 * KV_DIM, 4 * D_A + D_B + 6 * KV_DIM + 3 * B_HEADS
    w_hgrn = w[:, :o_q].astype(BF16)
    w_q = w[:, o_q:o_kc].astype(BF16)
    w_cmp = w[:, o_kc:o_ks].astype(BF16)
    w_kv = (w[:, o_ks:o_g].reshape(d, 2, 2, B_KV_HEADS, HEAD_DIM).transpose(0, 1, 3, 2, 4)
            .reshape(d, 4 * KV_DIM).astype(BF16))
    w_gate3 = w[:, o_g:o_m].reshape(d, B_KV_HEADS, 3 * B_GROUP)
    w_g = jnp.pad(w_gate3, ((0, 0), (0, 0), (0, LANES - 3 * B_GROUP))).reshape(d, B_KV_HEADS * LANES).astype(BF16)
    w_m = w[:, o_m:].astype(BF16)

    pos = jnp.arange(t)
    tab_q = _rope_tables(pos, HEAD_DIM, HEAD_DIM ** -0.5 * LOG2_E)
    tab_kv = _rope_tables(pos, LANES)
    end_pos = jnp.arange(n_cmp_pad) * CMP_STRIDE + (CMP_BLOCK - 1)
    ident = jnp.stack([jnp.ones((n_cmp_pad, LANES), F32), jnp.zeros((n_cmp_pad, LANES), F32),
                       jnp.zeros((n_cmp_pad, LANES), F32)])
    tab_cmp = jnp.concatenate([_rope_tables(end_pos, LANES), ident], axis=0)

    xn = _rmsnorm_call(x2, pre_mix_g[0].reshape(1, d))

    tm = 1024 if t % 1024 == 0 else 256
    proj4 = _proj_call(
        "proj_hgrn", xn, w_hgrn, tm=tm, tn=512, epilogue=_ep_plain,
        out_shape=jax.ShapeDtypeStruct((4, t, D_A), F32),
        out_specs=pl.BlockSpec((1, tm, 512), lambda i, j: (j // 2, i, j % 2)))
    q_hm = _proj_call(
        "proj_q", xn, w_q, tm=tm, tn=512, epilogue=_ep_q,
        out_shape=jax.ShapeDtypeStruct((B_HEADS // 2, t, LANES), F32),
        out_specs=pl.BlockSpec((512 // LANES, tm, LANES), lambda i, j: (j, i, 0)),
        extra=(tab_q,), extra_specs=(pl.BlockSpec((3, tm, LANES), lambda i, j: (0, i, 0)),))
    cmp_in = _proj_call(
        "proj_cmp", xn, w_cmp, tm=tm, tn=2 * KV_DIM, epilogue=_ep_cmp,
        out_shape=jax.ShapeDtypeStruct((2, B_KV_HEADS, t, HEAD_DIM), F32),
        out_specs=pl.BlockSpec((2, B_KV_HEADS, tm, HEAD_DIM), lambda i, j: (0, 0, i, 0)))
    kv_aug, v_aug = _proj_call(
        "proj_kv", xn, w_kv, tm=tm, tn=2 * KV_DIM, epilogue=functools.partial(_ep_kv, tm=tm),
        out_shape=[jax.ShapeDtypeStruct((2, B_KV_HEADS, t, 2 * LANES), BF16),
                   jax.ShapeDtypeStruct((2, B_KV_HEADS, t, LANES), BF16)],
        out_specs=[pl.BlockSpec((1, B_KV_HEADS, tm, 2 * LANES), lambda i, j: (j, 0, i, 0)),
                   pl.BlockSpec((1, B_KV_HEADS, tm, LANES), lambda i, j: (j, 0, i, 0))],
        extra=(tab_kv,), extra_specs=(pl.BlockSpec((3, tm, LANES), lambda i, j: (0, i, 0)),))
    gates = _proj_call(
        "proj_gate", xn, w_g, tm=tm, tn=B_KV_HEADS * LANES, epilogue=_ep_gate,
        out_shape=jax.ShapeDtypeStruct((B_KV_HEADS, t, LANES), F32),
        out_specs=pl.BlockSpec((B_KV_HEADS, tm, LANES), lambda i, j: (0, i, 0)))
    sg = _proj_call(
        "proj_merge", xn, w_m, tm=tm, tn=512, epilogue=_ep_sigmoid,
        out_shape=jax.ShapeDtypeStruct((2, t, d), BF16),
        out_specs=pl.BlockSpec((1, tm, 512), lambda i, j: (j // 4, i, j % 4)))

    oa = _hgrn_call(proj4, lb, a_norm_g[0].reshape(1, D_A))

    half = CMP_BLOCK // 2
    x16 = cmp_in.reshape(2, B_KV_HEADS, n_cmp_pad, CMP_STRIDE * HEAD_DIM)
    pe2 = jnp.stack([cmp_pe_k[0], cmp_pe_v[0]]).reshape(2, 2, half * HEAD_DIM)
    w1 = jnp.stack([cmp_w1_k[0], cmp_w1_v[0]]).astype(BF16)
    b1 = jnp.stack([cmp_b1_k[0], cmp_b1_v[0]]).reshape(2, 1, CMP_HIDDEN)
    w2p = jnp.pad(jnp.stack([cmp_w2_k[0], cmp_w2_v[0]]), ((0, 0), (0, 0), (0, LANES - HEAD_DIM))).astype(BF16)
    b2p = jnp.pad(jnp.stack([cmp_b2_k[0], cmp_b2_v[0]]), ((0, 0), (0, LANES - HEAD_DIM))).reshape(2, 1, LANES)
    cmp_aug = _compress_call(x16, pe2, w1, b1, w2p, b2p, tab_cmp)
    ocmp, bias = _nsa_select_call(q_hm, cmp_aug, _slc_weight_matrix(n_cmp_pad))
    ob = _nsa_attend_call(q_hm, bias, ocmp, gates, kv_aug, v_aug)

    h1 = _post_call(oa, ob, sg, x2, w_proj_a[0].astype(BF16), w_proj_b[0].astype(BF16),
                    w_out[0].astype(BF16), post_mix_g[0].reshape(1, d))
    out = _ffn_call(h1, pre_ffn_g[0].reshape(1, d), w_gate[0].astype(BF16), w_up[0].astype(BF16),
                    w_down[0].astype(BF16), post_ffn_g[0].reshape(1, d))
    return out.reshape(bsz, t, d)
```

```python
import functools

import numpy as np
import jax
import jax.numpy as jnp
from jax import lax
from jax.experimental import pallas as pl
from jax.experimental.pallas import tpu as pltpu

F32 = jnp.float32
BF16 = jnp.bfloat16

D_MODEL = 2048
D_A = 1024
D_B = 1024
A_HEADS = 8
A_DK = 128
HGRN_PAIRWISE = 4
B_HEADS = 16
B_KV_HEADS = 4
B_GROUP = 4
HEAD_DIM = 64
KV_DIM = 256
CMP_BLOCK = 32
CMP_STRIDE = 16
CMP_HIDDEN = 256
SLC_BLOCK = 64
SLC_TOP_N = 16
SLC_FORCED = 3
SLC_WEIGHTS = (1.0, 2.0, 2.0, 2.0, 1.0)
WINDOW = 512
Q_TILE = 512
ATT_TILE = 256
KV_PER_ATTEND_STEP = 2
KV_PER_SELECT_STEP = 2
PROJ_SUBTILE = 256
GATE_LANES = 256
ROPE_THETA = 500000.0
ROPE_DIM = 16
D_FF = 5632
EPS = 1e-6
LOG2_E = 1.4426950408889634
NEG_INF = -1e30
N_BLK_LANES = 128
LANES = 128
VMEM_LIMIT = 56 * 1024 * 1024

NT_DIMS = (((1,), (1,)), ((), ()))
TN_DIMS = (((0,), (0,)), ((), ()))


def _cparams(n_axes, vmem=None):
    return pltpu.CompilerParams(dimension_semantics=("arbitrary",) * n_axes,
                                vmem_limit_bytes=vmem)


def _rmsnorm_kernel(x_ref, g_ref, o_ref):
    x = x_ref[...]
    y = x * lax.rsqrt(jnp.mean(x * x, axis=-1, keepdims=True) + EPS)
    o_ref[...] = (y * g_ref[...]).astype(o_ref.dtype)


def _rmsnorm_call(x2, g, tm=512):
    t, d = x2.shape
    return pl.pallas_call(
        _rmsnorm_kernel, grid=(t // tm,),
        in_specs=[pl.BlockSpec((tm, d), lambda i: (i, 0)), pl.BlockSpec((1, d), lambda i: (0, 0))],
        out_specs=pl.BlockSpec((tm, d), lambda i: (i, 0)),
        out_shape=jax.ShapeDtypeStruct((t, d), BF16), name="pre_mix_rmsnorm",
        compiler_params=_cparams(1))(x2, g)


def _proj_call(name, xn, w, *, tm, tn, epilogue, out_shape, out_specs, extra=(), extra_specs=()):
    t, k = xn.shape
    n = w.shape[1]
    n_extra = len(extra)

    def body(x_ref, w_ref, *rest):
        x = x_ref[...]
        for c0 in range(0, tn, PROJ_SUBTILE):
            acc = jnp.dot(x, w_ref[:, c0:c0 + PROJ_SUBTILE], preferred_element_type=F32)
            epilogue(acc, c0, rest[:n_extra], rest[n_extra:])

    assert tn % PROJ_SUBTILE == 0
    return pl.pallas_call(
        body, grid=(t // tm, n // tn),
        in_specs=[pl.BlockSpec((tm, k), lambda i, j: (i, 0)),
                  pl.BlockSpec((k, tn), lambda i, j: (0, j)), *extra_specs],
        out_specs=out_specs, out_shape=out_shape, name=name,
        compiler_params=_cparams(2, VMEM_LIMIT))(xn, w, *extra)


def _rope_tile(a, tab_ref):
    return (a * tab_ref[0] + pltpu.roll(a, LANES - ROPE_DIM // 2, 1) * tab_ref[1]
            + pltpu.roll(a, ROPE_DIM // 2, 1) * tab_ref[2])


def _ep_plain(acc, c0, extra, outs):
    outs[0][0, :, c0:c0 + acc.shape[1]] = acc.astype(outs[0].dtype)


def _ep_q(acc, c0, extra, outs):
    for pair in range(acc.shape[1] // LANES):
        outs[0][c0 // LANES + pair] = _rope_tile(acc[:, pair * LANES:(pair + 1) * LANES], extra[0])


def _ep_cmp(acc, c0, extra, outs):
    for hh in range(acc.shape[1] // HEAD_DIM):
        head = c0 // HEAD_DIM + hh
        outs[0][head // B_KV_HEADS, head % B_KV_HEADS] = acc[:, hh * HEAD_DIM:(hh + 1) * HEAD_DIM]


def _ep_kv(acc, c0, extra, outs, *, tm):
    i = pl.program_id(0)
    j = pl.program_id(1)
    rowg = i * tm + lax.broadcasted_iota(jnp.int32, (tm, LANES), 0)
    lane = lax.broadcasted_iota(jnp.int32, (tm, LANES), 1)
    aux = jnp.where(j == 0, (rowg // SLC_BLOCK == lane).astype(F32), 0.0).astype(BF16)
    for hh in range(acc.shape[1] // LANES):
        h = c0 // LANES + hh
        r = _rope_tile(acc[:, hh * LANES:(hh + 1) * LANES], extra[0])
        outs[0][0, h, :, 0:LANES] = r.astype(BF16)
        outs[0][0, h, :, LANES:2 * LANES] = aux
        outs[1][0, h] = jnp.where(lane < HEAD_DIM, pltpu.roll(r, HEAD_DIM, 1), 1.0).astype(BF16)


def _ep_gate(acc, c0, extra, outs):
    outs[0][:, c0:c0 + acc.shape[1]] = jax.nn.sigmoid(acc)


def _ep_sigmoid(acc, c0, extra, outs):
    outs[0][0, :, c0:c0 + acc.shape[1]] = jax.nn.sigmoid(acc).astype(outs[0].dtype)


def _split3(x):
    hi = x.astype(BF16)
    r1 = x - hi.astype(F32)
    mid = r1.astype(BF16)
    lo = (r1 - mid.astype(F32)).astype(BF16)
    return hi, mid, lo


def _hgrn_kernel(q_ref, f_ref, i_ref, g_ref, lb_ref, gn_ref, o_ref, st_ref, *, chunk, heads):
    c = pl.program_id(1)

    @pl.when(c == 0)
    def _():
        st_ref[...] = jnp.zeros_like(st_ref)

    row = lax.broadcasted_iota(jnp.int32, (chunk, chunk), 0)
    col = lax.broadcasted_iota(jnp.int32, (chunk, chunk), 1)
    tri = (col <= row).astype(BF16)
    rowv = lax.broadcasted_iota(jnp.int32, (chunk, A_DK), 0)

    for hb in range(heads):
        sl = slice(hb * A_DK, (hb + 1) * A_DK)
        q = q_ref[0, :, sl]
        ii = i_ref[0, :, sl]
        gg = g_ref[0, :, sl]
        lbv = lb_ref[:, sl]
        f = lbv + (1.0 - lbv) * jax.nn.sigmoid(f_ref[0, :, sl])
        lf = jnp.log(f)
        k = 1.0 - f
        b = sum(jnp.dot(tri, p, preferred_element_type=F32) for p in _split3(lf)) * LOG2_E
        b_end = b[chunk - 1:chunk, :]
        ii16 = ii.astype(BF16)

        st = st_ref[hb]
        o = lax.dot_general((q * jnp.exp2(b)).astype(BF16), st.astype(BF16), NT_DIMS,
                            preferred_element_type=F32)
        kd = (k * jnp.exp2(b_end - b)).astype(BF16)
        st_ref[hb] = st * jnp.exp2(b_end) + lax.dot_general(ii16, kd, TN_DIMS,
                                                            preferred_element_type=F32)

        att = jnp.where(col == row, jnp.sum(q * k, axis=-1, keepdims=True), 0.0)
        for d in range(1, HGRN_PAIRWISE):
            valid = (rowv % HGRN_PAIRWISE) >= d
            w = jnp.exp2(b - pltpu.roll(b, d, 0))
            p = jnp.where(valid, q * pltpu.roll(k, d, 0) * w, 0.0)
            att = att + jnp.where(col == row - d, jnp.sum(p, axis=-1, keepdims=True), 0.0)
        m = HGRN_PAIRWISE
        while m < chunk:
            grp = chunk // (2 * m)
            b3 = b.reshape(grp, 2 * m, A_DK)
            refrow = jnp.broadcast_to(b3[:, m - 1:m, :], (grp, 2 * m, A_DK)).reshape(chunk, A_DK)
            second = (rowv % (2 * m)) >= m
            ql = jnp.where(second, q * jnp.exp2(b - refrow), 0.0)
            kl = jnp.where(second, 0.0, k * jnp.exp2(refrow - b))
            a = lax.dot_general(ql.astype(BF16), kl.astype(BF16), NT_DIMS, preferred_element_type=F32)
            if grp > 1:
                a = jnp.where(row // (2 * m) == col // (2 * m), a, 0.0)
            att = att + a
            m *= 2
        o = o + jnp.dot(att.astype(BF16), ii16, preferred_element_type=F32)

        o = o * lax.rsqrt(jnp.mean(o * o, axis=-1, keepdims=True) + EPS)
        o_ref[:, sl] = ((o * gn_ref[:, sl]) * (gg * jax.nn.sigmoid(gg))).astype(o_ref.dtype)


def _hgrn_call(proj4, lb, gn, *, chunk=128, heads=8):
    _, t, _ = proj4.shape
    w = heads * A_DK

    def spec(kind):
        return pl.BlockSpec((1, chunk, w), lambda h, c, kind=kind: (kind, c, h))

    vec = pl.BlockSpec((1, w), lambda h, c: (0, h))
    return pl.pallas_call(
        functools.partial(_hgrn_kernel, chunk=chunk, heads=heads),
        grid=(A_HEADS // heads, t // chunk),
        in_specs=[spec(0), spec(1), spec(2), spec(3), vec, vec],
        out_specs=pl.BlockSpec((chunk, w), lambda h, c: (c, h)),
        out_shape=jax.ShapeDtypeStruct((t, D_A), BF16),
        scratch_shapes=[pltpu.VMEM((heads, A_DK, A_DK), F32)], name="hgrn2_scan",
        compiler_params=_cparams(2))(proj4, proj4, proj4, proj4, lb, gn)


def _compress_kernel(x_ref, pe_ref, w1_ref, b1_ref, w2_ref, b2_ref, tab_ref, o_ref):
    half = CMP_BLOCK * HEAD_DIM // 2
    x = x_ref[0, 0]
    top = jnp.dot((x + pe_ref[0, 0:1, :]).astype(BF16), w1_ref[0, :half, :], preferred_element_type=F32)
    bot = jnp.dot((x + pe_ref[0, 1:2, :]).astype(BF16), w1_ref[0, half:, :], preferred_element_type=F32)
    n = x.shape[0]
    h = jax.nn.gelu(top + pltpu.roll(bot, n - 1, 0) + b1_ref[0])
    y = jnp.dot(h.astype(BF16), w2_ref[0], preferred_element_type=F32) + b2_ref[0]
    y = _rope_tile(y, tab_ref)
    hi = y.astype(BF16)
    lo = (y - hi.astype(F32)).astype(BF16)
    o_ref[0, 0, :, 0:LANES] = (y + pltpu.roll(y, HEAD_DIM, 1)).astype(BF16)
    o_ref[0, 0, :, LANES:2 * LANES] = lo


def _compress_call(x16, pe2, w1, b1, w2p, b2p, tab):
    _, nh, n, wid = x16.shape
    hid = w1.shape[-1]
    return pl.pallas_call(
        _compress_kernel, grid=(2, nh),
        in_specs=[pl.BlockSpec((1, 1, n, wid), lambda kv, h: (kv, h, 0, 0)),
                  pl.BlockSpec((1, 2, wid), lambda kv, h: (kv, 0, 0)),
                  pl.BlockSpec((1, 2 * wid, hid), lambda kv, h: (kv, 0, 0)),
                  pl.BlockSpec((1, 1, hid), lambda kv, h: (kv, 0, 0)),
                  pl.BlockSpec((1, hid, LANES), lambda kv, h: (kv, 0, 0)),
                  pl.BlockSpec((1, 1, LANES), lambda kv, h: (kv, 0, 0)),
                  pl.BlockSpec((3, n, LANES), lambda kv, h: (kv, 0, 0))],
        out_specs=pl.BlockSpec((1, 1, n, 2 * LANES), lambda kv, h: (kv, h, 0, 0)),
        out_shape=jax.ShapeDtypeStruct((2, nh, n, 2 * LANES), BF16),
        name="compress_mlp", compiler_params=_cparams(2, VMEM_LIMIT))(x16, pe2, w1, b1, w2p, b2p, tab)


def _nsa_select_kernel(q_ref, kc_ref, vc_ref, wt_ref, ocmp_ref, bias_ref, lhs_ref):
    qt = pl.program_id(1)
    q0 = qt * Q_TILE
    rows = B_GROUP * Q_TILE
    n_cmp = kc_ref.shape[2]

    def chains(n_cols, n_blk):
        t_col = q0 + lax.broadcasted_iota(jnp.int32, (rows, 1), 0) % Q_TILE
        n_idx = lax.broadcasted_iota(jnp.int32, (rows, n_cols), 1)
        vis = n_idx <= (t_col - (CMP_BLOCK - 1)) // CMP_STRIDE
        blk = lax.broadcasted_iota(jnp.int32, (n_blk, Q_TILE), 0)
        tok = q0 + lax.broadcasted_iota(jnp.int32, (n_blk, Q_TILE), 1)
        cur = tok // SLC_BLOCK
        forced = (blk == 0) | (blk == cur) | (blk == cur - 1)
        candidate = (blk * SLC_BLOCK <= tok) & jnp.logical_not(forced)
        wt = wt_ref[0:n_blk, 0:n_cols]
        zeros64 = jnp.zeros((rows, HEAD_DIM), BF16)
        never = jnp.zeros((N_BLK_LANES - n_blk, Q_TILE), F32)

        for c in range(KV_PER_SELECT_STEP):
            for pair in range(B_GROUP // 2):
                qp = q_ref[c * (B_GROUP // 2) + pair]
                hi = qp.astype(BF16)
                lo = (qp - hi.astype(F32)).astype(BF16)
                for half in range(2):
                    r = slice((2 * pair + half) * Q_TILE, (2 * pair + half + 1) * Q_TILE)
                    ln = slice(half * HEAD_DIM, (half + 1) * HEAD_DIM)
                    lhs_ref[c, r, 0:HEAD_DIM] = hi[:, ln]
                    lhs_ref[c, r, HEAD_DIM:2 * HEAD_DIM] = lo[:, ln]
                    lhs_ref[c, r, 2 * HEAD_DIM:3 * HEAD_DIM] = hi[:, ln]
            lhs_ref[c, :, 3 * HEAD_DIM:] = zeros64
            s = lax.dot_general(lhs_ref[c], kc_ref[0, c, 0:n_cols, :], NT_DIMS, preferred_element_type=F32)
            s = jnp.where(vis, s, NEG_INF)
            e = jnp.exp2(s - jnp.max(s, axis=-1, keepdims=True))
            l = jnp.sum(e, axis=-1, keepdims=True)
            p = e * jnp.where(t_col >= CMP_BLOCK - 1, 1.0 / l, 0.0)
            o_cmp = jnp.dot(p.astype(BF16), vc_ref[0, c, 0:n_cols, 0:LANES], preferred_element_type=F32)
            for g in range(B_GROUP):
                c0 = (c * B_GROUP + g) * HEAD_DIM
                ocmp_ref[:, c0:c0 + HEAD_DIM] = o_cmp[g * Q_TILE:(g + 1) * Q_TILE, :HEAD_DIM]

            psum = p[0:Q_TILE] + p[Q_TILE:2 * Q_TILE] + p[2 * Q_TILE:3 * Q_TILE] + p[3 * Q_TILE:]
            ps_hi = psum.astype(BF16)
            ps_lo = (psum - ps_hi.astype(F32)).astype(BF16)
            pslc = (lax.dot_general(wt, ps_hi, NT_DIMS, preferred_element_type=F32)
                    + lax.dot_general(wt, ps_lo, NT_DIMS, preferred_element_type=F32))
            score = jnp.where(candidate, pslc, jnp.where(forced, -jnp.inf, NEG_INF))
            sel_t = forced.astype(F32)
            for _ in range(SLC_TOP_N - SLC_FORCED):
                best = jnp.max(score, axis=0, keepdims=True)
                first = jnp.min(jnp.where(score == best, blk, N_BLK_LANES), axis=0, keepdims=True)
                hit = blk == first
                score = jnp.where(hit, -jnp.inf, score)
                sel_t = jnp.where(hit, 1.0, sel_t)
            if n_blk < N_BLK_LANES:
                sel_t = jnp.concatenate([sel_t, never], axis=0)
            bias_ref[c] = jnp.where(sel_t.T > 0.0, 0.0, NEG_INF).astype(BF16)

    ratio = SLC_BLOCK // CMP_STRIDE
    early = qt < pl.num_programs(1) // 2
    pl.when(early)(lambda: chains(n_cmp // 2, n_cmp // (2 * ratio)))
    pl.when(jnp.logical_not(early))(lambda: chains(n_cmp, N_BLK_LANES))


def _nsa_select_call(q_hm, cmp_aug, wt):
    _, t, _ = q_hm.shape
    n_cmp = cmp_aug.shape[2]
    kvs = KV_PER_SELECT_STEP
    return pl.pallas_call(
        _nsa_select_kernel, grid=(B_KV_HEADS // kvs, t // Q_TILE),
        in_specs=[pl.BlockSpec((kvs * B_GROUP // 2, Q_TILE, LANES), lambda p, i: (p, i, 0)),
                  pl.BlockSpec((1, kvs, n_cmp, 2 * LANES), lambda p, i: (0, p, 0, 0)),
                  pl.BlockSpec((1, kvs, n_cmp, 2 * LANES), lambda p, i: (1, p, 0, 0)),
                  pl.BlockSpec((N_BLK_LANES, n_cmp), lambda p, i: (0, 0))],
        out_specs=[pl.BlockSpec((Q_TILE, kvs * B_GROUP * HEAD_DIM), lambda p, i: (i, p)),
                   pl.BlockSpec((kvs, Q_TILE, N_BLK_LANES), lambda p, i: (p, i, 0))],
        out_shape=[jax.ShapeDtypeStruct((t, D_B), F32),
                   jax.ShapeDtypeStruct((B_KV_HEADS, t, N_BLK_LANES), BF16)],
        scratch_shapes=[pltpu.VMEM((kvs, B_GROUP * Q_TILE, 2 * LANES), BF16)], name="nsa_select",
        compiler_params=_cparams(2, VMEM_LIMIT))(q_hm, cmp_aug, cmp_aug, wt)


def _nsa_attend_kernel(q_ref, bias_ref, ocmp_ref, gate_ref, gx_ref, ksv_ref, kwv_ref, vs_ref, vw_ref, o_ref,
                       lhs_w_ref, lhs_s_ref, m_ref, acc_ref, *, key_tile, tiles_per_group):
    q0 = pl.program_id(1) * ATT_TILE
    rows = B_GROUP * ATT_TILE
    chains = range(KV_PER_ATTEND_STEP)
    t_col = q0 + lax.broadcasted_iota(jnp.int32, (rows, 1), 0) % ATT_TILE

    for c in chains:
        for g in range(B_GROUP):
            r = slice(g * ATT_TILE, (g + 1) * ATT_TILE)
            qp = q_ref[c * (B_GROUP // 2) + g // 2]
            qhi = qp[:, (g % 2) * HEAD_DIM:(g % 2 + 1) * HEAD_DIM].astype(BF16)
            lhs_s_ref[c, r, 0:HEAD_DIM] = qhi
            lhs_s_ref[c, r, LANES:] = bias_ref[c]
            lhs_w_ref[c, r, 0:HEAD_DIM] = qhi
        lhs_s_ref[c, :, HEAD_DIM:LANES] = jnp.zeros((rows, LANES - HEAD_DIM), BF16)
        lhs_w_ref[c, :, HEAD_DIM:] = jnp.zeros((rows, 2 * LANES - HEAD_DIM), BF16)

    for c in chains:
        m_ref[c] = jnp.full((rows, LANES), NEG_INF, F32)
        acc_ref[c] = jnp.zeros((rows, LANES), F32)

    def tile(ti, causal):
        start = pl.multiple_of(ti * key_tile, key_tile)
        for c in chains:
            st = lax.dot_general(lhs_s_ref[c], ksv_ref[0, c, pl.ds(start, key_tile), :], NT_DIMS,
                                 preferred_element_type=F32)
            if causal:
                kp = start + lax.broadcasted_iota(jnp.int32, (rows, key_tile), 1)
                st = jnp.where(kp <= t_col, st, NEG_INF)
            m_run = m_ref[c]
            m_new = jnp.maximum(m_run, jnp.max(st, axis=-1, keepdims=True))
            pt = jnp.exp2(st - jnp.concatenate([m_new] * (key_tile // LANES), axis=1))
            pv = jnp.dot(pt.astype(BF16), vs_ref[0, c, pl.ds(start, key_tile), :], preferred_element_type=F32)
            acc_ref[c] = acc_ref[c] * jnp.exp2(m_run - m_new) + pv
            m_ref[c] = m_new

    def group(gi, _):
        for u in range(tiles_per_group):
            tile(gi * tiles_per_group + u, False)
        return 0

    def single(ti, _):
        tile(ti, False)
        return 0

    n_full = q0 // key_tile
    n_groups = n_full // tiles_per_group
    lax.fori_loop(0, n_groups, group, 0)
    lax.fori_loop(n_groups * tiles_per_group, n_full, single, 0)
    tile(n_full, True)

    slab = WINDOW + ATT_TILE
    ws = pl.multiple_of(jnp.maximum(q0 - WINDOW, 0), ATT_TILE)
    kpos = ws + lax.broadcasted_iota(jnp.int32, (rows, slab), 1)
    in_window = lax.bitcast_convert_type(t_col - kpos, jnp.uint32) < jnp.uint32(WINDOW)
    lane = lax.broadcasted_iota(jnp.int32, (ATT_TILE, LANES), 1)
    per_branch = B_GROUP * HEAD_DIM

    def head_pair(acc, k):
        a_even = acc[2 * k * ATT_TILE:(2 * k + 1) * ATT_TILE]
        a_odd = acc[(2 * k + 1) * ATT_TILE:(2 * k + 2) * ATT_TILE]
        num = jnp.where(lane < HEAD_DIM, a_even, pltpu.roll(a_odd, HEAD_DIM, 1))
        den = jnp.where(lane < HEAD_DIM, pltpu.roll(a_even, HEAD_DIM, 1), a_odd)
        return num * (1.0 / den)

    gates = gate_ref[...]
    g_hi = gates.astype(BF16)
    g_lo = (gates - g_hi.astype(F32)).astype(BF16)

    for c in chains:
        acc_s = acc_ref[c]
        s = lax.dot_general(lhs_w_ref[c], kwv_ref[0, c, pl.ds(ws, slab), :], NT_DIMS,
                            preferred_element_type=F32)
        s = jnp.where(in_window, s, NEG_INF)
        e = jnp.exp2(s - jnp.max(s, axis=-1, keepdims=True))
        acc_w = jnp.dot(e.astype(BF16), vw_ref[0, c, pl.ds(ws, slab), :], preferred_element_type=F32)

        gx = (jnp.dot(g_hi, gx_ref[c], preferred_element_type=F32)
              + jnp.dot(g_lo, gx_ref[c], preferred_element_type=F32))
        for k in range(B_GROUP // 2):
            c0 = c * B_GROUP * HEAD_DIM + k * LANES
            g0 = k * LANES
            og = (gx[:, g0:g0 + LANES] * ocmp_ref[:, c0:c0 + LANES]
                  + gx[:, per_branch + g0:per_branch + g0 + LANES] * head_pair(acc_s, k)
                  + gx[:, 2 * per_branch + g0:2 * per_branch + g0 + LANES] * head_pair(acc_w, k))
            o_ref[:, c0:c0 + LANES] = og.astype(o_ref.dtype)


def _nsa_attend_call(q_hm, bias, ocmp, gates, kv_aug, v_aug, *, key_tile=512, tiles_per_group=2):
    _, t, _ = q_hm.shape
    assert t % key_tile == 0 and key_tile % ATT_TILE == 0 and t % ATT_TILE == 0
    rows = B_GROUP * ATT_TILE
    kvs = KV_PER_ATTEND_STEP
    wide = kvs * B_GROUP * HEAD_DIM
    resident = dict(pipeline_mode=pl.Buffered(1))
    return pl.pallas_call(
        functools.partial(_nsa_attend_kernel, key_tile=key_tile, tiles_per_group=tiles_per_group),
        grid=(B_KV_HEADS // kvs, t // ATT_TILE),
        in_specs=[pl.BlockSpec((kvs * B_GROUP // 2, ATT_TILE, LANES), lambda p, i: (p, i, 0)),
                  pl.BlockSpec((kvs, ATT_TILE, N_BLK_LANES), lambda p, i: (p, i, 0)),
                  pl.BlockSpec((ATT_TILE, wide), lambda p, i: (i, p)),
                  pl.BlockSpec((ATT_TILE, GATE_LANES), lambda p, i: (i, 0)),
                  pl.BlockSpec((kvs, GATE_LANES, 3 * B_GROUP * HEAD_DIM), lambda p, i: (p, 0, 0)),
                  pl.BlockSpec((1, kvs, t, 2 * LANES), lambda p, i: (0, p, 0, 0), **resident),
                  pl.BlockSpec((1, kvs, t, 2 * LANES), lambda p, i: (1, p, 0, 0), **resident),
                  pl.BlockSpec((1, kvs, t, LANES), lambda p, i: (0, p, 0, 0), **resident),
                  pl.BlockSpec((1, kvs, t, LANES), lambda p, i: (1, p, 0, 0), **resident)],
        out_specs=pl.BlockSpec((ATT_TILE, wide), lambda p, i: (i, p)),
        out_shape=jax.ShapeDtypeStruct((t, D_B), BF16),
        scratch_shapes=[pltpu.VMEM((kvs, rows, 2 * LANES), BF16), pltpu.VMEM((kvs, rows, 2 * LANES), BF16),
                        pltpu.VMEM((kvs, rows, LANES), F32), pltpu.VMEM((kvs, rows, LANES), F32)],
        name="nsa_attend",
        compiler_params=_cparams(2, VMEM_LIMIT))(q_hm, bias, ocmp, gates, _gate_expand_matrix(),
                                                 kv_aug, kv_aug, v_aug, v_aug)


def _post_kernel(oa_ref, ob_ref, sg_ref, x_ref, wa_ref, wb_ref, wo_ref, g_ref, o_ref):
    ya = jnp.dot(oa_ref[...], wa_ref[...], preferred_element_type=F32)
    yb = jnp.dot(ob_ref[...], wb_ref[...], preferred_element_type=F32)
    merged = sg_ref[0].astype(F32) * ya + sg_ref[1].astype(F32) * yb
    mix = jnp.dot(merged.astype(BF16), wo_ref[...], preferred_element_type=F32)
    y = mix * lax.rsqrt(jnp.mean(mix * mix, axis=-1, keepdims=True) + EPS)
    o_ref[...] = x_ref[...] + y * g_ref[...]


def _post_call(oa, ob, sg, x2, wa, wb, wo, g, tm=256):
    t, d = x2.shape
    const = dict(pipeline_mode=pl.Buffered(1))
    return pl.pallas_call(
        _post_kernel, grid=(t // tm,),
        in_specs=[pl.BlockSpec((tm, D_A), lambda i: (i, 0)),
                  pl.BlockSpec((tm, D_B), lambda i: (i, 0)),
                  pl.BlockSpec((2, tm, d), lambda i: (0, i, 0)),
                  pl.BlockSpec((tm, d), lambda i: (i, 0)),
                  pl.BlockSpec((D_A, d), lambda i: (0, 0), **const),
                  pl.BlockSpec((D_B, d), lambda i: (0, 0), **const),
                  pl.BlockSpec((d, d), lambda i: (0, 0), **const),
                  pl.BlockSpec((1, d), lambda i: (0, 0))],
        out_specs=pl.BlockSpec((tm, d), lambda i: (i, 0)),
        out_shape=jax.ShapeDtypeStruct((t, d), F32), name="mix_out_residual",
        compiler_params=_cparams(1, VMEM_LIMIT))(oa, ob, sg, x2, wa, wb, wo, g)


def _ffn_kernel(h_ref, g1_ref, wg_ref, wu_ref, wd_ref, g2_ref, o_ref, hn_ref, acc_ref):
    f = pl.program_id(1)

    @pl.when(f == 0)
    def _():
        h = h_ref[...]
        y = h * lax.rsqrt(jnp.mean(h * h, axis=-1, keepdims=True) + EPS)
        hn_ref[...] = (y * g1_ref[...]).astype(BF16)
        acc_ref[...] = jnp.zeros_like(acc_ref)

    hn = hn_ref[...]
    a = jnp.dot(hn, wg_ref[...], preferred_element_type=F32)
    u = jnp.dot(hn, wu_ref[...], preferred_element_type=F32)
    z = (a * jax.nn.sigmoid(a) * u).astype(BF16)
    acc_ref[...] += jnp.dot(z, wd_ref[...], preferred_element_type=F32)

    @pl.when(f == pl.num_programs(1) - 1)
    def _():
        ff = acc_ref[...]
        y = ff * lax.rsqrt(jnp.mean(ff * ff, axis=-1, keepdims=True) + EPS)
        o_ref[...] = h_ref[...] + y * g2_ref[...]


def _ffn_call(h1, g1, wg, wu, wd, g2, tm=512, tf=512):
    t, d = h1.shape
    dff = wg.shape[1]
    return pl.pallas_call(
        _ffn_kernel, grid=(t // tm, dff // tf),
        in_specs=[pl.BlockSpec((tm, d), lambda i, f: (i, 0)),
                  pl.BlockSpec((1, d), lambda i, f: (0, 0)),
                  pl.BlockSpec((d, tf), lambda i, f: (0, f)),
                  pl.BlockSpec((d, tf), lambda i, f: (0, f)),
                  pl.BlockSpec((tf, d), lambda i, f: (f, 0)),
                  pl.BlockSpec((1, d), lambda i, f: (0, 0))],
        out_specs=pl.BlockSpec((tm, d), lambda i, f: (i, 0)),
        out_shape=jax.ShapeDtypeStruct((t, d), F32),
        scratch_shapes=[pltpu.VMEM((tm, d), BF16), pltpu.VMEM((tm, d), F32)], name="swiglu_ffn",
        compiler_params=_cparams(2, VMEM_LIMIT))(h1, g1, wg, wu, wd, g2)


def _rope_tables(pos, period, scale=1.0):
    half = ROPE_DIM // 2
    inv = jnp.float32(ROPE_THETA) ** (-jnp.arange(half, dtype=F32) * 2.0 / ROPE_DIM)
    ang = pos.astype(F32)[:, None] * inv[None, :]
    cos, sin = jnp.cos(ang), jnp.sin(ang)
    r = np.arange(LANES) % period
    f = r % half
    lo = jnp.asarray((r < half)[None, :])
    hi = jnp.asarray(((r >= half) & (r < ROPE_DIM))[None, :])
    c = jnp.where(lo | hi, cos[:, f], 1.0)
    sa = jnp.where(lo, -sin[:, f], 0.0)
    sb = jnp.where(hi, sin[:, f], 0.0)
    return jnp.stack([c, sa, sb]) * scale


def _gate_expand_matrix():
    e = np.zeros((B_KV_HEADS, GATE_LANES, 3 * B_GROUP * HEAD_DIM), np.float32)
    for kvh in range(B_KV_HEADS):
        for g in range(B_GROUP):
            for b in range(3):
                c0 = (b * B_GROUP + g) * HEAD_DIM
                e[kvh, 3 * (kvh * B_GROUP + g) + b, c0:c0 + HEAD_DIM] = 1.0
    return jnp.asarray(e, BF16)


def _slc_weight_matrix(n_cmp_pad):
    ratio = SLC_BLOCK // CMP_STRIDE
    w = np.zeros((N_BLK_LANES, n_cmp_pad), np.float32)
    for j in range(N_BLK_LANES):
        for o, wv in enumerate(SLC_WEIGHTS):
            n = ratio * j + o - 1
            if 0 <= n < n_cmp_pad - 1:
                w[j, n] = wv
    return jnp.asarray(w, BF16)


def kernel(x, pre_mix_g, w_in, lb_table, a_norm_g, cmp_pe_k, cmp_w1_k, cmp_b1_k, cmp_w2_k, cmp_b2_k, cmp_pe_v, cmp_w1_v, cmp_b1_v, cmp_w2_v, cmp_b2_v, w_proj_a, w_proj_b, w_out, post_mix_g, pre_ffn_g, w_gate, w_up, w_down, post_ffn_g):
    bsz, t, d = x.shape
    assert bsz == 1 and d == D_MODEL and WINDOW + ATT_TILE <= t <= N_BLK_LANES * SLC_BLOCK
    x2 = x.reshape(t, d)
    n_cmp_pad = t // CMP_STRIDE
    assert n_cmp_pad % LANES == 0

    lb = jnp.cumsum(jax.nn.softmax(lb_table.astype(F32), axis=0), axis=0)[0].reshape(1, D_A)
    w = w_in[0]
    o_q, o_kc, o_ks, o_g, o_m = 4 * D_A, 4 * D_A + D_B, 4 * D_A + D_B + 2 * KV_DIM, 4 * D_A + D_B + 6 * KV_DIM, 4 * D_A + D_B + 6 * KV_DIM + 3 * B_HEADS
    w_hgrn = w[:, :o_q].astype(BF16)
    w_q = w[:, o_q:o_kc].astype(BF16)
    w_cmp = w[:, o_kc:o_ks].astype(BF16)
    w_kv = (w[:, o_ks:o_g].reshape(d, 2, 2, B_KV_HEADS, HEAD_DIM).transpose(0, 1, 3, 2, 4)
            .reshape(d, 4 * KV_DIM).astype(BF16))
    w_g = jnp.pad(w[:, o_g:o_m], ((0, 0), (0, GATE_LANES - 3 * B_HEADS))).astype(BF16)
    w_m = w[:, o_m:].astype(BF16)

    pos = jnp.arange(t)
    tab_q = _rope_tables(pos, HEAD_DIM, HEAD_DIM ** -0.5 * LOG2_E)
    tab_kv = _rope_tables(pos, LANES)
    end_pos = jnp.arange(n_cmp_pad) * CMP_STRIDE + (CMP_BLOCK - 1)
    ident = jnp.stack([jnp.ones((n_cmp_pad, LANES), F32), jnp.zeros((n_cmp_pad, LANES), F32),
                       jnp.zeros((n_cmp_pad, LANES), F32)])
    tab_cmp = jnp.concatenate([_rope_tables(end_pos, LANES), ident], axis=0)

    xn = _rmsnorm_call(x2, pre_mix_g[0].reshape(1, d))

    tm = 1024 if t % 1024 == 0 else 256
    proj4 = _proj_call(
        "proj_hgrn", xn, w_hgrn, tm=tm, tn=512, epilogue=_ep_plain,
        out_shape=jax.ShapeDtypeStruct((4, t, D_A), F32),
        out_specs=pl.BlockSpec((1, tm, 512), lambda i, j: (j // 2, i, j % 2)))
    q_hm = _proj_call(
        "proj_q", xn, w_q, tm=tm, tn=512, epilogue=_ep_q,
        out_shape=jax.ShapeDtypeStruct((B_HEADS // 2, t, LANES), F32),
        out_specs=pl.BlockSpec((512 // LANES, tm, LANES), lambda i, j: (j, i, 0)),
        extra=(tab_q,), extra_specs=(pl.BlockSpec((3, tm, LANES), lambda i, j: (0, i, 0)),))
    cmp_in = _proj_call(
        "proj_cmp", xn, w_cmp, tm=tm, tn=2 * KV_DIM, epilogue=_ep_cmp,
        out_shape=jax.ShapeDtypeStruct((2, B_KV_HEADS, t, HEAD_DIM), F32),
        out_specs=pl.BlockSpec((2, B_KV_HEADS, tm, HEAD_DIM), lambda i, j: (0, 0, i, 0)))
    kv_aug, v_aug = _proj_call(
        "proj_kv", xn, w_kv, tm=tm, tn=2 * KV_DIM, epilogue=functools.partial(_ep_kv, tm=tm),
        out_shape=[jax.ShapeDtypeStruct((2, B_KV_HEADS, t, 2 * LANES), BF16),
                   jax.ShapeDtypeStruct((2, B_KV_HEADS, t, LANES), BF16)],
        out_specs=[pl.BlockSpec((1, B_KV_HEADS, tm, 2 * LANES), lambda i, j: (j, 0, i, 0)),
                   pl.BlockSpec((1, B_KV_HEADS, tm, LANES), lambda i, j: (j, 0, i, 0))],
        extra=(tab_kv,), extra_specs=(pl.BlockSpec((3, tm, LANES), lambda i, j: (0, i, 0)),))
    gates = _proj_call(
        "proj_gate", xn, w_g, tm=tm, tn=GATE_LANES, epilogue=_ep_gate,
        out_shape=jax.ShapeDtypeStruct((t, GATE_LANES), F32),
        out_specs=pl.BlockSpec((tm, GATE_LANES), lambda i, j: (i, 0)))
    sg = _proj_call(
        "proj_merge", xn, w_m, tm=tm, tn=512, epilogue=_ep_sigmoid,
        out_shape=jax.ShapeDtypeStruct((2, t, d), BF16),
        out_specs=pl.BlockSpec((1, tm, 512), lambda i, j: (j // 4, i, j % 4)))

    oa = _hgrn_call(proj4, lb, a_norm_g[0].reshape(1, D_A))

    half = CMP_BLOCK // 2
    x16 = cmp_in.reshape(2, B_KV_HEADS, n_cmp_pad, CMP_STRIDE * HEAD_DIM)
    pe2 = jnp.stack([cmp_pe_k[0], cmp_pe_v[0]]).reshape(2, 2, half * HEAD_DIM)
    w1 = jnp.stack([cmp_w1_k[0], cmp_w1_v[0]]).astype(BF16)
    b1 = jnp.stack([cmp_b1_k[0], cmp_b1_v[0]]).reshape(2, 1, CMP_HIDDEN)
    w2p = jnp.pad(jnp.stack([cmp_w2_k[0], cmp_w2_v[0]]), ((0, 0), (0, 0), (0, LANES - HEAD_DIM))).astype(BF16)
    b2p = jnp.pad(jnp.stack([cmp_b2_k[0], cmp_b2_v[0]]), ((0, 0), (0, LANES - HEAD_DIM))).reshape(2, 1, LANES)
    cmp_aug = _compress_call(x16, pe2, w1, b1, w2p, b2p, tab_cmp)
    ocmp, bias = _nsa_select_call(q_hm, cmp_aug, _slc_weight_matrix(n_cmp_pad))
    ob = _nsa_attend_call(q_hm, bias, ocmp, gates, kv_aug, v_aug)

    h1 = _post_call(oa, ob, sg, x2, w_proj_a[0].astype(BF16), w_proj_b[0].astype(BF16),
                    w_out[0].astype(BF16), post_mix_g[0].reshape(1, d))
    out = _ffn_call(h1, pre_ffn_g[0].reshape(1, d), w_gate[0].astype(BF16), w_up[0].astype(BF16),
                    w_down[0].astype(BF16), post_ffn_g[0].reshape(1, d))
    return out.reshape(bsz, t, d)
```

```python
import functools

import numpy as np
import jax
import jax.numpy as jnp
from jax import lax
from jax.experimental import pallas as pl
from jax.experimental.pallas import tpu as pltpu

F32 = jnp.float32
BF16 = jnp.bfloat16

D_MODEL = 2048
D_A = 1024
D_B = 1024
A_HEADS = 8
A_DK = 128
HGRN_PAIRWISE = 4
B_HEADS = 16
B_KV_HEADS = 4
B_GROUP = 4
HEAD_DIM = 64
KV_DIM = 256
CMP_BLOCK = 32
CMP_STRIDE = 16
CMP_HIDDEN = 256
SLC_BLOCK = 64
SLC_TOP_N = 16
SLC_FORCED = 3
SLC_WEIGHTS = (1.0, 2.0, 2.0, 2.0, 1.0)
WINDOW = 512
Q_TILE = 512
ATT_TILE = 256
KV_PER_ATTEND_STEP = 2
KV_PER_SELECT_STEP = 2
PROJ_SUBTILE = 256
GATE_LANES = 256
ROPE_THETA = 500000.0
ROPE_DIM = 16
D_FF = 5632
EPS = 1e-6
LOG2_E = 1.4426950408889634
NEG_INF = -1e30
N_BLK_LANES = 128
LANES = 128
VMEM_LIMIT = 56 * 1024 * 1024

NT_DIMS = (((1,), (1,)), ((), ()))
TN_DIMS = (((0,), (0,)), ((), ()))


def _cparams(n_axes, vmem=None, fuse_inputs=None):
    return pltpu.CompilerParams(dimension_semantics=("arbitrary",) * n_axes,
                                vmem_limit_bytes=vmem, allow_input_fusion=fuse_inputs)


def _rmsnorm_kernel(x_ref, g_ref, o_ref):
    x = x_ref[...]
    y = x * lax.rsqrt(jnp.mean(x * x, axis=-1, keepdims=True) + EPS)
    o_ref[...] = (y * g_ref[...]).astype(o_ref.dtype)


def _rmsnorm_call(x2, g, tm=512):
    t, d = x2.shape
    return pl.pallas_call(
        _rmsnorm_kernel, grid=(t // tm,),
        in_specs=[pl.BlockSpec((tm, d), lambda i: (i, 0)), pl.BlockSpec((1, d), lambda i: (0, 0))],
        out_specs=pl.BlockSpec((tm, d), lambda i: (i, 0)),
        out_shape=jax.ShapeDtypeStruct((t, d), BF16), name="pre_mix_rmsnorm",
        compiler_params=_cparams(1))(x2, g)


def _proj_call(name, xn, w_t, *, tm, tn, epilogue, out_shape, out_specs, extra=(), extra_specs=()):
    t, k = xn.shape
    n = w_t.shape[0]
    n_extra = len(extra)

    def body(x_ref, w_ref, *rest):
        x = x_ref[...]
        for c0 in range(0, tn, PROJ_SUBTILE):
            acc = lax.dot_general(x, w_ref[c0:c0 + PROJ_SUBTILE, :], NT_DIMS, preferred_element_type=F32)
            epilogue(acc, c0, rest[:n_extra], rest[n_extra:])

    assert tn % PROJ_SUBTILE == 0
    return pl.pallas_call(
        body, grid=(t // tm, n // tn),
        in_specs=[pl.BlockSpec((tm, k), lambda i, j: (i, 0)),
                  pl.BlockSpec((tn, k), lambda i, j: (j, 0)), *extra_specs],
        out_specs=out_specs, out_shape=out_shape, name=name,
        compiler_params=_cparams(2, VMEM_LIMIT, [False, True] + [False] * n_extra))(xn, w_t, *extra)


def _rope_tile(a, tab_ref):
    return (a * tab_ref[0] + pltpu.roll(a, LANES - ROPE_DIM // 2, 1) * tab_ref[1]
            + pltpu.roll(a, ROPE_DIM // 2, 1) * tab_ref[2])


def _ep_plain(acc, c0, extra, outs):
    outs[0][0, :, c0:c0 + acc.shape[1]] = acc.astype(outs[0].dtype)


def _ep_q(acc, c0, extra, outs):
    for pair in range(acc.shape[1] // LANES):
        outs[0][c0 // LANES + pair] = _rope_tile(acc[:, pair * LANES:(pair + 1) * LANES], extra[0])


def _ep_cmp(acc, c0, extra, outs):
    for hh in range(acc.shape[1] // HEAD_DIM):
        head = c0 // HEAD_DIM + hh
        outs[0][head // B_KV_HEADS, head % B_KV_HEADS] = acc[:, hh * HEAD_DIM:(hh + 1) * HEAD_DIM]


def _ep_kv(acc, c0, extra, outs, *, tm):
    i = pl.program_id(0)
    j = pl.program_id(1)
    rowg = i * tm + lax.broadcasted_iota(jnp.int32, (tm, LANES), 0)
    lane = lax.broadcasted_iota(jnp.int32, (tm, LANES), 1)
    aux = jnp.where(j == 0, (rowg // SLC_BLOCK == lane).astype(F32), 0.0).astype(BF16)
    for hh in range(acc.shape[1] // LANES):
        h = c0 // LANES + hh
        r = _rope_tile(acc[:, hh * LANES:(hh + 1) * LANES], extra[0])
        outs[0][0, h, :, 0:LANES] = r.astype(BF16)
        outs[0][0, h, :, LANES:2 * LANES] = aux
        outs[1][0, h] = jnp.where(lane < HEAD_DIM, pltpu.roll(r, HEAD_DIM, 1), 1.0).astype(BF16)


def _ep_gate(acc, c0, extra, outs):
    outs[0][:, c0:c0 + acc.shape[1]] = jax.nn.sigmoid(acc)


def _ep_sigmoid(acc, c0, extra, outs):
    outs[0][0, :, c0:c0 + acc.shape[1]] = jax.nn.sigmoid(acc).astype(outs[0].dtype)


def _split3(x):
    hi = x.astype(BF16)
    r1 = x - hi.astype(F32)
    mid = r1.astype(BF16)
    lo = (r1 - mid.astype(F32)).astype(BF16)
    return hi, mid, lo


def _hgrn_kernel(q_ref, f_ref, i_ref, g_ref, lb_ref, gn_ref, o_ref, st_ref, *, chunk, heads):
    c = pl.program_id(1)

    @pl.when(c == 0)
    def _():
        st_ref[...] = jnp.zeros_like(st_ref)

    row = lax.broadcasted_iota(jnp.int32, (chunk, chunk), 0)
    col = lax.broadcasted_iota(jnp.int32, (chunk, chunk), 1)
    tri = (col <= row).astype(BF16)
    rowv = lax.broadcasted_iota(jnp.int32, (chunk, A_DK), 0)

    for hb in range(heads):
        sl = slice(hb * A_DK, (hb + 1) * A_DK)
        q = q_ref[0, :, sl]
        ii = i_ref[0, :, sl]
        gg = g_ref[0, :, sl]
        lbv = lb_ref[:, sl]
        f = lbv + (1.0 - lbv) * jax.nn.sigmoid(f_ref[0, :, sl])
        lf = jnp.log(f)
        k = 1.0 - f
        b = sum(jnp.dot(tri, p, preferred_element_type=F32) for p in _split3(lf)) * LOG2_E
        b_end = b[chunk - 1:chunk, :]
        ii16 = ii.astype(BF16)

        st = st_ref[hb]
        o = lax.dot_general((q * jnp.exp2(b)).astype(BF16), st.astype(BF16), NT_DIMS,
                            preferred_element_type=F32)
        kd = (k * jnp.exp2(b_end - b)).astype(BF16)
        st_ref[hb] = st * jnp.exp2(b_end) + lax.dot_general(ii16, kd, TN_DIMS,
                                                            preferred_element_type=F32)

        att = jnp.where(col == row, jnp.sum(q * k, axis=-1, keepdims=True), 0.0)
        for d in range(1, HGRN_PAIRWISE):
            valid = (rowv % HGRN_PAIRWISE) >= d
            w = jnp.exp2(b - pltpu.roll(b, d, 0))
            p = jnp.where(valid, q * pltpu.roll(k, d, 0) * w, 0.0)
            att = att + jnp.where(col == row - d, jnp.sum(p, axis=-1, keepdims=True), 0.0)
        m = HGRN_PAIRWISE
        while m < chunk:
            grp = chunk // (2 * m)
            b3 = b.reshape(grp, 2 * m, A_DK)
            refrow = jnp.broadcast_to(b3[:, m - 1:m, :], (grp, 2 * m, A_DK)).reshape(chunk, A_DK)
            second = (rowv % (2 * m)) >= m
            ql = jnp.where(second, q * jnp.exp2(b - refrow), 0.0)
            kl = jnp.where(second, 0.0, k * jnp.exp2(refrow - b))
            a = lax.dot_general(ql.astype(BF16), kl.astype(BF16), NT_DIMS, preferred_element_type=F32)
            if grp > 1:
                a = jnp.where(row // (2 * m) == col // (2 * m), a, 0.0)
            att = att + a
            m *= 2
        o = o + jnp.dot(att.astype(BF16), ii16, preferred_element_type=F32)

        o = o * lax.rsqrt(jnp.mean(o * o, axis=-1, keepdims=True) + EPS)
        o_ref[:, sl] = ((o * gn_ref[:, sl]) * (gg * jax.nn.sigmoid(gg))).astype(o_ref.dtype)


def _hgrn_call(proj4, lb, gn, *, chunk=128, heads=8):
    _, t, _ = proj4.shape
    w = heads * A_DK

    def spec(kind):
        return pl.BlockSpec((1, chunk, w), lambda h, c, kind=kind: (kind, c, h))

    vec = pl.BlockSpec((1, w), lambda h, c: (0, h))
    return pl.pallas_call(
        functools.partial(_hgrn_kernel, chunk=chunk, heads=heads),
        grid=(A_HEADS // heads, t // chunk),
        in_specs=[spec(0), spec(1), spec(2), spec(3), vec, vec],
        out_specs=pl.BlockSpec((chunk, w), lambda h, c: (c, h)),
        out_shape=jax.ShapeDtypeStruct((t, D_A), BF16),
        scratch_shapes=[pltpu.VMEM((heads, A_DK, A_DK), F32)], name="hgrn2_scan",
        compiler_params=_cparams(2))(proj4, proj4, proj4, proj4, lb, gn)


def _compress_kernel(x_ref, pe_ref, w1_ref, b1_ref, w2_ref, b2_ref, tab_ref, o_ref):
    half = CMP_BLOCK * HEAD_DIM // 2
    x = x_ref[0, 0]
    top = jnp.dot((x + pe_ref[0, 0:1, :]).astype(BF16), w1_ref[0, :half, :], preferred_element_type=F32)
    bot = jnp.dot((x + pe_ref[0, 1:2, :]).astype(BF16), w1_ref[0, half:, :], preferred_element_type=F32)
    n = x.shape[0]
    h = jax.nn.gelu(top + pltpu.roll(bot, n - 1, 0) + b1_ref[0])
    y = jnp.dot(h.astype(BF16), w2_ref[0], preferred_element_type=F32) + b2_ref[0]
    y = _rope_tile(y, tab_ref)
    hi = y.astype(BF16)
    lo = (y - hi.astype(F32)).astype(BF16)
    o_ref[0, 0, :, 0:LANES] = (y + pltpu.roll(y, HEAD_DIM, 1)).astype(BF16)
    o_ref[0, 0, :, LANES:2 * LANES] = lo


def _compress_call(x16, pe2, w1, b1, w2p, b2p, tab):
    _, nh, n, wid = x16.shape
    hid = w1.shape[-1]
    return pl.pallas_call(
        _compress_kernel, grid=(2, nh),
        in_specs=[pl.BlockSpec((1, 1, n, wid), lambda kv, h: (kv, h, 0, 0)),
                  pl.BlockSpec((1, 2, wid), lambda kv, h: (kv, 0, 0)),
                  pl.BlockSpec((1, 2 * wid, hid), lambda kv, h: (kv, 0, 0)),
                  pl.BlockSpec((1, 1, hid), lambda kv, h: (kv, 0, 0)),
                  pl.BlockSpec((1, hid, LANES), lambda kv, h: (kv, 0, 0)),
                  pl.BlockSpec((1, 1, LANES), lambda kv, h: (kv, 0, 0)),
                  pl.BlockSpec((3, n, LANES), lambda kv, h: (kv, 0, 0))],
        out_specs=pl.BlockSpec((1, 1, n, 2 * LANES), lambda kv, h: (kv, h, 0, 0)),
        out_shape=jax.ShapeDtypeStruct((2, nh, n, 2 * LANES), BF16),
        name="compress_mlp", compiler_params=_cparams(2, VMEM_LIMIT))(x16, pe2, w1, b1, w2p, b2p, tab)


def _nsa_select_kernel(q_ref, kc_ref, vc_ref, wt_ref, ocmp_ref, bias_ref, lhs_ref):
    qt = pl.program_id(1)
    q0 = qt * Q_TILE
    rows = B_GROUP * Q_TILE
    n_cmp = kc_ref.shape[2]

    def chains(n_cols, n_blk):
        t_col = q0 + lax.broadcasted_iota(jnp.int32, (rows, 1), 0) % Q_TILE
        n_idx = lax.broadcasted_iota(jnp.int32, (rows, n_cols), 1)
        vis = n_idx <= (t_col - (CMP_BLOCK - 1)) // CMP_STRIDE
        blk = lax.broadcasted_iota(jnp.int32, (n_blk, Q_TILE), 0)
        tok = q0 + lax.broadcasted_iota(jnp.int32, (n_blk, Q_TILE), 1)
        cur = tok // SLC_BLOCK
        forced = (blk == 0) | (blk == cur) | (blk == cur - 1)
        candidate = (blk * SLC_BLOCK <= tok) & jnp.logical_not(forced)
        wt = wt_ref[0:n_blk, 0:n_cols]
        zeros64 = jnp.zeros((rows, HEAD_DIM), BF16)
        never = jnp.zeros((N_BLK_LANES - n_blk, Q_TILE), F32)

        for c in range(KV_PER_SELECT_STEP):
            for pair in range(B_GROUP // 2):
                qp = q_ref[c * (B_GROUP // 2) + pair]
                hi = qp.astype(BF16)
                lo = (qp - hi.astype(F32)).astype(BF16)
                for half in range(2):
                    r = slice((2 * pair + half) * Q_TILE, (2 * pair + half + 1) * Q_TILE)
                    ln = slice(half * HEAD_DIM, (half + 1) * HEAD_DIM)
                    lhs_ref[c, r, 0:HEAD_DIM] = hi[:, ln]
                    lhs_ref[c, r, HEAD_DIM:2 * HEAD_DIM] = lo[:, ln]
                    lhs_ref[c, r, 2 * HEAD_DIM:3 * HEAD_DIM] = hi[:, ln]
            lhs_ref[c, :, 3 * HEAD_DIM:] = zeros64
            s = lax.dot_general(lhs_ref[c], kc_ref[0, c, 0:n_cols, :], NT_DIMS, preferred_element_type=F32)
            s = jnp.where(vis, s, NEG_INF)
            e = jnp.exp2(s - jnp.max(s, axis=-1, keepdims=True))
            l = jnp.sum(e, axis=-1, keepdims=True)
            p = e * jnp.where(t_col >= CMP_BLOCK - 1, 1.0 / l, 0.0)
            o_cmp = jnp.dot(p.astype(BF16), vc_ref[0, c, 0:n_cols, 0:LANES], preferred_element_type=F32)
            for g in range(B_GROUP):
                c0 = (c * B_GROUP + g) * HEAD_DIM
                ocmp_ref[:, c0:c0 + HEAD_DIM] = o_cmp[g * Q_TILE:(g + 1) * Q_TILE, :HEAD_DIM]

            psum = p[0:Q_TILE] + p[Q_TILE:2 * Q_TILE] + p[2 * Q_TILE:3 * Q_TILE] + p[3 * Q_TILE:]
            ps_hi = psum.astype(BF16)
            ps_lo = (psum - ps_hi.astype(F32)).astype(BF16)
            pslc = (lax.dot_general(wt, ps_hi, NT_DIMS, preferred_element_type=F32)
                    + lax.dot_general(wt, ps_lo, NT_DIMS, preferred_element_type=F32))
            score = jnp.where(candidate, pslc, jnp.where(forced, -jnp.inf, NEG_INF))
            sel_t = forced.astype(F32)
            for _ in range(SLC_TOP_N - SLC_FORCED):
                best = jnp.max(score, axis=0, keepdims=True)
                first = jnp.min(jnp.where(score == best, blk, N_BLK_LANES), axis=0, keepdims=True)
                hit = blk == first
                score = jnp.where(hit, -jnp.inf, score)
                sel_t = jnp.where(hit, 1.0, sel_t)
            if n_blk < N_BLK_LANES:
                sel_t = jnp.concatenate([sel_t, never], axis=0)
            bias_ref[c] = jnp.where(sel_t.T > 0.0, 0.0, NEG_INF).astype(BF16)

    ratio = SLC_BLOCK // CMP_STRIDE
    early = qt < pl.num_programs(1) // 2
    pl.when(early)(lambda: chains(n_cmp // 2, n_cmp // (2 * ratio)))
    pl.when(jnp.logical_not(early))(lambda: chains(n_cmp, N_BLK_LANES))


def _nsa_select_call(q_hm, cmp_aug, wt):
    _, t, _ = q_hm.shape
    n_cmp = cmp_aug.shape[2]
    kvs = KV_PER_SELECT_STEP
    return pl.pallas_call(
        _nsa_select_kernel, grid=(B_KV_HEADS // kvs, t // Q_TILE),
        in_specs=[pl.BlockSpec((kvs * B_GROUP // 2, Q_TILE, LANES), lambda p, i: (p, i, 0)),
                  pl.BlockSpec((1, kvs, n_cmp, 2 * LANES), lambda p, i: (0, p, 0, 0)),
                  pl.BlockSpec((1, kvs, n_cmp, 2 * LANES), lambda p, i: (1, p, 0, 0)),
                  pl.BlockSpec((N_BLK_LANES, n_cmp), lambda p, i: (0, 0))],
        out_specs=[pl.BlockSpec((Q_TILE, kvs * B_GROUP * HEAD_DIM), lambda p, i: (i, p)),
                   pl.BlockSpec((kvs, Q_TILE, N_BLK_LANES), lambda p, i: (p, i, 0))],
        out_shape=[jax.ShapeDtypeStruct((t, D_B), F32),
                   jax.ShapeDtypeStruct((B_KV_HEADS, t, N_BLK_LANES), BF16)],
        scratch_shapes=[pltpu.VMEM((kvs, B_GROUP * Q_TILE, 2 * LANES), BF16)], name="nsa_select",
        compiler_params=_cparams(2, VMEM_LIMIT))(q_hm, cmp_aug, cmp_aug, wt)


def _nsa_attend_kernel(q_ref, bias_ref, ocmp_ref, gate_ref, gx_ref, ksv_ref, kwv_ref, vs_ref, vw_ref, o_ref,
                       lhs_w_ref, lhs_s_ref, m_ref, acc_ref, *, key_tile, tiles_per_group):
    q0 = pl.program_id(1) * ATT_TILE
    rows = B_GROUP * ATT_TILE
    chains = range(KV_PER_ATTEND_STEP)
    t_col = q0 + lax.broadcasted_iota(jnp.int32, (rows, 1), 0) % ATT_TILE

    for c in chains:
        for g in range(B_GROUP):
            r = slice(g * ATT_TILE, (g + 1) * ATT_TILE)
            qp = q_ref[c * (B_GROUP // 2) + g // 2]
            qhi = qp[:, (g % 2) * HEAD_DIM:(g % 2 + 1) * HEAD_DIM].astype(BF16)
            lhs_s_ref[c, r, 0:HEAD_DIM] = qhi
            lhs_s_ref[c, r, LANES:] = bias_ref[c]
            lhs_w_ref[c, r, 0:HEAD_DIM] = qhi
        lhs_s_ref[c, :, HEAD_DIM:LANES] = jnp.zeros((rows, LANES - HEAD_DIM), BF16)
        lhs_w_ref[c, :, HEAD_DIM:] = jnp.zeros((rows, 2 * LANES - HEAD_DIM), BF16)

    for c in chains:
        m_ref[c] = jnp.full((rows, LANES), NEG_INF, F32)
        acc_ref[c] = jnp.zeros((rows, LANES), F32)

    def tile(ti, causal):
        start = pl.multiple_of(ti * key_tile, key_tile)
        for c in chains:
            st = lax.dot_general(lhs_s_ref[c], ksv_ref[0, c, pl.ds(start, key_tile), :], NT_DIMS,
                                 preferred_element_type=F32)
            if causal:
                kp = start + lax.broadcasted_iota(jnp.int32, (rows, key_tile), 1)
                st = jnp.where(kp <= t_col, st, NEG_INF)
            m_run = m_ref[c]
            m_new = jnp.maximum(m_run, jnp.max(st, axis=-1, keepdims=True))
            pt = jnp.exp2(st - jnp.concatenate([m_new] * (key_tile // LANES), axis=1))
            pv = jnp.dot(pt.astype(BF16), vs_ref[0, c, pl.ds(start, key_tile), :], preferred_element_type=F32)
            acc_ref[c] = acc_ref[c] * jnp.exp2(m_run - m_new) + pv
            m_ref[c] = m_new

    def group(gi, _):
        for u in range(tiles_per_group):
            tile(gi * tiles_per_group + u, False)
        return 0

    def single(ti, _):
        tile(ti, False)
        return 0

    n_full = q0 // key_tile
    n_groups = n_full // tiles_per_group
    lax.fori_loop(0, n_groups, group, 0)
    lax.fori_loop(n_groups * tiles_per_group, n_full, single, 0)
    tile(n_full, True)

    slab = WINDOW + ATT_TILE
    ws = pl.multiple_of(jnp.maximum(q0 - WINDOW, 0), ATT_TILE)
    kpos = ws + lax.broadcasted_iota(jnp.int32, (rows, slab), 1)
    in_window = lax.bitcast_convert_type(t_col - kpos, jnp.uint32) < jnp.uint32(WINDOW)
    lane = lax.broadcasted_iota(jnp.int32, (ATT_TILE, LANES), 1)
    per_branch = B_GROUP * HEAD_DIM

    def head_pair(acc, k):
        a_even = acc[2 * k * ATT_TILE:(2 * k + 1) * ATT_TILE]
        a_odd = acc[(2 * k + 1) * ATT_TILE:(2 * k + 2) * ATT_TILE]
        num = jnp.where(lane < HEAD_DIM, a_even, pltpu.roll(a_odd, HEAD_DIM, 1))
        den = jnp.where(lane < HEAD_DIM, pltpu.roll(a_even, HEAD_DIM, 1), a_odd)
        return num * (1.0 / den)

    gates = gate_ref[...]
    g_hi = gates.astype(BF16)
    g_lo = (gates - g_hi.astype(F32)).astype(BF16)

    for c in chains:
        acc_s = acc_ref[c]
        s = lax.dot_general(lhs_w_ref[c], kwv_ref[0, c, pl.ds(ws, slab), :], NT_DIMS,
                            preferred_element_type=F32)
        s = jnp.where(in_window, s, NEG_INF)
        e = jnp.exp2(s - jnp.max(s, axis=-1, keepdims=True))
        acc_w = jnp.dot(e.astype(BF16), vw_ref[0, c, pl.ds(ws, slab), :], preferred_element_type=F32)

        gx = (jnp.dot(g_hi, gx_ref[c], preferred_element_type=F32)
              + jnp.dot(g_lo, gx_ref[c], preferred_element_type=F32))
        for k in range(B_GROUP // 2):
            c0 = c * B_GROUP * HEAD_DIM + k * LANES
            g0 = k * LANES
            og = (gx[:, g0:g0 + LANES] * ocmp_ref[:, c0:c0 + LANES]
                  + gx[:, per_branch + g0:per_branch + g0 + LANES] * head_pair(acc_s, k)
                  + gx[:, 2 * per_branch + g0:2 * per_branch + g0 + LANES] * head_pair(acc_w, k))
            o_ref[:, c0:c0 + LANES] = og.astype(o_ref.dtype)


def _nsa_attend_call(q_hm, bias, ocmp, gates, kv_aug, v_aug, *, key_tile=512, tiles_per_group=2):
    _, t, _ = q_hm.shape
    assert t % key_tile == 0 and key_tile % ATT_TILE == 0 and t % ATT_TILE == 0
    rows = B_GROUP * ATT_TILE
    kvs = KV_PER_ATTEND_STEP
    wide = kvs * B_GROUP * HEAD_DIM
    resident = dict(pipeline_mode=pl.Buffered(1))
    return pl.pallas_call(
        functools.partial(_nsa_attend_kernel, key_tile=key_tile, tiles_per_group=tiles_per_group),
        grid=(B_KV_HEADS // kvs, t // ATT_TILE),
        in_specs=[pl.BlockSpec((kvs * B_GROUP // 2, ATT_TILE, LANES), lambda p, i: (p, i, 0)),
                  pl.BlockSpec((kvs, ATT_TILE, N_BLK_LANES), lambda p, i: (p, i, 0)),
                  pl.BlockSpec((ATT_TILE, wide), lambda p, i: (i, p)),
                  pl.BlockSpec((ATT_TILE, GATE_LANES), lambda p, i: (i, 0)),
                  pl.BlockSpec((kvs, GATE_LANES, 3 * B_GROUP * HEAD_DIM), lambda p, i: (p, 0, 0)),
                  pl.BlockSpec((1, kvs, t, 2 * LANES), lambda p, i: (0, p, 0, 0), **resident),
                  pl.BlockSpec((1, kvs, t, 2 * LANES), lambda p, i: (1, p, 0, 0), **resident),
                  pl.BlockSpec((1, kvs, t, LANES), lambda p, i: (0, p, 0, 0), **resident),
                  pl.BlockSpec((1, kvs, t, LANES), lambda p, i: (1, p, 0, 0), **resident)],
        out_specs=pl.BlockSpec((ATT_TILE, wide), lambda p, i: (i, p)),
        out_shape=jax.ShapeDtypeStruct((t, D_B), BF16),
        scratch_shapes=[pltpu.VMEM((kvs, rows, 2 * LANES), BF16), pltpu.VMEM((kvs, rows, 2 * LANES), BF16),
                        pltpu.VMEM((kvs, rows, LANES), F32), pltpu.VMEM((kvs, rows, LANES), F32)],
        name="nsa_attend",
        compiler_params=_cparams(2, VMEM_LIMIT))(q_hm, bias, ocmp, gates, _gate_expand_matrix(),
                                                 kv_aug, kv_aug, v_aug, v_aug)


def _post_kernel(oa_ref, ob_ref, sg_ref, x_ref, wa_ref, wb_ref, wo_ref, g_ref, o_ref):
    ya = jnp.dot(oa_ref[...], wa_ref[...], preferred_element_type=F32)
    yb = jnp.dot(ob_ref[...], wb_ref[...], preferred_element_type=F32)
    merged = sg_ref[0].astype(F32) * ya + sg_ref[1].astype(F32) * yb
    mix = jnp.dot(merged.astype(BF16), wo_ref[...], preferred_element_type=F32)
    y = mix * lax.rsqrt(jnp.mean(mix * mix, axis=-1, keepdims=True) + EPS)
    o_ref[...] = x_ref[...] + y * g_ref[...]


def _post_call(oa, ob, sg, x2, wa, wb, wo, g, tm=256):
    t, d = x2.shape
    const = dict(pipeline_mode=pl.Buffered(1))
    return pl.pallas_call(
        _post_kernel, grid=(t // tm,),
        in_specs=[pl.BlockSpec((tm, D_A), lambda i: (i, 0)),
                  pl.BlockSpec((tm, D_B), lambda i: (i, 0)),
                  pl.BlockSpec((2, tm, d), lambda i: (0, i, 0)),
                  pl.BlockSpec((tm, d), lambda i: (i, 0)),
                  pl.BlockSpec((D_A, d), lambda i: (0, 0), **const),
                  pl.BlockSpec((D_B, d), lambda i: (0, 0), **const),
                  pl.BlockSpec((d, d), lambda i: (0, 0), **const),
                  pl.BlockSpec((1, d), lambda i: (0, 0))],
        out_specs=pl.BlockSpec((tm, d), lambda i: (i, 0)),
        out_shape=jax.ShapeDtypeStruct((t, d), F32), name="mix_out_residual",
        compiler_params=_cparams(1, VMEM_LIMIT))(oa, ob, sg, x2, wa, wb, wo, g)


def _ffn_kernel(h_ref, g1_ref, wg_ref, wu_ref, wd_ref, g2_ref, o_ref, hn_ref, acc_ref):
    f = pl.program_id(1)

    @pl.when(f == 0)
    def _():
        h = h_ref[...]
        y = h * lax.rsqrt(jnp.mean(h * h, axis=-1, keepdims=True) + EPS)
        hn_ref[...] = (y * g1_ref[...]).astype(BF16)
        acc_ref[...] = jnp.zeros_like(acc_ref)

    hn = hn_ref[...]
    a = jnp.dot(hn, wg_ref[...], preferred_element_type=F32)
    u = jnp.dot(hn, wu_ref[...], preferred_element_type=F32)
    z = (a * jax.nn.sigmoid(a) * u).astype(BF16)
    acc_ref[...] += jnp.dot(z, wd_ref[...], preferred_element_type=F32)

    @pl.when(f == pl.num_programs(1) - 1)
    def _():
        ff = acc_ref[...]
        y = ff * lax.rsqrt(jnp.mean(ff * ff, axis=-1, keepdims=True) + EPS)
        o_ref[...] = h_ref[...] + y * g2_ref[...]


def _ffn_call(h1, g1, wg, wu, wd, g2, tm=512, tf=512):
    t, d = h1.shape
    dff = wg.shape[1]
    return pl.pallas_call(
        _ffn_kernel, grid=(t // tm, dff // tf),
        in_specs=[pl.BlockSpec((tm, d), lambda i, f: (i, 0)),
                  pl.BlockSpec((1, d), lambda i, f: (0, 0)),
                  pl.BlockSpec((d, tf), lambda i, f: (0, f)),
                  pl.BlockSpec((d, tf), lambda i, f: (0, f)),
                  pl.BlockSpec((tf, d), lambda i, f: (f, 0)),
                  pl.BlockSpec((1, d), lambda i, f: (0, 0))],
        out_specs=pl.BlockSpec((tm, d), lambda i, f: (i, 0)),
        out_shape=jax.ShapeDtypeStruct((t, d), F32),
        scratch_shapes=[pltpu.VMEM((tm, d), BF16), pltpu.VMEM((tm, d), F32)], name="swiglu_ffn",
        compiler_params=_cparams(2, VMEM_LIMIT))(h1, g1, wg, wu, wd, g2)


def _rope_tables(pos, period, scale=1.0):
    half = ROPE_DIM // 2
    inv = jnp.float32(ROPE_THETA) ** (-jnp.arange(half, dtype=F32) * 2.0 / ROPE_DIM)
    ang = pos.astype(F32)[:, None] * inv[None, :]
    cos, sin = jnp.cos(ang), jnp.sin(ang)
    r = np.arange(LANES) % period
    f = r % half
    lo = jnp.asarray((r < half)[None, :])
    hi = jnp.asarray(((r >= half) & (r < ROPE_DIM))[None, :])
    c = jnp.where(lo | hi, cos[:, f], 1.0)
    sa = jnp.where(lo, -sin[:, f], 0.0)
    sb = jnp.where(hi, sin[:, f], 0.0)
    return jnp.stack([c, sa, sb]) * scale


def _gate_expand_matrix():
    e = np.zeros((B_KV_HEADS, GATE_LANES, 3 * B_GROUP * HEAD_DIM), np.float32)
    for kvh in range(B_KV_HEADS):
        for g in range(B_GROUP):
            for b in range(3):
                c0 = (b * B_GROUP + g) * HEAD_DIM
                e[kvh, 3 * (kvh * B_GROUP + g) + b, c0:c0 + HEAD_DIM] = 1.0
    return jnp.asarray(e, BF16)


def _slc_weight_matrix(n_cmp_pad):
    ratio = SLC_BLOCK // CMP_STRIDE
    w = np.zeros((N_BLK_LANES, n_cmp_pad), np.float32)
    for j in range(N_BLK_LANES):
        for o, wv in enumerate(SLC_WEIGHTS):
            n = ratio * j + o - 1
            if 0 <= n < n_cmp_pad - 1:
                w[j, n] = wv
    return jnp.asarray(w, BF16)


def kernel(x, pre_mix_g, w_in, lb_table, a_norm_g, cmp_pe_k, cmp_w1_k, cmp_b1_k, cmp_w2_k, cmp_b2_k, cmp_pe_v, cmp_w1_v, cmp_b1_v, cmp_w2_v, cmp_b2_v, w_proj_a, w_proj_b, w_out, post_mix_g, pre_ffn_g, w_gate, w_up, w_down, post_ffn_g):
    bsz, t, d = x.shape
    assert bsz == 1 and d == D_MODEL and WINDOW + ATT_TILE <= t <= N_BLK_LANES * SLC_BLOCK
    x2 = x.reshape(t, d)
    n_cmp_pad = t // CMP_STRIDE
    assert n_cmp_pad % LANES == 0

    lb = jnp.cumsum(jax.nn.softmax(lb_table.astype(F32), axis=0), axis=0)[0].reshape(1, D_A)
    w = jnp.swapaxes(w_in[0], 0, 1)
    o_q, o_kc, o_ks, o_g, o_m = 4 * D_A, 4 * D_A + D_B, 4 * D_A + D_B + 2 * KV_DIM, 4 * D_A + D_B + 6 * KV_DIM, 4 * D_A + D_B + 6 * KV_DIM + 3 * B_HEADS
    w_hgrn = w[:o_q].astype(BF16)
    w_q = w[o_q:o_kc].astype(BF16)
    w_cmp = w[o_kc:o_ks].astype(BF16)
    w_kv = (w[o_ks:o_g].reshape(2, 2, B_KV_HEADS, HEAD_DIM, d).transpose(0, 2, 1, 3, 4)
            .reshape(4 * KV_DIM, d).astype(BF16))
    w_g = jnp.pad(w[o_g:o_m], ((0, GATE_LANES - 3 * B_HEADS), (0, 0))).astype(BF16)
    w_m = w[o_m:].astype(BF16)

    pos = jnp.arange(t)
    tab_q = _rope_tables(pos, HEAD_DIM, HEAD_DIM ** -0.5 * LOG2_E)
    tab_kv = _rope_tables(pos, LANES)
    end_pos = jnp.arange(n_cmp_pad) * CMP_STRIDE + (CMP_BLOCK - 1)
    ident = jnp.stack([jnp.ones((n_cmp_pad, LANES), F32), jnp.zeros((n_cmp_pad, LANES), F32),
                       jnp.zeros((n_cmp_pad, LANES), F32)])
    tab_cmp = jnp.concatenate([_rope_tables(end_pos, LANES), ident], axis=0)

    xn = _rmsnorm_call(x2, pre_mix_g[0].reshape(1, d))

    tm = 1024 if t % 1024 == 0 else 256
    proj4 = _proj_call(
        "proj_hgrn", xn, w_hgrn, tm=tm, tn=512, epilogue=_ep_plain,
        out_shape=jax.ShapeDtypeStruct((4, t, D_A), F32),
        out_specs=pl.BlockSpec((1, tm, 512), lambda i, j: (j // 2, i, j % 2)))
    q_hm = _proj_call(
        "proj_q", xn, w_q, tm=tm, tn=512, epilogue=_ep_q,
        out_shape=jax.ShapeDtypeStruct((B_HEADS // 2, t, LANES), F32),
        out_specs=pl.BlockSpec((512 // LANES, tm, LANES), lambda i, j: (j, i, 0)),
        extra=(tab_q,), extra_specs=(pl.BlockSpec((3, tm, LANES), lambda i, j: (0, i, 0)),))
    cmp_in = _proj_call(
        "proj_cmp", xn, w_cmp, tm=tm, tn=2 * KV_DIM, epilogue=_ep_cmp,
        out_shape=jax.ShapeDtypeStruct((2, B_KV_HEADS, t, HEAD_DIM), F32),
        out_specs=pl.BlockSpec((2, B_KV_HEADS, tm, HEAD_DIM), lambda i, j: (0, 0, i, 0)))
    kv_aug, v_aug = _proj_call(
        "proj_kv", xn, w_kv, tm=tm, tn=2 * KV_DIM, epilogue=functools.partial(_ep_kv, tm=tm),
        out_shape=[jax.ShapeDtypeStruct((2, B_KV_HEADS, t, 2 * LANES), BF16),
                   jax.ShapeDtypeStruct((2, B_KV_HEADS, t, LANES), BF16)],
        out_specs=[pl.BlockSpec((1, B_KV_HEADS, tm, 2 * LANES), lambda i, j: (j, 0, i, 0)),
                   pl.BlockSpec((1, B_KV_HEADS, tm, LANES), lambda i, j: (j, 0, i, 0))],
        extra=(tab_kv,), extra_specs=(pl.BlockSpec((3, tm, LANES), lambda i, j: (0, i, 0)),))
    gates = _proj_call(
        "proj_gate", xn, w_g, tm=tm, tn=GATE_LANES, epilogue=_ep_gate,
        out_shape=jax.ShapeDtypeStruct((t, GATE_LANES), F32),
        out_specs=pl.BlockSpec((tm, GATE_LANES), lambda i, j: (i, 0)))
    sg = _proj_call(
        "proj_merge", xn, w_m, tm=tm, tn=512, epilogue=_ep_sigmoid,
        out_shape=jax.ShapeDtypeStruct((2, t, d), BF16),
        out_specs=pl.BlockSpec((1, tm, 512), lambda i, j: (j // 4, i, j % 4)))

    oa = _hgrn_call(proj4, lb, a_norm_g[0].reshape(1, D_A))

    half = CMP_BLOCK // 2
    x16 = cmp_in.reshape(2, B_KV_HEADS, n_cmp_pad, CMP_STRIDE * HEAD_DIM)
    pe2 = jnp.stack([cmp_pe_k[0], cmp_pe_v[0]]).reshape(2, 2, half * HEAD_DIM)
    w1 = jnp.stack([cmp_w1_k[0], cmp_w1_v[0]]).astype(BF16)
    b1 = jnp.stack([cmp_b1_k[0], cmp_b1_v[0]]).reshape(2, 1, CMP_HIDDEN)
    w2p = jnp.pad(jnp.stack([cmp_w2_k[0], cmp_w2_v[0]]), ((0, 0), (0, 0), (0, LANES - HEAD_DIM))).astype(BF16)
    b2p = jnp.pad(jnp.stack([cmp_b2_k[0], cmp_b2_v[0]]), ((0, 0), (0, LANES - HEAD_DIM))).reshape(2, 1, LANES)
    cmp_aug = _compress_call(x16, pe2, w1, b1, w2p, b2p, tab_cmp)
    ocmp, bias = _nsa_select_call(q_hm, cmp_aug, _slc_weight_matrix(n_cmp_pad))
    ob = _nsa_attend_call(q_hm, bias, ocmp, gates, kv_aug, v_aug)

    h1 = _post_call(oa, ob, sg, x2, w_proj_a[0].astype(BF16), w_proj_b[0].astype(BF16),
                    w_out[0].astype(BF16), post_mix_g[0].reshape(1, d))
    out = _ffn_call(h1, pre_ffn_g[0].reshape(1, d), w_gate[0].astype(BF16), w_up[0].astype(BF16),
                    w_down[0].astype(BF16), post_ffn_g[0].reshape(1, d))
    return out.reshape(bsz, t, d)
```

```python
import functools

import numpy as np
import jax
import jax.numpy as jnp
from jax import lax
from jax.experimental import pallas as pl
from jax.experimental.pallas import tpu as pltpu

F32 = jnp.float32
BF16 = jnp.bfloat16

D_MODEL = 2048
D_A = 1024
D_B = 1024
A_HEADS = 8
A_DK = 128
HGRN_PAIRWISE = 4
B_HEADS = 16
B_KV_HEADS = 4
B_GROUP = 4
HEAD_DIM = 64
KV_DIM = 256
CMP_BLOCK = 32
CMP_STRIDE = 16
CMP_HIDDEN = 256
SLC_BLOCK = 64
SLC_TOP_N = 16
SLC_FORCED = 3
SLC_WEIGHTS = (1.0, 2.0, 2.0, 2.0, 1.0)
WINDOW = 512
Q_TILE = 512
ATT_TILE = 256
KV_PER_ATTEND_STEP = 2
KV_PER_SELECT_STEP = 2
PROJ_SUBTILE = 256
GATE_LANES = 256
ROPE_THETA = 500000.0
ROPE_DIM = 16
D_FF = 5632
EPS = 1e-6
LOG2_E = 1.4426950408889634
NEG_INF = -1e30
N_BLK_LANES = 128
LANES = 128
VMEM_LIMIT = 56 * 1024 * 1024

NT_DIMS = (((1,), (1,)), ((), ()))
TN_DIMS = (((0,), (0,)), ((), ()))


def _cparams(n_axes, vmem=None):
    return pltpu.CompilerParams(dimension_semantics=("arbitrary",) * n_axes,
                                vmem_limit_bytes=vmem)


def _rmsnorm_kernel(x_ref, g_ref, o_ref):
    x = x_ref[...]
    y = x * lax.rsqrt(jnp.mean(x * x, axis=-1, keepdims=True) + EPS)
    o_ref[...] = (y * g_ref[...]).astype(o_ref.dtype)


def _rmsnorm_call(x2, g, tm=512):
    t, d = x2.shape
    return pl.pallas_call(
        _rmsnorm_kernel, grid=(t // tm,),
        in_specs=[pl.BlockSpec((tm, d), lambda i: (i, 0)), pl.BlockSpec((1, d), lambda i: (0, 0))],
        out_specs=pl.BlockSpec((tm, d), lambda i: (i, 0)),
        out_shape=jax.ShapeDtypeStruct((t, d), BF16), name="pre_mix_rmsnorm",
        compiler_params=_cparams(1))(x2, g)


def _proj_call(name, xn, w_t, *, tm, tn, epilogue, out_shape, out_specs, extra=(), extra_specs=()):
    t, k = xn.shape
    n = w_t.shape[0]
    n_extra = len(extra)

    def body(x_ref, w_ref, *rest):
        x = x_ref[...]
        for c0 in range(0, tn, PROJ_SUBTILE):
            acc = lax.dot_general(x, w_ref[c0:c0 + PROJ_SUBTILE, :], NT_DIMS, preferred_element_type=F32)
            epilogue(acc, c0, rest[:n_extra], rest[n_extra:])

    assert tn % PROJ_SUBTILE == 0
    return pl.pallas_call(
        body, grid=(t // tm, n // tn),
        in_specs=[pl.BlockSpec((tm, k), lambda i, j: (i, 0)),
                  pl.BlockSpec((tn, k), lambda i, j: (j, 0)), *extra_specs],
        out_specs=out_specs, out_shape=out_shape, name=name,
        compiler_params=_cparams(2, VMEM_LIMIT))(xn, w_t, *extra)


def _rope_tile(a, tab_ref):
    return (a * tab_ref[0] + pltpu.roll(a, LANES - ROPE_DIM // 2, 1) * tab_ref[1]
            + pltpu.roll(a, ROPE_DIM // 2, 1) * tab_ref[2])


def _ep_plain(acc, c0, extra, outs):
    outs[0][0, :, c0:c0 + acc.shape[1]] = acc.astype(outs[0].dtype)


def _ep_q(acc, c0, extra, outs):
    for pair in range(acc.shape[1] // LANES):
        outs[0][c0 // LANES + pair] = _rope_tile(acc[:, pair * LANES:(pair + 1) * LANES], extra[0])


def _ep_cmp(acc, c0, extra, outs):
    for hh in range(acc.shape[1] // HEAD_DIM):
        head = c0 // HEAD_DIM + hh
        outs[0][head // B_KV_HEADS, head % B_KV_HEADS] = acc[:, hh * HEAD_DIM:(hh + 1) * HEAD_DIM]


def _ep_kv(acc, c0, extra, outs, *, tm):
    i = pl.program_id(0)
    j = pl.program_id(1)
    rowg = i * tm + lax.broadcasted_iota(jnp.int32, (tm, LANES), 0)
    lane = lax.broadcasted_iota(jnp.int32, (tm, LANES), 1)
    aux = jnp.where(j == 0, (rowg // SLC_BLOCK == lane).astype(F32), 0.0).astype(BF16)
    for hh in range(acc.shape[1] // LANES):
        h = c0 // LANES + hh
        r = _rope_tile(acc[:, hh * LANES:(hh + 1) * LANES], extra[0])
        outs[0][0, h, :, 0:LANES] = r.astype(BF16)
        outs[0][0, h, :, LANES:2 * LANES] = aux
        outs[1][0, h] = jnp.where(lane < HEAD_DIM, pltpu.roll(r, HEAD_DIM, 1), 1.0).astype(BF16)


def _ep_gate(acc, c0, extra, outs):
    outs[0][:, c0:c0 + acc.shape[1]] = jax.nn.sigmoid(acc)


def _ep_sigmoid(acc, c0, extra, outs):
    outs[0][0, :, c0:c0 + acc.shape[1]] = jax.nn.sigmoid(acc).astype(outs[0].dtype)


def _split3(x):
    hi = x.astype(BF16)
    r1 = x - hi.astype(F32)
    mid = r1.astype(BF16)
    lo = (r1 - mid.astype(F32)).astype(BF16)
    return hi, mid, lo


def _hgrn_kernel(q_ref, f_ref, i_ref, g_ref, lb_ref, gn_ref, o_ref, st_ref, *, chunk, heads):
    c = pl.program_id(1)

    @pl.when(c == 0)
    def _():
        st_ref[...] = jnp.zeros_like(st_ref)

    row = lax.broadcasted_iota(jnp.int32, (chunk, chunk), 0)
    col = lax.broadcasted_iota(jnp.int32, (chunk, chunk), 1)
    tri = (col <= row).astype(BF16)
    rowv = lax.broadcasted_iota(jnp.int32, (chunk, A_DK), 0)

    for hb in range(heads):
        sl = slice(hb * A_DK, (hb + 1) * A_DK)
        q = q_ref[0, :, sl]
        ii = i_ref[0, :, sl]
        gg = g_ref[0, :, sl]
        lbv = lb_ref[:, sl]
        f = lbv + (1.0 - lbv) * jax.nn.sigmoid(f_ref[0, :, sl])
        lf = jnp.log(f)
        k = 1.0 - f
        b = sum(jnp.dot(tri, p, preferred_element_type=F32) for p in _split3(lf)) * LOG2_E
        b_end = b[chunk - 1:chunk, :]
        ii16 = ii.astype(BF16)

        st = st_ref[hb]
        o = lax.dot_general((q * jnp.exp2(b)).astype(BF16), st.astype(BF16), NT_DIMS,
                            preferred_element_type=F32)
        kd = (k * jnp.exp2(b_end - b)).astype(BF16)
        st_ref[hb] = st * jnp.exp2(b_end) + lax.dot_general(ii16, kd, TN_DIMS,
                                                            preferred_element_type=F32)

        att = jnp.where(col == row, jnp.sum(q * k, axis=-1, keepdims=True), 0.0)
        for d in range(1, HGRN_PAIRWISE):
            valid = (rowv % HGRN_PAIRWISE) >= d
            w = jnp.exp2(b - pltpu.roll(b, d, 0))
            p = jnp.where(valid, q * pltpu.roll(k, d, 0) * w, 0.0)
            att = att + jnp.where(col == row - d, jnp.sum(p, axis=-1, keepdims=True), 0.0)
        m = HGRN_PAIRWISE
        while m < chunk:
            grp = chunk // (2 * m)
            b3 = b.reshape(grp, 2 * m, A_DK)
            refrow = jnp.broadcast_to(b3[:, m - 1:m, :], (grp, 2 * m, A_DK)).reshape(chunk, A_DK)
            second = (rowv % (2 * m)) >= m
            ql = jnp.where(second, q * jnp.exp2(b - refrow), 0.0)
            kl = jnp.where(second, 0.0, k * jnp.exp2(refrow - b))
            a = lax.dot_general(ql.astype(BF16), kl.astype(BF16), NT_DIMS, preferred_element_type=F32)
            if grp > 1:
                a = jnp.where(row // (2 * m) == col // (2 * m), a, 0.0)
            att = att + a
            m *= 2
        o = o + jnp.dot(att.astype(BF16), ii16, preferred_element_type=F32)

        o = o * lax.rsqrt(jnp.mean(o * o, axis=-1, keepdims=True) + EPS)
        o_ref[:, sl] = ((o * gn_ref[:, sl]) * (gg * jax.nn.sigmoid(gg))).astype(o_ref.dtype)


def _hgrn_call(proj4, lb, gn, *, chunk=128, heads=8):
    _, t, _ = proj4.shape
    w = heads * A_DK

    def spec(kind):
        return pl.BlockSpec((1, chunk, w), lambda h, c, kind=kind: (kind, c, h))

    vec = pl.BlockSpec((1, w), lambda h, c: (0, h))
    return pl.pallas_call(
        functools.partial(_hgrn_kernel, chunk=chunk, heads=heads),
        grid=(A_HEADS // heads, t // chunk),
        in_specs=[spec(0), spec(1), spec(2), spec(3), vec, vec],
        out_specs=pl.BlockSpec((chunk, w), lambda h, c: (c, h)),
        out_shape=jax.ShapeDtypeStruct((t, D_A), BF16),
        scratch_shapes=[pltpu.VMEM((heads, A_DK, A_DK), F32)], name="hgrn2_scan",
        compiler_params=_cparams(2))(proj4, proj4, proj4, proj4, lb, gn)


def _compress_kernel(x_ref, pe_ref, w1_ref, b1_ref, w2_ref, b2_ref, tab_ref, o_ref):
    half = CMP_BLOCK * HEAD_DIM // 2
    x = x_ref[0, 0]
    top = jnp.dot((x + pe_ref[0, 0:1, :]).astype(BF16), w1_ref[0, :half, :], preferred_element_type=F32)
    bot = jnp.dot((x + pe_ref[0, 1:2, :]).astype(BF16), w1_ref[0, half:, :], preferred_element_type=F32)
    n = x.shape[0]
    h = jax.nn.gelu(top + pltpu.roll(bot, n - 1, 0) + b1_ref[0])
    y = jnp.dot(h.astype(BF16), w2_ref[0], preferred_element_type=F32) + b2_ref[0]
    y = _rope_tile(y, tab_ref)
    hi = y.astype(BF16)
    lo = (y - hi.astype(F32)).astype(BF16)
    o_ref[0, 0, :, 0:LANES] = (y + pltpu.roll(y, HEAD_DIM, 1)).astype(BF16)
    o_ref[0, 0, :, LANES:2 * LANES] = lo


def _compress_call(x16, pe2, w1, b1, w2p, b2p, tab):
    _, nh, n, wid = x16.shape
    hid = w1.shape[-1]
    return pl.pallas_call(
        _compress_kernel, grid=(2, nh),
        in_specs=[pl.BlockSpec((1, 1, n, wid), lambda kv, h: (kv, h, 0, 0)),
                  pl.BlockSpec((1, 2, wid), lambda kv, h: (kv, 0, 0)),
                  pl.BlockSpec((1, 2 * wid, hid), lambda kv, h: (kv, 0, 0)),
                  pl.BlockSpec((1, 1, hid), lambda kv, h: (kv, 0, 0)),
                  pl.BlockSpec((1, hid, LANES), lambda kv, h: (kv, 0, 0)),
                  pl.BlockSpec((1, 1, LANES), lambda kv, h: (kv, 0, 0)),
                  pl.BlockSpec((3, n, LANES), lambda kv, h: (kv, 0, 0))],
        out_specs=pl.BlockSpec((1, 1, n, 2 * LANES), lambda kv, h: (kv, h, 0, 0)),
        out_shape=jax.ShapeDtypeStruct((2, nh, n, 2 * LANES), BF16),
        name="compress_mlp", compiler_params=_cparams(2, VMEM_LIMIT))(x16, pe2, w1, b1, w2p, b2p, tab)


def _nsa_select_kernel(q_ref, kc_ref, vc_ref, wt_ref, ocmp_ref, bias_ref, lhs_ref):
    qt = pl.program_id(1)
    q0 = qt * Q_TILE
    rows = B_GROUP * Q_TILE
    n_cmp = kc_ref.shape[2]

    def chains(n_cols, n_blk):
        t_col = q0 + lax.broadcasted_iota(jnp.int32, (rows, 1), 0) % Q_TILE
        n_idx = lax.broadcasted_iota(jnp.int32, (rows, n_cols), 1)
        vis = n_idx <= (t_col - (CMP_BLOCK - 1)) // CMP_STRIDE
        blk = lax.broadcasted_iota(jnp.int32, (n_blk, Q_TILE), 0)
        tok = q0 + lax.broadcasted_iota(jnp.int32, (n_blk, Q_TILE), 1)
        cur = tok // SLC_BLOCK
        forced = (blk == 0) | (blk == cur) | (blk == cur - 1)
        candidate = (blk * SLC_BLOCK <= tok) & jnp.logical_not(forced)
        wt = wt_ref[0:n_blk, 0:n_cols]
        zeros64 = jnp.zeros((rows, HEAD_DIM), BF16)
        never = jnp.zeros((N_BLK_LANES - n_blk, Q_TILE), F32)

        for c in range(KV_PER_SELECT_STEP):
            for pair in range(B_GROUP // 2):
                qp = q_ref[c * (B_GROUP // 2) + pair]
                hi = qp.astype(BF16)
                lo = (qp - hi.astype(F32)).astype(BF16)
                for half in range(2):
                    r = slice((2 * pair + half) * Q_TILE, (2 * pair + half + 1) * Q_TILE)
                    ln = slice(half * HEAD_DIM, (half + 1) * HEAD_DIM)
                    lhs_ref[c, r, 0:HEAD_DIM] = hi[:, ln]
                    lhs_ref[c, r, HEAD_DIM:2 * HEAD_DIM] = lo[:, ln]
                    lhs_ref[c, r, 2 * HEAD_DIM:3 * HEAD_DIM] = hi[:, ln]
            lhs_ref[c, :, 3 * HEAD_DIM:] = zeros64
            s = lax.dot_general(lhs_ref[c], kc_ref[0, c, 0:n_cols, :], NT_DIMS, preferred_element_type=F32)
            s = jnp.where(vis, s, NEG_INF)
            e = jnp.exp2(s - jnp.max(s, axis=-1, keepdims=True))
            l = jnp.sum(e, axis=-1, keepdims=True)
            p = e * jnp.where(t_col >= CMP_BLOCK - 1, 1.0 / l, 0.0)
            o_cmp = jnp.dot(p.astype(BF16), vc_ref[0, c, 0:n_cols, 0:LANES], preferred_element_type=F32)
            for g in range(B_GROUP):
                c0 = (c * B_GROUP + g) * HEAD_DIM
                ocmp_ref[:, c0:c0 + HEAD_DIM] = o_cmp[g * Q_TILE:(g + 1) * Q_TILE, :HEAD_DIM]

            psum = p[0:Q_TILE] + p[Q_TILE:2 * Q_TILE] + p[2 * Q_TILE:3 * Q_TILE] + p[3 * Q_TILE:]
            ps_hi = psum.astype(BF16)
            ps_lo = (psum - ps_hi.astype(F32)).astype(BF16)
            pslc = (lax.dot_general(wt, ps_hi, NT_DIMS, preferred_element_type=F32)
                    + lax.dot_general(wt, ps_lo, NT_DIMS, preferred_element_type=F32))
            score = jnp.where(candidate, pslc, jnp.where(forced, -jnp.inf, NEG_INF))
            sel_t = forced.astype(F32)
            for _ in range(SLC_TOP_N - SLC_FORCED):
                best = jnp.max(score, axis=0, keepdims=True)
                first = jnp.min(jnp.where(score == best, blk, N_BLK_LANES), axis=0, keepdims=True)
                hit = blk == first
                score = jnp.where(hit, -jnp.inf, score)
                sel_t = jnp.where(hit, 1.0, sel_t)
            if n_blk < N_BLK_LANES:
                sel_t = jnp.concatenate([sel_t, never], axis=0)
            bias_ref[c] = jnp.where(sel_t.T > 0.0, 0.0, NEG_INF).astype(BF16)

    ratio = SLC_BLOCK // CMP_STRIDE
    early = qt < pl.num_programs(1) // 2
    pl.when(early)(lambda: chains(n_cmp // 2, n_cmp // (2 * ratio)))
    pl.when(jnp.logical_not(early))(lambda: chains(n_cmp, N_BLK_LANES))


def _nsa_select_call(q_hm, cmp_aug, wt):
    _, t, _ = q_hm.shape
    n_cmp = cmp_aug.shape[2]
    kvs = KV_PER_SELECT_STEP
    return pl.pallas_call(
        _nsa_select_kernel, grid=(B_KV_HEADS // kvs, t // Q_TILE),
        in_specs=[pl.BlockSpec((kvs * B_GROUP // 2, Q_TILE, LANES), lambda p, i: (p, i, 0)),
                  pl.BlockSpec((1, kvs, n_cmp, 2 * LANES), lambda p, i: (0, p, 0, 0)),
                  pl.BlockSpec((1, kvs, n_cmp, 2 * LANES), lambda p, i: (1, p, 0, 0)),
                  pl.BlockSpec((N_BLK_LANES, n_cmp), lambda p, i: (0, 0))],
        out_specs=[pl.BlockSpec((Q_TILE, kvs * B_GROUP * HEAD_DIM), lambda p, i: (i, p)),
                   pl.BlockSpec((kvs, Q_TILE, N_BLK_LANES), lambda p, i: (p, i, 0))],
        out_shape=[jax.ShapeDtypeStruct((t, D_B), F32),
                   jax.ShapeDtypeStruct((B_KV_HEADS, t, N_BLK_LANES), BF16)],
        scratch_shapes=[pltpu.VMEM((kvs, B_GROUP * Q_TILE, 2 * LANES), BF16)], name="nsa_select",
        compiler_params=_cparams(2, VMEM_LIMIT))(q_hm, cmp_aug, cmp_aug, wt)


def _nsa_attend_kernel(q_ref, bias_ref, ocmp_ref, gate_ref, gx_ref, ksv_ref, kwv_ref, vs_ref, vw_ref, o_ref,
                       lhs_w_ref, lhs_s_ref, m_ref, acc_ref, *, key_tile, tiles_per_group):
    q0 = pl.program_id(1) * ATT_TILE
    rows = B_GROUP * ATT_TILE
    chains = range(KV_PER_ATTEND_STEP)
    t_col = q0 + lax.broadcasted_iota(jnp.int32, (rows, 1), 0) % ATT_TILE

    for c in chains:
        for g in range(B_GROUP):
            r = slice(g * ATT_TILE, (g + 1) * ATT_TILE)
            qp = q_ref[c * (B_GROUP // 2) + g // 2]
            qhi = qp[:, (g % 2) * HEAD_DIM:(g % 2 + 1) * HEAD_DIM].astype(BF16)
            lhs_s_ref[c, r, 0:HEAD_DIM] = qhi
            lhs_s_ref[c, r, LANES:] = bias_ref[c]
            lhs_w_ref[c, r, 0:HEAD_DIM] = qhi
        lhs_s_ref[c, :, HEAD_DIM:LANES] = jnp.zeros((rows, LANES - HEAD_DIM), BF16)
        lhs_w_ref[c, :, HEAD_DIM:] = jnp.zeros((rows, 2 * LANES - HEAD_DIM), BF16)

    for c in chains:
        m_ref[c] = jnp.full((rows, LANES), NEG_INF, F32)
        acc_ref[c] = jnp.zeros((rows, LANES), F32)

    def tile(ti, causal):
        start = pl.multiple_of(ti * key_tile, key_tile)
        for c in chains:
            st = lax.dot_general(lhs_s_ref[c], ksv_ref[0, c, pl.ds(start, key_tile), :], NT_DIMS,
                                 preferred_element_type=F32)
            if causal:
                kp = start + lax.broadcasted_iota(jnp.int32, (rows, key_tile), 1)
                st = jnp.where(kp <= t_col, st, NEG_INF)
            m_run = m_ref[c]
            m_new = jnp.maximum(m_run, jnp.max(st, axis=-1, keepdims=True))
            pt = jnp.exp2(st - jnp.concatenate([m_new] * (key_tile // LANES), axis=1))
            pv = jnp.dot(pt.astype(BF16), vs_ref[0, c, pl.ds(start, key_tile), :], preferred_element_type=F32)
            acc_ref[c] = acc_ref[c] * jnp.exp2(m_run - m_new) + pv
            m_ref[c] = m_new

    def group(gi, _):
        for u in range(tiles_per_group):
            tile(gi * tiles_per_group + u, False)
        return 0

    def single(ti, _):
        tile(ti, False)
        return 0

    n_full = q0 // key_tile
    n_groups = n_full // tiles_per_group
    lax.fori_loop(0, n_groups, group, 0)
    lax.fori_loop(n_groups * tiles_per_group, n_full, single, 0)
    tile(n_full, True)

    slab = WINDOW + ATT_TILE
    ws = pl.multiple_of(jnp.maximum(q0 - WINDOW, 0), ATT_TILE)
    kpos = ws + lax.broadcasted_iota(jnp.int32, (rows, slab), 1)
    in_window = lax.bitcast_convert_type(t_col - kpos, jnp.uint32) < jnp.uint32(WINDOW)
    lane = lax.broadcasted_iota(jnp.int32, (ATT_TILE, LANES), 1)
    per_branch = B_GROUP * HEAD_DIM

    def head_pair(acc, k):
        a_even = acc[2 * k * ATT_TILE:(2 * k + 1) * ATT_TILE]
        a_odd = acc[(2 * k + 1) * ATT_TILE:(2 * k + 2) * ATT_TILE]
        num = jnp.where(lane < HEAD_DIM, a_even, pltpu.roll(a_odd, HEAD_DIM, 1))
        den = jnp.where(lane < HEAD_DIM, pltpu.roll(a_even, HEAD_DIM, 1), a_odd)
        return num * (1.0 / den)

    gates = gate_ref[...]
    g_hi = gates.astype(BF16)
    g_lo = (gates - g_hi.astype(F32)).astype(BF16)

    for c in chains:
        acc_s = acc_ref[c]
        s = lax.dot_general(lhs_w_ref[c], kwv_ref[0, c, pl.ds(ws, slab), :], NT_DIMS,
                            preferred_element_type=F32)
        s = jnp.where(in_window, s, NEG_INF)
        e = jnp.exp2(s - jnp.max(s, axis=-1, keepdims=True))
        acc_w = jnp.dot(e.astype(BF16), vw_ref[0, c, pl.ds(ws, slab), :], preferred_element_type=F32)

        gx = (jnp.dot(g_hi, gx_ref[c], preferred_element_type=F32)
              + jnp.dot(g_lo, gx_ref[c], preferred_element_type=F32))
        for k in range(B_GROUP // 2):
            c0 = c * B_GROUP * HEAD_DIM + k * LANES
            g0 = k * LANES
            og = (gx[:, g0:g0 + LANES] * ocmp_ref[:, c0:c0 + LANES]
                  + gx[:, per_branch + g0:per_branch + g0 + LANES] * head_pair(acc_s, k)
                  + gx[:, 2 * per_branch + g0:2 * per_branch + g0 + LANES] * head_pair(acc_w, k))
            o_ref[:, c0:c0 + LANES] = og.astype(o_ref.dtype)


def _nsa_attend_call(q_hm, bias, ocmp, gates, kv_aug, v_aug, *, key_tile=512, tiles_per_group=2):
    _, t, _ = q_hm.shape
    assert t % key_tile == 0 and key_tile % ATT_TILE == 0 and t % ATT_TILE == 0
    rows = B_GROUP * ATT_TILE
    kvs = KV_PER_ATTEND_STEP
    wide = kvs * B_GROUP * HEAD_DIM
    resident = dict(pipeline_mode=pl.Buffered(1))
    return pl.pallas_call(
        functools.partial(_nsa_attend_kernel, key_tile=key_tile, tiles_per_group=tiles_per_group),
        grid=(B_KV_HEADS // kvs, t // ATT_TILE),
        in_specs=[pl.BlockSpec((kvs * B_GROUP // 2, ATT_TILE, LANES), lambda p, i: (p, i, 0)),
                  pl.BlockSpec((kvs, ATT_TILE, N_BLK_LANES), lambda p, i: (p, i, 0)),
                  pl.BlockSpec((ATT_TILE, wide), lambda p, i: (i, p)),
                  pl.BlockSpec((ATT_TILE, GATE_LANES), lambda p, i: (i, 0)),
                  pl.BlockSpec((kvs, GATE_LANES, 3 * B_GROUP * HEAD_DIM), lambda p, i: (p, 0, 0)),
                  pl.BlockSpec((1, kvs, t, 2 * LANES), lambda p, i: (0, p, 0, 0), **resident),
                  pl.BlockSpec((1, kvs, t, 2 * LANES), lambda p, i: (1, p, 0, 0), **resident),
                  pl.BlockSpec((1, kvs, t, LANES), lambda p, i: (0, p, 0, 0), **resident),
                  pl.BlockSpec((1, kvs, t, LANES), lambda p, i: (1, p, 0, 0), **resident)],
        out_specs=pl.BlockSpec((ATT_TILE, wide), lambda p, i: (i, p)),
        out_shape=jax.ShapeDtypeStruct((t, D_B), BF16),
        scratch_shapes=[pltpu.VMEM((kvs, rows, 2 * LANES), BF16), pltpu.VMEM((kvs, rows, 2 * LANES), BF16),
                        pltpu.VMEM((kvs, rows, LANES), F32), pltpu.VMEM((kvs, rows, LANES), F32)],
        name="nsa_attend",
        compiler_params=_cparams(2, VMEM_LIMIT))(q_hm, bias, ocmp, gates, _gate_expand_matrix(),
                                                 kv_aug, kv_aug, v_aug, v_aug)


def _post_kernel(oa_ref, ob_ref, sg_ref, x_ref, wa_ref, wb_ref, wo_ref, g_ref, wg_ref, wu_ref, wd_ref,
                 o_ref, wg16_ref, wu16_ref, wd16_ref):
    ya = jnp.dot(oa_ref[...], wa_ref[...], preferred_element_type=F32)
    yb = jnp.dot(ob_ref[...], wb_ref[...], preferred_element_type=F32)
    merged = sg_ref[0].astype(F32) * ya + sg_ref[1].astype(F32) * yb
    mix = jnp.dot(merged.astype(BF16), wo_ref[...], preferred_element_type=F32)
    y = mix * lax.rsqrt(jnp.mean(mix * mix, axis=-1, keepdims=True) + EPS)
    o_ref[...] = x_ref[...] + y * g_ref[...]
    wg16_ref[...] = wg_ref[...].astype(BF16)
    wu16_ref[...] = wu_ref[...].astype(BF16)
    wd16_ref[...] = wd_ref[...].astype(BF16)


def _post_call(oa, ob, sg, x2, wa, wb, wo, g, wg, wu, wd, tm=256):
    t, d = x2.shape
    steps = t // tm
    dff = wg.shape[1]
    assert d % (steps * 16) == 0 and dff % (steps * 16) == 0
    const = dict(pipeline_mode=pl.Buffered(1))
    return pl.pallas_call(
        _post_kernel, grid=(steps,),
        in_specs=[pl.BlockSpec((tm, D_A), lambda i: (i, 0)),
                  pl.BlockSpec((tm, D_B), lambda i: (i, 0)),
                  pl.BlockSpec((2, tm, d), lambda i: (0, i, 0)),
                  pl.BlockSpec((tm, d), lambda i: (i, 0)),
                  pl.BlockSpec((D_A, d), lambda i: (0, 0), **const),
                  pl.BlockSpec((D_B, d), lambda i: (0, 0), **const),
                  pl.BlockSpec((d, d), lambda i: (0, 0), **const),
                  pl.BlockSpec((1, d), lambda i: (0, 0)),
                  pl.BlockSpec((d // steps, dff), lambda i: (i, 0)),
                  pl.BlockSpec((d // steps, dff), lambda i: (i, 0)),
                  pl.BlockSpec((dff // steps, d), lambda i: (i, 0))],
        out_specs=[pl.BlockSpec((tm, d), lambda i: (i, 0)),
                   pl.BlockSpec((d // steps, dff), lambda i: (i, 0)),
                   pl.BlockSpec((d // steps, dff), lambda i: (i, 0)),
                   pl.BlockSpec((dff // steps, d), lambda i: (i, 0))],
        out_shape=[jax.ShapeDtypeStruct((t, d), F32), jax.ShapeDtypeStruct((d, dff), BF16),
                   jax.ShapeDtypeStruct((d, dff), BF16), jax.ShapeDtypeStruct((dff, d), BF16)],
        name="mix_out_residual",
        compiler_params=_cparams(1, VMEM_LIMIT))(oa, ob, sg, x2, wa, wb, wo, g, wg, wu, wd)


def _ffn_kernel(h_ref, g1_ref, wg_ref, wu_ref, wd_ref, g2_ref, o_ref, hn_ref, acc_ref):
    f = pl.program_id(1)

    @pl.when(f == 0)
    def _():
        h = h_ref[...]
        y = h * lax.rsqrt(jnp.mean(h * h, axis=-1, keepdims=True) + EPS)
        hn_ref[...] = (y * g1_ref[...]).astype(BF16)
        acc_ref[...] = jnp.zeros_like(acc_ref)

    hn = hn_ref[...]
    a = jnp.dot(hn, wg_ref[...], preferred_element_type=F32)
    u = jnp.dot(hn, wu_ref[...], preferred_element_type=F32)
    z = (a * jax.nn.sigmoid(a) * u).astype(BF16)
    acc_ref[...] += jnp.dot(z, wd_ref[...], preferred_element_type=F32)

    @pl.when(f == pl.num_programs(1) - 1)
    def _():
        ff = acc_ref[...]
        y = ff * lax.rsqrt(jnp.mean(ff * ff, axis=-1, keepdims=True) + EPS)
        o_ref[...] = h_ref[...] + y * g2_ref[...]


def _ffn_call(h1, g1, wg, wu, wd, g2, tm=512, tf=512):
    t, d = h1.shape
    dff = wg.shape[1]
    return pl.pallas_call(
        _ffn_kernel, grid=(t // tm, dff // tf),
        in_specs=[pl.BlockSpec((tm, d), lambda i, f: (i, 0)),
                  pl.BlockSpec((1, d), lambda i, f: (0, 0)),
                  pl.BlockSpec((d, tf), lambda i, f: (0, f)),
                  pl.BlockSpec((d, tf), lambda i, f: (0, f)),
                  pl.BlockSpec((tf, d), lambda i, f: (f, 0)),
                  pl.BlockSpec((1, d), lambda i, f: (0, 0))],
        out_specs=pl.BlockSpec((tm, d), lambda i, f: (i, 0)),
        out_shape=jax.ShapeDtypeStruct((t, d), F32),
        scratch_shapes=[pltpu.VMEM((tm, d), BF16), pltpu.VMEM((tm, d), F32)], name="swiglu_ffn",
        compiler_params=_cparams(2, VMEM_LIMIT))(h1, g1, wg, wu, wd, g2)


def _rope_tables(pos, period, scale=1.0):
    half = ROPE_DIM // 2
    inv = jnp.float32(ROPE_THETA) ** (-jnp.arange(half, dtype=F32) * 2.0 / ROPE_DIM)
    ang = pos.astype(F32)[:, None] * inv[None, :]
    cos, sin = jnp.cos(ang), jnp.sin(ang)
    r = np.arange(LANES) % period
    f = r % half
    lo = jnp.asarray((r < half)[None, :])
    hi = jnp.asarray(((r >= half) & (r < ROPE_DIM))[None, :])
    c = jnp.where(lo | hi, cos[:, f], 1.0)
    sa = jnp.where(lo, -sin[:, f], 0.0)
    sb = jnp.where(hi, sin[:, f], 0.0)
    return jnp.stack([c, sa, sb]) * scale


def _gate_expand_matrix():
    e = np.zeros((B_KV_HEADS, GATE_LANES, 3 * B_GROUP * HEAD_DIM), np.float32)
    for kvh in range(B_KV_HEADS):
        for g in range(B_GROUP):
            for b in range(3):
                c0 = (b * B_GROUP + g) * HEAD_DIM
                e[kvh, 3 * (kvh * B_GROUP + g) + b, c0:c0 + HEAD_DIM] = 1.0
    return jnp.asarray(e, BF16)


def _slc_weight_matrix(n_cmp_pad):
    ratio = SLC_BLOCK // CMP_STRIDE
    w = np.zeros((N_BLK_LANES, n_cmp_pad), np.float32)
    for j in range(N_BLK_LANES):
        for o, wv in enumerate(SLC_WEIGHTS):
            n = ratio * j + o - 1
            if 0 <= n < n_cmp_pad - 1:
                w[j, n] = wv
    return jnp.asarray(w, BF16)


def kernel(x, pre_mix_g, w_in, lb_table, a_norm_g, cmp_pe_k, cmp_w1_k, cmp_b1_k, cmp_w2_k, cmp_b2_k, cmp_pe_v, cmp_w1_v, cmp_b1_v, cmp_w2_v, cmp_b2_v, w_proj_a, w_proj_b, w_out, post_mix_g, pre_ffn_g, w_gate, w_up, w_down, post_ffn_g):
    bsz, t, d = x.shape
    assert bsz == 1 and d == D_MODEL and WINDOW + ATT_TILE <= t <= N_BLK_LANES * SLC_BLOCK
    x2 = x.reshape(t, d)
    n_cmp_pad = t // CMP_STRIDE
    assert n_cmp_pad % LANES == 0

    lb = jnp.cumsum(jax.nn.softmax(lb_table.astype(F32), axis=0), axis=0)[0].reshape(1, D_A)
    w = jnp.swapaxes(w_in[0], 0, 1)
    o_q, o_kc, o_ks, o_g, o_m = 4 * D_A, 4 * D_A + D_B, 4 * D_A + D_B + 2 * KV_DIM, 4 * D_A + D_B + 6 * KV_DIM, 4 * D_A + D_B + 6 * KV_DIM + 3 * B_HEADS
    w_hgrn = w[:o_q].astype(BF16)
    w_q = w[o_q:o_kc].astype(BF16)
    w_cmp = w[o_kc:o_ks].astype(BF16)
    w_kv = (w[o_ks:o_g].reshape(2, 2, B_KV_HEADS, HEAD_DIM, d).transpose(0, 2, 1, 3, 4)
            .reshape(4 * KV_DIM, d).astype(BF16))
    w_g = jnp.pad(w[o_g:o_m], ((0, GATE_LANES - 3 * B_HEADS), (0, 0))).astype(BF16)
    w_m = w[o_m:].astype(BF16)

    pos = jnp.arange(t)
    tab_q = _rope_tables(pos, HEAD_DIM, HEAD_DIM ** -0.5 * LOG2_E)
    tab_kv = _rope_tables(pos, LANES)
    end_pos = jnp.arange(n_cmp_pad) * CMP_STRIDE + (CMP_BLOCK - 1)
    ident = jnp.stack([jnp.ones((n_cmp_pad, LANES), F32), jnp.zeros((n_cmp_pad, LANES), F32),
                       jnp.zeros((n_cmp_pad, LANES), F32)])
    tab_cmp = jnp.concatenate([_rope_tables(end_pos, LANES), ident], axis=0)

    xn = _rmsnorm_call(x2, pre_mix_g[0].reshape(1, d))

    tm = 1024 if t % 1024 == 0 else 256
    proj4 = _proj_call(
        "proj_hgrn", xn, w_hgrn, tm=tm, tn=512, epilogue=_ep_plain,
        out_shape=jax.ShapeDtypeStruct((4, t, D_A), F32),
        out_specs=pl.BlockSpec((1, tm, 512), lambda i, j: (j // 2, i, j % 2)))
    q_hm = _proj_call(
        "proj_q", xn, w_q, tm=tm, tn=512, epilogue=_ep_q,
        out_shape=jax.ShapeDtypeStruct((B_HEADS // 2, t, LANES), F32),
        out_specs=pl.BlockSpec((512 // LANES, tm, LANES), lambda i, j: (j, i, 0)),
        extra=(tab_q,), extra_specs=(pl.BlockSpec((3, tm, LANES), lambda i, j: (0, i, 0)),))
    cmp_in = _proj_call(
        "proj_cmp", xn, w_cmp, tm=tm, tn=2 * KV_DIM, epilogue=_ep_cmp,
        out_shape=jax.ShapeDtypeStruct((2, B_KV_HEADS, t, HEAD_DIM), F32),
        out_specs=pl.BlockSpec((2, B_KV_HEADS, tm, HEAD_DIM), lambda i, j: (0, 0, i, 0)))
    kv_aug, v_aug = _proj_call(
        "proj_kv", xn, w_kv, tm=tm, tn=2 * KV_DIM, epilogue=functools.partial(_ep_kv, tm=tm),
        out_shape=[jax.ShapeDtypeStruct((2, B_KV_HEADS, t, 2 * LANES), BF16),
                   jax.ShapeDtypeStruct((2, B_KV_HEADS, t, LANES), BF16)],
        out_specs=[pl.BlockSpec((1, B_KV_HEADS, tm, 2 * LANES), lambda i, j: (j, 0, i, 0)),
                   pl.BlockSpec((1, B_KV_HEADS, tm, LANES), lambda i, j: (j, 0, i, 0))],
        extra=(tab_kv,), extra_specs=(pl.BlockSpec((3, tm, LANES), lambda i, j: (0, i, 0)),))
    gates = _proj_call(
        "proj_gate", xn, w_g, tm=tm, tn=GATE_LANES, epilogue=_ep_gate,
        out_shape=jax.ShapeDtypeStruct((t, GATE_LANES), F32),
        out_specs=pl.BlockSpec((tm, GATE_LANES), lambda i, j: (i, 0)))
    sg = _proj_call(
        "proj_merge", xn, w_m, tm=tm, tn=512, epilogue=_ep_sigmoid,
        out_shape=jax.ShapeDtypeStruct((2, t, d), BF16),
        out_specs=pl.BlockSpec((1, tm, 512), lambda i, j: (j // 4, i, j % 4)))

    oa = _hgrn_call(proj4, lb, a_norm_g[0].reshape(1, D_A))

    half = CMP_BLOCK // 2
    x16 = cmp_in.reshape(2, B_KV_HEADS, n_cmp_pad, CMP_STRIDE * HEAD_DIM)
    pe2 = jnp.stack([cmp_pe_k[0], cmp_pe_v[0]]).reshape(2, 2, half * HEAD_DIM)
    w1 = jnp.stack([cmp_w1_k[0], cmp_w1_v[0]]).astype(BF16)
    b1 = jnp.stack([cmp_b1_k[0], cmp_b1_v[0]]).reshape(2, 1, CMP_HIDDEN)
    w2p = jnp.pad(jnp.stack([cmp_w2_k[0], cmp_w2_v[0]]), ((0, 0), (0, 0), (0, LANES - HEAD_DIM))).astype(BF16)
    b2p = jnp.pad(jnp.stack([cmp_b2_k[0], cmp_b2_v[0]]), ((0, 0), (0, LANES - HEAD_DIM))).reshape(2, 1, LANES)
    cmp_aug = _compress_call(x16, pe2, w1, b1, w2p, b2p, tab_cmp)
    ocmp, bias = _nsa_select_call(q_hm, cmp_aug, _slc_weight_matrix(n_cmp_pad))
    ob = _nsa_attend_call(q_hm, bias, ocmp, gates, kv_aug, v_aug)

    h1, wg16, wu16, wd16 = _post_call(oa, ob, sg, x2, w_proj_a[0].astype(BF16), w_proj_b[0].astype(BF16),
                                      w_out[0].astype(BF16), post_mix_g[0].reshape(1, d),
                                      w_gate[0], w_up[0], w_down[0])
    out = _ffn_call(h1, pre_ffn_g[0].reshape(1, d), wg16, wu16, wd16, post_ffn_g[0].reshape(1, d))
    return out.reshape(bsz, t, d)
```

```python
import functools

import numpy as np
import jax
import jax.numpy as jnp
from jax import lax
from jax.experimental import pallas as pl
from jax.experimental.pallas import tpu as pltpu

F32 = jnp.float32
BF16 = jnp.bfloat16

D_MODEL = 2048
D_A = 1024
D_B = 1024
A_HEADS = 8
A_DK = 128
HGRN_PAIRWISE = 4
B_HEADS = 16
B_KV_HEADS = 4
B_GROUP = 4
HEAD_DIM = 64
KV_DIM = 256
CMP_BLOCK = 32
CMP_STRIDE = 16
CMP_HIDDEN = 256
SLC_BLOCK = 64
SLC_TOP_N = 16
SLC_FORCED = 3
SLC_WEIGHTS = (1.0, 2.0, 2.0, 2.0, 1.0)
WINDOW = 512
Q_TILE = 512
ATT_TILE = 256
KV_PER_ATTEND_STEP = 2
KV_PER_SELECT_STEP = 2
PROJ_SUBTILE = 256
GATE_LANES = 256
ROPE_THETA = 500000.0
ROPE_DIM = 16
D_FF = 5632
EPS = 1e-6
LOG2_E = 1.4426950408889634
NEG_INF = -1e30
N_BLK_LANES = 128
LANES = 128
VMEM_LIMIT = 56 * 1024 * 1024

NT_DIMS = (((1,), (1,)), ((), ()))
TN_DIMS = (((0,), (0,)), ((), ()))


def _cparams(n_axes, vmem=None):
    return pltpu.CompilerParams(dimension_semantics=("arbitrary",) * n_axes,
                                vmem_limit_bytes=vmem)


def _rmsnorm_kernel(x_ref, g_ref, o_ref):
    x = x_ref[...]
    y = x * lax.rsqrt(jnp.mean(x * x, axis=-1, keepdims=True) + EPS)
    o_ref[...] = (y * g_ref[...]).astype(o_ref.dtype)


def _rmsnorm_call(x2, g, tm=512):
    t, d = x2.shape
    return pl.pallas_call(
        _rmsnorm_kernel, grid=(t // tm,),
        in_specs=[pl.BlockSpec((tm, d), lambda i: (i, 0)), pl.BlockSpec((1, d), lambda i: (0, 0))],
        out_specs=pl.BlockSpec((tm, d), lambda i: (i, 0)),
        out_shape=jax.ShapeDtypeStruct((t, d), BF16), name="pre_mix_rmsnorm",
        compiler_params=_cparams(1))(x2, g)


def _proj_call(name, xn, w_t, *, tm, tn, epilogue, out_shape, out_specs, extra=(), extra_specs=(),
               w_rows=None, to_bf16=()):
    t, k = xn.shape
    row0, n = (0, w_t.shape[0]) if w_rows is None else w_rows
    assert tn % PROJ_SUBTILE == 0 and n % tn == 0 and row0 % tn == 0
    grid = (t // tm, n // tn)
    steps = grid[0] * grid[1]
    n_extra, n_cast = len(extra), len(to_bf16)
    out_specs = list(out_specs) if isinstance(out_specs, (list, tuple)) else [out_specs]
    out_shape = list(out_shape) if isinstance(out_shape, (list, tuple)) else [out_shape]
    n_out = len(out_specs)

    def body(x_ref, w_ref, *rest):
        extra_refs = rest[:n_extra]
        cast_in = rest[n_extra:n_extra + n_cast]
        outs = rest[n_extra + n_cast:n_extra + n_cast + n_out]
        cast_out = rest[n_extra + n_cast + n_out:]
        x = x_ref[...]
        for c0 in range(0, tn, PROJ_SUBTILE):
            acc = lax.dot_general(x, w_ref[c0:c0 + PROJ_SUBTILE, :], NT_DIMS, preferred_element_type=F32)
            epilogue(acc, c0, extra_refs, outs)
        for src, dst in zip(cast_in, cast_out):
            dst[...] = src[...].astype(BF16)

    def slab(a):
        rows = a.shape[0] // steps
        assert a.shape[0] % steps == 0 and rows % 16 == 0
        return pl.BlockSpec((rows, a.shape[1]), lambda i, j: (i * grid[1] + j, 0))

    res = pl.pallas_call(
        body, grid=grid,
        in_specs=[pl.BlockSpec((tm, k), lambda i, j: (i, 0)),
                  pl.BlockSpec((tn, k), lambda i, j: (row0 // tn + j, 0)), *extra_specs,
                  *[slab(a) for a in to_bf16]],
        out_specs=out_specs + [slab(a) for a in to_bf16],
        out_shape=out_shape + [jax.ShapeDtypeStruct(a.shape, BF16) for a in to_bf16], name=name,
        compiler_params=_cparams(2, VMEM_LIMIT))(xn, w_t, *extra, *to_bf16)
    return res[0] if len(res) == 1 else res


def _rope_tile(a, tab_ref):
    return (a * tab_ref[0] + pltpu.roll(a, LANES - ROPE_DIM // 2, 1) * tab_ref[1]
            + pltpu.roll(a, ROPE_DIM // 2, 1) * tab_ref[2])


def _ep_plain(acc, c0, extra, outs):
    outs[0][0, :, c0:c0 + acc.shape[1]] = acc.astype(outs[0].dtype)


def _ep_q(acc, c0, extra, outs):
    for pair in range(acc.shape[1] // LANES):
        outs[0][c0 // LANES + pair] = _rope_tile(acc[:, pair * LANES:(pair + 1) * LANES], extra[0])


def _ep_cmp(acc, c0, extra, outs):
    for hh in range(acc.shape[1] // HEAD_DIM):
        head = c0 // HEAD_DIM + hh
        outs[0][head // B_KV_HEADS, head % B_KV_HEADS] = acc[:, hh * HEAD_DIM:(hh + 1) * HEAD_DIM]


def _ep_kv(acc, c0, extra, outs, *, tm):
    i = pl.program_id(0)
    j = pl.program_id(1)
    rowg = i * tm + lax.broadcasted_iota(jnp.int32, (tm, LANES), 0)
    lane = lax.broadcasted_iota(jnp.int32, (tm, LANES), 1)
    aux = jnp.where(j == 0, (rowg // SLC_BLOCK == lane).astype(F32), 0.0).astype(BF16)
    for hh in range(acc.shape[1] // LANES):
        h = c0 // LANES + hh
        r = _rope_tile(acc[:, hh * LANES:(hh + 1) * LANES], extra[0])
        outs[0][0, h, :, 0:LANES] = r.astype(BF16)
        outs[0][0, h, :, LANES:2 * LANES] = aux
        outs[1][0, h] = jnp.where(lane < HEAD_DIM, pltpu.roll(r, HEAD_DIM, 1), 1.0).astype(BF16)


def _ep_gate(acc, c0, extra, outs):
    outs[0][:, c0:c0 + acc.shape[1]] = jax.nn.sigmoid(acc)


def _ep_sigmoid(acc, c0, extra, outs):
    outs[0][0, :, c0:c0 + acc.shape[1]] = jax.nn.sigmoid(acc).astype(outs[0].dtype)


def _split3(x):
    hi = x.astype(BF16)
    r1 = x - hi.astype(F32)
    mid = r1.astype(BF16)
    lo = (r1 - mid.astype(F32)).astype(BF16)
    return hi, mid, lo


def _hgrn_kernel(q_ref, f_ref, i_ref, g_ref, lb_ref, gn_ref, o_ref, st_ref, *, chunk, heads):
    c = pl.program_id(1)

    @pl.when(c == 0)
    def _():
        st_ref[...] = jnp.zeros_like(st_ref)

    row = lax.broadcasted_iota(jnp.int32, (chunk, chunk), 0)
    col = lax.broadcasted_iota(jnp.int32, (chunk, chunk), 1)
    tri = (col <= row).astype(BF16)
    rowv = lax.broadcasted_iota(jnp.int32, (chunk, A_DK), 0)

    for hb in range(heads):
        sl = slice(hb * A_DK, (hb + 1) * A_DK)
        q = q_ref[0, :, sl]
        ii = i_ref[0, :, sl]
        gg = g_ref[0, :, sl]
        lbv = lb_ref[:, sl]
        f = lbv + (1.0 - lbv) * jax.nn.sigmoid(f_ref[0, :, sl])
        lf = jnp.log(f)
        k = 1.0 - f
        b = sum(jnp.dot(tri, p, preferred_element_type=F32) for p in _split3(lf)) * LOG2_E
        b_end = b[chunk - 1:chunk, :]
        ii16 = ii.astype(BF16)

        st = st_ref[hb]
        o = lax.dot_general((q * jnp.exp2(b)).astype(BF16), st.astype(BF16), NT_DIMS,
                            preferred_element_type=F32)
        kd = (k * jnp.exp2(b_end - b)).astype(BF16)
        st_ref[hb] = st * jnp.exp2(b_end) + lax.dot_general(ii16, kd, TN_DIMS,
                                                            preferred_element_type=F32)

        att = jnp.where(col == row, jnp.sum(q * k, axis=-1, keepdims=True), 0.0)
        for d in range(1, HGRN_PAIRWISE):
            valid = (rowv % HGRN_PAIRWISE) >= d
            w = jnp.exp2(b - pltpu.roll(b, d, 0))
            p = jnp.where(valid, q * pltpu.roll(k, d, 0) * w, 0.0)
            att = att + jnp.where(col == row - d, jnp.sum(p, axis=-1, keepdims=True), 0.0)
        m = HGRN_PAIRWISE
        while m < chunk:
            grp = chunk // (2 * m)
            b3 = b.reshape(grp, 2 * m, A_DK)
            refrow = jnp.broadcast_to(b3[:, m - 1:m, :], (grp, 2 * m, A_DK)).reshape(chunk, A_DK)
            second = (rowv % (2 * m)) >= m
            ql = jnp.where(second, q * jnp.exp2(b - refrow), 0.0)
            kl = jnp.where(second, 0.0, k * jnp.exp2(refrow - b))
            a = lax.dot_general(ql.astype(BF16), kl.astype(BF16), NT_DIMS, preferred_element_type=F32)
            if grp > 1:
                a = jnp.where(row // (2 * m) == col // (2 * m), a, 0.0)
            att = att + a
            m *= 2
        o = o + jnp.dot(att.astype(BF16), ii16, preferred_element_type=F32)

        o = o * lax.rsqrt(jnp.mean(o * o, axis=-1, keepdims=True) + EPS)
        o_ref[:, sl] = ((o * gn_ref[:, sl]) * (gg * jax.nn.sigmoid(gg))).astype(o_ref.dtype)


def _hgrn_call(proj4, lb, gn, *, chunk=128, heads=8):
    _, t, _ = proj4.shape
    w = heads * A_DK

    def spec(kind):
        return pl.BlockSpec((1, chunk, w), lambda h, c, kind=kind: (kind, c, h))

    vec = pl.BlockSpec((1, w), lambda h, c: (0, h))
    return pl.pallas_call(
        functools.partial(_hgrn_kernel, chunk=chunk, heads=heads),
        grid=(A_HEADS // heads, t // chunk),
        in_specs=[spec(0), spec(1), spec(2), spec(3), vec, vec],
        out_specs=pl.BlockSpec((chunk, w), lambda h, c: (c, h)),
        out_shape=jax.ShapeDtypeStruct((t, D_A), BF16),
        scratch_shapes=[pltpu.VMEM((heads, A_DK, A_DK), F32)], name="hgrn2_scan",
        compiler_params=_cparams(2))(proj4, proj4, proj4, proj4, lb, gn)


def _compress_kernel(x_ref, pe_ref, w1_ref, b1_ref, w2_ref, b2_ref, tab_ref, o_ref):
    half = CMP_BLOCK * HEAD_DIM // 2
    x = x_ref[0, 0]
    top = jnp.dot((x + pe_ref[0, 0:1, :]).astype(BF16), w1_ref[0, :half, :], preferred_element_type=F32)
    bot = jnp.dot((x + pe_ref[0, 1:2, :]).astype(BF16), w1_ref[0, half:, :], preferred_element_type=F32)
    n = x.shape[0]
    h = jax.nn.gelu(top + pltpu.roll(bot, n - 1, 0) + b1_ref[0])
    y = jnp.dot(h.astype(BF16), w2_ref[0], preferred_element_type=F32) + b2_ref[0]
    y = _rope_tile(y, tab_ref)
    hi = y.astype(BF16)
    lo = (y - hi.astype(F32)).astype(BF16)
    o_ref[0, 0, :, 0:LANES] = (y + pltpu.roll(y, HEAD_DIM, 1)).astype(BF16)
    o_ref[0, 0, :, LANES:2 * LANES] = lo


def _compress_call(x16, pe2, w1, b1, w2p, b2p, tab):
    _, nh, n, wid = x16.shape
    hid = w1.shape[-1]
    return pl.pallas_call(
        _compress_kernel, grid=(2, nh),
        in_specs=[pl.BlockSpec((1, 1, n, wid), lambda kv, h: (kv, h, 0, 0)),
                  pl.BlockSpec((1, 2, wid), lambda kv, h: (kv, 0, 0)),
                  pl.BlockSpec((1, 2 * wid, hid), lambda kv, h: (kv, 0, 0)),
                  pl.BlockSpec((1, 1, hid), lambda kv, h: (kv, 0, 0)),
                  pl.BlockSpec((1, hid, LANES), lambda kv, h: (kv, 0, 0)),
                  pl.BlockSpec((1, 1, LANES), lambda kv, h: (kv, 0, 0)),
                  pl.BlockSpec((3, n, LANES), lambda kv, h: (kv, 0, 0))],
        out_specs=pl.BlockSpec((1, 1, n, 2 * LANES), lambda kv, h: (kv, h, 0, 0)),
        out_shape=jax.ShapeDtypeStruct((2, nh, n, 2 * LANES), BF16),
        name="compress_mlp", compiler_params=_cparams(2, VMEM_LIMIT))(x16, pe2, w1, b1, w2p, b2p, tab)


def _nsa_select_kernel(q_ref, kc_ref, vc_ref, wt_ref, ocmp_ref, bias_ref, lhs_ref):
    qt = pl.program_id(1)
    q0 = qt * Q_TILE
    rows = B_GROUP * Q_TILE
    n_cmp = kc_ref.shape[2]

    def chains(n_cols, n_blk):
        t_col = q0 + lax.broadcasted_iota(jnp.int32, (rows, 1), 0) % Q_TILE
        n_idx = lax.broadcasted_iota(jnp.int32, (rows, n_cols), 1)
        vis = n_idx <= (t_col - (CMP_BLOCK - 1)) // CMP_STRIDE
        blk = lax.broadcasted_iota(jnp.int32, (n_blk, Q_TILE), 0)
        tok = q0 + lax.broadcasted_iota(jnp.int32, (n_blk, Q_TILE), 1)
        cur = tok // SLC_BLOCK
        forced = (blk == 0) | (blk == cur) | (blk == cur - 1)
        candidate = (blk * SLC_BLOCK <= tok) & jnp.logical_not(forced)
        wt = wt_ref[0:n_blk, 0:n_cols]
        zeros64 = jnp.zeros((rows, HEAD_DIM), BF16)
        never = jnp.zeros((N_BLK_LANES - n_blk, Q_TILE), F32)

        for c in range(KV_PER_SELECT_STEP):
            for pair in range(B_GROUP // 2):
                qp = q_ref[c * (B_GROUP // 2) + pair]
                hi = qp.astype(BF16)
                lo = (qp - hi.astype(F32)).astype(BF16)
                for half in range(2):
                    r = slice((2 * pair + half) * Q_TILE, (2 * pair + half + 1) * Q_TILE)
                    ln = slice(half * HEAD_DIM, (half + 1) * HEAD_DIM)
                    lhs_ref[c, r, 0:HEAD_DIM] = hi[:, ln]
                    lhs_ref[c, r, HEAD_DIM:2 * HEAD_DIM] = lo[:, ln]
                    lhs_ref[c, r, 2 * HEAD_DIM:3 * HEAD_DIM] = hi[:, ln]
            lhs_ref[c, :, 3 * HEAD_DIM:] = zeros64
            s = lax.dot_general(lhs_ref[c], kc_ref[0, c, 0:n_cols, :], NT_DIMS, preferred_element_type=F32)
            s = jnp.where(vis, s, NEG_INF)
            e = jnp.exp2(s - jnp.max(s, axis=-1, keepdims=True))
            l = jnp.sum(e, axis=-1, keepdims=True)
            p = e * jnp.where(t_col >= CMP_BLOCK - 1, 1.0 / l, 0.0)
            o_cmp = jnp.dot(p.astype(BF16), vc_ref[0, c, 0:n_cols, 0:LANES], preferred_element_type=F32)
            for g in range(B_GROUP):
                c0 = (c * B_GROUP + g) * HEAD_DIM
                ocmp_ref[:, c0:c0 + HEAD_DIM] = o_cmp[g * Q_TILE:(g + 1) * Q_TILE, :HEAD_DIM]

            psum = p[0:Q_TILE] + p[Q_TILE:2 * Q_TILE] + p[2 * Q_TILE:3 * Q_TILE] + p[3 * Q_TILE:]
            ps_hi = psum.astype(BF16)
            ps_lo = (psum - ps_hi.astype(F32)).astype(BF16)
            pslc = (lax.dot_general(wt, ps_hi, NT_DIMS, preferred_element_type=F32)
                    + lax.dot_general(wt, ps_lo, NT_DIMS, preferred_element_type=F32))
            score = jnp.where(candidate, pslc, jnp.where(forced, -jnp.inf, NEG_INF))
            sel_t = forced.astype(F32)
            for _ in range(SLC_TOP_N - SLC_FORCED):
                best = jnp.max(score, axis=0, keepdims=True)
                first = jnp.min(jnp.where(score == best, blk, N_BLK_LANES), axis=0, keepdims=True)
                hit = blk == first
                score = jnp.where(hit, -jnp.inf, score)
                sel_t = jnp.where(hit, 1.0, sel_t)
            if n_blk < N_BLK_LANES:
                sel_t = jnp.concatenate([sel_t, never], axis=0)
            bias_ref[c] = jnp.where(sel_t.T > 0.0, 0.0, NEG_INF).astype(BF16)

    ratio = SLC_BLOCK // CMP_STRIDE
    early = qt < pl.num_programs(1) // 2
    pl.when(early)(lambda: chains(n_cmp // 2, n_cmp // (2 * ratio)))
    pl.when(jnp.logical_not(early))(lambda: chains(n_cmp, N_BLK_LANES))


def _nsa_select_call(q_hm, cmp_aug, wt):
    _, t, _ = q_hm.shape
    n_cmp = cmp_aug.shape[2]
    kvs = KV_PER_SELECT_STEP
    return pl.pallas_call(
        _nsa_select_kernel, grid=(B_KV_HEADS // kvs, t // Q_TILE),
        in_specs=[pl.BlockSpec((kvs * B_GROUP // 2, Q_TILE, LANES), lambda p, i: (p, i, 0)),
                  pl.BlockSpec((1, kvs, n_cmp, 2 * LANES), lambda p, i: (0, p, 0, 0)),
                  pl.BlockSpec((1, kvs, n_cmp, 2 * LANES), lambda p, i: (1, p, 0, 0)),
                  pl.BlockSpec((N_BLK_LANES, n_cmp), lambda p, i: (0, 0))],
        out_specs=[pl.BlockSpec((Q_TILE, kvs * B_GROUP * HEAD_DIM), lambda p, i: (i, p)),
                   pl.BlockSpec((kvs, Q_TILE, N_BLK_LANES), lambda p, i: (p, i, 0))],
        out_shape=[jax.ShapeDtypeStruct((t, D_B), F32),
                   jax.ShapeDtypeStruct((B_KV_HEADS, t, N_BLK_LANES), BF16)],
        scratch_shapes=[pltpu.VMEM((kvs, B_GROUP * Q_TILE, 2 * LANES), BF16)], name="nsa_select",
        compiler_params=_cparams(2, VMEM_LIMIT))(q_hm, cmp_aug, cmp_aug, wt)


def _nsa_attend_kernel(q_ref, bias_ref, ocmp_ref, gate_ref, gx_ref, ksv_ref, kwv_ref, vs_ref, vw_ref, o_ref,
                       lhs_w_ref, lhs_s_ref, m_ref, acc_ref, *, key_tile, tiles_per_group):
    q0 = pl.program_id(1) * ATT_TILE
    rows = B_GROUP * ATT_TILE
    chains = range(KV_PER_ATTEND_STEP)
    t_col = q0 + lax.broadcasted_iota(jnp.int32, (rows, 1), 0) % ATT_TILE

    for c in chains:
        for g in range(B_GROUP):
            r = slice(g * ATT_TILE, (g + 1) * ATT_TILE)
            qp = q_ref[c * (B_GROUP // 2) + g // 2]
            qhi = qp[:, (g % 2) * HEAD_DIM:(g % 2 + 1) * HEAD_DIM].astype(BF16)
            lhs_s_ref[c, r, 0:HEAD_DIM] = qhi
            lhs_s_ref[c, r, LANES:] = bias_ref[c]
            lhs_w_ref[c, r, 0:HEAD_DIM] = qhi
        lhs_s_ref[c, :, HEAD_DIM:LANES] = jnp.zeros((rows, LANES - HEAD_DIM), BF16)
        lhs_w_ref[c, :, HEAD_DIM:] = jnp.zeros((rows, 2 * LANES - HEAD_DIM), BF16)

    for c in chains:
        m_ref[c] = jnp.full((rows, LANES), NEG_INF, F32)
        acc_ref[c] = jnp.zeros((rows, LANES), F32)

    def tile(ti, causal):
        start = pl.multiple_of(ti * key_tile, key_tile)
        for c in chains:
            st = lax.dot_general(lhs_s_ref[c], ksv_ref[0, c, pl.ds(start, key_tile), :], NT_DIMS,
                                 preferred_element_type=F32)
            if causal:
                kp = start + lax.broadcasted_iota(jnp.int32, (rows, key_tile), 1)
                st = jnp.where(kp <= t_col, st, NEG_INF)
            m_run = m_ref[c]
            m_new = jnp.maximum(m_run, jnp.max(st, axis=-1, keepdims=True))
            pt = jnp.exp2(st - jnp.concatenate([m_new] * (key_tile // LANES), axis=1))
            pv = jnp.dot(pt.astype(BF16), vs_ref[0, c, pl.ds(start, key_tile), :], preferred_element_type=F32)
            acc_ref[c] = acc_ref[c] * jnp.exp2(m_run - m_new) + pv
            m_ref[c] = m_new

    def group(gi, _):
        for u in range(tiles_per_group):
            tile(gi * tiles_per_group + u, False)
        return 0

    def single(ti, _):
        tile(ti, False)
        return 0

    n_full = q0 // key_tile
    n_groups = n_full // tiles_per_group
    lax.fori_loop(0, n_groups, group, 0)
    lax.fori_loop(n_groups * tiles_per_group, n_full, single, 0)
    tile(n_full, True)

    slab = WINDOW + ATT_TILE
    ws = pl.multiple_of(jnp.maximum(q0 - WINDOW, 0), ATT_TILE)
    kpos = ws + lax.broadcasted_iota(jnp.int32, (rows, slab), 1)
    in_window = lax.bitcast_convert_type(t_col - kpos, jnp.uint32) < jnp.uint32(WINDOW)
    lane = lax.broadcasted_iota(jnp.int32, (ATT_TILE, LANES), 1)
    per_branch = B_GROUP * HEAD_DIM

    def head_pair(acc, k):
        a_even = acc[2 * k * ATT_TILE:(2 * k + 1) * ATT_TILE]
        a_odd = acc[(2 * k + 1) * ATT_TILE:(2 * k + 2) * ATT_TILE]
        num = jnp.where(lane < HEAD_DIM, a_even, pltpu.roll(a_odd, HEAD_DIM, 1))
        den = jnp.where(lane < HEAD_DIM, pltpu.roll(a_even, HEAD_DIM, 1), a_odd)
        return num * (1.0 / den)

    gates = gate_ref[...]
    g_hi = gates.astype(BF16)
    g_lo = (gates - g_hi.astype(F32)).astype(BF16)

    for c in chains:
        acc_s = acc_ref[c]
        s = lax.dot_general(lhs_w_ref[c], kwv_ref[0, c, pl.ds(ws, slab), :], NT_DIMS,
                            preferred_element_type=F32)
        s = jnp.where(in_window, s, NEG_INF)
        e = jnp.exp2(s - jnp.max(s, axis=-1, keepdims=True))
        acc_w = jnp.dot(e.astype(BF16), vw_ref[0, c, pl.ds(ws, slab), :], preferred_element_type=F32)

        gx = (jnp.dot(g_hi, gx_ref[c], preferred_element_type=F32)
              + jnp.dot(g_lo, gx_ref[c], preferred_element_type=F32))
        for k in range(B_GROUP // 2):
            c0 = c * B_GROUP * HEAD_DIM + k * LANES
            g0 = k * LANES
            og = (gx[:, g0:g0 + LANES] * ocmp_ref[:, c0:c0 + LANES]
                  + gx[:, per_branch + g0:per_branch + g0 + LANES] * head_pair(acc_s, k)
                  + gx[:, 2 * per_branch + g0:2 * per_branch + g0 + LANES] * head_pair(acc_w, k))
            o_ref[:, c0:c0 + LANES] = og.astype(o_ref.dtype)


def _nsa_attend_call(q_hm, bias, ocmp, gates, kv_aug, v_aug, *, key_tile=512, tiles_per_group=2):
    _, t, _ = q_hm.shape
    assert t % key_tile == 0 and key_tile % ATT_TILE == 0 and t % ATT_TILE == 0
    rows = B_GROUP * ATT_TILE
    kvs = KV_PER_ATTEND_STEP
    wide = kvs * B_GROUP * HEAD_DIM
    resident = dict(pipeline_mode=pl.Buffered(1))
    return pl.pallas_call(
        functools.partial(_nsa_attend_kernel, key_tile=key_tile, tiles_per_group=tiles_per_group),
        grid=(B_KV_HEADS // kvs, t // ATT_TILE),
        in_specs=[pl.BlockSpec((kvs * B_GROUP // 2, ATT_TILE, LANES), lambda p, i: (p, i, 0)),
                  pl.BlockSpec((kvs, ATT_TILE, N_BLK_LANES), lambda p, i: (p, i, 0)),
                  pl.BlockSpec((ATT_TILE, wide), lambda p, i: (i, p)),
                  pl.BlockSpec((ATT_TILE, GATE_LANES), lambda p, i: (i, 0)),
                  pl.BlockSpec((kvs, GATE_LANES, 3 * B_GROUP * HEAD_DIM), lambda p, i: (p, 0, 0)),
                  pl.BlockSpec((1, kvs, t, 2 * LANES), lambda p, i: (0, p, 0, 0), **resident),
                  pl.BlockSpec((1, kvs, t, 2 * LANES), lambda p, i: (1, p, 0, 0), **resident),
                  pl.BlockSpec((1, kvs, t, LANES), lambda p, i: (0, p, 0, 0), **resident),
                  pl.BlockSpec((1, kvs, t, LANES), lambda p, i: (1, p, 0, 0), **resident)],
        out_specs=pl.BlockSpec((ATT_TILE, wide), lambda p, i: (i, p)),
        out_shape=jax.ShapeDtypeStruct((t, D_B), BF16),
        scratch_shapes=[pltpu.VMEM((kvs, rows, 2 * LANES), BF16), pltpu.VMEM((kvs, rows, 2 * LANES), BF16),
                        pltpu.VMEM((kvs, rows, LANES), F32), pltpu.VMEM((kvs, rows, LANES), F32)],
        name="nsa_attend",
        compiler_params=_cparams(2, VMEM_LIMIT))(q_hm, bias, ocmp, gates, _gate_expand_matrix(),
                                                 kv_aug, kv_aug, v_aug, v_aug)


def _post_kernel(oa_ref, ob_ref, sg_ref, x_ref, wa_ref, wb_ref, wo_ref, g_ref, wg_ref, wu_ref, wd_ref,
                 o_ref, wg16_ref, wu16_ref, wd16_ref):
    ya = jnp.dot(oa_ref[...], wa_ref[...], preferred_element_type=F32)
    yb = jnp.dot(ob_ref[...], wb_ref[...], preferred_element_type=F32)
    merged = sg_ref[0].astype(F32) * ya + sg_ref[1].astype(F32) * yb
    mix = jnp.dot(merged.astype(BF16), wo_ref[...], preferred_element_type=F32)
    y = mix * lax.rsqrt(jnp.mean(mix * mix, axis=-1, keepdims=True) + EPS)
    o_ref[...] = x_ref[...] + y * g_ref[...]
    wg16_ref[...] = wg_ref[...].astype(BF16)
    wu16_ref[...] = wu_ref[...].astype(BF16)
    wd16_ref[...] = wd_ref[...].astype(BF16)


def _post_call(oa, ob, sg, x2, wa, wb, wo, g, wg, wu, wd, tm=256):
    t, d = x2.shape
    steps = t // tm
    dff = wg.shape[1]
    assert d % (steps * 16) == 0 and dff % (steps * 16) == 0
    const = dict(pipeline_mode=pl.Buffered(1))
    return pl.pallas_call(
        _post_kernel, grid=(steps,),
        in_specs=[pl.BlockSpec((tm, D_A), lambda i: (i, 0)),
                  pl.BlockSpec((tm, D_B), lambda i: (i, 0)),
                  pl.BlockSpec((2, tm, d), lambda i: (0, i, 0)),
                  pl.BlockSpec((tm, d), lambda i: (i, 0)),
                  pl.BlockSpec((D_A, d), lambda i: (0, 0), **const),
                  pl.BlockSpec((D_B, d), lambda i: (0, 0), **const),
                  pl.BlockSpec((d, d), lambda i: (0, 0), **const),
                  pl.BlockSpec((1, d), lambda i: (0, 0)),
                  pl.BlockSpec((d // steps, dff), lambda i: (i, 0)),
                  pl.BlockSpec((d // steps, dff), lambda i: (i, 0)),
                  pl.BlockSpec((dff // steps, d), lambda i: (i, 0))],
        out_specs=[pl.BlockSpec((tm, d), lambda i: (i, 0)),
                   pl.BlockSpec((d // steps, dff), lambda i: (i, 0)),
                   pl.BlockSpec((d // steps, dff), lambda i: (i, 0)),
                   pl.BlockSpec((dff // steps, d), lambda i: (i, 0))],
        out_shape=[jax.ShapeDtypeStruct((t, d), F32), jax.ShapeDtypeStruct((d, dff), BF16),
                   jax.ShapeDtypeStruct((d, dff), BF16), jax.ShapeDtypeStruct((dff, d), BF16)],
        name="mix_out_residual",
        compiler_params=_cparams(1, VMEM_LIMIT))(oa, ob, sg, x2, wa, wb, wo, g, wg, wu, wd)


def _ffn_kernel(h_ref, g1_ref, wg_ref, wu_ref, wd_ref, g2_ref, o_ref, hn_ref, acc_ref):
    f = pl.program_id(1)

    @pl.when(f == 0)
    def _():
        h = h_ref[...]
        y = h * lax.rsqrt(jnp.mean(h * h, axis=-1, keepdims=True) + EPS)
        hn_ref[...] = (y * g1_ref[...]).astype(BF16)
        acc_ref[...] = jnp.zeros_like(acc_ref)

    hn = hn_ref[...]
    a = jnp.dot(hn, wg_ref[...], preferred_element_type=F32)
    u = jnp.dot(hn, wu_ref[...], preferred_element_type=F32)
    z = (a * jax.nn.sigmoid(a) * u).astype(BF16)
    acc_ref[...] += jnp.dot(z, wd_ref[...], preferred_element_type=F32)

    @pl.when(f == pl.num_programs(1) - 1)
    def _():
        ff = acc_ref[...]
        y = ff * lax.rsqrt(jnp.mean(ff * ff, axis=-1, keepdims=True) + EPS)
        o_ref[...] = h_ref[...] + y * g2_ref[...]


def _ffn_call(h1, g1, wg, wu, wd, g2, tm=512, tf=512):
    t, d = h1.shape
    dff = wg.shape[1]
    return pl.pallas_call(
        _ffn_kernel, grid=(t // tm, dff // tf),
        in_specs=[pl.BlockSpec((tm, d), lambda i, f: (i, 0)),
                  pl.BlockSpec((1, d), lambda i, f: (0, 0)),
                  pl.BlockSpec((d, tf), lambda i, f: (0, f)),
                  pl.BlockSpec((d, tf), lambda i, f: (0, f)),
                  pl.BlockSpec((tf, d), lambda i, f: (f, 0)),
                  pl.BlockSpec((1, d), lambda i, f: (0, 0))],
        out_specs=pl.BlockSpec((tm, d), lambda i, f: (i, 0)),
        out_shape=jax.ShapeDtypeStruct((t, d), F32),
        scratch_shapes=[pltpu.VMEM((tm, d), BF16), pltpu.VMEM((tm, d), F32)], name="swiglu_ffn",
        compiler_params=_cparams(2, VMEM_LIMIT))(h1, g1, wg, wu, wd, g2)


def _rope_tables(pos, period, scale=1.0):
    half = ROPE_DIM // 2
    inv = jnp.float32(ROPE_THETA) ** (-jnp.arange(half, dtype=F32) * 2.0 / ROPE_DIM)
    r = np.arange(LANES) % period
    pick = np.zeros((half, LANES), np.float32)
    pick[r % half, np.arange(LANES)] = 1.0
    inv_lane = jnp.sum(inv[:, None] * pick, axis=0)
    ang = pos.astype(F32)[:, None] * inv_lane[None, :]
    cos, sin = jnp.cos(ang), jnp.sin(ang)
    lo = jnp.asarray((r < half)[None, :])
    hi = jnp.asarray(((r >= half) & (r < ROPE_DIM))[None, :])
    c = jnp.where(lo | hi, cos, 1.0)
    sa = jnp.where(lo, -sin, 0.0)
    sb = jnp.where(hi, sin, 0.0)
    return jnp.stack([c, sa, sb]) * scale


def _gate_expand_matrix():
    e = np.zeros((B_KV_HEADS, GATE_LANES, 3 * B_GROUP * HEAD_DIM), np.float32)
    for kvh in range(B_KV_HEADS):
        for g in range(B_GROUP):
            for b in range(3):
                c0 = (b * B_GROUP + g) * HEAD_DIM
                e[kvh, 3 * (kvh * B_GROUP + g) + b, c0:c0 + HEAD_DIM] = 1.0
    return jnp.asarray(e, BF16)


def _slc_weight_matrix(n_cmp_pad):
    ratio = SLC_BLOCK // CMP_STRIDE
    w = np.zeros((N_BLK_LANES, n_cmp_pad), np.float32)
    for j in range(N_BLK_LANES):
        for o, wv in enumerate(SLC_WEIGHTS):
            n = ratio * j + o - 1
            if 0 <= n < n_cmp_pad - 1:
                w[j, n] = wv
    return jnp.asarray(w, BF16)


def kernel(x, pre_mix_g, w_in, lb_table, a_norm_g, cmp_pe_k, cmp_w1_k, cmp_b1_k, cmp_w2_k, cmp_b2_k, cmp_pe_v, cmp_w1_v, cmp_b1_v, cmp_w2_v, cmp_b2_v, w_proj_a, w_proj_b, w_out, post_mix_g, pre_ffn_g, w_gate, w_up, w_down, post_ffn_g):
    bsz, t, d = x.shape
    assert bsz == 1 and d == D_MODEL and WINDOW + ATT_TILE <= t <= N_BLK_LANES * SLC_BLOCK
    x2 = x.reshape(t, d)
    n_cmp_pad = t // CMP_STRIDE
    assert n_cmp_pad % LANES == 0

    lb = jnp.cumsum(jax.nn.softmax(lb_table.astype(F32), axis=0), axis=0)[0].reshape(1, D_A)
    w = jnp.swapaxes(w_in[0], 0, 1).astype(BF16)
    o_q, o_kc, o_ks, o_g, o_m = 4 * D_A, 4 * D_A + D_B, 4 * D_A + D_B + 2 * KV_DIM, 4 * D_A + D_B + 6 * KV_DIM, 4 * D_A + D_B + 6 * KV_DIM + 3 * B_HEADS
    w_kv = (w[o_ks:o_g].reshape(2, 2, B_KV_HEADS, HEAD_DIM, d).transpose(0, 2, 1, 3, 4)
            .reshape(4 * KV_DIM, d))
    w_g = jnp.pad(w[o_g:o_m], ((0, GATE_LANES - 3 * B_HEADS), (0, 0)))
    w_m = w[o_m:]

    pos = jnp.arange(t)
    tab_q = _rope_tables(pos, HEAD_DIM, HEAD_DIM ** -0.5 * LOG2_E)
    tab_kv = _rope_tables(pos, LANES)
    end_pos = jnp.arange(n_cmp_pad) * CMP_STRIDE + (CMP_BLOCK - 1)
    ident = jnp.stack([jnp.ones((n_cmp_pad, LANES), F32), jnp.zeros((n_cmp_pad, LANES), F32),
                       jnp.zeros((n_cmp_pad, LANES), F32)])
    tab_cmp = jnp.concatenate([_rope_tables(end_pos, LANES), ident], axis=0)

    xn = _rmsnorm_call(x2, pre_mix_g[0].reshape(1, d))

    tm = 1024 if t % 1024 == 0 else 256
    proj4 = _proj_call(
        "proj_hgrn", xn, w, w_rows=(0, o_q), tm=tm, tn=512, epilogue=_ep_plain,
        out_shape=jax.ShapeDtypeStruct((4, t, D_A), F32),
        out_specs=pl.BlockSpec((1, tm, 512), lambda i, j: (j // 2, i, j % 2)))
    q_hm = _proj_call(
        "proj_q", xn, w, w_rows=(o_q, D_B), tm=tm, tn=512, epilogue=_ep_q,
        out_shape=jax.ShapeDtypeStruct((B_HEADS // 2, t, LANES), F32),
        out_specs=pl.BlockSpec((512 // LANES, tm, LANES), lambda i, j: (j, i, 0)),
        extra=(tab_q,), extra_specs=(pl.BlockSpec((3, tm, LANES), lambda i, j: (0, i, 0)),))
    cmp_in = _proj_call(
        "proj_cmp", xn, w, w_rows=(o_kc, 2 * KV_DIM), tm=tm, tn=2 * KV_DIM, epilogue=_ep_cmp,
        out_shape=jax.ShapeDtypeStruct((2, B_KV_HEADS, t, HEAD_DIM), F32),
        out_specs=pl.BlockSpec((2, B_KV_HEADS, tm, HEAD_DIM), lambda i, j: (0, 0, i, 0)))
    kv_aug, v_aug = _proj_call(
        "proj_kv", xn, w_kv, tm=tm, tn=2 * KV_DIM, epilogue=functools.partial(_ep_kv, tm=tm),
        out_shape=[jax.ShapeDtypeStruct((2, B_KV_HEADS, t, 2 * LANES), BF16),
                   jax.ShapeDtypeStruct((2, B_KV_HEADS, t, LANES), BF16)],
        out_specs=[pl.BlockSpec((1, B_KV_HEADS, tm, 2 * LANES), lambda i, j: (j, 0, i, 0)),
                   pl.BlockSpec((1, B_KV_HEADS, tm, LANES), lambda i, j: (j, 0, i, 0))],
        extra=(tab_kv,), extra_specs=(pl.BlockSpec((3, tm, LANES), lambda i, j: (0, i, 0)),))
    gates = _proj_call(
        "proj_gate", xn, w_g, tm=tm, tn=GATE_LANES, epilogue=_ep_gate,
        out_shape=jax.ShapeDtypeStruct((t, GATE_LANES), F32),
        out_specs=pl.BlockSpec((tm, GATE_LANES), lambda i, j: (i, 0)))
    sg, wa16, wb16, wo16 = _proj_call(
        "proj_merge", xn, w_m, tm=tm, tn=512, epilogue=_ep_sigmoid,
        out_shape=jax.ShapeDtypeStruct((2, t, d), BF16),
        out_specs=pl.BlockSpec((1, tm, 512), lambda i, j: (j // 4, i, j % 4)),
        to_bf16=(w_proj_a[0], w_proj_b[0], w_out[0]))

    oa = _hgrn_call(proj4, lb, a_norm_g[0].reshape(1, D_A))

    half = CMP_BLOCK // 2
    x16 = cmp_in.reshape(2, B_KV_HEADS, n_cmp_pad, CMP_STRIDE * HEAD_DIM)
    pe2 = jnp.stack([cmp_pe_k[0], cmp_pe_v[0]]).reshape(2, 2, half * HEAD_DIM)
    w1 = jnp.stack([cmp_w1_k[0], cmp_w1_v[0]]).astype(BF16)
    b1 = jnp.stack([cmp_b1_k[0], cmp_b1_v[0]]).reshape(2, 1, CMP_HIDDEN)
    w2p = jnp.pad(jnp.stack([cmp_w2_k[0], cmp_w2_v[0]]), ((0, 0), (0, 0), (0, LANES - HEAD_DIM))).astype(BF16)
    b2p = jnp.pad(jnp.stack([cmp_b2_k[0], cmp_b2_v[0]]), ((0, 0), (0, LANES - HEAD_DIM))).reshape(2, 1, LANES)
    cmp_aug = _compress_call(x16, pe2, w1, b1, w2p, b2p, tab_cmp)
    ocmp, bias = _nsa_select_call(q_hm, cmp_aug, _slc_weight_matrix(n_cmp_pad))
    ob = _nsa_attend_call(q_hm, bias, ocmp, gates, kv_aug, v_aug)

    h1, wg16, wu16, wd16 = _post_call(oa, ob, sg, x2, wa16, wb16, wo16, post_mix_g[0].reshape(1, d),
                                      w_gate[0], w_up[0], w_down[0])
    out = _ffn_call(h1, pre_ffn_g[0].reshape(1, d), wg16, wu16, wd16, post_ffn_g[0].reshape(1, d))
    return out.reshape(bsz, t, d)
```

```python
import functools

import numpy as np
import jax
import jax.numpy as jnp
from jax import lax
from jax.experimental import pallas as pl
from jax.experimental.pallas import tpu as pltpu

F32 = jnp.float32
BF16 = jnp.bfloat16

D_MODEL = 2048
D_A = 1024
D_B = 1024
A_HEADS = 8
A_DK = 128
HGRN_PAIRWISE = 4
B_HEADS = 16
B_KV_HEADS = 4
B_GROUP = 4
HEAD_DIM = 64
KV_DIM = 256
CMP_BLOCK = 32
CMP_STRIDE = 16
CMP_HIDDEN = 256
SLC_BLOCK = 64
SLC_TOP_N = 16
SLC_FORCED = 3
SLC_WEIGHTS = (1.0, 2.0, 2.0, 2.0, 1.0)
WINDOW = 512
Q_TILE = 512
ATT_TILE = 256
KV_PER_ATTEND_STEP = 2
KV_PER_SELECT_STEP = 2
PROJ_SUBTILE = 256
GATE_LANES = 256
ROPE_THETA = 500000.0
ROPE_DIM = 16
D_FF = 5632
EPS = 1e-6
LOG2_E = 1.4426950408889634
NEG_INF = -1e30
N_BLK_LANES = 128
LANES = 128
VMEM_LIMIT = 56 * 1024 * 1024

NT_DIMS = (((1,), (1,)), ((), ()))
TN_DIMS = (((0,), (0,)), ((), ()))


def _cparams(n_axes, vmem=None):
    return pltpu.CompilerParams(dimension_semantics=("arbitrary",) * n_axes,
                                vmem_limit_bytes=vmem)


def _rmsnorm_kernel(x_ref, g_ref, o_ref):
    x = x_ref[...]
    y = x * lax.rsqrt(jnp.mean(x * x, axis=-1, keepdims=True) + EPS)
    o_ref[...] = (y * g_ref[...]).astype(o_ref.dtype)


def _rmsnorm_call(x2, g, tm=512):
    t, d = x2.shape
    return pl.pallas_call(
        _rmsnorm_kernel, grid=(t // tm,),
        in_specs=[pl.BlockSpec((tm, d), lambda i: (i, 0)), pl.BlockSpec((1, d), lambda i: (0, 0))],
        out_specs=pl.BlockSpec((tm, d), lambda i: (i, 0)),
        out_shape=jax.ShapeDtypeStruct((t, d), BF16), name="pre_mix_rmsnorm",
        compiler_params=_cparams(1))(x2, g)


def _proj_call(name, xn, w_t, *, tm, tn, epilogue, out_shape, out_specs, extra=(), extra_specs=(),
               w_rows=None, to_bf16=()):
    t, k = xn.shape
    row0, n = (0, w_t.shape[0]) if w_rows is None else w_rows
    assert tn % PROJ_SUBTILE == 0 and n % tn == 0 and row0 % tn == 0
    grid = (t // tm, n // tn)
    steps = grid[0] * grid[1]
    n_extra, n_cast = len(extra), len(to_bf16)
    out_specs = list(out_specs) if isinstance(out_specs, (list, tuple)) else [out_specs]
    out_shape = list(out_shape) if isinstance(out_shape, (list, tuple)) else [out_shape]
    n_out = len(out_specs)

    def body(x_ref, w_ref, *rest):
        extra_refs = rest[:n_extra]
        cast_in = rest[n_extra:n_extra + n_cast]
        outs = rest[n_extra + n_cast:n_extra + n_cast + n_out]
        cast_out = rest[n_extra + n_cast + n_out:]
        x = x_ref[...]
        for c0 in range(0, tn, PROJ_SUBTILE):
            acc = lax.dot_general(x, w_ref[c0:c0 + PROJ_SUBTILE, :], NT_DIMS, preferred_element_type=F32)
            epilogue(acc, c0, extra_refs, outs)
        for src, dst in zip(cast_in, cast_out):
            dst[...] = src[...].astype(BF16)

    def slab(a):
        rows = a.shape[0] // steps
        assert a.shape[0] % steps == 0 and rows % 16 == 0
        return pl.BlockSpec((rows, a.shape[1]), lambda i, j: (i * grid[1] + j, 0))

    res = pl.pallas_call(
        body, grid=grid,
        in_specs=[pl.BlockSpec((tm, k), lambda i, j: (i, 0)),
                  pl.BlockSpec((tn, k), lambda i, j: (row0 // tn + j, 0)), *extra_specs,
                  *[slab(a) for a in to_bf16]],
        out_specs=out_specs + [slab(a) for a in to_bf16],
        out_shape=out_shape + [jax.ShapeDtypeStruct(a.shape, BF16) for a in to_bf16], name=name,
        compiler_params=_cparams(2, VMEM_LIMIT))(xn, w_t, *extra, *to_bf16)
    return res[0] if len(res) == 1 else res


def _rope_tile(a, tab_ref):
    return (a * tab_ref[0] + pltpu.roll(a, LANES - ROPE_DIM // 2, 1) * tab_ref[1]
            + pltpu.roll(a, ROPE_DIM // 2, 1) * tab_ref[2])


def _ep_plain(acc, c0, extra, outs):
    outs[0][0, :, c0:c0 + acc.shape[1]] = acc.astype(outs[0].dtype)


def _ep_q(acc, c0, extra, outs):
    for pair in range(acc.shape[1] // LANES):
        outs[0][c0 // LANES + pair] = _rope_tile(acc[:, pair * LANES:(pair + 1) * LANES], extra[0])


def _ep_cmp(acc, c0, extra, outs):
    for hh in range(acc.shape[1] // HEAD_DIM):
        head = c0 // HEAD_DIM + hh
        outs[0][head // B_KV_HEADS, head % B_KV_HEADS] = acc[:, hh * HEAD_DIM:(hh + 1) * HEAD_DIM]


def _ep_kv(acc, c0, extra, outs, *, tm):
    i = pl.program_id(0)
    j = pl.program_id(1)
    rowg = i * tm + lax.broadcasted_iota(jnp.int32, (tm, LANES), 0)
    lane = lax.broadcasted_iota(jnp.int32, (tm, LANES), 1)
    aux = jnp.where(j == 0, (rowg // SLC_BLOCK == lane).astype(F32), 0.0).astype(BF16)
    for hh in range(acc.shape[1] // LANES):
        h = c0 // LANES + hh
        r = _rope_tile(acc[:, hh * LANES:(hh + 1) * LANES], extra[0])
        outs[0][0, h, :, 0:LANES] = r.astype(BF16)
        outs[0][0, h, :, LANES:2 * LANES] = aux
        outs[1][0, h] = jnp.where(lane < HEAD_DIM, pltpu.roll(r, HEAD_DIM, 1), 1.0).astype(BF16)


def _ep_gate(acc, c0, extra, outs):
    outs[0][:, c0:c0 + acc.shape[1]] = jax.nn.sigmoid(acc)


def _ep_sigmoid(acc, c0, extra, outs):
    outs[0][0, :, c0:c0 + acc.shape[1]] = jax.nn.sigmoid(acc).astype(outs[0].dtype)


def _split3(x):
    hi = x.astype(BF16)
    r1 = x - hi.astype(F32)
    mid = r1.astype(BF16)
    lo = (r1 - mid.astype(F32)).astype(BF16)
    return hi, mid, lo


def _hgrn_kernel(q_ref, f_ref, i_ref, g_ref, lb_ref, gn_ref, o_ref, st_ref, *, chunk, heads):
    c = pl.program_id(1)

    @pl.when(c == 0)
    def _():
        st_ref[...] = jnp.zeros_like(st_ref)

    row = lax.broadcasted_iota(jnp.int32, (chunk, chunk), 0)
    col = lax.broadcasted_iota(jnp.int32, (chunk, chunk), 1)
    tri = (col <= row).astype(BF16)
    rowv = lax.broadcasted_iota(jnp.int32, (chunk, A_DK), 0)

    for hb in range(heads):
        sl = slice(hb * A_DK, (hb + 1) * A_DK)
        q = q_ref[0, :, sl]
        ii = i_ref[0, :, sl]
        gg = g_ref[0, :, sl]
        lbv = lb_ref[:, sl]
        f = lbv + (1.0 - lbv) * jax.nn.sigmoid(f_ref[0, :, sl])
        lf = jnp.log(f)
        k = 1.0 - f
        b = sum(jnp.dot(tri, p, preferred_element_type=F32) for p in _split3(lf)) * LOG2_E
        b_end = b[chunk - 1:chunk, :]
        ii16 = ii.astype(BF16)

        st = st_ref[hb]
        o = lax.dot_general((q * jnp.exp2(b)).astype(BF16), st.astype(BF16), NT_DIMS,
                            preferred_element_type=F32)
        kd = (k * jnp.exp2(b_end - b)).astype(BF16)
        st_ref[hb] = st * jnp.exp2(b_end) + lax.dot_general(ii16, kd, TN_DIMS,
                                                            preferred_element_type=F32)

        att = jnp.where(col == row, jnp.sum(q * k, axis=-1, keepdims=True), 0.0)
        for d in range(1, HGRN_PAIRWISE):
            valid = (rowv % HGRN_PAIRWISE) >= d
            w = jnp.exp2(b - pltpu.roll(b, d, 0))
            p = jnp.where(valid, q * pltpu.roll(k, d, 0) * w, 0.0)
            att = att + jnp.where(col == row - d, jnp.sum(p, axis=-1, keepdims=True), 0.0)
        m = HGRN_PAIRWISE
        while m < chunk:
            grp = chunk // (2 * m)
            b3 = b.reshape(grp, 2 * m, A_DK)
            refrow = jnp.broadcast_to(b3[:, m - 1:m, :], (grp, 2 * m, A_DK)).reshape(chunk, A_DK)
            second = (rowv % (2 * m)) >= m
            ql = jnp.where(second, q * jnp.exp2(b - refrow), 0.0)
            kl = jnp.where(second, 0.0, k * jnp.exp2(refrow - b))
            a = lax.dot_general(ql.astype(BF16), kl.astype(BF16), NT_DIMS, preferred_element_type=F32)
            if grp > 1:
                a = jnp.where(row // (2 * m) == col // (2 * m), a, 0.0)
            att = att + a
            m *= 2
        o = o + jnp.dot(att.astype(BF16), ii16, preferred_element_type=F32)

        o = o * lax.rsqrt(jnp.mean(o * o, axis=-1, keepdims=True) + EPS)
        o_ref[:, sl] = ((o * gn_ref[:, sl]) * (gg * jax.nn.sigmoid(gg))).astype(o_ref.dtype)


def _hgrn_call(proj4, lb, gn, *, chunk=128, heads=8):
    _, t, _ = proj4.shape
    w = heads * A_DK

    def spec(kind):
        return pl.BlockSpec((1, chunk, w), lambda h, c, kind=kind: (kind, c, h))

    vec = pl.BlockSpec((1, w), lambda h, c: (0, h))
    return pl.pallas_call(
        functools.partial(_hgrn_kernel, chunk=chunk, heads=heads),
        grid=(A_HEADS // heads, t // chunk),
        in_specs=[spec(0), spec(1), spec(2), spec(3), vec, vec],
        out_specs=pl.BlockSpec((chunk, w), lambda h, c: (c, h)),
        out_shape=jax.ShapeDtypeStruct((t, D_A), BF16),
        scratch_shapes=[pltpu.VMEM((heads, A_DK, A_DK), F32)], name="hgrn2_scan",
        compiler_params=_cparams(2))(proj4, proj4, proj4, proj4, lb, gn)


def _compress_kernel(x_ref, pe_ref, w1_ref, b1_ref, w2_ref, b2_ref, tab_ref, o_ref):
    half = CMP_BLOCK * HEAD_DIM // 2
    x = x_ref[0, 0]
    top = jnp.dot((x + pe_ref[0, 0:1, :]).astype(BF16), w1_ref[0, :half, :], preferred_element_type=F32)
    bot = jnp.dot((x + pe_ref[0, 1:2, :]).astype(BF16), w1_ref[0, half:, :], preferred_element_type=F32)
    n = x.shape[0]
    h = jax.nn.gelu(top + pltpu.roll(bot, n - 1, 0) + b1_ref[0])
    y = jnp.dot(h.astype(BF16), w2_ref[0], preferred_element_type=F32) + b2_ref[0]
    y = _rope_tile(y, tab_ref)
    hi = y.astype(BF16)
    lo = (y - hi.astype(F32)).astype(BF16)
    o_ref[0, 0, :, 0:LANES] = (y + pltpu.roll(y, HEAD_DIM, 1)).astype(BF16)
    o_ref[0, 0, :, LANES:2 * LANES] = lo


def _compress_call(x16, pe2, w1, b1, w2p, b2p, tab):
    _, nh, n, wid = x16.shape
    hid = w1.shape[-1]
    return pl.pallas_call(
        _compress_kernel, grid=(2, nh),
        in_specs=[pl.BlockSpec((1, 1, n, wid), lambda kv, h: (kv, h, 0, 0)),
                  pl.BlockSpec((1, 2, wid), lambda kv, h: (kv, 0, 0)),
                  pl.BlockSpec((1, 2 * wid, hid), lambda kv, h: (kv, 0, 0)),
                  pl.BlockSpec((1, 1, hid), lambda kv, h: (kv, 0, 0)),
                  pl.BlockSpec((1, hid, LANES), lambda kv, h: (kv, 0, 0)),
                  pl.BlockSpec((1, 1, LANES), lambda kv, h: (kv, 0, 0)),
                  pl.BlockSpec((3, n, LANES), lambda kv, h: (kv, 0, 0))],
        out_specs=pl.BlockSpec((1, 1, n, 2 * LANES), lambda kv, h: (kv, h, 0, 0)),
        out_shape=jax.ShapeDtypeStruct((2, nh, n, 2 * LANES), BF16),
        name="compress_mlp", compiler_params=_cparams(2, VMEM_LIMIT))(x16, pe2, w1, b1, w2p, b2p, tab)


def _nsa_select_kernel(q_ref, kc_ref, vc_ref, wt_ref, ocmp_ref, bias_ref, lhs_ref):
    qt = pl.program_id(1)
    q0 = qt * Q_TILE
    rows = B_GROUP * Q_TILE
    n_cmp = kc_ref.shape[2]

    def chains(n_cols, n_blk):
        t_col = q0 + lax.broadcasted_iota(jnp.int32, (rows, 1), 0) % Q_TILE
        n_idx = lax.broadcasted_iota(jnp.int32, (rows, n_cols), 1)
        vis = n_idx <= (t_col - (CMP_BLOCK - 1)) // CMP_STRIDE
        blk = lax.broadcasted_iota(jnp.int32, (n_blk, Q_TILE), 0)
        tok = q0 + lax.broadcasted_iota(jnp.int32, (n_blk, Q_TILE), 1)
        cur = tok // SLC_BLOCK
        forced = (blk == 0) | (blk == cur) | (blk == cur - 1)
        candidate = (blk * SLC_BLOCK <= tok) & jnp.logical_not(forced)
        wt = wt_ref[0:n_blk, 0:n_cols]
        zeros64 = jnp.zeros((rows, HEAD_DIM), BF16)
        never = jnp.zeros((N_BLK_LANES - n_blk, Q_TILE), F32)

        for c in range(KV_PER_SELECT_STEP):
            for pair in range(B_GROUP // 2):
                qp = q_ref[c * (B_GROUP // 2) + pair]
                hi = qp.astype(BF16)
                lo = (qp - hi.astype(F32)).astype(BF16)
                for half in range(2):
                    r = slice((2 * pair + half) * Q_TILE, (2 * pair + half + 1) * Q_TILE)
                    ln = slice(half * HEAD_DIM, (half + 1) * HEAD_DIM)
                    lhs_ref[c, r, 0:HEAD_DIM] = hi[:, ln]
                    lhs_ref[c, r, HEAD_DIM:2 * HEAD_DIM] = lo[:, ln]
                    lhs_ref[c, r, 2 * HEAD_DIM:3 * HEAD_DIM] = hi[:, ln]
            lhs_ref[c, :, 3 * HEAD_DIM:] = zeros64
            s = lax.dot_general(lhs_ref[c], kc_ref[0, c, 0:n_cols, :], NT_DIMS, preferred_element_type=F32)
            s = jnp.where(vis, s, NEG_INF)
            e = jnp.exp2(s - jnp.max(s, axis=-1, keepdims=True))
            l = jnp.sum(e, axis=-1, keepdims=True)
            p = e * jnp.where(t_col >= CMP_BLOCK - 1, 1.0 / l, 0.0)
            o_cmp = jnp.dot(p.astype(BF16), vc_ref[0, c, 0:n_cols, 0:LANES], preferred_element_type=F32)
            for g in range(B_GROUP):
                c0 = (c * B_GROUP + g) * HEAD_DIM
                ocmp_ref[:, c0:c0 + HEAD_DIM] = o_cmp[g * Q_TILE:(g + 1) * Q_TILE, :HEAD_DIM]

            psum = p[0:Q_TILE] + p[Q_TILE:2 * Q_TILE] + p[2 * Q_TILE:3 * Q_TILE] + p[3 * Q_TILE:]
            ps_hi = psum.astype(BF16)
            ps_lo = (psum - ps_hi.astype(F32)).astype(BF16)
            pslc = (lax.dot_general(wt, ps_hi, NT_DIMS, preferred_element_type=F32)
                    + lax.dot_general(wt, ps_lo, NT_DIMS, preferred_element_type=F32))
            score = jnp.where(candidate, pslc, jnp.where(forced, -jnp.inf, NEG_INF))
            sel_t = forced.astype(F32)
            for _ in range(SLC_TOP_N - SLC_FORCED):
                best = jnp.max(score, axis=0, keepdims=True)
                first = jnp.min(jnp.where(score == best, blk, N_BLK_LANES), axis=0, keepdims=True)
                hit = blk == first
                score = jnp.where(hit, -jnp.inf, score)
                sel_t = jnp.where(hit, 1.0, sel_t)
            if n_blk < N_BLK_LANES:
                sel_t = jnp.concatenate([sel_t, never], axis=0)
            bias_ref[c] = jnp.where(sel_t.T > 0.0, 0.0, NEG_INF).astype(BF16)

    ratio = SLC_BLOCK // CMP_STRIDE
    early = qt < pl.num_programs(1) // 2
    pl.when(early)(lambda: chains(n_cmp // 2, n_cmp // (2 * ratio)))
    pl.when(jnp.logical_not(early))(lambda: chains(n_cmp, N_BLK_LANES))


def _nsa_select_call(q_hm, cmp_aug, wt):
    _, t, _ = q_hm.shape
    n_cmp = cmp_aug.shape[2]
    kvs = KV_PER_SELECT_STEP
    return pl.pallas_call(
        _nsa_select_kernel, grid=(B_KV_HEADS // kvs, t // Q_TILE),
        in_specs=[pl.BlockSpec((kvs * B_GROUP // 2, Q_TILE, LANES), lambda p, i: (p, i, 0)),
                  pl.BlockSpec((1, kvs, n_cmp, 2 * LANES), lambda p, i: (0, p, 0, 0)),
                  pl.BlockSpec((1, kvs, n_cmp, 2 * LANES), lambda p, i: (1, p, 0, 0)),
                  pl.BlockSpec((N_BLK_LANES, n_cmp), lambda p, i: (0, 0))],
        out_specs=[pl.BlockSpec((Q_TILE, kvs * B_GROUP * HEAD_DIM), lambda p, i: (i, p)),
                   pl.BlockSpec((kvs, Q_TILE, N_BLK_LANES), lambda p, i: (p, i, 0))],
        out_shape=[jax.ShapeDtypeStruct((t, D_B), F32),
                   jax.ShapeDtypeStruct((B_KV_HEADS, t, N_BLK_LANES), BF16)],
        scratch_shapes=[pltpu.VMEM((kvs, B_GROUP * Q_TILE, 2 * LANES), BF16)], name="nsa_select",
        compiler_params=_cparams(2, VMEM_LIMIT))(q_hm, cmp_aug, cmp_aug, wt)


def _nsa_attend_kernel(q_ref, bias_ref, ocmp_ref, gate_ref, gx_ref, ksv_ref, kwv_ref, vs_ref, vw_ref, o_ref,
                       lhs_w_ref, lhs_s_ref, m_ref, acc_ref, *, key_tile, tiles_per_group):
    q0 = pl.program_id(1) * ATT_TILE
    rows = B_GROUP * ATT_TILE
    chains = range(KV_PER_ATTEND_STEP)
    t_col = q0 + lax.broadcasted_iota(jnp.int32, (rows, 1), 0) % ATT_TILE

    for c in chains:
        for g in range(B_GROUP):
            r = slice(g * ATT_TILE, (g + 1) * ATT_TILE)
            qp = q_ref[c * (B_GROUP // 2) + g // 2]
            qhi = qp[:, (g % 2) * HEAD_DIM:(g % 2 + 1) * HEAD_DIM].astype(BF16)
            lhs_s_ref[c, r, 0:HEAD_DIM] = qhi
            lhs_s_ref[c, r, LANES:] = bias_ref[c]
            lhs_w_ref[c, r, 0:HEAD_DIM] = qhi
        lhs_s_ref[c, :, HEAD_DIM:LANES] = jnp.zeros((rows, LANES - HEAD_DIM), BF16)
        lhs_w_ref[c, :, HEAD_DIM:] = jnp.zeros((rows, 2 * LANES - HEAD_DIM), BF16)

    for c in chains:
        m_ref[c] = jnp.full((rows, LANES), NEG_INF, F32)
        acc_ref[c] = jnp.zeros((rows, LANES), F32)

    def tile(ti, causal):
        start = pl.multiple_of(ti * key_tile, key_tile)
        for c in chains:
            st = lax.dot_general(lhs_s_ref[c], ksv_ref[0, c, pl.ds(start, key_tile), :], NT_DIMS,
                                 preferred_element_type=F32)
            if causal:
                kp = start + lax.broadcasted_iota(jnp.int32, (rows, key_tile), 1)
                st = jnp.where(kp <= t_col, st, NEG_INF)
            m_run = m_ref[c]
            m_new = jnp.maximum(m_run, jnp.max(st, axis=-1, keepdims=True))
            pt = jnp.exp2(st - jnp.concatenate([m_new] * (key_tile // LANES), axis=1))
            pv = jnp.dot(pt.astype(BF16), vs_ref[0, c, pl.ds(start, key_tile), :], preferred_element_type=F32)
            acc_ref[c] = acc_ref[c] * jnp.exp2(m_run - m_new) + pv
            m_ref[c] = m_new

    def group(gi, _):
        for u in range(tiles_per_group):
            tile(gi * tiles_per_group + u, False)
        return 0

    def single(ti, _):
        tile(ti, False)
        return 0

    n_full = q0 // key_tile
    n_groups = n_full // tiles_per_group
    lax.fori_loop(0, n_groups, group, 0)
    lax.fori_loop(n_groups * tiles_per_group, n_full, single, 0)
    tile(n_full, True)

    slab = WINDOW + ATT_TILE
    ws = pl.multiple_of(jnp.maximum(q0 - WINDOW, 0), ATT_TILE)
    kpos = ws + lax.broadcasted_iota(jnp.int32, (rows, slab), 1)
    in_window = lax.bitcast_convert_type(t_col - kpos, jnp.uint32) < jnp.uint32(WINDOW)
    lane = lax.broadcasted_iota(jnp.int32, (ATT_TILE, LANES), 1)
    per_branch = B_GROUP * HEAD_DIM

    def head_pair(acc, k):
        a_even = acc[2 * k * ATT_TILE:(2 * k + 1) * ATT_TILE]
        a_odd = acc[(2 * k + 1) * ATT_TILE:(2 * k + 2) * ATT_TILE]
        num = jnp.where(lane < HEAD_DIM, a_even, pltpu.roll(a_odd, HEAD_DIM, 1))
        den = jnp.where(lane < HEAD_DIM, pltpu.roll(a_even, HEAD_DIM, 1), a_odd)
        return num * (1.0 / den)

    gates = gate_ref[...]
    g_hi = gates.astype(BF16)
    g_lo = (gates - g_hi.astype(F32)).astype(BF16)

    for c in chains:
        acc_s = acc_ref[c]
        s = lax.dot_general(lhs_w_ref[c], kwv_ref[0, c, pl.ds(ws, slab), :], NT_DIMS,
                            preferred_element_type=F32)
        s = jnp.where(in_window, s, NEG_INF)
        e = jnp.exp2(s - jnp.max(s, axis=-1, keepdims=True))
        acc_w = jnp.dot(e.astype(BF16), vw_ref[0, c, pl.ds(ws, slab), :], preferred_element_type=F32)

        gx = (jnp.dot(g_hi, gx_ref[c], preferred_element_type=F32)
              + jnp.dot(g_lo, gx_ref[c], preferred_element_type=F32))
        for k in range(B_GROUP // 2):
            c0 = c * B_GROUP * HEAD_DIM + k * LANES
            g0 = k * LANES
            og = (gx[:, g0:g0 + LANES] * ocmp_ref[:, c0:c0 + LANES]
                  + gx[:, per_branch + g0:per_branch + g0 + LANES] * head_pair(acc_s, k)
                  + gx[:, 2 * per_branch + g0:2 * per_branch + g0 + LANES] * head_pair(acc_w, k))
            o_ref[:, c0:c0 + LANES] = og.astype(o_ref.dtype)


def _nsa_attend_call(q_hm, bias, ocmp, gates, kv_aug, v_aug, *, key_tile=512, tiles_per_group=2):
    _, t, _ = q_hm.shape
    assert t % key_tile == 0 and key_tile % ATT_TILE == 0 and t % ATT_TILE == 0
    rows = B_GROUP * ATT_TILE
    kvs = KV_PER_ATTEND_STEP
    wide = kvs * B_GROUP * HEAD_DIM
    resident = dict(pipeline_mode=pl.Buffered(1))
    return pl.pallas_call(
        functools.partial(_nsa_attend_kernel, key_tile=key_tile, tiles_per_group=tiles_per_group),
        grid=(B_KV_HEADS // kvs, t // ATT_TILE),
        in_specs=[pl.BlockSpec((kvs * B_GROUP // 2, ATT_TILE, LANES), lambda p, i: (p, i, 0)),
                  pl.BlockSpec((kvs, ATT_TILE, N_BLK_LANES), lambda p, i: (p, i, 0)),
                  pl.BlockSpec((ATT_TILE, wide), lambda p, i: (i, p)),
                  pl.BlockSpec((ATT_TILE, GATE_LANES), lambda p, i: (i, 0)),
                  pl.BlockSpec((kvs, GATE_LANES, 3 * B_GROUP * HEAD_DIM), lambda p, i: (p, 0, 0)),
                  pl.BlockSpec((1, kvs, t, 2 * LANES), lambda p, i: (0, p, 0, 0), **resident),
                  pl.BlockSpec((1, kvs, t, 2 * LANES), lambda p, i: (1, p, 0, 0), **resident),
                  pl.BlockSpec((1, kvs, t, LANES), lambda p, i: (0, p, 0, 0), **resident),
                  pl.BlockSpec((1, kvs, t, LANES), lambda p, i: (1, p, 0, 0), **resident)],
        out_specs=pl.BlockSpec((ATT_TILE, wide), lambda p, i: (i, p)),
        out_shape=jax.ShapeDtypeStruct((t, D_B), BF16),
        scratch_shapes=[pltpu.VMEM((kvs, rows, 2 * LANES), BF16), pltpu.VMEM((kvs, rows, 2 * LANES), BF16),
                        pltpu.VMEM((kvs, rows, LANES), F32), pltpu.VMEM((kvs, rows, LANES), F32)],
        name="nsa_attend",
        compiler_params=_cparams(2, VMEM_LIMIT))(q_hm, bias, ocmp, gates, _gate_expand_matrix(),
                                                 kv_aug, kv_aug, v_aug, v_aug)


def _post_kernel(oa_ref, ob_ref, sg_ref, x_ref, wa_ref, wb_ref, wo_ref, g_ref, wg_ref, wu_ref, wd_ref,
                 o_ref, wg16_ref, wu16_ref, wd16_ref):
    ya = jnp.dot(oa_ref[...], wa_ref[...], preferred_element_type=F32)
    yb = jnp.dot(ob_ref[...], wb_ref[...], preferred_element_type=F32)
    merged = sg_ref[0].astype(F32) * ya + sg_ref[1].astype(F32) * yb
    mix = jnp.dot(merged.astype(BF16), wo_ref[...], preferred_element_type=F32)
    y = mix * lax.rsqrt(jnp.mean(mix * mix, axis=-1, keepdims=True) + EPS)
    o_ref[...] = x_ref[...] + y * g_ref[...]
    wg16_ref[...] = wg_ref[...].astype(BF16)
    wu16_ref[...] = wu_ref[...].astype(BF16)
    wd16_ref[...] = wd_ref[...].astype(BF16)


def _post_call(oa, ob, sg, x2, wa, wb, wo, g, wg, wu, wd, tm=256):
    t, d = x2.shape
    steps = t // tm
    dff = wg.shape[1]
    assert d % (steps * 16) == 0 and dff % (steps * 16) == 0
    const = dict(pipeline_mode=pl.Buffered(1))
    return pl.pallas_call(
        _post_kernel, grid=(steps,),
        in_specs=[pl.BlockSpec((tm, D_A), lambda i: (i, 0)),
                  pl.BlockSpec((tm, D_B), lambda i: (i, 0)),
                  pl.BlockSpec((2, tm, d), lambda i: (0, i, 0)),
                  pl.BlockSpec((tm, d), lambda i: (i, 0)),
                  pl.BlockSpec((D_A, d), lambda i: (0, 0), **const),
                  pl.BlockSpec((D_B, d), lambda i: (0, 0), **const),
                  pl.BlockSpec((d, d), lambda i: (0, 0), **const),
                  pl.BlockSpec((1, d), lambda i: (0, 0)),
                  pl.BlockSpec((d // steps, dff), lambda i: (i, 0)),
                  pl.BlockSpec((d // steps, dff), lambda i: (i, 0)),
                  pl.BlockSpec((dff // steps, d), lambda i: (i, 0))],
        out_specs=[pl.BlockSpec((tm, d), lambda i: (i, 0)),
                   pl.BlockSpec((d // steps, dff), lambda i: (i, 0)),
                   pl.BlockSpec((d // steps, dff), lambda i: (i, 0)),
                   pl.BlockSpec((dff // steps, d), lambda i: (i, 0))],
        out_shape=[jax.ShapeDtypeStruct((t, d), F32), jax.ShapeDtypeStruct((d, dff), BF16),
                   jax.ShapeDtypeStruct((d, dff), BF16), jax.ShapeDtypeStruct((dff, d), BF16)],
        name="mix_out_residual",
        compiler_params=_cparams(1, VMEM_LIMIT))(oa, ob, sg, x2, wa, wb, wo, g, wg, wu, wd)


def _ffn_kernel(h_ref, g1_ref, wg_ref, wu_ref, wd_ref, g2_ref, o_ref, hn_ref, acc_ref):
    f = pl.program_id(1)

    @pl.when(f == 0)
    def _():
        h = h_ref[...]
        y = h * lax.rsqrt(jnp.mean(h * h, axis=-1, keepdims=True) + EPS)
        hn_ref[...] = (y * g1_ref[...]).astype(BF16)
        acc_ref[...] = jnp.zeros_like(acc_ref)

    hn = hn_ref[...]
    a = jnp.dot(hn, wg_ref[...], preferred_element_type=F32)
    u = jnp.dot(hn, wu_ref[...], preferred_element_type=F32)
    z = (a * jax.nn.sigmoid(a) * u).astype(BF16)
    acc_ref[...] += jnp.dot(z, wd_ref[...], preferred_element_type=F32)

    @pl.when(f == pl.num_programs(1) - 1)
    def _():
        ff = acc_ref[...]
        y = ff * lax.rsqrt(jnp.mean(ff * ff, axis=-1, keepdims=True) + EPS)
        o_ref[...] = h_ref[...] + y * g2_ref[...]


def _ffn_call(h1, g1, wg, wu, wd, g2, tm=512, tf=512):
    t, d = h1.shape
    dff = wg.shape[1]
    return pl.pallas_call(
        _ffn_kernel, grid=(t // tm, dff // tf),
        in_specs=[pl.BlockSpec((tm, d), lambda i, f: (i, 0)),
                  pl.BlockSpec((1, d), lambda i, f: (0, 0)),
                  pl.BlockSpec((d, tf), lambda i, f: (0, f)),
                  pl.BlockSpec((d, tf), lambda i, f: (0, f)),
                  pl.BlockSpec((tf, d), lambda i, f: (f, 0)),
                  pl.BlockSpec((1, d), lambda i, f: (0, 0))],
        out_specs=pl.BlockSpec((tm, d), lambda i, f: (i, 0)),
        out_shape=jax.ShapeDtypeStruct((t, d), F32),
        scratch_shapes=[pltpu.VMEM((tm, d), BF16), pltpu.VMEM((tm, d), F32)], name="swiglu_ffn",
        compiler_params=_cparams(2, VMEM_LIMIT))(h1, g1, wg, wu, wd, g2)


def _rope_tables(pos, period, scale=1.0):
    half = ROPE_DIM // 2
    inv = jnp.float32(ROPE_THETA) ** (-jnp.arange(half, dtype=F32) * 2.0 / ROPE_DIM)
    r = np.arange(LANES) % period
    pick = np.zeros((half, LANES), np.float32)
    pick[r % half, np.arange(LANES)] = 1.0
    inv_lane = jnp.sum(inv[:, None] * pick, axis=0)
    ang = pos.astype(F32)[:, None] * inv_lane[None, :]
    cos, sin = jnp.cos(ang), jnp.sin(ang)
    lo = jnp.asarray((r < half)[None, :])
    hi = jnp.asarray(((r >= half) & (r < ROPE_DIM))[None, :])
    c = jnp.where(lo | hi, cos, 1.0)
    sa = jnp.where(lo, -sin, 0.0)
    sb = jnp.where(hi, sin, 0.0)
    return jnp.stack([c, sa, sb]) * scale


def _gate_expand_matrix():
    e = np.zeros((B_KV_HEADS, GATE_LANES, 3 * B_GROUP * HEAD_DIM), np.float32)
    for kvh in range(B_KV_HEADS):
        for g in range(B_GROUP):
            for b in range(3):
                c0 = (b * B_GROUP + g) * HEAD_DIM
                e[kvh, 3 * (kvh * B_GROUP + g) + b, c0:c0 + HEAD_DIM] = 1.0
    return jnp.asarray(e, BF16)


def _slc_weight_matrix(n_cmp_pad):
    ratio = SLC_BLOCK // CMP_STRIDE
    w = np.zeros((N_BLK_LANES, n_cmp_pad), np.float32)
    for j in range(N_BLK_LANES):
        for o, wv in enumerate(SLC_WEIGHTS):
            n = ratio * j + o - 1
            if 0 <= n < n_cmp_pad - 1:
                w[j, n] = wv
    return jnp.asarray(w, BF16)


def kernel(x, pre_mix_g, w_in, lb_table, a_norm_g, cmp_pe_k, cmp_w1_k, cmp_b1_k, cmp_w2_k, cmp_b2_k, cmp_pe_v, cmp_w1_v, cmp_b1_v, cmp_w2_v, cmp_b2_v, w_proj_a, w_proj_b, w_out, post_mix_g, pre_ffn_g, w_gate, w_up, w_down, post_ffn_g):
    bsz, t, d = x.shape
    assert bsz == 1 and d == D_MODEL and WINDOW + ATT_TILE <= t <= N_BLK_LANES * SLC_BLOCK
    x2 = x.reshape(t, d)
    n_cmp_pad = t // CMP_STRIDE
    assert n_cmp_pad % LANES == 0

    lb = jnp.cumsum(jax.nn.softmax(lb_table.astype(F32), axis=0), axis=0)[0].reshape(1, D_A)
    w = jnp.swapaxes(w_in[0], 0, 1).astype(BF16)
    o_q, o_kc, o_ks, o_g, o_m = 4 * D_A, 4 * D_A + D_B, 4 * D_A + D_B + 2 * KV_DIM, 4 * D_A + D_B + 6 * KV_DIM, 4 * D_A + D_B + 6 * KV_DIM + 3 * B_HEADS
    w_kv = (w[o_ks:o_g].reshape(2, 2, B_KV_HEADS, HEAD_DIM, d).transpose(0, 2, 1, 3, 4)
            .reshape(4 * KV_DIM, d))
    w_g = jnp.pad(w[o_g:o_m], ((0, GATE_LANES - 3 * B_HEADS), (0, 0)))
    w_m = w[o_m:]

    pos = jnp.arange(t)
    tab_q = _rope_tables(pos, HEAD_DIM, HEAD_DIM ** -0.5 * LOG2_E)
    tab_kv = _rope_tables(pos, LANES)
    end_pos = jnp.arange(n_cmp_pad) * CMP_STRIDE + (CMP_BLOCK - 1)
    ident = jnp.stack([jnp.ones((n_cmp_pad, LANES), F32), jnp.zeros((n_cmp_pad, LANES), F32),
                       jnp.zeros((n_cmp_pad, LANES), F32)])
    tab_cmp = jnp.concatenate([_rope_tables(end_pos, LANES), ident], axis=0)

    xn = _rmsnorm_call(x2, pre_mix_g[0].reshape(1, d))

    tm = 1024 if t % 1024 == 0 else 256
    proj4 = _proj_call(
        "proj_hgrn", xn, w, w_rows=(0, o_q), tm=tm, tn=D_A, epilogue=_ep_plain,
        out_shape=jax.ShapeDtypeStruct((4, t, D_A), F32),
        out_specs=pl.BlockSpec((1, tm, D_A), lambda i, j: (j, i, 0)))
    q_hm = _proj_call(
        "proj_q", xn, w, w_rows=(o_q, D_B), tm=tm, tn=512, epilogue=_ep_q,
        out_shape=jax.ShapeDtypeStruct((B_HEADS // 2, t, LANES), F32),
        out_specs=pl.BlockSpec((512 // LANES, tm, LANES), lambda i, j: (j, i, 0)),
        extra=(tab_q,), extra_specs=(pl.BlockSpec((3, tm, LANES), lambda i, j: (0, i, 0)),))
    cmp_in = _proj_call(
        "proj_cmp", xn, w, w_rows=(o_kc, 2 * KV_DIM), tm=tm, tn=2 * KV_DIM, epilogue=_ep_cmp,
        out_shape=jax.ShapeDtypeStruct((2, B_KV_HEADS, t, HEAD_DIM), F32),
        out_specs=pl.BlockSpec((2, B_KV_HEADS, tm, HEAD_DIM), lambda i, j: (0, 0, i, 0)))
    kv_aug, v_aug = _proj_call(
        "proj_kv", xn, w_kv, tm=tm, tn=2 * KV_DIM, epilogue=functools.partial(_ep_kv, tm=tm),
        out_shape=[jax.ShapeDtypeStruct((2, B_KV_HEADS, t, 2 * LANES), BF16),
                   jax.ShapeDtypeStruct((2, B_KV_HEADS, t, LANES), BF16)],
        out_specs=[pl.BlockSpec((1, B_KV_HEADS, tm, 2 * LANES), lambda i, j: (j, 0, i, 0)),
                   pl.BlockSpec((1, B_KV_HEADS, tm, LANES), lambda i, j: (j, 0, i, 0))],
        extra=(tab_kv,), extra_specs=(pl.BlockSpec((3, tm, LANES), lambda i, j: (0, i, 0)),))
    gates = _proj_call(
        "proj_gate", xn, w_g, tm=tm, tn=GATE_LANES, epilogue=_ep_gate,
        out_shape=jax.ShapeDtypeStruct((t, GATE_LANES), F32),
        out_specs=pl.BlockSpec((tm, GATE_LANES), lambda i, j: (i, 0)))
    sg, wa16, wb16, wo16 = _proj_call(
        "proj_merge", xn, w_m, tm=tm, tn=1024, epilogue=_ep_sigmoid,
        out_shape=jax.ShapeDtypeStruct((2, t, d), BF16),
        out_specs=pl.BlockSpec((1, tm, 1024), lambda i, j: (j // 2, i, j % 2)),
        to_bf16=(w_proj_a[0], w_proj_b[0], w_out[0]))

    oa = _hgrn_call(proj4, lb, a_norm_g[0].reshape(1, D_A))

    half = CMP_BLOCK // 2
    x16 = cmp_in.reshape(2, B_KV_HEADS, n_cmp_pad, CMP_STRIDE * HEAD_DIM)
    pe2 = jnp.stack([cmp_pe_k[0], cmp_pe_v[0]]).reshape(2, 2, half * HEAD_DIM)
    w1 = jnp.stack([cmp_w1_k[0], cmp_w1_v[0]]).astype(BF16)
    b1 = jnp.stack([cmp_b1_k[0], cmp_b1_v[0]]).reshape(2, 1, CMP_HIDDEN)
    w2p = jnp.pad(jnp.stack([cmp_w2_k[0], cmp_w2_v[0]]), ((0, 0), (0, 0), (0, LANES - HEAD_DIM))).astype(BF16)
    b2p = jnp.pad(jnp.stack([cmp_b2_k[0], cmp_b2_v[0]]), ((0, 0), (0, LANES - HEAD_DIM))).reshape(2, 1, LANES)
    cmp_aug = _compress_call(x16, pe2, w1, b1, w2p, b2p, tab_cmp)
    ocmp, bias = _nsa_select_call(q_hm, cmp_aug, _slc_weight_matrix(n_cmp_pad))
    ob = _nsa_attend_call(q_hm, bias, ocmp, gates, kv_aug, v_aug)

    h1, wg16, wu16, wd16 = _post_call(oa, ob, sg, x2, wa16, wb16, wo16, post_mix_g[0].reshape(1, d),
                                      w_gate[0], w_up[0], w_down[0])
    out = _ffn_call(h1, pre_ffn_g[0].reshape(1, d), wg16, wu16, wd16, post_ffn_g[0].reshape(1, d))
    return out.reshape(bsz, t, d)
```

```python
import functools

import numpy as np
import jax
import jax.numpy as jnp
from jax import lax
from jax.experimental import pallas as pl
from jax.experimental.pallas import tpu as pltpu

F32 = jnp.float32
BF16 = jnp.bfloat16

D_MODEL = 2048
D_A = 1024
D_B = 1024
A_HEADS = 8
A_DK = 128
HGRN_PAIRWISE = 4
B_HEADS = 16
B_KV_HEADS = 4
B_GROUP = 4
HEAD_DIM = 64
KV_DIM = 256
CMP_BLOCK = 32
CMP_STRIDE = 16
CMP_HIDDEN = 256
SLC_BLOCK = 64
SLC_TOP_N = 16
SLC_FORCED = 3
SLC_WEIGHTS = (1.0, 2.0, 2.0, 2.0, 1.0)
WINDOW = 512
Q_TILE = 512
ATT_TILE = 256
KV_PER_ATTEND_STEP = 2
KV_PER_SELECT_STEP = 2
SELECT_VARIANTS = 4
PROJ_SUBTILE = 256
GATE_LANES = 256
ROPE_THETA = 500000.0
ROPE_DIM = 16
D_FF = 5632
EPS = 1e-6
LOG2_E = 1.4426950408889634
NEG_INF = -1e30
N_BLK_LANES = 128
LANES = 128
VMEM_LIMIT = 56 * 1024 * 1024

NT_DIMS = (((1,), (1,)), ((), ()))
TN_DIMS = (((0,), (0,)), ((), ()))


def _cparams(n_axes, vmem=None):
    return pltpu.CompilerParams(dimension_semantics=("arbitrary",) * n_axes,
                                vmem_limit_bytes=vmem)


def _rmsnorm_kernel(x_ref, g_ref, o_ref):
    x = x_ref[...]
    y = x * lax.rsqrt(jnp.mean(x * x, axis=-1, keepdims=True) + EPS)
    o_ref[...] = (y * g_ref[...]).astype(o_ref.dtype)


def _rmsnorm_call(x2, g, tm=512):
    t, d = x2.shape
    return pl.pallas_call(
        _rmsnorm_kernel, grid=(t // tm,),
        in_specs=[pl.BlockSpec((tm, d), lambda i: (i, 0)), pl.BlockSpec((1, d), lambda i: (0, 0))],
        out_specs=pl.BlockSpec((tm, d), lambda i: (i, 0)),
        out_shape=jax.ShapeDtypeStruct((t, d), BF16), name="pre_mix_rmsnorm",
        compiler_params=_cparams(1))(x2, g)


def _proj_call(name, xn, w_t, *, tm, tn, epilogue, out_shape, out_specs, extra=(), extra_specs=(),
               w_rows=None, to_bf16=()):
    t, k = xn.shape
    row0, n = (0, w_t.shape[0]) if w_rows is None else w_rows
    assert tn % PROJ_SUBTILE == 0 and n % tn == 0 and row0 % tn == 0
    grid = (t // tm, n // tn)
    steps = grid[0] * grid[1]
    n_extra, n_cast = len(extra), len(to_bf16)
    out_specs = list(out_specs) if isinstance(out_specs, (list, tuple)) else [out_specs]
    out_shape = list(out_shape) if isinstance(out_shape, (list, tuple)) else [out_shape]
    n_out = len(out_specs)

    def body(x_ref, w_ref, *rest):
        extra_refs = rest[:n_extra]
        cast_in = rest[n_extra:n_extra + n_cast]
        outs = rest[n_extra + n_cast:n_extra + n_cast + n_out]
        cast_out = rest[n_extra + n_cast + n_out:]
        x = x_ref[...]
        for c0 in range(0, tn, PROJ_SUBTILE):
            acc = lax.dot_general(x, w_ref[c0:c0 + PROJ_SUBTILE, :], NT_DIMS, preferred_element_type=F32)
            epilogue(acc, c0, extra_refs, outs)
        for src, dst in zip(cast_in, cast_out):
            dst[...] = src[...].astype(BF16)

    def slab(a):
        rows = a.shape[0] // steps
        assert a.shape[0] % steps == 0 and rows % 16 == 0
        return pl.BlockSpec((rows, a.shape[1]), lambda i, j: (i * grid[1] + j, 0))

    res = pl.pallas_call(
        body, grid=grid,
        in_specs=[pl.BlockSpec((tm, k), lambda i, j: (i, 0)),
                  pl.BlockSpec((tn, k), lambda i, j: (row0 // tn + j, 0)), *extra_specs,
                  *[slab(a) for a in to_bf16]],
        out_specs=out_specs + [slab(a) for a in to_bf16],
        out_shape=out_shape + [jax.ShapeDtypeStruct(a.shape, BF16) for a in to_bf16], name=name,
        compiler_params=_cparams(2, VMEM_LIMIT))(xn, w_t, *extra, *to_bf16)
    return res[0] if len(res) == 1 else res


def _rope_tile(a, tab_ref):
    return (a * tab_ref[0] + pltpu.roll(a, LANES - ROPE_DIM // 2, 1) * tab_ref[1]
            + pltpu.roll(a, ROPE_DIM // 2, 1) * tab_ref[2])


def _ep_plain(acc, c0, extra, outs):
    outs[0][0, :, c0:c0 + acc.shape[1]] = acc.astype(outs[0].dtype)


def _ep_q(acc, c0, extra, outs):
    for pair in range(acc.shape[1] // LANES):
        outs[0][c0 // LANES + pair] = _rope_tile(acc[:, pair * LANES:(pair + 1) * LANES], extra[0])


def _ep_cmp(acc, c0, extra, outs):
    for hh in range(acc.shape[1] // HEAD_DIM):
        head = c0 // HEAD_DIM + hh
        outs[0][head // B_KV_HEADS, head % B_KV_HEADS] = acc[:, hh * HEAD_DIM:(hh + 1) * HEAD_DIM]


def _ep_kv(acc, c0, extra, outs, *, tm):
    i = pl.program_id(0)
    j = pl.program_id(1)
    rowg = i * tm + lax.broadcasted_iota(jnp.int32, (tm, LANES), 0)
    lane = lax.broadcasted_iota(jnp.int32, (tm, LANES), 1)
    aux = jnp.where(j == 0, (rowg // SLC_BLOCK == lane).astype(F32), 0.0).astype(BF16)
    for hh in range(acc.shape[1] // LANES):
        h = c0 // LANES + hh
        r = _rope_tile(acc[:, hh * LANES:(hh + 1) * LANES], extra[0])
        outs[0][0, h, :, 0:LANES] = r.astype(BF16)
        outs[0][0, h, :, LANES:2 * LANES] = aux
        outs[1][0, h] = jnp.where(lane < HEAD_DIM, pltpu.roll(r, HEAD_DIM, 1), 1.0).astype(BF16)


def _ep_gate(acc, c0, extra, outs):
    outs[0][:, c0:c0 + acc.shape[1]] = jax.nn.sigmoid(acc)


def _ep_sigmoid(acc, c0, extra, outs):
    outs[0][0, :, c0:c0 + acc.shape[1]] = jax.nn.sigmoid(acc).astype(outs[0].dtype)


def _split3(x):
    hi = x.astype(BF16)
    r1 = x - hi.astype(F32)
    mid = r1.astype(BF16)
    lo = (r1 - mid.astype(F32)).astype(BF16)
    return hi, mid, lo


def _hgrn_kernel(q_ref, f_ref, i_ref, g_ref, lb_ref, gn_ref, o_ref, st_ref, *, chunk, heads):
    c = pl.program_id(1)

    @pl.when(c == 0)
    def _():
        st_ref[...] = jnp.zeros_like(st_ref)

    row = lax.broadcasted_iota(jnp.int32, (chunk, chunk), 0)
    col = lax.broadcasted_iota(jnp.int32, (chunk, chunk), 1)
    tri = (col <= row).astype(BF16)
    rowv = lax.broadcasted_iota(jnp.int32, (chunk, A_DK), 0)

    for hb in range(heads):
        sl = slice(hb * A_DK, (hb + 1) * A_DK)
        q = q_ref[0, :, sl]
        ii = i_ref[0, :, sl]
        gg = g_ref[0, :, sl]
        lbv = lb_ref[:, sl]
        f = lbv + (1.0 - lbv) * jax.nn.sigmoid(f_ref[0, :, sl])
        lf = jnp.log(f)
        k = 1.0 - f
        b = sum(jnp.dot(tri, p, preferred_element_type=F32) for p in _split3(lf)) * LOG2_E
        b_end = b[chunk - 1:chunk, :]
        ii16 = ii.astype(BF16)

        st = st_ref[hb]
        o = lax.dot_general((q * jnp.exp2(b)).astype(BF16), st.astype(BF16), NT_DIMS,
                            preferred_element_type=F32)
        kd = (k * jnp.exp2(b_end - b)).astype(BF16)
        st_ref[hb] = st * jnp.exp2(b_end) + lax.dot_general(ii16, kd, TN_DIMS,
                                                            preferred_element_type=F32)

        att = jnp.where(col == row, jnp.sum(q * k, axis=-1, keepdims=True), 0.0)
        for d in range(1, HGRN_PAIRWISE):
            valid = (rowv % HGRN_PAIRWISE) >= d
            w = jnp.exp2(b - pltpu.roll(b, d, 0))
            p = jnp.where(valid, q * pltpu.roll(k, d, 0) * w, 0.0)
            att = att + jnp.where(col == row - d, jnp.sum(p, axis=-1, keepdims=True), 0.0)
        m = HGRN_PAIRWISE
        while m < chunk:
            grp = chunk // (2 * m)
            b3 = b.reshape(grp, 2 * m, A_DK)
            refrow = jnp.broadcast_to(b3[:, m - 1:m, :], (grp, 2 * m, A_DK)).reshape(chunk, A_DK)
            second = (rowv % (2 * m)) >= m
            ql = jnp.where(second, q * jnp.exp2(b - refrow), 0.0)
            kl = jnp.where(second, 0.0, k * jnp.exp2(refrow - b))
            a = lax.dot_general(ql.astype(BF16), kl.astype(BF16), NT_DIMS, preferred_element_type=F32)
            if grp > 1:
                a = jnp.where(row // (2 * m) == col // (2 * m), a, 0.0)
            att = att + a
            m *= 2
        o = o + jnp.dot(att.astype(BF16), ii16, preferred_element_type=F32)

        o = o * lax.rsqrt(jnp.mean(o * o, axis=-1, keepdims=True) + EPS)
        o_ref[:, sl] = ((o * gn_ref[:, sl]) * (gg * jax.nn.sigmoid(gg))).astype(o_ref.dtype)


def _hgrn_call(proj4, lb, gn, *, chunk=128, heads=8):
    _, t, _ = proj4.shape
    w = heads * A_DK

    def spec(kind):
        return pl.BlockSpec((1, chunk, w), lambda h, c, kind=kind: (kind, c, h))

    vec = pl.BlockSpec((1, w), lambda h, c: (0, h))
    return pl.pallas_call(
        functools.partial(_hgrn_kernel, chunk=chunk, heads=heads),
        grid=(A_HEADS // heads, t // chunk),
        in_specs=[spec(0), spec(1), spec(2), spec(3), vec, vec],
        out_specs=pl.BlockSpec((chunk, w), lambda h, c: (c, h)),
        out_shape=jax.ShapeDtypeStruct((t, D_A), BF16),
        scratch_shapes=[pltpu.VMEM((heads, A_DK, A_DK), F32)], name="hgrn2_scan",
        compiler_params=_cparams(2))(proj4, proj4, proj4, proj4, lb, gn)


def _compress_kernel(x_ref, pe_ref, w1_ref, b1_ref, w2_ref, b2_ref, tab_ref, o_ref):
    half = CMP_BLOCK * HEAD_DIM // 2
    x = x_ref[0, 0]
    top = jnp.dot((x + pe_ref[0, 0:1, :]).astype(BF16), w1_ref[0, :half, :], preferred_element_type=F32)
    bot = jnp.dot((x + pe_ref[0, 1:2, :]).astype(BF16), w1_ref[0, half:, :], preferred_element_type=F32)
    n = x.shape[0]
    h = jax.nn.gelu(top + pltpu.roll(bot, n - 1, 0) + b1_ref[0])
    y = jnp.dot(h.astype(BF16), w2_ref[0], preferred_element_type=F32) + b2_ref[0]
    y = _rope_tile(y, tab_ref)
    hi = y.astype(BF16)
    lo = (y - hi.astype(F32)).astype(BF16)
    o_ref[0, 0, :, 0:LANES] = (y + pltpu.roll(y, HEAD_DIM, 1)).astype(BF16)
    o_ref[0, 0, :, LANES:2 * LANES] = lo


def _compress_call(x16, pe2, w1, b1, w2p, b2p, tab):
    _, nh, n, wid = x16.shape
    hid = w1.shape[-1]
    return pl.pallas_call(
        _compress_kernel, grid=(2, nh),
        in_specs=[pl.BlockSpec((1, 1, n, wid), lambda kv, h: (kv, h, 0, 0)),
                  pl.BlockSpec((1, 2, wid), lambda kv, h: (kv, 0, 0)),
                  pl.BlockSpec((1, 2 * wid, hid), lambda kv, h: (kv, 0, 0)),
                  pl.BlockSpec((1, 1, hid), lambda kv, h: (kv, 0, 0)),
                  pl.BlockSpec((1, hid, LANES), lambda kv, h: (kv, 0, 0)),
                  pl.BlockSpec((1, 1, LANES), lambda kv, h: (kv, 0, 0)),
                  pl.BlockSpec((3, n, LANES), lambda kv, h: (kv, 0, 0))],
        out_specs=pl.BlockSpec((1, 1, n, 2 * LANES), lambda kv, h: (kv, h, 0, 0)),
        out_shape=jax.ShapeDtypeStruct((2, nh, n, 2 * LANES), BF16),
        name="compress_mlp", compiler_params=_cparams(2, VMEM_LIMIT))(x16, pe2, w1, b1, w2p, b2p, tab)


def _nsa_select_kernel(q_ref, kc_ref, vc_ref, wt_ref, ocmp_ref, bias_ref, lhs_ref):
    qt = pl.program_id(1)
    q0 = qt * Q_TILE
    rows = B_GROUP * Q_TILE
    n_cmp = kc_ref.shape[2]

    def chains(n_cols, n_blk):
        t_col = q0 + lax.broadcasted_iota(jnp.int32, (rows, 1), 0) % Q_TILE
        n_idx = lax.broadcasted_iota(jnp.int32, (rows, n_cols), 1)
        vis = n_idx <= (t_col - (CMP_BLOCK - 1)) // CMP_STRIDE
        blk = lax.broadcasted_iota(jnp.int32, (n_blk, Q_TILE), 0)
        tok = q0 + lax.broadcasted_iota(jnp.int32, (n_blk, Q_TILE), 1)
        cur = tok // SLC_BLOCK
        forced = (blk == 0) | (blk == cur) | (blk == cur - 1)
        candidate = (blk * SLC_BLOCK <= tok) & jnp.logical_not(forced)
        wt = wt_ref[0:n_blk, 0:n_cols]
        zeros64 = jnp.zeros((rows, HEAD_DIM), BF16)
        never = jnp.zeros((N_BLK_LANES - n_blk, Q_TILE), F32)

        for c in range(KV_PER_SELECT_STEP):
            for pair in range(B_GROUP // 2):
                qp = q_ref[c * (B_GROUP // 2) + pair]
                hi = qp.astype(BF16)
                lo = (qp - hi.astype(F32)).astype(BF16)
                for half in range(2):
                    r = slice((2 * pair + half) * Q_TILE, (2 * pair + half + 1) * Q_TILE)
                    ln = slice(half * HEAD_DIM, (half + 1) * HEAD_DIM)
                    lhs_ref[c, r, 0:HEAD_DIM] = hi[:, ln]
                    lhs_ref[c, r, HEAD_DIM:2 * HEAD_DIM] = lo[:, ln]
                    lhs_ref[c, r, 2 * HEAD_DIM:3 * HEAD_DIM] = hi[:, ln]
            lhs_ref[c, :, 3 * HEAD_DIM:] = zeros64
            s = lax.dot_general(lhs_ref[c], kc_ref[0, c, 0:n_cols, :], NT_DIMS, preferred_element_type=F32)
            s = jnp.where(vis, s, NEG_INF)
            e = jnp.exp2(s - jnp.max(s, axis=-1, keepdims=True))
            l = jnp.sum(e, axis=-1, keepdims=True)
            p = e * jnp.where(t_col >= CMP_BLOCK - 1, 1.0 / l, 0.0)
            o_cmp = jnp.dot(p.astype(BF16), vc_ref[0, c, 0:n_cols, 0:LANES], preferred_element_type=F32)
            for g in range(B_GROUP):
                c0 = (c * B_GROUP + g) * HEAD_DIM
                ocmp_ref[:, c0:c0 + HEAD_DIM] = o_cmp[g * Q_TILE:(g + 1) * Q_TILE, :HEAD_DIM]

            psum = p[0:Q_TILE] + p[Q_TILE:2 * Q_TILE] + p[2 * Q_TILE:3 * Q_TILE] + p[3 * Q_TILE:]
            ps_hi = psum.astype(BF16)
            ps_lo = (psum - ps_hi.astype(F32)).astype(BF16)
            pslc = (lax.dot_general(wt, ps_hi, NT_DIMS, preferred_element_type=F32)
                    + lax.dot_general(wt, ps_lo, NT_DIMS, preferred_element_type=F32))
            score = jnp.where(candidate, pslc, jnp.where(forced, -jnp.inf, NEG_INF))
            sel_t = forced.astype(F32)
            for _ in range(SLC_TOP_N - SLC_FORCED):
                best = jnp.max(score, axis=0, keepdims=True)
                first = jnp.min(jnp.where(score == best, blk, N_BLK_LANES), axis=0, keepdims=True)
                hit = blk == first
                score = jnp.where(hit, -jnp.inf, score)
                sel_t = jnp.where(hit, 1.0, sel_t)
            if n_blk < N_BLK_LANES:
                sel_t = jnp.concatenate([sel_t, never], axis=0)
            bias_ref[c] = jnp.where(sel_t.T > 0.0, 0.0, NEG_INF).astype(BF16)

    n_blk_all = n_cmp // (SLC_BLOCK // CMP_STRIDE)
    tiles_per_variant = pl.num_programs(1) // SELECT_VARIANTS
    for v in range(1, SELECT_VARIANTS + 1):
        pl.when(qt // tiles_per_variant == v - 1)(functools.partial(
            chains, n_cmp * v // SELECT_VARIANTS, n_blk_all * v // SELECT_VARIANTS))


def _nsa_select_call(q_hm, cmp_aug, wt):
    _, t, _ = q_hm.shape
    n_cmp = cmp_aug.shape[2]
    kvs = KV_PER_SELECT_STEP
    assert (t // Q_TILE) % SELECT_VARIANTS == 0 and (t // SLC_BLOCK) % (8 * SELECT_VARIANTS) == 0
    return pl.pallas_call(
        _nsa_select_kernel, grid=(B_KV_HEADS // kvs, t // Q_TILE),
        in_specs=[pl.BlockSpec((kvs * B_GROUP // 2, Q_TILE, LANES), lambda p, i: (p, i, 0)),
                  pl.BlockSpec((1, kvs, n_cmp, 2 * LANES), lambda p, i: (0, p, 0, 0)),
                  pl.BlockSpec((1, kvs, n_cmp, 2 * LANES), lambda p, i: (1, p, 0, 0)),
                  pl.BlockSpec((N_BLK_LANES, n_cmp), lambda p, i: (0, 0))],
        out_specs=[pl.BlockSpec((Q_TILE, kvs * B_GROUP * HEAD_DIM), lambda p, i: (i, p)),
                   pl.BlockSpec((kvs, Q_TILE, N_BLK_LANES), lambda p, i: (p, i, 0))],
        out_shape=[jax.ShapeDtypeStruct((t, D_B), F32),
                   jax.ShapeDtypeStruct((B_KV_HEADS, t, N_BLK_LANES), BF16)],
        scratch_shapes=[pltpu.VMEM((kvs, B_GROUP * Q_TILE, 2 * LANES), BF16)], name="nsa_select",
        compiler_params=_cparams(2, VMEM_LIMIT))(q_hm, cmp_aug, cmp_aug, wt)


def _nsa_attend_kernel(q_ref, bias_ref, ocmp_ref, gate_ref, gx_ref, ksv_ref, kwv_ref, vs_ref, vw_ref, o_ref,
                       lhs_w_ref, lhs_s_ref, m_ref, acc_ref, *, key_tile, tiles_per_group):
    q0 = pl.program_id(1) * ATT_TILE
    rows = B_GROUP * ATT_TILE
    chains = range(KV_PER_ATTEND_STEP)
    t_col = q0 + lax.broadcasted_iota(jnp.int32, (rows, 1), 0) % ATT_TILE

    for c in chains:
        for g in range(B_GROUP):
            r = slice(g * ATT_TILE, (g + 1) * ATT_TILE)
            qp = q_ref[c * (B_GROUP // 2) + g // 2]
            qhi = qp[:, (g % 2) * HEAD_DIM:(g % 2 + 1) * HEAD_DIM].astype(BF16)
            lhs_s_ref[c, r, 0:HEAD_DIM] = qhi
            lhs_s_ref[c, r, LANES:] = bias_ref[c]
            lhs_w_ref[c, r, 0:HEAD_DIM] = qhi
        lhs_s_ref[c, :, HEAD_DIM:LANES] = jnp.zeros((rows, LANES - HEAD_DIM), BF16)
        lhs_w_ref[c, :, HEAD_DIM:] = jnp.zeros((rows, 2 * LANES - HEAD_DIM), BF16)

    for c in chains:
        m_ref[c] = jnp.full((rows, LANES), NEG_INF, F32)
        acc_ref[c] = jnp.zeros((rows, LANES), F32)

    def tile(ti, causal):
        start = pl.multiple_of(ti * key_tile, key_tile)
        for c in chains:
            st = lax.dot_general(lhs_s_ref[c], ksv_ref[0, c, pl.ds(start, key_tile), :], NT_DIMS,
                                 preferred_element_type=F32)
            if causal:
                kp = start + lax.broadcasted_iota(jnp.int32, (rows, key_tile), 1)
                st = jnp.where(kp <= t_col, st, NEG_INF)
            m_run = m_ref[c]
            m_new = jnp.maximum(m_run, jnp.max(st, axis=-1, keepdims=True))
            pt = jnp.exp2(st - jnp.concatenate([m_new] * (key_tile // LANES), axis=1))
            pv = jnp.dot(pt.astype(BF16), vs_ref[0, c, pl.ds(start, key_tile), :], preferred_element_type=F32)
            acc_ref[c] = acc_ref[c] * jnp.exp2(m_run - m_new) + pv
            m_ref[c] = m_new

    def group(gi, _):
        for u in range(tiles_per_group):
            tile(gi * tiles_per_group + u, False)
        return 0

    def single(ti, _):
        tile(ti, False)
        return 0

    n_full = q0 // key_tile
    n_groups = n_full // tiles_per_group
    lax.fori_loop(0, n_groups, group, 0)
    lax.fori_loop(n_groups * tiles_per_group, n_full, single, 0)
    tile(n_full, True)

    slab = WINDOW + ATT_TILE
    ws = pl.multiple_of(jnp.maximum(q0 - WINDOW, 0), ATT_TILE)
    kpos = ws + lax.broadcasted_iota(jnp.int32, (rows, slab), 1)
    in_window = lax.bitcast_convert_type(t_col - kpos, jnp.uint32) < jnp.uint32(WINDOW)
    lane = lax.broadcasted_iota(jnp.int32, (ATT_TILE, LANES), 1)
    per_branch = B_GROUP * HEAD_DIM

    def head_pair(acc, k):
        a_even = acc[2 * k * ATT_TILE:(2 * k + 1) * ATT_TILE]
        a_odd = acc[(2 * k + 1) * ATT_TILE:(2 * k + 2) * ATT_TILE]
        num = jnp.where(lane < HEAD_DIM, a_even, pltpu.roll(a_odd, HEAD_DIM, 1))
        den = jnp.where(lane < HEAD_DIM, pltpu.roll(a_even, HEAD_DIM, 1), a_odd)
        return num * (1.0 / den)

    gates = gate_ref[...]
    g_hi = gates.astype(BF16)
    g_lo = (gates - g_hi.astype(F32)).astype(BF16)

    for c in chains:
        acc_s = acc_ref[c]
        s = lax.dot_general(lhs_w_ref[c], kwv_ref[0, c, pl.ds(ws, slab), :], NT_DIMS,
                            preferred_element_type=F32)
        s = jnp.where(in_window, s, NEG_INF)
        e = jnp.exp2(s - jnp.max(s, axis=-1, keepdims=True))
        acc_w = jnp.dot(e.astype(BF16), vw_ref[0, c, pl.ds(ws, slab), :], preferred_element_type=F32)

        gx = (jnp.dot(g_hi, gx_ref[c], preferred_element_type=F32)
              + jnp.dot(g_lo, gx_ref[c], preferred_element_type=F32))
        for k in range(B_GROUP // 2):
            c0 = c * B_GROUP * HEAD_DIM + k * LANES
            g0 = k * LANES
            og = (gx[:, g0:g0 + LANES] * ocmp_ref[:, c0:c0 + LANES]
                  + gx[:, per_branch + g0:per_branch + g0 + LANES] * head_pair(acc_s, k)
                  + gx[:, 2 * per_branch + g0:2 * per_branch + g0 + LANES] * head_pair(acc_w, k))
            o_ref[:, c0:c0 + LANES] = og.astype(o_ref.dtype)


def _nsa_attend_call(q_hm, bias, ocmp, gates, kv_aug, v_aug, *, key_tile=512, tiles_per_group=2):
    _, t, _ = q_hm.shape
    assert t % key_tile == 0 and key_tile % ATT_TILE == 0 and t % ATT_TILE == 0
    rows = B_GROUP * ATT_TILE
    kvs = KV_PER_ATTEND_STEP
    wide = kvs * B_GROUP * HEAD_DIM
    resident = dict(pipeline_mode=pl.Buffered(1))
    return pl.pallas_call(
        functools.partial(_nsa_attend_kernel, key_tile=key_tile, tiles_per_group=tiles_per_group),
        grid=(B_KV_HEADS // kvs, t // ATT_TILE),
        in_specs=[pl.BlockSpec((kvs * B_GROUP // 2, ATT_TILE, LANES), lambda p, i: (p, i, 0)),
                  pl.BlockSpec((kvs, ATT_TILE, N_BLK_LANES), lambda p, i: (p, i, 0)),
                  pl.BlockSpec((ATT_TILE, wide), lambda p, i: (i, p)),
                  pl.BlockSpec((ATT_TILE, GATE_LANES), lambda p, i: (i, 0)),
                  pl.BlockSpec((kvs, GATE_LANES, 3 * B_GROUP * HEAD_DIM), lambda p, i: (p, 0, 0)),
                  pl.BlockSpec((1, kvs, t, 2 * LANES), lambda p, i: (0, p, 0, 0), **resident),
                  pl.BlockSpec((1, kvs, t, 2 * LANES), lambda p, i: (1, p, 0, 0), **resident),
                  pl.BlockSpec((1, kvs, t, LANES), lambda p, i: (0, p, 0, 0), **resident),
                  pl.BlockSpec((1, kvs, t, LANES), lambda p, i: (1, p, 0, 0), **resident)],
        out_specs=pl.BlockSpec((ATT_TILE, wide), lambda p, i: (i, p)),
        out_shape=jax.ShapeDtypeStruct((t, D_B), BF16),
        scratch_shapes=[pltpu.VMEM((kvs, rows, 2 * LANES), BF16), pltpu.VMEM((kvs, rows, 2 * LANES), BF16),
                        pltpu.VMEM((kvs, rows, LANES), F32), pltpu.VMEM((kvs, rows, LANES), F32)],
        name="nsa_attend",
        compiler_params=_cparams(2, VMEM_LIMIT))(q_hm, bias, ocmp, gates, _gate_expand_matrix(),
                                                 kv_aug, kv_aug, v_aug, v_aug)


def _post_kernel(oa_ref, ob_ref, sg_ref, x_ref, wa_ref, wb_ref, wo_ref, g_ref, wg_ref, wu_ref, wd_ref,
                 o_ref, wg16_ref, wu16_ref, wd16_ref):
    ya = jnp.dot(oa_ref[...], wa_ref[...], preferred_element_type=F32)
    yb = jnp.dot(ob_ref[...], wb_ref[...], preferred_element_type=F32)
    merged = sg_ref[0].astype(F32) * ya + sg_ref[1].astype(F32) * yb
    mix = jnp.dot(merged.astype(BF16), wo_ref[...], preferred_element_type=F32)
    y = mix * lax.rsqrt(jnp.mean(mix * mix, axis=-1, keepdims=True) + EPS)
    o_ref[...] = x_ref[...] + y * g_ref[...]
    wg16_ref[...] = wg_ref[...].astype(BF16)
    wu16_ref[...] = wu_ref[...].astype(BF16)
    wd16_ref[...] = wd_ref[...].astype(BF16)


def _post_call(oa, ob, sg, x2, wa, wb, wo, g, wg, wu, wd, tm=256):
    t, d = x2.shape
    steps = t // tm
    dff = wg.shape[1]
    assert d % (steps * 16) == 0 and dff % (steps * 16) == 0
    const = dict(pipeline_mode=pl.Buffered(1))
    return pl.pallas_call(
        _post_kernel, grid=(steps,),
        in_specs=[pl.BlockSpec((tm, D_A), lambda i: (i, 0)),
                  pl.BlockSpec((tm, D_B), lambda i: (i, 0)),
                  pl.BlockSpec((2, tm, d), lambda i: (0, i, 0)),
                  pl.BlockSpec((tm, d), lambda i: (i, 0)),
                  pl.BlockSpec((D_A, d), lambda i: (0, 0), **const),
                  pl.BlockSpec((D_B, d), lambda i: (0, 0), **const),
                  pl.BlockSpec((d, d), lambda i: (0, 0), **const),
                  pl.BlockSpec((1, d), lambda i: (0, 0)),
                  pl.BlockSpec((d // steps, dff), lambda i: (i, 0)),
                  pl.BlockSpec((d // steps, dff), lambda i: (i, 0)),
                  pl.BlockSpec((dff // steps, d), lambda i: (i, 0))],
        out_specs=[pl.BlockSpec((tm, d), lambda i: (i, 0)),
                   pl.BlockSpec((d // steps, dff), lambda i: (i, 0)),
                   pl.BlockSpec((d // steps, dff), lambda i: (i, 0)),
                   pl.BlockSpec((dff // steps, d), lambda i: (i, 0))],
        out_shape=[jax.ShapeDtypeStruct((t, d), F32), jax.ShapeDtypeStruct((d, dff), BF16),
                   jax.ShapeDtypeStruct((d, dff), BF16), jax.ShapeDtypeStruct((dff, d), BF16)],
        name="mix_out_residual",
        compiler_params=_cparams(1, VMEM_LIMIT))(oa, ob, sg, x2, wa, wb, wo, g, wg, wu, wd)


def _ffn_kernel(h_ref, g1_ref, wg_ref, wu_ref, wd_ref, g2_ref, o_ref, hn_ref, acc_ref):
    f = pl.program_id(1)

    @pl.when(f == 0)
    def _():
        h = h_ref[...]
        y = h * lax.rsqrt(jnp.mean(h * h, axis=-1, keepdims=True) + EPS)
        hn_ref[...] = (y * g1_ref[...]).astype(BF16)
        acc_ref[...] = jnp.zeros_like(acc_ref)

    hn = hn_ref[...]
    a = jnp.dot(hn, wg_ref[...], preferred_element_type=F32)
    u = jnp.dot(hn, wu_ref[...], preferred_element_type=F32)
    z = (a * jax.nn.sigmoid(a) * u).astype(BF16)
    acc_ref[...] += jnp.dot(z, wd_ref[...], preferred_element_type=F32)

    @pl.when(f == pl.num_programs(1) - 1)
    def _():
        ff = acc_ref[...]
        y = ff * lax.rsqrt(jnp.mean(ff * ff, axis=-1, keepdims=True) + EPS)
        o_ref[...] = h_ref[...] + y * g2_ref[...]


def _ffn_call(h1, g1, wg, wu, wd, g2, tm=512, tf=512):
    t, d = h1.shape
    dff = wg.shape[1]
    return pl.pallas_call(
        _ffn_kernel, grid=(t // tm, dff // tf),
        in_specs=[pl.BlockSpec((tm, d), lambda i, f: (i, 0)),
                  pl.BlockSpec((1, d), lambda i, f: (0, 0)),
                  pl.BlockSpec((d, tf), lambda i, f: (0, f)),
                  pl.BlockSpec((d, tf), lambda i, f: (0, f)),
                  pl.BlockSpec((tf, d), lambda i, f: (f, 0)),
                  pl.BlockSpec((1, d), lambda i, f: (0, 0))],
        out_specs=pl.BlockSpec((tm, d), lambda i, f: (i, 0)),
        out_shape=jax.ShapeDtypeStruct((t, d), F32),
        scratch_shapes=[pltpu.VMEM((tm, d), BF16), pltpu.VMEM((tm, d), F32)], name="swiglu_ffn",
        compiler_params=_cparams(2, VMEM_LIMIT))(h1, g1, wg, wu, wd, g2)


def _rope_tables(pos, period, scale=1.0):
    half = ROPE_DIM // 2
    inv = jnp.float32(ROPE_THETA) ** (-jnp.arange(half, dtype=F32) * 2.0 / ROPE_DIM)
    r = np.arange(LANES) % period
    pick = np.zeros((half, LANES), np.float32)
    pick[r % half, np.arange(LANES)] = 1.0
    inv_lane = jnp.sum(inv[:, None] * pick, axis=0)
    ang = pos.astype(F32)[:, None] * inv_lane[None, :]
    cos, sin = jnp.cos(ang), jnp.sin(ang)
    lo = jnp.asarray((r < half)[None, :])
    hi = jnp.asarray(((r >= half) & (r < ROPE_DIM))[None, :])
    c = jnp.where(lo | hi, cos, 1.0)
    sa = jnp.where(lo, -sin, 0.0)
    sb = jnp.where(hi, sin, 0.0)
    return jnp.stack([c, sa, sb]) * scale


def _gate_expand_matrix():
    e = np.zeros((B_KV_HEADS, GATE_LANES, 3 * B_GROUP * HEAD_DIM), np.float32)
    for kvh in range(B_KV_HEADS):
        for g in range(B_GROUP):
            for b in range(3):
                c0 = (b * B_GROUP + g) * HEAD_DIM
                e[kvh, 3 * (kvh * B_GROUP + g) + b, c0:c0 + HEAD_DIM] = 1.0
    return jnp.asarray(e, BF16)


def _slc_weight_matrix(n_cmp_pad):
    ratio = SLC_BLOCK // CMP_STRIDE
    w = np.zeros((N_BLK_LANES, n_cmp_pad), np.float32)
    for j in range(N_BLK_LANES):
        for o, wv in enumerate(SLC_WEIGHTS):
            n = ratio * j + o - 1
            if 0 <= n < n_cmp_pad - 1:
                w[j, n] = wv
    return jnp.asarray(w, BF16)


def kernel(x, pre_mix_g, w_in, lb_table, a_norm_g, cmp_pe_k, cmp_w1_k, cmp_b1_k, cmp_w2_k, cmp_b2_k, cmp_pe_v, cmp_w1_v, cmp_b1_v, cmp_w2_v, cmp_b2_v, w_proj_a, w_proj_b, w_out, post_mix_g, pre_ffn_g, w_gate, w_up, w_down, post_ffn_g):
    bsz, t, d = x.shape
    assert bsz == 1 and d == D_MODEL and WINDOW + ATT_TILE <= t <= N_BLK_LANES * SLC_BLOCK
    x2 = x.reshape(t, d)
    n_cmp_pad = t // CMP_STRIDE
    assert n_cmp_pad % LANES == 0

    lb = jnp.cumsum(jax.nn.softmax(lb_table.astype(F32), axis=0), axis=0)[0].reshape(1, D_A)
    w = jnp.swapaxes(w_in[0], 0, 1).astype(BF16)
    o_q, o_kc, o_ks, o_g, o_m = 4 * D_A, 4 * D_A + D_B, 4 * D_A + D_B + 2 * KV_DIM, 4 * D_A + D_B + 6 * KV_DIM, 4 * D_A + D_B + 6 * KV_DIM + 3 * B_HEADS
    w_kv = (w[o_ks:o_g].reshape(2, 2, B_KV_HEADS, HEAD_DIM, d).transpose(0, 2, 1, 3, 4)
            .reshape(4 * KV_DIM, d))
    w_g = jnp.pad(w[o_g:o_m], ((0, GATE_LANES - 3 * B_HEADS), (0, 0)))
    w_m = w[o_m:]

    pos = jnp.arange(t)
    tab_q = _rope_tables(pos, HEAD_DIM, HEAD_DIM ** -0.5 * LOG2_E)
    tab_kv = _rope_tables(pos, LANES)
    end_pos = jnp.arange(n_cmp_pad) * CMP_STRIDE + (CMP_BLOCK - 1)
    ident = jnp.stack([jnp.ones((n_cmp_pad, LANES), F32), jnp.zeros((n_cmp_pad, LANES), F32),
                       jnp.zeros((n_cmp_pad, LANES), F32)])
    tab_cmp = jnp.concatenate([_rope_tables(end_pos, LANES), ident], axis=0)

    xn = _rmsnorm_call(x2, pre_mix_g[0].reshape(1, d))

    tm = 1024 if t % 1024 == 0 else 256
    proj4 = _proj_call(
        "proj_hgrn", xn, w, w_rows=(0, o_q), tm=tm, tn=D_A, epilogue=_ep_plain,
        out_shape=jax.ShapeDtypeStruct((4, t, D_A), F32),
        out_specs=pl.BlockSpec((1, tm, D_A), lambda i, j: (j, i, 0)))
    q_hm = _proj_call(
        "proj_q", xn, w, w_rows=(o_q, D_B), tm=tm, tn=512, epilogue=_ep_q,
        out_shape=jax.ShapeDtypeStruct((B_HEADS // 2, t, LANES), F32),
        out_specs=pl.BlockSpec((512 // LANES, tm, LANES), lambda i, j: (j, i, 0)),
        extra=(tab_q,), extra_specs=(pl.BlockSpec((3, tm, LANES), lambda i, j: (0, i, 0)),))
    cmp_in = _proj_call(
        "proj_cmp", xn, w, w_rows=(o_kc, 2 * KV_DIM), tm=tm, tn=2 * KV_DIM, epilogue=_ep_cmp,
        out_shape=jax.ShapeDtypeStruct((2, B_KV_HEADS, t, HEAD_DIM), F32),
        out_specs=pl.BlockSpec((2, B_KV_HEADS, tm, HEAD_DIM), lambda i, j: (0, 0, i, 0)))
    kv_aug, v_aug = _proj_call(
        "proj_kv", xn, w_kv, tm=tm, tn=2 * KV_DIM, epilogue=functools.partial(_ep_kv, tm=tm),
        out_shape=[jax.ShapeDtypeStruct((2, B_KV_HEADS, t, 2 * LANES), BF16),
                   jax.ShapeDtypeStruct((2, B_KV_HEADS, t, LANES), BF16)],
        out_specs=[pl.BlockSpec((1, B_KV_HEADS, tm, 2 * LANES), lambda i, j: (j, 0, i, 0)),
                   pl.BlockSpec((1, B_KV_HEADS, tm, LANES), lambda i, j: (j, 0, i, 0))],
        extra=(tab_kv,), extra_specs=(pl.BlockSpec((3, tm, LANES), lambda i, j: (0, i, 0)),))
    gates = _proj_call(
        "proj_gate", xn, w_g, tm=tm, tn=GATE_LANES, epilogue=_ep_gate,
        out_shape=jax.ShapeDtypeStruct((t, GATE_LANES), F32),
        out_specs=pl.BlockSpec((tm, GATE_LANES), lambda i, j: (i, 0)))
    sg, wa16, wb16, wo16 = _proj_call(
        "proj_merge", xn, w_m, tm=tm, tn=1024, epilogue=_ep_sigmoid,
        out_shape=jax.ShapeDtypeStruct((2, t, d), BF16),
        out_specs=pl.BlockSpec((1, tm, 1024), lambda i, j: (j // 2, i, j % 2)),
        to_bf16=(w_proj_a[0], w_proj_b[0], w_out[0]))

    oa = _hgrn_call(proj4, lb, a_norm_g[0].reshape(1, D_A))

    half = CMP_BLOCK // 2
    x16 = cmp_in.reshape(2, B_KV_HEADS, n_cmp_pad, CMP_STRIDE * HEAD_DIM)
    pe2 = jnp.stack([cmp_pe_k[0], cmp_pe_v[0]]).reshape(2, 2, half * HEAD_DIM)
    w1 = jnp.stack([cmp_w1_k[0], cmp_w1_v[0]]).astype(BF16)
    b1 = jnp.stack([cmp_b1_k[0], cmp_b1_v[0]]).reshape(2, 1, CMP_HIDDEN)
    w2p = jnp.pad(jnp.stack([cmp_w2_k[0], cmp_w2_v[0]]), ((0, 0), (0, 0), (0, LANES - HEAD_DIM))).astype(BF16)
    b2p = jnp.pad(jnp.stack([cmp_b2_k[0], cmp_b2_v[0]]), ((0, 0), (0, LANES - HEAD_DIM))).reshape(2, 1, LANES)
    cmp_aug = _compress_call(x16, pe2, w1, b1, w2p, b2p, tab_cmp)
    ocmp, bias = _nsa_select_call(q_hm, cmp_aug, _slc_weight_matrix(n_cmp_pad))
    ob = _nsa_attend_call(q_hm, bias, ocmp, gates, kv_aug, v_aug)

    h1, wg16, wu16, wd16 = _post_call(oa, ob, sg, x2, wa16, wb16, wo16, post_mix_g[0].reshape(1, d),
                                      w_gate[0], w_up[0], w_down[0])
    out = _ffn_call(h1, pre_ffn_g[0].reshape(1, d), wg16, wu16, wd16, post_ffn_g[0].reshape(1, d))
    return out.reshape(bsz, t, d)
```

```python
import functools

import numpy as np
import jax
import jax.numpy as jnp
from jax import lax
from jax.experimental import pallas as pl
from jax.experimental.pallas import tpu as pltpu

F32 = jnp.float32
BF16 = jnp.bfloat16

D_MODEL = 2048
D_A = 1024
D_B = 1024
A_HEADS = 8
A_DK = 128
HGRN_PAIRWISE = 4
B_HEADS = 16
B_KV_HEADS = 4
B_GROUP = 4
HEAD_DIM = 64
KV_DIM = 256
CMP_BLOCK = 32
CMP_STRIDE = 16
CMP_HIDDEN = 256
SLC_BLOCK = 64
SLC_TOP_N = 16
SLC_FORCED = 3
SLC_WEIGHTS = (1.0, 2.0, 2.0, 2.0, 1.0)
WINDOW = 512
Q_TILE = 512
ATT_TILE = 256
KV_PER_ATTEND_STEP = 2
KV_PER_SELECT_STEP = 2
SELECT_VARIANTS = 4
PROJ_SUBTILE = 256
GATE_LANES = 256
ROPE_THETA = 500000.0
ROPE_DIM = 16
D_FF = 5632
EPS = 1e-6
LOG2_E = 1.4426950408889634
Q_SCALE = HEAD_DIM ** -0.5 * LOG2_E
NEG_INF = -1e30
N_BLK_LANES = 128
LANES = 128
VMEM_LIMIT = 56 * 1024 * 1024

NT_DIMS = (((1,), (1,)), ((), ()))
TN_DIMS = (((0,), (0,)), ((), ()))


def _cparams(n_axes, vmem=None):
    return pltpu.CompilerParams(dimension_semantics=("arbitrary",) * n_axes,
                                vmem_limit_bytes=vmem)


def _rmsnorm_kernel(x_ref, g_ref, o_ref):
    x = x_ref[...]
    y = x * lax.rsqrt(jnp.mean(x * x, axis=-1, keepdims=True) + EPS)
    o_ref[...] = (y * g_ref[...]).astype(o_ref.dtype)


def _rmsnorm_call(x2, g, tm=512):
    t, d = x2.shape
    return pl.pallas_call(
        _rmsnorm_kernel, grid=(t // tm,),
        in_specs=[pl.BlockSpec((tm, d), lambda i: (i, 0)), pl.BlockSpec((1, d), lambda i: (0, 0))],
        out_specs=pl.BlockSpec((tm, d), lambda i: (i, 0)),
        out_shape=jax.ShapeDtypeStruct((t, d), BF16), name="pre_mix_rmsnorm",
        compiler_params=_cparams(1))(x2, g)


def _proj_call(name, xn, w_t, *, tm, tn, epilogue, out_shape, out_specs, extra=(), extra_specs=(),
               w_rows=None, to_bf16=()):
    t, k = xn.shape
    row0, n = (0, w_t.shape[0]) if w_rows is None else w_rows
    assert tn % PROJ_SUBTILE == 0 and n % tn == 0 and row0 % tn == 0
    grid = (t // tm, n // tn)
    steps = grid[0] * grid[1]
    n_extra, n_cast = len(extra), len(to_bf16)
    out_specs = list(out_specs) if isinstance(out_specs, (list, tuple)) else [out_specs]
    out_shape = list(out_shape) if isinstance(out_shape, (list, tuple)) else [out_shape]
    n_out = len(out_specs)

    def body(x_ref, w_ref, *rest):
        extra_refs = rest[:n_extra]
        cast_in = rest[n_extra:n_extra + n_cast]
        outs = rest[n_extra + n_cast:n_extra + n_cast + n_out]
        cast_out = rest[n_extra + n_cast + n_out:]
        x = x_ref[...]
        for c0 in range(0, tn, PROJ_SUBTILE):
            acc = lax.dot_general(x, w_ref[c0:c0 + PROJ_SUBTILE, :], NT_DIMS, preferred_element_type=F32)
            epilogue(acc, c0, extra_refs, outs)
        for src, dst in zip(cast_in, cast_out):
            dst[...] = src[...].astype(BF16)

    def slab(a):
        rows = a.shape[0] // steps
        assert a.shape[0] % steps == 0 and rows % 16 == 0
        return pl.BlockSpec((rows, a.shape[1]), lambda i, j: (i * grid[1] + j, 0))

    res = pl.pallas_call(
        body, grid=grid,
        in_specs=[pl.BlockSpec((tm, k), lambda i, j: (i, 0)),
                  pl.BlockSpec((tn, k), lambda i, j: (row0 // tn + j, 0)), *extra_specs,
                  *[slab(a) for a in to_bf16]],
        out_specs=out_specs + [slab(a) for a in to_bf16],
        out_shape=out_shape + [jax.ShapeDtypeStruct(a.shape, BF16) for a in to_bf16], name=name,
        compiler_params=_cparams(2, VMEM_LIMIT))(xn, w_t, *extra, *to_bf16)
    return res[0] if len(res) == 1 else res


def _rope_tile(a, tab_ref):
    return (a * tab_ref[0] + pltpu.roll(a, LANES - ROPE_DIM // 2, 1) * tab_ref[1]
            + pltpu.roll(a, ROPE_DIM // 2, 1) * tab_ref[2])


def _ep_plain(acc, c0, extra, outs):
    outs[0][0, :, c0:c0 + acc.shape[1]] = acc.astype(outs[0].dtype)


def _ep_q(acc, c0, extra, outs):
    for pair in range(acc.shape[1] // LANES):
        outs[0][c0 // LANES + pair] = _rope_tile(acc[:, pair * LANES:(pair + 1) * LANES], extra[0]) * Q_SCALE


def _ep_cmp(acc, c0, extra, outs):
    for hh in range(acc.shape[1] // HEAD_DIM):
        head = c0 // HEAD_DIM + hh
        outs[0][head // B_KV_HEADS, head % B_KV_HEADS] = acc[:, hh * HEAD_DIM:(hh + 1) * HEAD_DIM]


def _ep_kv(acc, c0, extra, outs, *, tm):
    i = pl.program_id(0)
    j = pl.program_id(1)
    rowg = i * tm + lax.broadcasted_iota(jnp.int32, (tm, LANES), 0)
    lane = lax.broadcasted_iota(jnp.int32, (tm, LANES), 1)
    aux = jnp.where(j == 0, (rowg // SLC_BLOCK == lane).astype(F32), 0.0).astype(BF16)
    for hh in range(acc.shape[1] // LANES):
        h = c0 // LANES + hh
        a = acc[:, hh * LANES:(hh + 1) * LANES]
        r = jnp.where(lane < HEAD_DIM, _rope_tile(a, extra[0]), a)
        outs[0][0, h, :, 0:LANES] = r.astype(BF16)
        outs[0][0, h, :, LANES:2 * LANES] = aux
        outs[1][0, h] = jnp.where(lane < HEAD_DIM, pltpu.roll(r, HEAD_DIM, 1), 1.0).astype(BF16)


def _ep_gate(acc, c0, extra, outs):
    outs[0][:, c0:c0 + acc.shape[1]] = jax.nn.sigmoid(acc)


def _ep_sigmoid(acc, c0, extra, outs):
    outs[0][0, :, c0:c0 + acc.shape[1]] = jax.nn.sigmoid(acc).astype(outs[0].dtype)


def _split3(x):
    hi = x.astype(BF16)
    r1 = x - hi.astype(F32)
    mid = r1.astype(BF16)
    lo = (r1 - mid.astype(F32)).astype(BF16)
    return hi, mid, lo


def _hgrn_kernel(q_ref, f_ref, i_ref, g_ref, lb_ref, gn_ref, o_ref, st_ref, *, chunk, heads):
    c = pl.program_id(1)

    @pl.when(c == 0)
    def _():
        st_ref[...] = jnp.zeros_like(st_ref)

    row = lax.broadcasted_iota(jnp.int32, (chunk, chunk), 0)
    col = lax.broadcasted_iota(jnp.int32, (chunk, chunk), 1)
    tri = (col <= row).astype(BF16)
    rowv = lax.broadcasted_iota(jnp.int32, (chunk, A_DK), 0)

    for hb in range(heads):
        sl = slice(hb * A_DK, (hb + 1) * A_DK)
        q = q_ref[0, :, sl]
        ii = i_ref[0, :, sl]
        gg = g_ref[0, :, sl]
        lbv = lb_ref[:, sl]
        f = lbv + (1.0 - lbv) * jax.nn.sigmoid(f_ref[0, :, sl])
        lf = jnp.log(f)
        k = 1.0 - f
        b = sum(jnp.dot(tri, p, preferred_element_type=F32) for p in _split3(lf)) * LOG2_E
        b_end = b[chunk - 1:chunk, :]
        ii16 = ii.astype(BF16)

        st = st_ref[hb]
        o = lax.dot_general((q * jnp.exp2(b)).astype(BF16), st.astype(BF16), NT_DIMS,
                            preferred_element_type=F32)
        kd = (k * jnp.exp2(b_end - b)).astype(BF16)
        st_ref[hb] = st * jnp.exp2(b_end) + lax.dot_general(ii16, kd, TN_DIMS,
                                                            preferred_element_type=F32)

        att = jnp.where(col == row, jnp.sum(q * k, axis=-1, keepdims=True), 0.0)
        for d in range(1, HGRN_PAIRWISE):
            valid = (rowv % HGRN_PAIRWISE) >= d
            w = jnp.exp2(b - pltpu.roll(b, d, 0))
            p = jnp.where(valid, q * pltpu.roll(k, d, 0) * w, 0.0)
            att = att + jnp.where(col == row - d, jnp.sum(p, axis=-1, keepdims=True), 0.0)
        m = HGRN_PAIRWISE
        while m < chunk:
            grp = chunk // (2 * m)
            b3 = b.reshape(grp, 2 * m, A_DK)
            refrow = jnp.broadcast_to(b3[:, m - 1:m, :], (grp, 2 * m, A_DK)).reshape(chunk, A_DK)
            second = (rowv % (2 * m)) >= m
            ql = jnp.where(second, q * jnp.exp2(b - refrow), 0.0)
            kl = jnp.where(second, 0.0, k * jnp.exp2(refrow - b))
            a = lax.dot_general(ql.astype(BF16), kl.astype(BF16), NT_DIMS, preferred_element_type=F32)
            if grp > 1:
                a = jnp.where(row // (2 * m) == col // (2 * m), a, 0.0)
            att = att + a
            m *= 2
        o = o + jnp.dot(att.astype(BF16), ii16, preferred_element_type=F32)

        o = o * lax.rsqrt(jnp.mean(o * o, axis=-1, keepdims=True) + EPS)
        o_ref[:, sl] = ((o * gn_ref[:, sl]) * (gg * jax.nn.sigmoid(gg))).astype(o_ref.dtype)


def _hgrn_call(proj4, lb, gn, *, chunk=128, heads=8):
    _, t, _ = proj4.shape
    w = heads * A_DK

    def spec(kind):
        return pl.BlockSpec((1, chunk, w), lambda h, c, kind=kind: (kind, c, h))

    vec = pl.BlockSpec((1, w), lambda h, c: (0, h))
    return pl.pallas_call(
        functools.partial(_hgrn_kernel, chunk=chunk, heads=heads),
        grid=(A_HEADS // heads, t // chunk),
        in_specs=[spec(0), spec(1), spec(2), spec(3), vec, vec],
        out_specs=pl.BlockSpec((chunk, w), lambda h, c: (c, h)),
        out_shape=jax.ShapeDtypeStruct((t, D_A), BF16),
        scratch_shapes=[pltpu.VMEM((heads, A_DK, A_DK), F32)], name="hgrn2_scan",
        compiler_params=_cparams(2))(proj4, proj4, proj4, proj4, lb, gn)


def _compress_kernel(x_ref, pe_ref, w1_ref, b1_ref, w2_ref, b2_ref, tab_ref, o_ref):
    half = CMP_BLOCK * HEAD_DIM // 2
    x = x_ref[0, 0]
    top = jnp.dot((x + pe_ref[0, 0:1, :]).astype(BF16), w1_ref[0, :half, :], preferred_element_type=F32)
    bot = jnp.dot((x + pe_ref[0, 1:2, :]).astype(BF16), w1_ref[0, half:, :], preferred_element_type=F32)
    n = x.shape[0]
    h = jax.nn.gelu(top + pltpu.roll(bot, n - 1, 0) + b1_ref[0])
    y = jnp.dot(h.astype(BF16), w2_ref[0], preferred_element_type=F32) + b2_ref[0]
    y = _rope_tile(y, tab_ref)
    hi = y.astype(BF16)
    lo = (y - hi.astype(F32)).astype(BF16)
    o_ref[0, 0, :, 0:LANES] = (y + pltpu.roll(y, HEAD_DIM, 1)).astype(BF16)
    o_ref[0, 0, :, LANES:2 * LANES] = lo


def _compress_call(x16, pe2, w1, b1, w2p, b2p, tab):
    _, nh, n, wid = x16.shape
    hid = w1.shape[-1]
    return pl.pallas_call(
        _compress_kernel, grid=(2, nh),
        in_specs=[pl.BlockSpec((1, 1, n, wid), lambda kv, h: (kv, h, 0, 0)),
                  pl.BlockSpec((1, 2, wid), lambda kv, h: (kv, 0, 0)),
                  pl.BlockSpec((1, 2 * wid, hid), lambda kv, h: (kv, 0, 0)),
                  pl.BlockSpec((1, 1, hid), lambda kv, h: (kv, 0, 0)),
                  pl.BlockSpec((1, hid, LANES), lambda kv, h: (kv, 0, 0)),
                  pl.BlockSpec((1, 1, LANES), lambda kv, h: (kv, 0, 0)),
                  pl.BlockSpec((3, n, LANES), lambda kv, h: (kv, 0, 0))],
        out_specs=pl.BlockSpec((1, 1, n, 2 * LANES), lambda kv, h: (kv, h, 0, 0)),
        out_shape=jax.ShapeDtypeStruct((2, nh, n, 2 * LANES), BF16),
        name="compress_mlp", compiler_params=_cparams(2, VMEM_LIMIT))(x16, pe2, w1, b1, w2p, b2p, tab)


def _nsa_select_kernel(q_ref, kc_ref, vc_ref, wt_ref, ocmp_ref, bias_ref, lhs_ref):
    qt = pl.program_id(1)
    q0 = qt * Q_TILE
    rows = B_GROUP * Q_TILE
    n_cmp = kc_ref.shape[2]

    def chains(n_cols, n_blk):
        t_col = q0 + lax.broadcasted_iota(jnp.int32, (rows, 1), 0) % Q_TILE
        n_idx = lax.broadcasted_iota(jnp.int32, (rows, n_cols), 1)
        vis = n_idx <= (t_col - (CMP_BLOCK - 1)) // CMP_STRIDE
        blk = lax.broadcasted_iota(jnp.int32, (n_blk, Q_TILE), 0)
        tok = q0 + lax.broadcasted_iota(jnp.int32, (n_blk, Q_TILE), 1)
        cur = tok // SLC_BLOCK
        forced = (blk == 0) | (blk == cur) | (blk == cur - 1)
        candidate = (blk * SLC_BLOCK <= tok) & jnp.logical_not(forced)
        wt = wt_ref[0:n_blk, 0:n_cols]
        zeros64 = jnp.zeros((rows, HEAD_DIM), BF16)
        never = jnp.zeros((N_BLK_LANES - n_blk, Q_TILE), F32)

        for c in range(KV_PER_SELECT_STEP):
            for pair in range(B_GROUP // 2):
                qp = q_ref[c * (B_GROUP // 2) + pair]
                hi = qp.astype(BF16)
                lo = (qp - hi.astype(F32)).astype(BF16)
                for half in range(2):
                    r = slice((2 * pair + half) * Q_TILE, (2 * pair + half + 1) * Q_TILE)
                    ln = slice(half * HEAD_DIM, (half + 1) * HEAD_DIM)
                    lhs_ref[c, r, 0:HEAD_DIM] = hi[:, ln]
                    lhs_ref[c, r, HEAD_DIM:2 * HEAD_DIM] = lo[:, ln]
                    lhs_ref[c, r, 2 * HEAD_DIM:3 * HEAD_DIM] = hi[:, ln]
            lhs_ref[c, :, 3 * HEAD_DIM:] = zeros64
            s = lax.dot_general(lhs_ref[c], kc_ref[0, c, 0:n_cols, :], NT_DIMS, preferred_element_type=F32)
            s = jnp.where(vis, s, NEG_INF)
            e = jnp.exp2(s - jnp.max(s, axis=-1, keepdims=True))
            l = jnp.sum(e, axis=-1, keepdims=True)
            p = e * jnp.where(t_col >= CMP_BLOCK - 1, 1.0 / l, 0.0)
            o_cmp = jnp.dot(p.astype(BF16), vc_ref[0, c, 0:n_cols, 0:LANES], preferred_element_type=F32)
            for g in range(B_GROUP):
                c0 = (c * B_GROUP + g) * HEAD_DIM
                ocmp_ref[:, c0:c0 + HEAD_DIM] = o_cmp[g * Q_TILE:(g + 1) * Q_TILE, :HEAD_DIM]

            psum = p[0:Q_TILE] + p[Q_TILE:2 * Q_TILE] + p[2 * Q_TILE:3 * Q_TILE] + p[3 * Q_TILE:]
            ps_hi = psum.astype(BF16)
            ps_lo = (psum - ps_hi.astype(F32)).astype(BF16)
            pslc = (lax.dot_general(wt, ps_hi, NT_DIMS, preferred_element_type=F32)
                    + lax.dot_general(wt, ps_lo, NT_DIMS, preferred_element_type=F32))
            score = jnp.where(candidate, pslc, jnp.where(forced, -jnp.inf, NEG_INF))
            sel_t = forced.astype(F32)
            for _ in range(SLC_TOP_N - SLC_FORCED):
                best = jnp.max(score, axis=0, keepdims=True)
                first = jnp.min(jnp.where(score == best, blk, N_BLK_LANES), axis=0, keepdims=True)
                hit = blk == first
                score = jnp.where(hit, -jnp.inf, score)
                sel_t = jnp.where(hit, 1.0, sel_t)
            if n_blk < N_BLK_LANES:
                sel_t = jnp.concatenate([sel_t, never], axis=0)
            bias_ref[c] = jnp.where(sel_t.T > 0.0, 0.0, NEG_INF).astype(BF16)

    n_blk_all = n_cmp // (SLC_BLOCK // CMP_STRIDE)
    tiles_per_variant = pl.num_programs(1) // SELECT_VARIANTS
    for v in range(1, SELECT_VARIANTS + 1):
        pl.when(qt // tiles_per_variant == v - 1)(functools.partial(
            chains, n_cmp * v // SELECT_VARIANTS, n_blk_all * v // SELECT_VARIANTS))


def _nsa_select_call(q_hm, cmp_aug, wt):
    _, t, _ = q_hm.shape
    n_cmp = cmp_aug.shape[2]
    kvs = KV_PER_SELECT_STEP
    assert (t // Q_TILE) % SELECT_VARIANTS == 0 and (t // SLC_BLOCK) % (8 * SELECT_VARIANTS) == 0
    return pl.pallas_call(
        _nsa_select_kernel, grid=(B_KV_HEADS // kvs, t // Q_TILE),
        in_specs=[pl.BlockSpec((kvs * B_GROUP // 2, Q_TILE, LANES), lambda p, i: (p, i, 0)),
                  pl.BlockSpec((1, kvs, n_cmp, 2 * LANES), lambda p, i: (0, p, 0, 0)),
                  pl.BlockSpec((1, kvs, n_cmp, 2 * LANES), lambda p, i: (1, p, 0, 0)),
                  pl.BlockSpec((N_BLK_LANES, n_cmp), lambda p, i: (0, 0))],
        out_specs=[pl.BlockSpec((Q_TILE, kvs * B_GROUP * HEAD_DIM), lambda p, i: (i, p)),
                   pl.BlockSpec((kvs, Q_TILE, N_BLK_LANES), lambda p, i: (p, i, 0))],
        out_shape=[jax.ShapeDtypeStruct((t, D_B), F32),
                   jax.ShapeDtypeStruct((B_KV_HEADS, t, N_BLK_LANES), BF16)],
        scratch_shapes=[pltpu.VMEM((kvs, B_GROUP * Q_TILE, 2 * LANES), BF16)], name="nsa_select",
        compiler_params=_cparams(2, VMEM_LIMIT))(q_hm, cmp_aug, cmp_aug, wt)


def _nsa_attend_kernel(q_ref, bias_ref, ocmp_ref, gate_ref, gx_ref, ksv_ref, kwv_ref, vs_ref, vw_ref, o_ref,
                       lhs_w_ref, lhs_s_ref, m_ref, acc_ref, *, key_tile, tiles_per_group):
    q0 = pl.program_id(1) * ATT_TILE
    rows = B_GROUP * ATT_TILE
    chains = range(KV_PER_ATTEND_STEP)
    t_col = q0 + lax.broadcasted_iota(jnp.int32, (rows, 1), 0) % ATT_TILE

    for c in chains:
        for g in range(B_GROUP):
            r = slice(g * ATT_TILE, (g + 1) * ATT_TILE)
            qp = q_ref[c * (B_GROUP // 2) + g // 2]
            qhi = qp[:, (g % 2) * HEAD_DIM:(g % 2 + 1) * HEAD_DIM].astype(BF16)
            lhs_s_ref[c, r, 0:HEAD_DIM] = qhi
            lhs_s_ref[c, r, LANES:] = bias_ref[c]
            lhs_w_ref[c, r, 0:HEAD_DIM] = qhi
        lhs_s_ref[c, :, HEAD_DIM:LANES] = jnp.zeros((rows, LANES - HEAD_DIM), BF16)
        lhs_w_ref[c, :, HEAD_DIM:] = jnp.zeros((rows, 2 * LANES - HEAD_DIM), BF16)

    for c in chains:
        m_ref[c] = jnp.full((rows, LANES), NEG_INF, F32)
        acc_ref[c] = jnp.zeros((rows, LANES), F32)

    def tile(ti, causal):
        start = pl.multiple_of(ti * key_tile, key_tile)
        for c in chains:
            st = lax.dot_general(lhs_s_ref[c], ksv_ref[0, c, pl.ds(start, key_tile), :], NT_DIMS,
                                 preferred_element_type=F32)
            if causal:
                kp = start + lax.broadcasted_iota(jnp.int32, (rows, key_tile), 1)
                st = jnp.where(kp <= t_col, st, NEG_INF)
            m_run = m_ref[c]
            m_new = jnp.maximum(m_run, jnp.max(st, axis=-1, keepdims=True))
            pt = jnp.exp2(st - jnp.concatenate([m_new] * (key_tile // LANES), axis=1))
            pv = jnp.dot(pt.astype(BF16), vs_ref[0, c, pl.ds(start, key_tile), :], preferred_element_type=F32)
            acc_ref[c] = acc_ref[c] * jnp.exp2(m_run - m_new) + pv
            m_ref[c] = m_new

    def group(gi, _):
        for u in range(tiles_per_group):
            tile(gi * tiles_per_group + u, False)
        return 0

    def single(ti, _):
        tile(ti, False)
        return 0

    n_full = q0 // key_tile
    n_groups = n_full // tiles_per_group
    lax.fori_loop(0, n_groups, group, 0)
    lax.fori_loop(n_groups * tiles_per_group, n_full, single, 0)
    tile(n_full, True)

    slab = WINDOW + ATT_TILE
    ws = pl.multiple_of(jnp.maximum(q0 - WINDOW, 0), ATT_TILE)
    kpos = ws + lax.broadcasted_iota(jnp.int32, (rows, slab), 1)
    in_window = lax.bitcast_convert_type(t_col - kpos, jnp.uint32) < jnp.uint32(WINDOW)
    lane = lax.broadcasted_iota(jnp.int32, (ATT_TILE, LANES), 1)
    per_branch = B_GROUP * HEAD_DIM

    def head_pair(acc, k):
        a_even = acc[2 * k * ATT_TILE:(2 * k + 1) * ATT_TILE]
        a_odd = acc[(2 * k + 1) * ATT_TILE:(2 * k + 2) * ATT_TILE]
        num = jnp.where(lane < HEAD_DIM, a_even, pltpu.roll(a_odd, HEAD_DIM, 1))
        den = jnp.where(lane < HEAD_DIM, pltpu.roll(a_even, HEAD_DIM, 1), a_odd)
        return num * (1.0 / den)

    gates = gate_ref[...]
    g_hi = gates.astype(BF16)
    g_lo = (gates - g_hi.astype(F32)).astype(BF16)

    for c in chains:
        acc_s = acc_ref[c]
        s = lax.dot_general(lhs_w_ref[c], kwv_ref[0, c, pl.ds(ws, slab), :], NT_DIMS,
                            preferred_element_type=F32)
        s = jnp.where(in_window, s, NEG_INF)
        e = jnp.exp2(s - jnp.max(s, axis=-1, keepdims=True))
        acc_w = jnp.dot(e.astype(BF16), vw_ref[0, c, pl.ds(ws, slab), :], preferred_element_type=F32)

        gx = (jnp.dot(g_hi, gx_ref[c], preferred_element_type=F32)
              + jnp.dot(g_lo, gx_ref[c], preferred_element_type=F32))
        for k in range(B_GROUP // 2):
            c0 = c * B_GROUP * HEAD_DIM + k * LANES
            g0 = k * LANES
            og = (gx[:, g0:g0 + LANES] * ocmp_ref[:, c0:c0 + LANES]
                  + gx[:, per_branch + g0:per_branch + g0 + LANES] * head_pair(acc_s, k)
                  + gx[:, 2 * per_branch + g0:2 * per_branch + g0 + LANES] * head_pair(acc_w, k))
            o_ref[:, c0:c0 + LANES] = og.astype(o_ref.dtype)


def _nsa_attend_call(q_hm, bias, ocmp, gates, kv_aug, v_aug, *, key_tile=512, tiles_per_group=2):
    _, t, _ = q_hm.shape
    assert t % key_tile == 0 and key_tile % ATT_TILE == 0 and t % ATT_TILE == 0
    rows = B_GROUP * ATT_TILE
    kvs = KV_PER_ATTEND_STEP
    wide = kvs * B_GROUP * HEAD_DIM
    resident = dict(pipeline_mode=pl.Buffered(1))
    return pl.pallas_call(
        functools.partial(_nsa_attend_kernel, key_tile=key_tile, tiles_per_group=tiles_per_group),
        grid=(B_KV_HEADS // kvs, t // ATT_TILE),
        in_specs=[pl.BlockSpec((kvs * B_GROUP // 2, ATT_TILE, LANES), lambda p, i: (p, i, 0)),
                  pl.BlockSpec((kvs, ATT_TILE, N_BLK_LANES), lambda p, i: (p, i, 0)),
                  pl.BlockSpec((ATT_TILE, wide), lambda p, i: (i, p)),
                  pl.BlockSpec((ATT_TILE, GATE_LANES), lambda p, i: (i, 0)),
                  pl.BlockSpec((kvs, GATE_LANES, 3 * B_GROUP * HEAD_DIM), lambda p, i: (p, 0, 0)),
                  pl.BlockSpec((1, kvs, t, 2 * LANES), lambda p, i: (0, p, 0, 0), **resident),
                  pl.BlockSpec((1, kvs, t, 2 * LANES), lambda p, i: (1, p, 0, 0), **resident),
                  pl.BlockSpec((1, kvs, t, LANES), lambda p, i: (0, p, 0, 0), **resident),
                  pl.BlockSpec((1, kvs, t, LANES), lambda p, i: (1, p, 0, 0), **resident)],
        out_specs=pl.BlockSpec((ATT_TILE, wide), lambda p, i: (i, p)),
        out_shape=jax.ShapeDtypeStruct((t, D_B), BF16),
        scratch_shapes=[pltpu.VMEM((kvs, rows, 2 * LANES), BF16), pltpu.VMEM((kvs, rows, 2 * LANES), BF16),
                        pltpu.VMEM((kvs, rows, LANES), F32), pltpu.VMEM((kvs, rows, LANES), F32)],
        name="nsa_attend",
        compiler_params=_cparams(2, VMEM_LIMIT))(q_hm, bias, ocmp, gates, _gate_expand_matrix(),
                                                 kv_aug, kv_aug, v_aug, v_aug)


def _post_kernel(oa_ref, ob_ref, sg_ref, x_ref, wa_ref, wb_ref, wo_ref, g_ref, wg_ref, wu_ref, wd_ref,
                 o_ref, wg16_ref, wu16_ref, wd16_ref):
    ya = jnp.dot(oa_ref[...], wa_ref[...], preferred_element_type=F32)
    yb = jnp.dot(ob_ref[...], wb_ref[...], preferred_element_type=F32)
    merged = sg_ref[0].astype(F32) * ya + sg_ref[1].astype(F32) * yb
    mix = jnp.dot(merged.astype(BF16), wo_ref[...], preferred_element_type=F32)
    y = mix * lax.rsqrt(jnp.mean(mix * mix, axis=-1, keepdims=True) + EPS)
    o_ref[...] = x_ref[...] + y * g_ref[...]
    wg16_ref[...] = wg_ref[...].astype(BF16)
    wu16_ref[...] = wu_ref[...].astype(BF16)
    wd16_ref[...] = wd_ref[...].astype(BF16)


def _post_call(oa, ob, sg, x2, wa, wb, wo, g, wg, wu, wd, tm=256):
    t, d = x2.shape
    steps = t // tm
    dff = wg.shape[1]
    assert d % (steps * 16) == 0 and dff % (steps * 16) == 0
    const = dict(pipeline_mode=pl.Buffered(1))
    return pl.pallas_call(
        _post_kernel, grid=(steps,),
        in_specs=[pl.BlockSpec((tm, D_A), lambda i: (i, 0)),
                  pl.BlockSpec((tm, D_B), lambda i: (i, 0)),
                  pl.BlockSpec((2, tm, d), lambda i: (0, i, 0)),
                  pl.BlockSpec((tm, d), lambda i: (i, 0)),
                  pl.BlockSpec((D_A, d), lambda i: (0, 0), **const),
                  pl.BlockSpec((D_B, d), lambda i: (0, 0), **const),
                  pl.BlockSpec((d, d), lambda i: (0, 0), **const),
                  pl.BlockSpec((1, d), lambda i: (0, 0)),
                  pl.BlockSpec((d // steps, dff), lambda i: (i, 0)),
                  pl.BlockSpec((d // steps, dff), lambda i: (i, 0)),
                  pl.BlockSpec((dff // steps, d), lambda i: (i, 0))],
        out_specs=[pl.BlockSpec((tm, d), lambda i: (i, 0)),
                   pl.BlockSpec((d // steps, dff), lambda i: (i, 0)),
                   pl.BlockSpec((d // steps, dff), lambda i: (i, 0)),
                   pl.BlockSpec((dff // steps, d), lambda i: (i, 0))],
        out_shape=[jax.ShapeDtypeStruct((t, d), F32), jax.ShapeDtypeStruct((d, dff), BF16),
                   jax.ShapeDtypeStruct((d, dff), BF16), jax.ShapeDtypeStruct((dff, d), BF16)],
        name="mix_out_residual",
        compiler_params=_cparams(1, VMEM_LIMIT))(oa, ob, sg, x2, wa, wb, wo, g, wg, wu, wd)


def _ffn_kernel(h_ref, g1_ref, wg_ref, wu_ref, wd_ref, g2_ref, o_ref, hn_ref, acc_ref):
    f = pl.program_id(1)

    @pl.when(f == 0)
    def _():
        h = h_ref[...]
        y = h * lax.rsqrt(jnp.mean(h * h, axis=-1, keepdims=True) + EPS)
        hn_ref[...] = (y * g1_ref[...]).astype(BF16)
        acc_ref[...] = jnp.zeros_like(acc_ref)

    hn = hn_ref[...]
    a = jnp.dot(hn, wg_ref[...], preferred_element_type=F32)
    u = jnp.dot(hn, wu_ref[...], preferred_element_type=F32)
    z = (a * jax.nn.sigmoid(a) * u).astype(BF16)
    acc_ref[...] += jnp.dot(z, wd_ref[...], preferred_element_type=F32)

    @pl.when(f == pl.num_programs(1) - 1)
    def _():
        ff = acc_ref[...]
        y = ff * lax.rsqrt(jnp.mean(ff * ff, axis=-1, keepdims=True) + EPS)
        o_ref[...] = h_ref[...] + y * g2_ref[...]


def _ffn_call(h1, g1, wg, wu, wd, g2, tm=512, tf=512):
    t, d = h1.shape
    dff = wg.shape[1]
    return pl.pallas_call(
        _ffn_kernel, grid=(t // tm, dff // tf),
        in_specs=[pl.BlockSpec((tm, d), lambda i, f: (i, 0)),
                  pl.BlockSpec((1, d), lambda i, f: (0, 0)),
                  pl.BlockSpec((d, tf), lambda i, f: (0, f)),
                  pl.BlockSpec((d, tf), lambda i, f: (0, f)),
                  pl.BlockSpec((tf, d), lambda i, f: (f, 0)),
                  pl.BlockSpec((1, d), lambda i, f: (0, 0))],
        out_specs=pl.BlockSpec((tm, d), lambda i, f: (i, 0)),
        out_shape=jax.ShapeDtypeStruct((t, d), F32),
        scratch_shapes=[pltpu.VMEM((tm, d), BF16), pltpu.VMEM((tm, d), F32)], name="swiglu_ffn",
        compiler_params=_cparams(2, VMEM_LIMIT))(h1, g1, wg, wu, wd, g2)


def _rope_tables(pos, period):
    half = ROPE_DIM // 2
    inv = jnp.float32(ROPE_THETA) ** (-jnp.arange(half, dtype=F32) * 2.0 / ROPE_DIM)
    r = np.arange(LANES) % period
    pick = np.zeros((half, LANES), np.float32)
    pick[r % half, np.arange(LANES)] = 1.0
    inv_lane = jnp.sum(inv[:, None] * pick, axis=0)
    ang = pos.astype(F32)[:, None] * inv_lane[None, :]
    cos, sin = jnp.cos(ang), jnp.sin(ang)
    lo = jnp.asarray((r < half)[None, :])
    hi = jnp.asarray(((r >= half) & (r < ROPE_DIM))[None, :])
    c = jnp.where(lo | hi, cos, 1.0)
    sa = jnp.where(lo, -sin, 0.0)
    sb = jnp.where(hi, sin, 0.0)
    return jnp.stack([c, sa, sb])


def _gate_expand_matrix():
    e = np.zeros((B_KV_HEADS, GATE_LANES, 3 * B_GROUP * HEAD_DIM), np.float32)
    for kvh in range(B_KV_HEADS):
        for g in range(B_GROUP):
            for b in range(3):
                c0 = (b * B_GROUP + g) * HEAD_DIM
                e[kvh, 3 * (kvh * B_GROUP + g) + b, c0:c0 + HEAD_DIM] = 1.0
    return jnp.asarray(e, BF16)


def _slc_weight_matrix(n_cmp_pad):
    ratio = SLC_BLOCK // CMP_STRIDE
    w = np.zeros((N_BLK_LANES, n_cmp_pad), np.float32)
    for j in range(N_BLK_LANES):
        for o, wv in enumerate(SLC_WEIGHTS):
            n = ratio * j + o - 1
            if 0 <= n < n_cmp_pad - 1:
                w[j, n] = wv
    return jnp.asarray(w, BF16)


def kernel(x, pre_mix_g, w_in, lb_table, a_norm_g, cmp_pe_k, cmp_w1_k, cmp_b1_k, cmp_w2_k, cmp_b2_k, cmp_pe_v, cmp_w1_v, cmp_b1_v, cmp_w2_v, cmp_b2_v, w_proj_a, w_proj_b, w_out, post_mix_g, pre_ffn_g, w_gate, w_up, w_down, post_ffn_g):
    bsz, t, d = x.shape
    assert bsz == 1 and d == D_MODEL and WINDOW + ATT_TILE <= t <= N_BLK_LANES * SLC_BLOCK
    x2 = x.reshape(t, d)
    n_cmp_pad = t // CMP_STRIDE
    assert n_cmp_pad % LANES == 0

    lb = jnp.cumsum(jax.nn.softmax(lb_table.astype(F32), axis=0), axis=0)[0].reshape(1, D_A)
    w = jnp.swapaxes(w_in[0], 0, 1).astype(BF16)
    o_q, o_kc, o_ks, o_g, o_m = 4 * D_A, 4 * D_A + D_B, 4 * D_A + D_B + 2 * KV_DIM, 4 * D_A + D_B + 6 * KV_DIM, 4 * D_A + D_B + 6 * KV_DIM + 3 * B_HEADS
    w_kv = (w[o_ks:o_g].reshape(2, 2, B_KV_HEADS, HEAD_DIM, d).transpose(0, 2, 1, 3, 4)
            .reshape(4 * KV_DIM, d))
    w_g = jnp.pad(w[o_g:o_m], ((0, GATE_LANES - 3 * B_HEADS), (0, 0)))
    w_m = w[o_m:]

    pos = jnp.arange(t)
    tab_pos = _rope_tables(pos, HEAD_DIM)
    end_pos = jnp.arange(n_cmp_pad) * CMP_STRIDE + (CMP_BLOCK - 1)
    ident = jnp.stack([jnp.ones((n_cmp_pad, LANES), F32), jnp.zeros((n_cmp_pad, LANES), F32),
                       jnp.zeros((n_cmp_pad, LANES), F32)])
    tab_cmp = jnp.concatenate([_rope_tables(end_pos, LANES), ident], axis=0)

    xn = _rmsnorm_call(x2, pre_mix_g[0].reshape(1, d))

    tm = 1024 if t % 1024 == 0 else 256
    proj4 = _proj_call(
        "proj_hgrn", xn, w, w_rows=(0, o_q), tm=tm, tn=D_A, epilogue=_ep_plain,
        out_shape=jax.ShapeDtypeStruct((4, t, D_A), F32),
        out_specs=pl.BlockSpec((1, tm, D_A), lambda i, j: (j, i, 0)))
    q_hm = _proj_call(
        "proj_q", xn, w, w_rows=(o_q, D_B), tm=tm, tn=512, epilogue=_ep_q,
        out_shape=jax.ShapeDtypeStruct((B_HEADS // 2, t, LANES), F32),
        out_specs=pl.BlockSpec((512 // LANES, tm, LANES), lambda i, j: (j, i, 0)),
        extra=(tab_pos,), extra_specs=(pl.BlockSpec((3, tm, LANES), lambda i, j: (0, i, 0)),))
    cmp_in = _proj_call(
        "proj_cmp", xn, w, w_rows=(o_kc, 2 * KV_DIM), tm=tm, tn=2 * KV_DIM, epilogue=_ep_cmp,
        out_shape=jax.ShapeDtypeStruct((2, B_KV_HEADS, t, HEAD_DIM), F32),
        out_specs=pl.BlockSpec((2, B_KV_HEADS, tm, HEAD_DIM), lambda i, j: (0, 0, i, 0)))
    kv_aug, v_aug = _proj_call(
        "proj_kv", xn, w_kv, tm=tm, tn=2 * KV_DIM, epilogue=functools.partial(_ep_kv, tm=tm),
        out_shape=[jax.ShapeDtypeStruct((2, B_KV_HEADS, t, 2 * LANES), BF16),
                   jax.ShapeDtypeStruct((2, B_KV_HEADS, t, LANES), BF16)],
        out_specs=[pl.BlockSpec((1, B_KV_HEADS, tm, 2 * LANES), lambda i, j: (j, 0, i, 0)),
                   pl.BlockSpec((1, B_KV_HEADS, tm, LANES), lambda i, j: (j, 0, i, 0))],
        extra=(tab_pos,), extra_specs=(pl.BlockSpec((3, tm, LANES), lambda i, j: (0, i, 0)),))
    gates = _proj_call(
        "proj_gate", xn, w_g, tm=tm, tn=GATE_LANES, epilogue=_ep_gate,
        out_shape=jax.ShapeDtypeStruct((t, GATE_LANES), F32),
        out_specs=pl.BlockSpec((tm, GATE_LANES), lambda i, j: (i, 0)))
    sg, wa16, wb16, wo16 = _proj_call(
        "proj_merge", xn, w_m, tm=tm, tn=1024, epilogue=_ep_sigmoid,
        out_shape=jax.ShapeDtypeStruct((2, t, d), BF16),
        out_specs=pl.BlockSpec((1, tm, 1024), lambda i, j: (j // 2, i, j % 2)),
        to_bf16=(w_proj_a[0], w_proj_b[0], w_out[0]))

    oa = _hgrn_call(proj4, lb, a_norm_g[0].reshape(1, D_A))

    half = CMP_BLOCK // 2
    x16 = cmp_in.reshape(2, B_KV_HEADS, n_cmp_pad, CMP_STRIDE * HEAD_DIM)
    pe2 = jnp.stack([cmp_pe_k[0], cmp_pe_v[0]]).reshape(2, 2, half * HEAD_DIM)
    w1 = jnp.stack([cmp_w1_k[0], cmp_w1_v[0]]).astype(BF16)
    b1 = jnp.stack([cmp_b1_k[0], cmp_b1_v[0]]).reshape(2, 1, CMP_HIDDEN)
    w2p = jnp.pad(jnp.stack([cmp_w2_k[0], cmp_w2_v[0]]), ((0, 0), (0, 0), (0, LANES - HEAD_DIM))).astype(BF16)
    b2p = jnp.pad(jnp.stack([cmp_b2_k[0], cmp_b2_v[0]]), ((0, 0), (0, LANES - HEAD_DIM))).reshape(2, 1, LANES)
    cmp_aug = _compress_call(x16, pe2, w1, b1, w2p, b2p, tab_cmp)
    ocmp, bias = _nsa_select_call(q_hm, cmp_aug, _slc_weight_matrix(n_cmp_pad))
    ob = _nsa_attend_call(q_hm, bias, ocmp, gates, kv_aug, v_aug)

    h1, wg16, wu16, wd16 = _post_call(oa, ob, sg, x2, wa16, wb16, wo16, post_mix_g[0].reshape(1, d),
                                      w_gate[0], w_up[0], w_down[0])
    out = _ffn_call(h1, pre_ffn_g[0].reshape(1, d), wg16, wu16, wd16, post_ffn_g[0].reshape(1, d))
    return out.reshape(bsz, t, d)
```

```python
import functools

import numpy as np
import jax
import jax.numpy as jnp
from jax import lax
from jax.experimental import pallas as pl
from jax.experimental.pallas import tpu as pltpu

F32 = jnp.float32
BF16 = jnp.bfloat16

D_MODEL = 2048
D_A = 1024
D_B = 1024
A_HEADS = 8
A_DK = 128
HGRN_PAIRWISE = 4
B_HEADS = 16
B_KV_HEADS = 4
B_GROUP = 4
HEAD_DIM = 64
KV_DIM = 256
CMP_BLOCK = 32
CMP_STRIDE = 16
CMP_HIDDEN = 256
SLC_BLOCK = 64
SLC_TOP_N = 16
SLC_FORCED = 3
SLC_WEIGHTS = (1.0, 2.0, 2.0, 2.0, 1.0)
WINDOW = 512
Q_TILE = 512
ATT_TILE = 256
KV_PER_ATTEND_STEP = 2
KV_PER_SELECT_STEP = 2
SELECT_VARIANTS = 4
PROJ_SUBTILE = 256
GATE_LANES = 256
ROPE_THETA = 500000.0
ROPE_DIM = 16
D_FF = 5632
EPS = 1e-6
LOG2_E = 1.4426950408889634
Q_SCALE = HEAD_DIM ** -0.5 * LOG2_E
NEG_INF = -1e30
N_BLK_LANES = 128
LANES = 128
VMEM_LIMIT = 56 * 1024 * 1024

NT_DIMS = (((1,), (1,)), ((), ()))
TN_DIMS = (((0,), (0,)), ((), ()))


def _cparams(n_axes, vmem=None):
    return pltpu.CompilerParams(dimension_semantics=("arbitrary",) * n_axes,
                                vmem_limit_bytes=vmem)


def _rmsnorm_kernel(x_ref, g_ref, o_ref):
    x = x_ref[...]
    y = x * lax.rsqrt(jnp.mean(x * x, axis=-1, keepdims=True) + EPS)
    o_ref[...] = (y * g_ref[...]).astype(o_ref.dtype)


def _rmsnorm_call(x2, g, tm=512):
    t, d = x2.shape
    return pl.pallas_call(
        _rmsnorm_kernel, grid=(t // tm,),
        in_specs=[pl.BlockSpec((tm, d), lambda i: (i, 0)), pl.BlockSpec((1, d), lambda i: (0, 0))],
        out_specs=pl.BlockSpec((tm, d), lambda i: (i, 0)),
        out_shape=jax.ShapeDtypeStruct((t, d), BF16), name="pre_mix_rmsnorm",
        compiler_params=_cparams(1))(x2, g)


def _proj_call(name, xn, w_t, *, tm, tn, epilogue, out_shape, out_specs, extra=(), extra_specs=(),
               w_rows=None, to_bf16=()):
    t, k = xn.shape
    row0, n = (0, w_t.shape[0]) if w_rows is None else w_rows
    assert tn % PROJ_SUBTILE == 0 and n % tn == 0 and row0 % tn == 0
    grid = (t // tm, n // tn)
    steps = grid[0] * grid[1]
    n_extra, n_cast = len(extra), len(to_bf16)
    out_specs = list(out_specs) if isinstance(out_specs, (list, tuple)) else [out_specs]
    out_shape = list(out_shape) if isinstance(out_shape, (list, tuple)) else [out_shape]
    n_out = len(out_specs)

    def body(x_ref, w_ref, *rest):
        extra_refs = rest[:n_extra]
        cast_in = rest[n_extra:n_extra + n_cast]
        outs = rest[n_extra + n_cast:n_extra + n_cast + n_out]
        cast_out = rest[n_extra + n_cast + n_out:]
        x = x_ref[...]
        for c0 in range(0, tn, PROJ_SUBTILE):
            acc = lax.dot_general(x, w_ref[c0:c0 + PROJ_SUBTILE, :], NT_DIMS, preferred_element_type=F32)
            epilogue(acc, c0, extra_refs, outs)
        for src, dst in zip(cast_in, cast_out):
            dst[...] = src[...].astype(BF16)

    def slab(a):
        rows = a.shape[0] // steps
        assert a.shape[0] % steps == 0 and rows % 16 == 0
        return pl.BlockSpec((rows, a.shape[1]), lambda i, j: (i * grid[1] + j, 0))

    res = pl.pallas_call(
        body, grid=grid,
        in_specs=[pl.BlockSpec((tm, k), lambda i, j: (i, 0)),
                  pl.BlockSpec((tn, k), lambda i, j: (row0 // tn + j, 0)), *extra_specs,
                  *[slab(a) for a in to_bf16]],
        out_specs=out_specs + [slab(a) for a in to_bf16],
        out_shape=out_shape + [jax.ShapeDtypeStruct(a.shape, BF16) for a in to_bf16], name=name,
        compiler_params=_cparams(2, VMEM_LIMIT))(xn, w_t, *extra, *to_bf16)
    return res[0] if len(res) == 1 else res


def _rope_tile(a, tab_ref):
    return (a * tab_ref[0] + pltpu.roll(a, LANES - ROPE_DIM // 2, 1) * tab_ref[1]
            + pltpu.roll(a, ROPE_DIM // 2, 1) * tab_ref[2])


def _ep_plain(acc, c0, extra, outs):
    outs[0][0, :, c0:c0 + acc.shape[1]] = acc.astype(outs[0].dtype)


def _ep_q(acc, c0, extra, outs):
    for pair in range(acc.shape[1] // LANES):
        outs[0][c0 // LANES + pair] = _rope_tile(acc[:, pair * LANES:(pair + 1) * LANES], extra[0]) * Q_SCALE


def _ep_cmp(acc, c0, extra, outs):
    for hh in range(acc.shape[1] // HEAD_DIM):
        head = c0 // HEAD_DIM + hh
        outs[0][head // B_KV_HEADS, head % B_KV_HEADS] = acc[:, hh * HEAD_DIM:(hh + 1) * HEAD_DIM]


def _ep_kv(acc, c0, extra, outs, *, tm):
    i = pl.program_id(0)
    j = pl.program_id(1)
    rowg = i * tm + lax.broadcasted_iota(jnp.int32, (tm, LANES), 0)
    lane = lax.broadcasted_iota(jnp.int32, (tm, LANES), 1)
    aux = jnp.where(j == 0, (rowg // SLC_BLOCK == lane).astype(F32), 0.0).astype(BF16)
    for hh in range(acc.shape[1] // LANES):
        h = c0 // LANES + hh
        a = acc[:, hh * LANES:(hh + 1) * LANES]
        r = jnp.where(lane < HEAD_DIM, _rope_tile(a, extra[0]), a)
        outs[0][0, h, :, 0:LANES] = r.astype(BF16)
        outs[0][0, h, :, LANES:2 * LANES] = aux
        outs[1][0, h] = jnp.where(lane < HEAD_DIM, pltpu.roll(r, HEAD_DIM, 1), 1.0).astype(BF16)


def _ep_gate(acc, c0, extra, outs):
    outs[0][:, c0:c0 + acc.shape[1]] = jax.nn.sigmoid(acc)


def _ep_sigmoid(acc, c0, extra, outs):
    outs[0][0, :, c0:c0 + acc.shape[1]] = jax.nn.sigmoid(acc).astype(outs[0].dtype)


def _split3(x):
    hi = x.astype(BF16)
    r1 = x - hi.astype(F32)
    mid = r1.astype(BF16)
    lo = (r1 - mid.astype(F32)).astype(BF16)
    return hi, mid, lo


def _hgrn_kernel(q_ref, f_ref, i_ref, g_ref, lb_ref, gn_ref, o_ref, st_ref, *, chunk, heads):
    c = pl.program_id(1)

    @pl.when(c == 0)
    def _():
        st_ref[...] = jnp.zeros_like(st_ref)

    row = lax.broadcasted_iota(jnp.int32, (chunk, chunk), 0)
    col = lax.broadcasted_iota(jnp.int32, (chunk, chunk), 1)
    tri = (col <= row).astype(BF16)
    rowv = lax.broadcasted_iota(jnp.int32, (chunk, A_DK), 0)

    for hb in range(heads):
        sl = slice(hb * A_DK, (hb + 1) * A_DK)
        q = q_ref[0, :, sl]
        ii = i_ref[0, :, sl]
        gg = g_ref[0, :, sl]
        lbv = lb_ref[:, sl]
        f = lbv + (1.0 - lbv) * jax.nn.sigmoid(f_ref[0, :, sl])
        lf = jnp.log(f)
        k = 1.0 - f
        b = sum(jnp.dot(tri, p, preferred_element_type=F32) for p in _split3(lf)) * LOG2_E
        b_end = b[chunk - 1:chunk, :]
        ii16 = ii.astype(BF16)

        st = st_ref[hb]
        o = lax.dot_general((q * jnp.exp2(b)).astype(BF16), st.astype(BF16), NT_DIMS,
                            preferred_element_type=F32)
        kd = (k * jnp.exp2(b_end - b)).astype(BF16)
        st_ref[hb] = st * jnp.exp2(b_end) + lax.dot_general(ii16, kd, TN_DIMS,
                                                            preferred_element_type=F32)

        att = jnp.where(col == row, jnp.sum(q * k, axis=-1, keepdims=True), 0.0)
        for d in range(1, HGRN_PAIRWISE):
            valid = (rowv % HGRN_PAIRWISE) >= d
            w = jnp.exp2(b - pltpu.roll(b, d, 0))
            p = jnp.where(valid, q * pltpu.roll(k, d, 0) * w, 0.0)
            att = att + jnp.where(col == row - d, jnp.sum(p, axis=-1, keepdims=True), 0.0)
        m = HGRN_PAIRWISE
        while m < chunk:
            grp = chunk // (2 * m)
            b3 = b.reshape(grp, 2 * m, A_DK)
            refrow = jnp.broadcast_to(b3[:, m - 1:m, :], (grp, 2 * m, A_DK)).reshape(chunk, A_DK)
            second = (rowv % (2 * m)) >= m
            ql = jnp.where(second, q * jnp.exp2(b - refrow), 0.0)
            kl = jnp.where(second, 0.0, k * jnp.exp2(refrow - b))
            a = lax.dot_general(ql.astype(BF16), kl.astype(BF16), NT_DIMS, preferred_element_type=F32)
            if grp > 1:
                a = jnp.where(row // (2 * m) == col // (2 * m), a, 0.0)
            att = att + a
            m *= 2
        o = o + jnp.dot(att.astype(BF16), ii16, preferred_element_type=F32)

        o = o * lax.rsqrt(jnp.mean(o * o, axis=-1, keepdims=True) + EPS)
        o_ref[:, sl] = ((o * gn_ref[:, sl]) * (gg * jax.nn.sigmoid(gg))).astype(o_ref.dtype)


def _hgrn_call(proj4, lb, gn, *, chunk=128, heads=8):
    _, t, _ = proj4.shape
    w = heads * A_DK

    def spec(kind):
        return pl.BlockSpec((1, chunk, w), lambda h, c, kind=kind: (kind, c, h))

    vec = pl.BlockSpec((1, w), lambda h, c: (0, h))
    return pl.pallas_call(
        functools.partial(_hgrn_kernel, chunk=chunk, heads=heads),
        grid=(A_HEADS // heads, t // chunk),
        in_specs=[spec(0), spec(1), spec(2), spec(3), vec, vec],
        out_specs=pl.BlockSpec((chunk, w), lambda h, c: (c, h)),
        out_shape=jax.ShapeDtypeStruct((t, D_A), BF16),
        scratch_shapes=[pltpu.VMEM((heads, A_DK, A_DK), F32)], name="hgrn2_scan",
        compiler_params=_cparams(2))(proj4, proj4, proj4, proj4, lb, gn)


def _compress_kernel(x_ref, pe_ref, w1_ref, b1_ref, w2_ref, b2_ref, tab_ref, o_ref):
    half = CMP_BLOCK * HEAD_DIM // 2
    x = x_ref[0, 0]
    top = jnp.dot((x + pe_ref[0, 0:1, :]).astype(BF16), w1_ref[0, :half, :], preferred_element_type=F32)
    bot = jnp.dot((x + pe_ref[0, 1:2, :]).astype(BF16), w1_ref[0, half:, :], preferred_element_type=F32)
    n = x.shape[0]
    h = jax.nn.gelu(top + pltpu.roll(bot, n - 1, 0) + b1_ref[0])
    y = jnp.dot(h.astype(BF16), w2_ref[0], preferred_element_type=F32) + b2_ref[0]
    y = _rope_tile(y, tab_ref)
    hi = y.astype(BF16)
    lo = (y - hi.astype(F32)).astype(BF16)
    o_ref[0, 0, :, 0:LANES] = (y + pltpu.roll(y, HEAD_DIM, 1)).astype(BF16)
    o_ref[0, 0, :, LANES:2 * LANES] = lo


def _compress_call(x16, pe2, w1, b1, w2p, b2p, tab):
    _, nh, n, wid = x16.shape
    hid = w1.shape[-1]
    return pl.pallas_call(
        _compress_kernel, grid=(2, nh),
        in_specs=[pl.BlockSpec((1, 1, n, wid), lambda kv, h: (kv, h, 0, 0)),
                  pl.BlockSpec((1, 2, wid), lambda kv, h: (kv, 0, 0)),
                  pl.BlockSpec((1, 2 * wid, hid), lambda kv, h: (kv, 0, 0)),
                  pl.BlockSpec((1, 1, hid), lambda kv, h: (kv, 0, 0)),
                  pl.BlockSpec((1, hid, LANES), lambda kv, h: (kv, 0, 0)),
                  pl.BlockSpec((1, 1, LANES), lambda kv, h: (kv, 0, 0)),
                  pl.BlockSpec((3, n, LANES), lambda kv, h: (kv, 0, 0))],
        out_specs=pl.BlockSpec((1, 1, n, 2 * LANES), lambda kv, h: (kv, h, 0, 0)),
        out_shape=jax.ShapeDtypeStruct((2, nh, n, 2 * LANES), BF16),
        name="compress_mlp", compiler_params=_cparams(2, VMEM_LIMIT))(x16, pe2, w1, b1, w2p, b2p, tab)


def _nsa_select_kernel(q_ref, kc_ref, vc_ref, wt_ref, ocmp_ref, bias_ref, lhs_ref):
    qt = pl.program_id(1)
    q0 = qt * Q_TILE
    rows = B_GROUP * Q_TILE
    n_cmp = kc_ref.shape[2]

    def chains(n_cols, n_blk):
        t_col = q0 + lax.broadcasted_iota(jnp.int32, (rows, 1), 0) % Q_TILE
        n_idx = lax.broadcasted_iota(jnp.int32, (rows, n_cols), 1)
        vis = n_idx <= (t_col - (CMP_BLOCK - 1)) // CMP_STRIDE
        blk = lax.broadcasted_iota(jnp.int32, (n_blk, Q_TILE), 0)
        tok = q0 + lax.broadcasted_iota(jnp.int32, (n_blk, Q_TILE), 1)
        cur = tok // SLC_BLOCK
        forced = (blk == 0) | (blk == cur) | (blk == cur - 1)
        candidate = (blk * SLC_BLOCK <= tok) & jnp.logical_not(forced)
        wt = wt_ref[0:n_blk, 0:n_cols]
        zeros64 = jnp.zeros((rows, HEAD_DIM), BF16)
        never = jnp.zeros((N_BLK_LANES - n_blk, Q_TILE), F32)

        for c in range(KV_PER_SELECT_STEP):
            for pair in range(B_GROUP // 2):
                qp = q_ref[c * (B_GROUP // 2) + pair]
                hi = qp.astype(BF16)
                lo = (qp - hi.astype(F32)).astype(BF16)
                for half in range(2):
                    r = slice((2 * pair + half) * Q_TILE, (2 * pair + half + 1) * Q_TILE)
                    ln = slice(half * HEAD_DIM, (half + 1) * HEAD_DIM)
                    lhs_ref[c, r, 0:HEAD_DIM] = hi[:, ln]
                    lhs_ref[c, r, HEAD_DIM:2 * HEAD_DIM] = lo[:, ln]
                    lhs_ref[c, r, 2 * HEAD_DIM:3 * HEAD_DIM] = hi[:, ln]
            lhs_ref[c, :, 3 * HEAD_DIM:] = zeros64
            s = lax.dot_general(lhs_ref[c], kc_ref[0, c, 0:n_cols, :], NT_DIMS, preferred_element_type=F32)
            s = jnp.where(vis, s, NEG_INF)
            e = jnp.exp2(s - jnp.max(s, axis=-1, keepdims=True))
            l = jnp.sum(e, axis=-1, keepdims=True)
            p = e * jnp.where(t_col >= CMP_BLOCK - 1, 1.0 / l, 0.0)
            o_cmp = jnp.dot(p.astype(BF16), vc_ref[0, c, 0:n_cols, 0:LANES], preferred_element_type=F32)
            for g in range(B_GROUP):
                c0 = (c * B_GROUP + g) * HEAD_DIM
                ocmp_ref[:, c0:c0 + HEAD_DIM] = o_cmp[g * Q_TILE:(g + 1) * Q_TILE, :HEAD_DIM]

            psum = p[0:Q_TILE] + p[Q_TILE:2 * Q_TILE] + p[2 * Q_TILE:3 * Q_TILE] + p[3 * Q_TILE:]
            ps_hi = psum.astype(BF16)
            ps_lo = (psum - ps_hi.astype(F32)).astype(BF16)
            pslc = (lax.dot_general(wt, ps_hi, NT_DIMS, preferred_element_type=F32)
                    + lax.dot_general(wt, ps_lo, NT_DIMS, preferred_element_type=F32))
            score = jnp.where(candidate, pslc, jnp.where(forced, -jnp.inf, NEG_INF))
            sel_t = forced.astype(F32)
            for _ in range(SLC_TOP_N - SLC_FORCED):
                best = jnp.max(score, axis=0, keepdims=True)
                first = jnp.min(jnp.where(score == best, blk, N_BLK_LANES), axis=0, keepdims=True)
                hit = blk == first
                score = jnp.where(hit, -jnp.inf, score)
                sel_t = jnp.where(hit, 1.0, sel_t)
            if n_blk < N_BLK_LANES:
                sel_t = jnp.concatenate([sel_t, never], axis=0)
            bias_ref[c] = jnp.where(sel_t.T > 0.0, 0.0, NEG_INF).astype(BF16)

    n_blk_all = n_cmp // (SLC_BLOCK // CMP_STRIDE)
    tiles_per_variant = pl.num_programs(1) // SELECT_VARIANTS
    for v in range(1, SELECT_VARIANTS + 1):
        pl.when(qt // tiles_per_variant == v - 1)(functools.partial(
            chains, n_cmp * v // SELECT_VARIANTS, n_blk_all * v // SELECT_VARIANTS))


def _nsa_select_call(q_hm, cmp_aug, wt):
    _, t, _ = q_hm.shape
    n_cmp = cmp_aug.shape[2]
    kvs = KV_PER_SELECT_STEP
    assert (t // Q_TILE) % SELECT_VARIANTS == 0 and (t // SLC_BLOCK) % (8 * SELECT_VARIANTS) == 0
    return pl.pallas_call(
        _nsa_select_kernel, grid=(B_KV_HEADS // kvs, t // Q_TILE),
        in_specs=[pl.BlockSpec((kvs * B_GROUP // 2, Q_TILE, LANES), lambda p, i: (p, i, 0)),
                  pl.BlockSpec((1, kvs, n_cmp, 2 * LANES), lambda p, i: (0, p, 0, 0)),
                  pl.BlockSpec((1, kvs, n_cmp, 2 * LANES), lambda p, i: (1, p, 0, 0)),
                  pl.BlockSpec((N_BLK_LANES, n_cmp), lambda p, i: (0, 0))],
        out_specs=[pl.BlockSpec((Q_TILE, kvs * B_GROUP * HEAD_DIM), lambda p, i: (i, p)),
                   pl.BlockSpec((kvs, Q_TILE, N_BLK_LANES), lambda p, i: (p, i, 0))],
        out_shape=[jax.ShapeDtypeStruct((t, D_B), F32),
                   jax.ShapeDtypeStruct((B_KV_HEADS, t, N_BLK_LANES), BF16)],
        scratch_shapes=[pltpu.VMEM((kvs, B_GROUP * Q_TILE, 2 * LANES), BF16)], name="nsa_select",
        compiler_params=_cparams(2, VMEM_LIMIT))(q_hm, cmp_aug, cmp_aug, wt)


def _nsa_attend_kernel(q_ref, bias_ref, ocmp_ref, gate_ref, gx_ref, ksv_ref, kwv_ref, vs_ref, vw_ref, o_ref,
                       lhs_w_ref, lhs_s_ref, m_ref, acc_ref, *, key_tile, tiles_per_group):
    q0 = pl.program_id(1) * ATT_TILE
    rows = B_GROUP * ATT_TILE
    chains = range(KV_PER_ATTEND_STEP)
    t_col = q0 + lax.broadcasted_iota(jnp.int32, (rows, 1), 0) % ATT_TILE

    for c in chains:
        for g in range(B_GROUP):
            r = slice(g * ATT_TILE, (g + 1) * ATT_TILE)
            qp = q_ref[c * (B_GROUP // 2) + g // 2]
            qhi = qp[:, (g % 2) * HEAD_DIM:(g % 2 + 1) * HEAD_DIM].astype(BF16)
            lhs_s_ref[c, r, 0:HEAD_DIM] = qhi
            lhs_s_ref[c, r, LANES:] = bias_ref[c]
            lhs_w_ref[c, r, 0:HEAD_DIM] = qhi
        lhs_s_ref[c, :, HEAD_DIM:LANES] = jnp.zeros((rows, LANES - HEAD_DIM), BF16)
        lhs_w_ref[c, :, HEAD_DIM:] = jnp.zeros((rows, 2 * LANES - HEAD_DIM), BF16)

    for c in chains:
        m_ref[c] = jnp.full((rows, LANES), NEG_INF, F32)
        acc_ref[c] = jnp.zeros((rows, LANES), F32)

    def tile(ti, causal):
        start = pl.multiple_of(ti * key_tile, key_tile)
        for c in chains:
            st = lax.dot_general(lhs_s_ref[c], ksv_ref[0, c, pl.ds(start, key_tile), :], NT_DIMS,
                                 preferred_element_type=F32)
            if causal:
                kp = start + lax.broadcasted_iota(jnp.int32, (rows, key_tile), 1)
                st = jnp.where(kp <= t_col, st, NEG_INF)
            m_run = m_ref[c]
            m_new = jnp.maximum(m_run, jnp.max(st, axis=-1, keepdims=True))
            pt = jnp.exp2(st - jnp.concatenate([m_new] * (key_tile // LANES), axis=1))
            pv = jnp.dot(pt.astype(BF16), vs_ref[0, c, pl.ds(start, key_tile), :], preferred_element_type=F32)
            acc_ref[c] = acc_ref[c] * jnp.exp2(m_run - m_new) + pv
            m_ref[c] = m_new

    def group(gi, _):
        for u in range(tiles_per_group):
            tile(gi * tiles_per_group + u, False)
        return 0

    def single(ti, _):
        tile(ti, False)
        return 0

    n_full = q0 // key_tile
    n_groups = n_full // tiles_per_group
    lax.fori_loop(0, n_groups, group, 0)
    lax.fori_loop(n_groups * tiles_per_group, n_full, single, 0)
    tile(n_full, True)

    slab = WINDOW + ATT_TILE
    ws = pl.multiple_of(jnp.maximum(q0 - WINDOW, 0), ATT_TILE)
    kpos = ws + lax.broadcasted_iota(jnp.int32, (rows, slab), 1)
    in_window = lax.bitcast_convert_type(t_col - kpos, jnp.uint32) < jnp.uint32(WINDOW)
    lane = lax.broadcasted_iota(jnp.int32, (ATT_TILE, LANES), 1)
    per_branch = B_GROUP * HEAD_DIM

    def head_pair(acc, k):
        a_even = acc[2 * k * ATT_TILE:(2 * k + 1) * ATT_TILE]
        a_odd = acc[(2 * k + 1) * ATT_TILE:(2 * k + 2) * ATT_TILE]
        num = jnp.where(lane < HEAD_DIM, a_even, pltpu.roll(a_odd, HEAD_DIM, 1))
        den = jnp.where(lane < HEAD_DIM, pltpu.roll(a_even, HEAD_DIM, 1), a_odd)
        return num * (1.0 / den)

    gates = gate_ref[...]
    g_hi = gates.astype(BF16)
    g_lo = (gates - g_hi.astype(F32)).astype(BF16)

    for c in chains:
        acc_s = acc_ref[c]
        s = lax.dot_general(lhs_w_ref[c], kwv_ref[0, c, pl.ds(ws, slab), :], NT_DIMS,
                            preferred_element_type=F32)
        s = jnp.where(in_window, s, NEG_INF)
        e = jnp.exp2(s - jnp.max(s, axis=-1, keepdims=True))
        acc_w = jnp.dot(e.astype(BF16), vw_ref[0, c, pl.ds(ws, slab), :], preferred_element_type=F32)

        gx = (jnp.dot(g_hi, gx_ref[c], preferred_element_type=F32)
              + jnp.dot(g_lo, gx_ref[c], preferred_element_type=F32))
        for k in range(B_GROUP // 2):
            c0 = c * B_GROUP * HEAD_DIM + k * LANES
            g0 = k * LANES
            og = (gx[:, g0:g0 + LANES] * ocmp_ref[:, c0:c0 + LANES]
                  + gx[:, per_branch + g0:per_branch + g0 + LANES] * head_pair(acc_s, k)
                  + gx[:, 2 * per_branch + g0:2 * per_branch + g0 + LANES] * head_pair(acc_w, k))
            o_ref[:, c0:c0 + LANES] = og.astype(o_ref.dtype)


def _nsa_attend_call(q_hm, bias, ocmp, gates, kv_aug, v_aug, *, key_tile=512, tiles_per_group=2):
    _, t, _ = q_hm.shape
    assert t % key_tile == 0 and key_tile % ATT_TILE == 0 and t % ATT_TILE == 0
    rows = B_GROUP * ATT_TILE
    kvs = KV_PER_ATTEND_STEP
    wide = kvs * B_GROUP * HEAD_DIM
    resident = dict(pipeline_mode=pl.Buffered(1))
    return pl.pallas_call(
        functools.partial(_nsa_attend_kernel, key_tile=key_tile, tiles_per_group=tiles_per_group),
        grid=(B_KV_HEADS // kvs, t // ATT_TILE),
        in_specs=[pl.BlockSpec((kvs * B_GROUP // 2, ATT_TILE, LANES), lambda p, i: (p, i, 0)),
                  pl.BlockSpec((kvs, ATT_TILE, N_BLK_LANES), lambda p, i: (p, i, 0)),
                  pl.BlockSpec((ATT_TILE, wide), lambda p, i: (i, p)),
                  pl.BlockSpec((ATT_TILE, GATE_LANES), lambda p, i: (i, 0)),
                  pl.BlockSpec((kvs, GATE_LANES, 3 * B_GROUP * HEAD_DIM), lambda p, i: (p, 0, 0)),
                  pl.BlockSpec((1, kvs, t, 2 * LANES), lambda p, i: (0, p, 0, 0), **resident),
                  pl.BlockSpec((1, kvs, t, 2 * LANES), lambda p, i: (1, p, 0, 0), **resident),
                  pl.BlockSpec((1, kvs, t, LANES), lambda p, i: (0, p, 0, 0), **resident),
                  pl.BlockSpec((1, kvs, t, LANES), lambda p, i: (1, p, 0, 0), **resident)],
        out_specs=pl.BlockSpec((ATT_TILE, wide), lambda p, i: (i, p)),
        out_shape=jax.ShapeDtypeStruct((t, D_B), BF16),
        scratch_shapes=[pltpu.VMEM((kvs, rows, 2 * LANES), BF16), pltpu.VMEM((kvs, rows, 2 * LANES), BF16),
                        pltpu.VMEM((kvs, rows, LANES), F32), pltpu.VMEM((kvs, rows, LANES), F32)],
        name="nsa_attend",
        compiler_params=_cparams(2, VMEM_LIMIT))(q_hm, bias, ocmp, gates, _gate_expand_matrix(),
                                                 kv_aug, kv_aug, v_aug, v_aug)


def _post_kernel(oa_ref, ob_ref, sg_ref, x_ref, wa_ref, wb_ref, wo_ref, g_ref, wg_ref, wu_ref, wd_ref,
                 o_ref, wg16_ref, wu16_ref, wd16_ref):
    ya = jnp.dot(oa_ref[...], wa_ref[...], preferred_element_type=F32)
    yb = jnp.dot(ob_ref[...], wb_ref[...], preferred_element_type=F32)
    merged = sg_ref[0].astype(F32) * ya + sg_ref[1].astype(F32) * yb
    mix = jnp.dot(merged.astype(BF16), wo_ref[...], preferred_element_type=F32)
    y = mix * lax.rsqrt(jnp.mean(mix * mix, axis=-1, keepdims=True) + EPS)
    o_ref[...] = x_ref[...] + y * g_ref[...]
    wg16_ref[...] = wg_ref[...].astype(BF16)
    wu16_ref[...] = wu_ref[...].astype(BF16)
    wd16_ref[...] = wd_ref[...].astype(BF16)


def _post_call(oa, ob, sg, x2, wa, wb, wo, g, wg, wu, wd, tm=256):
    t, d = x2.shape
    steps = t // tm
    dff = wg.shape[1]
    assert d % (steps * 16) == 0 and dff % (steps * 16) == 0
    const = dict(pipeline_mode=pl.Buffered(1))
    return pl.pallas_call(
        _post_kernel, grid=(steps,),
        in_specs=[pl.BlockSpec((tm, D_A), lambda i: (i, 0)),
                  pl.BlockSpec((tm, D_B), lambda i: (i, 0)),
                  pl.BlockSpec((2, tm, d), lambda i: (0, i, 0)),
                  pl.BlockSpec((tm, d), lambda i: (i, 0)),
                  pl.BlockSpec((D_A, d), lambda i: (0, 0), **const),
                  pl.BlockSpec((D_B, d), lambda i: (0, 0), **const),
                  pl.BlockSpec((d, d), lambda i: (0, 0), **const),
                  pl.BlockSpec((1, d), lambda i: (0, 0)),
                  pl.BlockSpec((d // steps, dff), lambda i: (i, 0)),
                  pl.BlockSpec((d // steps, dff), lambda i: (i, 0)),
                  pl.BlockSpec((dff // steps, d), lambda i: (i, 0))],
        out_specs=[pl.BlockSpec((tm, d), lambda i: (i, 0)),
                   pl.BlockSpec((d // steps, dff), lambda i: (i, 0)),
                   pl.BlockSpec((d // steps, dff), lambda i: (i, 0)),
                   pl.BlockSpec((dff // steps, d), lambda i: (i, 0))],
        out_shape=[jax.ShapeDtypeStruct((t, d), F32), jax.ShapeDtypeStruct((d, dff), BF16),
                   jax.ShapeDtypeStruct((d, dff), BF16), jax.ShapeDtypeStruct((dff, d), BF16)],
        name="mix_out_residual",
        compiler_params=_cparams(1, VMEM_LIMIT))(oa, ob, sg, x2, wa, wb, wo, g, wg, wu, wd)


def _ffn_kernel(h_ref, g1_ref, wg_ref, wu_ref, wd_ref, g2_ref, o_ref, hn_ref, acc_ref):
    f = pl.program_id(1)

    @pl.when(f == 0)
    def _():
        h = h_ref[...]
        y = h * lax.rsqrt(jnp.mean(h * h, axis=-1, keepdims=True) + EPS)
        hn_ref[...] = (y * g1_ref[...]).astype(BF16)
        acc_ref[...] = jnp.zeros_like(acc_ref)

    hn = hn_ref[...]
    a = jnp.dot(hn, wg_ref[...], preferred_element_type=F32)
    u = jnp.dot(hn, wu_ref[...], preferred_element_type=F32)
    z = (a * jax.nn.sigmoid(a) * u).astype(BF16)
    acc_ref[...] += jnp.dot(z, wd_ref[...], preferred_element_type=F32)

    @pl.when(f == pl.num_programs(1) - 1)
    def _():
        ff = acc_ref[...]
        y = ff * lax.rsqrt(jnp.mean(ff * ff, axis=-1, keepdims=True) + EPS)
        o_ref[...] = h_ref[...] + y * g2_ref[...]


def _ffn_call(h1, g1, wg, wu, wd, g2, tm=512, tf=512):
    t, d = h1.shape
    dff = wg.shape[1]
    return pl.pallas_call(
        _ffn_kernel, grid=(t // tm, dff // tf),
        in_specs=[pl.BlockSpec((tm, d), lambda i, f: (i, 0)),
                  pl.BlockSpec((1, d), lambda i, f: (0, 0)),
                  pl.BlockSpec((d, tf), lambda i, f: (0, f)),
                  pl.BlockSpec((d, tf), lambda i, f: (0, f)),
                  pl.BlockSpec((tf, d), lambda i, f: (f, 0)),
                  pl.BlockSpec((1, d), lambda i, f: (0, 0))],
        out_specs=pl.BlockSpec((tm, d), lambda i, f: (i, 0)),
        out_shape=jax.ShapeDtypeStruct((t, d), F32),
        scratch_shapes=[pltpu.VMEM((tm, d), BF16), pltpu.VMEM((tm, d), F32)], name="swiglu_ffn",
        compiler_params=_cparams(2, VMEM_LIMIT))(h1, g1, wg, wu, wd, g2)


def _rope_tables(pos, period):
    half = ROPE_DIM // 2
    inv = jnp.float32(ROPE_THETA) ** (-jnp.arange(half, dtype=F32) * 2.0 / ROPE_DIM)
    r = np.arange(LANES) % period
    pick = np.zeros((half, LANES), np.float32)
    pick[r % half, np.arange(LANES)] = 1.0
    inv_lane = jnp.sum(inv[:, None] * pick, axis=0)
    ang = pos.astype(F32)[:, None] * inv_lane[None, :]
    cos, sin = jnp.cos(ang), jnp.sin(ang)
    lo = jnp.asarray((r < half)[None, :])
    hi = jnp.asarray(((r >= half) & (r < ROPE_DIM))[None, :])
    c = jnp.where(lo | hi, cos, 1.0)
    sa = jnp.where(lo, -sin, 0.0)
    sb = jnp.where(hi, sin, 0.0)
    return jnp.stack([c, sa, sb])


def _gate_expand_matrix():
    e = np.zeros((B_KV_HEADS, GATE_LANES, 3 * B_GROUP * HEAD_DIM), np.float32)
    for kvh in range(B_KV_HEADS):
        for g in range(B_GROUP):
            for b in range(3):
                c0 = (b * B_GROUP + g) * HEAD_DIM
                e[kvh, 3 * (kvh * B_GROUP + g) + b, c0:c0 + HEAD_DIM] = 1.0
    return jnp.asarray(e, BF16)


def _slc_weight_matrix(n_cmp_pad):
    ratio = SLC_BLOCK // CMP_STRIDE
    w = np.zeros((N_BLK_LANES, n_cmp_pad), np.float32)
    for j in range(N_BLK_LANES):
        for o, wv in enumerate(SLC_WEIGHTS):
            n = ratio * j + o - 1
            if 0 <= n < n_cmp_pad - 1:
                w[j, n] = wv
    return jnp.asarray(w, BF16)


def kernel(x, pre_mix_g, w_in, lb_table, a_norm_g, cmp_pe_k, cmp_w1_k, cmp_b1_k, cmp_w2_k, cmp_b2_k, cmp_pe_v, cmp_w1_v, cmp_b1_v, cmp_w2_v, cmp_b2_v, w_proj_a, w_proj_b, w_out, post_mix_g, pre_ffn_g, w_gate, w_up, w_down, post_ffn_g):
    bsz, t, d = x.shape
    assert bsz == 1 and d == D_MODEL and WINDOW + ATT_TILE <= t <= N_BLK_LANES * SLC_BLOCK
    x2 = x.reshape(t, d)
    n_cmp_pad = t // CMP_STRIDE
    assert n_cmp_pad % LANES == 0

    lb = jnp.cumsum(jax.nn.softmax(lb_table.astype(F32), axis=0), axis=0)[0].reshape(1, D_A)
    o_q, o_kc, o_ks, o_g, o_m = 4 * D_A, 4 * D_A + D_B, 4 * D_A + D_B + 2 * KV_DIM, 4 * D_A + D_B + 6 * KV_DIM, 4 * D_A + D_B + 6 * KV_DIM + 3 * B_HEADS
    w_rows = jnp.swapaxes(w_in[0], 0, 1)
    w = w_rows[:o_m].astype(BF16)
    w_m = w_rows[o_m:].astype(BF16)
    w_kv = (w[o_ks:o_g].reshape(2, 2, B_KV_HEADS, HEAD_DIM, d).transpose(0, 2, 1, 3, 4)
            .reshape(4 * KV_DIM, d))
    w_g = jnp.pad(w[o_g:o_m], ((0, GATE_LANES - 3 * B_HEADS), (0, 0)))

    pos = jnp.arange(t)
    tab_pos = _rope_tables(pos, HEAD_DIM)
    end_pos = jnp.arange(n_cmp_pad) * CMP_STRIDE + (CMP_BLOCK - 1)
    ident = jnp.stack([jnp.ones((n_cmp_pad, LANES), F32), jnp.zeros((n_cmp_pad, LANES), F32),
                       jnp.zeros((n_cmp_pad, LANES), F32)])
    tab_cmp = jnp.concatenate([_rope_tables(end_pos, LANES), ident], axis=0)

    xn = _rmsnorm_call(x2, pre_mix_g[0].reshape(1, d))

    tm = 1024 if t % 1024 == 0 else 256
    proj4 = _proj_call(
        "proj_hgrn", xn, w, w_rows=(0, o_q), tm=tm, tn=D_A, epilogue=_ep_plain,
        out_shape=jax.ShapeDtypeStruct((4, t, D_A), F32),
        out_specs=pl.BlockSpec((1, tm, D_A), lambda i, j: (j, i, 0)))
    q_hm = _proj_call(
        "proj_q", xn, w, w_rows=(o_q, D_B), tm=tm, tn=512, epilogue=_ep_q,
        out_shape=jax.ShapeDtypeStruct((B_HEADS // 2, t, LANES), F32),
        out_specs=pl.BlockSpec((512 // LANES, tm, LANES), lambda i, j: (j, i, 0)),
        extra=(tab_pos,), extra_specs=(pl.BlockSpec((3, tm, LANES), lambda i, j: (0, i, 0)),))
    cmp_in = _proj_call(
        "proj_cmp", xn, w, w_rows=(o_kc, 2 * KV_DIM), tm=tm, tn=2 * KV_DIM, epilogue=_ep_cmp,
        out_shape=jax.ShapeDtypeStruct((2, B_KV_HEADS, t, HEAD_DIM), F32),
        out_specs=pl.BlockSpec((2, B_KV_HEADS, tm, HEAD_DIM), lambda i, j: (0, 0, i, 0)))
    kv_aug, v_aug = _proj_call(
        "proj_kv", xn, w_kv, tm=tm, tn=2 * KV_DIM, epilogue=functools.partial(_ep_kv, tm=tm),
        out_shape=[jax.ShapeDtypeStruct((2, B_KV_HEADS, t, 2 * LANES), BF16),
                   jax.ShapeDtypeStruct((2, B_KV_HEADS, t, LANES), BF16)],
        out_specs=[pl.BlockSpec((1, B_KV_HEADS, tm, 2 * LANES), lambda i, j: (j, 0, i, 0)),
                   pl.BlockSpec((1, B_KV_HEADS, tm, LANES), lambda i, j: (j, 0, i, 0))],
        extra=(tab_pos,), extra_specs=(pl.BlockSpec((3, tm, LANES), lambda i, j: (0, i, 0)),))
    gates = _proj_call(
        "proj_gate", xn, w_g, tm=tm, tn=GATE_LANES, epilogue=_ep_gate,
        out_shape=jax.ShapeDtypeStruct((t, GATE_LANES), F32),
        out_specs=pl.BlockSpec((tm, GATE_LANES), lambda i, j: (i, 0)))
    sg, wa16, wb16, wo16 = _proj_call(
        "proj_merge", xn, w_m, tm=tm, tn=1024, epilogue=_ep_sigmoid,
        out_shape=jax.ShapeDtypeStruct((2, t, d), BF16),
        out_specs=pl.BlockSpec((1, tm, 1024), lambda i, j: (j // 2, i, j % 2)),
        to_bf16=(w_proj_a[0], w_proj_b[0], w_out[0]))

    oa = _hgrn_call(proj4, lb, a_norm_g[0].reshape(1, D_A))

    half = CMP_BLOCK // 2
    x16 = cmp_in.reshape(2, B_KV_HEADS, n_cmp_pad, CMP_STRIDE * HEAD_DIM)
    pe2 = jnp.stack([cmp_pe_k[0], cmp_pe_v[0]]).reshape(2, 2, half * HEAD_DIM)
    w1 = jnp.stack([cmp_w1_k[0], cmp_w1_v[0]]).astype(BF16)
    b1 = jnp.stack([cmp_b1_k[0], cmp_b1_v[0]]).reshape(2, 1, CMP_HIDDEN)
    w2p = jnp.pad(jnp.stack([cmp_w2_k[0], cmp_w2_v[0]]), ((0, 0), (0, 0), (0, LANES - HEAD_DIM))).astype(BF16)
    b2p = jnp.pad(jnp.stack([cmp_b2_k[0], cmp_b2_v[0]]), ((0, 0), (0, LANES - HEAD_DIM))).reshape(2, 1, LANES)
    cmp_aug = _compress_call(x16, pe2, w1, b1, w2p, b2p, tab_cmp)
    ocmp, bias = _nsa_select_call(q_hm, cmp_aug, _slc_weight_matrix(n_cmp_pad))
    ob = _nsa_attend_call(q_hm, bias, ocmp, gates, kv_aug, v_aug)

    h1, wg16, wu16, wd16 = _post_call(oa, ob, sg, x2, wa16, wb16, wo16, post_mix_g[0].reshape(1, d),
                                      w_gate[0], w_up[0], w_down[0])
    out = _ffn_call(h1, pre_ffn_g[0].reshape(1, d), wg16, wu16, wd16, post_ffn_g[0].reshape(1, d))
    return out.reshape(bsz, t, d)
```

```python
import functools

import numpy as np
import jax
import jax.numpy as jnp
from jax import lax
from jax.experimental import pallas as pl
from jax.experimental.pallas import tpu as pltpu

F32 = jnp.float32
BF16 = jnp.bfloat16

D_MODEL = 2048
D_A = 1024
D_B = 1024
A_HEADS = 8
A_DK = 128
HGRN_PAIRWISE = 4
B_HEADS = 16
B_KV_HEADS = 4
B_GROUP = 4
HEAD_DIM = 64
KV_DIM = 256
CMP_BLOCK = 32
CMP_STRIDE = 16
CMP_HIDDEN = 256
SLC_BLOCK = 64
SLC_TOP_N = 16
SLC_FORCED = 3
SLC_WEIGHTS = (1.0, 2.0, 2.0, 2.0, 1.0)
WINDOW = 512
Q_TILE = 512
ATT_TILE = 256
KV_PER_ATTEND_STEP = 2
KV_PER_SELECT_STEP = 2
SELECT_VARIANTS = 4
PROJ_SUBTILE = 256
GATE_LANES = 256
ROPE_THETA = 500000.0
ROPE_DIM = 16
D_FF = 5632
EPS = 1e-6
LOG2_E = 1.4426950408889634
Q_SCALE = HEAD_DIM ** -0.5 * LOG2_E
NEG_INF = -1e30
N_BLK_LANES = 128
LANES = 128
VMEM_LIMIT = 56 * 1024 * 1024

NT_DIMS = (((1,), (1,)), ((), ()))
TN_DIMS = (((0,), (0,)), ((), ()))


def _cparams(n_axes, vmem=None):
    return pltpu.CompilerParams(dimension_semantics=("arbitrary",) * n_axes,
                                vmem_limit_bytes=vmem)


def _rmsnorm_kernel(x_ref, g_ref, o_ref):
    x = x_ref[...]
    y = x * lax.rsqrt(jnp.mean(x * x, axis=-1, keepdims=True) + EPS)
    o_ref[...] = (y * g_ref[...]).astype(o_ref.dtype)


def _rmsnorm_call(x2, g, tm=512):
    t, d = x2.shape
    return pl.pallas_call(
        _rmsnorm_kernel, grid=(t // tm,),
        in_specs=[pl.BlockSpec((tm, d), lambda i: (i, 0)), pl.BlockSpec((1, d), lambda i: (0, 0))],
        out_specs=pl.BlockSpec((tm, d), lambda i: (i, 0)),
        out_shape=jax.ShapeDtypeStruct((t, d), BF16), name="pre_mix_rmsnorm",
        compiler_params=_cparams(1))(x2, g)


def _proj_call(name, xn, w_t, *, tm, tn, epilogue, out_shape, out_specs, extra=(), extra_specs=(),
               w_rows=None, to_bf16=(), norm_gain=None):
    t, k = xn.shape
    row0, n = (0, w_t.shape[0]) if w_rows is None else w_rows
    assert tn % PROJ_SUBTILE == 0 and n % tn == 0 and row0 % tn == 0
    grid = (t // tm, n // tn)
    steps = grid[0] * grid[1]
    n_extra, n_cast = len(extra), len(to_bf16)
    out_specs = list(out_specs) if isinstance(out_specs, (list, tuple)) else [out_specs]
    out_shape = list(out_shape) if isinstance(out_shape, (list, tuple)) else [out_shape]
    n_out = len(out_specs)

    def body(x_ref, w_ref, *rest):
        if norm_gain is not None:
            g_ref, xn_out, xs_ref = rest[0], rest[-2], rest[-1]
            rest = rest[1:-2]

            @pl.when(pl.program_id(1) == 0)
            def _():
                xf = x_ref[...]
                y = xf * lax.rsqrt(jnp.mean(xf * xf, axis=-1, keepdims=True) + EPS)
                xs_ref[...] = (y * g_ref[...]).astype(BF16)
                xn_out[...] = xs_ref[...]

            x = xs_ref[...]
        else:
            x = x_ref[...]
        extra_refs = rest[:n_extra]
        cast_in = rest[n_extra:n_extra + n_cast]
        outs = rest[n_extra + n_cast:n_extra + n_cast + n_out]
        cast_out = rest[n_extra + n_cast + n_out:]
        for c0 in range(0, tn, PROJ_SUBTILE):
            acc = lax.dot_general(x, w_ref[c0:c0 + PROJ_SUBTILE, :], NT_DIMS, preferred_element_type=F32)
            epilogue(acc, c0, extra_refs, outs)
        for src, dst in zip(cast_in, cast_out):
            dst[...] = src[...].astype(BF16)

    def slab(a):
        rows = a.shape[0] // steps
        assert a.shape[0] % steps == 0 and rows % 16 == 0
        return pl.BlockSpec((rows, a.shape[1]), lambda i, j: (i * grid[1] + j, 0))

    normed = norm_gain is not None
    row_tile = pl.BlockSpec((tm, k), lambda i, j: (i, 0))
    res = pl.pallas_call(
        body, grid=grid,
        in_specs=[row_tile, pl.BlockSpec((tn, k), lambda i, j: (row0 // tn + j, 0)),
                  *([pl.BlockSpec((1, k), lambda i, j: (0, 0))] if normed else []), *extra_specs,
                  *[slab(a) for a in to_bf16]],
        out_specs=out_specs + [slab(a) for a in to_bf16] + ([row_tile] if normed else []),
        out_shape=(out_shape + [jax.ShapeDtypeStruct(a.shape, BF16) for a in to_bf16]
                   + ([jax.ShapeDtypeStruct((t, k), BF16)] if normed else [])),
        scratch_shapes=[pltpu.VMEM((tm, k), BF16)] if normed else [], name=name,
        compiler_params=_cparams(2, VMEM_LIMIT))(xn, w_t, *([norm_gain] if normed else []), *extra, *to_bf16)
    return res[0] if len(res) == 1 else res


def _rope_tile(a, tab_ref):
    return (a * tab_ref[0] + pltpu.roll(a, LANES - ROPE_DIM // 2, 1) * tab_ref[1]
            + pltpu.roll(a, ROPE_DIM // 2, 1) * tab_ref[2])


def _ep_plain(acc, c0, extra, outs):
    outs[0][0, :, c0:c0 + acc.shape[1]] = acc.astype(outs[0].dtype)


def _ep_q(acc, c0, extra, outs):
    for pair in range(acc.shape[1] // LANES):
        outs[0][c0 // LANES + pair] = _rope_tile(acc[:, pair * LANES:(pair + 1) * LANES], extra[0]) * Q_SCALE


def _ep_cmp(acc, c0, extra, outs):
    for hh in range(acc.shape[1] // HEAD_DIM):
        head = c0 // HEAD_DIM + hh
        outs[0][head // B_KV_HEADS, head % B_KV_HEADS] = acc[:, hh * HEAD_DIM:(hh + 1) * HEAD_DIM]


def _ep_kv(acc, c0, extra, outs, *, tm):
    i = pl.program_id(0)
    j = pl.program_id(1)
    rowg = i * tm + lax.broadcasted_iota(jnp.int32, (tm, LANES), 0)
    lane = lax.broadcasted_iota(jnp.int32, (tm, LANES), 1)
    aux = jnp.where(j == 0, (rowg // SLC_BLOCK == lane).astype(F32), 0.0).astype(BF16)
    for hh in range(acc.shape[1] // LANES):
        h = c0 // LANES + hh
        a = acc[:, hh * LANES:(hh + 1) * LANES]
        r = jnp.where(lane < HEAD_DIM, _rope_tile(a, extra[0]), a)
        outs[0][0, h, :, 0:LANES] = r.astype(BF16)
        outs[0][0, h, :, LANES:2 * LANES] = aux
        outs[1][0, h] = jnp.where(lane < HEAD_DIM, pltpu.roll(r, HEAD_DIM, 1), 1.0).astype(BF16)


def _ep_gate(acc, c0, extra, outs):
    outs[0][:, c0:c0 + acc.shape[1]] = jax.nn.sigmoid(acc)


def _ep_sigmoid(acc, c0, extra, outs):
    outs[0][0, :, c0:c0 + acc.shape[1]] = jax.nn.sigmoid(acc).astype(outs[0].dtype)


def _split3(x):
    hi = x.astype(BF16)
    r1 = x - hi.astype(F32)
    mid = r1.astype(BF16)
    lo = (r1 - mid.astype(F32)).astype(BF16)
    return hi, mid, lo


def _hgrn_kernel(q_ref, f_ref, i_ref, g_ref, lb_ref, gn_ref, o_ref, st_ref, *, chunk, heads):
    c = pl.program_id(1)

    @pl.when(c == 0)
    def _():
        st_ref[...] = jnp.zeros_like(st_ref)

    row = lax.broadcasted_iota(jnp.int32, (chunk, chunk), 0)
    col = lax.broadcasted_iota(jnp.int32, (chunk, chunk), 1)
    tri = (col <= row).astype(BF16)
    rowv = lax.broadcasted_iota(jnp.int32, (chunk, A_DK), 0)

    for hb in range(heads):
        sl = slice(hb * A_DK, (hb + 1) * A_DK)
        q = q_ref[0, :, sl]
        ii = i_ref[0, :, sl]
        gg = g_ref[0, :, sl]
        lbv = lb_ref[:, sl]
        f = lbv + (1.0 - lbv) * jax.nn.sigmoid(f_ref[0, :, sl])
        lf = jnp.log(f)
        k = 1.0 - f
        b = sum(jnp.dot(tri, p, preferred_element_type=F32) for p in _split3(lf)) * LOG2_E
        b_end = b[chunk - 1:chunk, :]
        ii16 = ii.astype(BF16)

        st = st_ref[hb]
        o = lax.dot_general((q * jnp.exp2(b)).astype(BF16), st.astype(BF16), NT_DIMS,
                            preferred_element_type=F32)
        kd = (k * jnp.exp2(b_end - b)).astype(BF16)
        st_ref[hb] = st * jnp.exp2(b_end) + lax.dot_general(ii16, kd, TN_DIMS,
                                                            preferred_element_type=F32)

        att = jnp.where(col == row, jnp.sum(q * k, axis=-1, keepdims=True), 0.0)
        for d in range(1, HGRN_PAIRWISE):
            valid = (rowv % HGRN_PAIRWISE) >= d
            w = jnp.exp2(b - pltpu.roll(b, d, 0))
            p = jnp.where(valid, q * pltpu.roll(k, d, 0) * w, 0.0)
            att = att + jnp.where(col == row - d, jnp.sum(p, axis=-1, keepdims=True), 0.0)
        m = HGRN_PAIRWISE
        while m < chunk:
            grp = chunk // (2 * m)
            b3 = b.reshape(grp, 2 * m, A_DK)
            refrow = jnp.broadcast_to(b3[:, m - 1:m, :], (grp, 2 * m, A_DK)).reshape(chunk, A_DK)
            second = (rowv % (2 * m)) >= m
            ql = jnp.where(second, q * jnp.exp2(b - refrow), 0.0)
            kl = jnp.where(second, 0.0, k * jnp.exp2(refrow - b))
            a = lax.dot_general(ql.astype(BF16), kl.astype(BF16), NT_DIMS, preferred_element_type=F32)
            if grp > 1:
                a = jnp.where(row // (2 * m) == col // (2 * m), a, 0.0)
            att = att + a
            m *= 2
        o = o + jnp.dot(att.astype(BF16), ii16, preferred_element_type=F32)

        o = o * lax.rsqrt(jnp.mean(o * o, axis=-1, keepdims=True) + EPS)
        o_ref[:, sl] = ((o * gn_ref[:, sl]) * (gg * jax.nn.sigmoid(gg))).astype(o_ref.dtype)


def _hgrn_call(proj4, lb, gn, *, chunk=128, heads=8):
    _, t, _ = proj4.shape
    w = heads * A_DK

    def spec(kind):
        return pl.BlockSpec((1, chunk, w), lambda h, c, kind=kind: (kind, c, h))

    vec = pl.BlockSpec((1, w), lambda h, c: (0, h))
    return pl.pallas_call(
        functools.partial(_hgrn_kernel, chunk=chunk, heads=heads),
        grid=(A_HEADS // heads, t // chunk),
        in_specs=[spec(0), spec(1), spec(2), spec(3), vec, vec],
        out_specs=pl.BlockSpec((chunk, w), lambda h, c: (c, h)),
        out_shape=jax.ShapeDtypeStruct((t, D_A), BF16),
        scratch_shapes=[pltpu.VMEM((heads, A_DK, A_DK), F32)], name="hgrn2_scan",
        compiler_params=_cparams(2))(proj4, proj4, proj4, proj4, lb, gn)


def _compress_kernel(x_ref, pe_ref, w1_ref, b1_ref, w2_ref, b2_ref, tab_ref, o_ref):
    half = CMP_BLOCK * HEAD_DIM // 2
    x = x_ref[0, 0]
    top = jnp.dot((x + pe_ref[0, 0:1, :]).astype(BF16), w1_ref[0, :half, :], preferred_element_type=F32)
    bot = jnp.dot((x + pe_ref[0, 1:2, :]).astype(BF16), w1_ref[0, half:, :], preferred_element_type=F32)
    n = x.shape[0]
    h = jax.nn.gelu(top + pltpu.roll(bot, n - 1, 0) + b1_ref[0])
    y = jnp.dot(h.astype(BF16), w2_ref[0], preferred_element_type=F32) + b2_ref[0]
    y = _rope_tile(y, tab_ref)
    hi = y.astype(BF16)
    lo = (y - hi.astype(F32)).astype(BF16)
    o_ref[0, 0, :, 0:LANES] = (y + pltpu.roll(y, HEAD_DIM, 1)).astype(BF16)
    o_ref[0, 0, :, LANES:2 * LANES] = lo


def _compress_call(x16, pe2, w1, b1, w2p, b2p, tab):
    _, nh, n, wid = x16.shape
    hid = w1.shape[-1]
    return pl.pallas_call(
        _compress_kernel, grid=(2, nh),
        in_specs=[pl.BlockSpec((1, 1, n, wid), lambda kv, h: (kv, h, 0, 0)),
                  pl.BlockSpec((1, 2, wid), lambda kv, h: (kv, 0, 0)),
                  pl.BlockSpec((1, 2 * wid, hid), lambda kv, h: (kv, 0, 0)),
                  pl.BlockSpec((1, 1, hid), lambda kv, h: (kv, 0, 0)),
                  pl.BlockSpec((1, hid, LANES), lambda kv, h: (kv, 0, 0)),
                  pl.BlockSpec((1, 1, LANES), lambda kv, h: (kv, 0, 0)),
                  pl.BlockSpec((3, n, LANES), lambda kv, h: (kv, 0, 0))],
        out_specs=pl.BlockSpec((1, 1, n, 2 * LANES), lambda kv, h: (kv, h, 0, 0)),
        out_shape=jax.ShapeDtypeStruct((2, nh, n, 2 * LANES), BF16),
        name="compress_mlp", compiler_params=_cparams(2, VMEM_LIMIT))(x16, pe2, w1, b1, w2p, b2p, tab)


def _nsa_select_kernel(q_ref, kc_ref, vc_ref, wt_ref, ocmp_ref, bias_ref, lhs_ref):
    qt = pl.program_id(1)
    q0 = qt * Q_TILE
    rows = B_GROUP * Q_TILE
    n_cmp = kc_ref.shape[2]

    def chains(n_cols, n_blk):
        t_col = q0 + lax.broadcasted_iota(jnp.int32, (rows, 1), 0) % Q_TILE
        n_idx = lax.broadcasted_iota(jnp.int32, (rows, n_cols), 1)
        vis = n_idx <= (t_col - (CMP_BLOCK - 1)) // CMP_STRIDE
        blk = lax.broadcasted_iota(jnp.int32, (n_blk, Q_TILE), 0)
        tok = q0 + lax.broadcasted_iota(jnp.int32, (n_blk, Q_TILE), 1)
        cur = tok // SLC_BLOCK
        forced = (blk == 0) | (blk == cur) | (blk == cur - 1)
        candidate = (blk * SLC_BLOCK <= tok) & jnp.logical_not(forced)
        wt = wt_ref[0:n_blk, 0:n_cols]
        zeros64 = jnp.zeros((rows, HEAD_DIM), BF16)
        never = jnp.zeros((N_BLK_LANES - n_blk, Q_TILE), F32)

        for c in range(KV_PER_SELECT_STEP):
            for pair in range(B_GROUP // 2):
                qp = q_ref[c * (B_GROUP // 2) + pair]
                hi = qp.astype(BF16)
                lo = (qp - hi.astype(F32)).astype(BF16)
                for half in range(2):
                    r = slice((2 * pair + half) * Q_TILE, (2 * pair + half + 1) * Q_TILE)
                    ln = slice(half * HEAD_DIM, (half + 1) * HEAD_DIM)
                    lhs_ref[c, r, 0:HEAD_DIM] = hi[:, ln]
                    lhs_ref[c, r, HEAD_DIM:2 * HEAD_DIM] = lo[:, ln]
                    lhs_ref[c, r, 2 * HEAD_DIM:3 * HEAD_DIM] = hi[:, ln]
            lhs_ref[c, :, 3 * HEAD_DIM:] = zeros64
            s = lax.dot_general(lhs_ref[c], kc_ref[0, c, 0:n_cols, :], NT_DIMS, preferred_element_type=F32)
            s = jnp.where(vis, s, NEG_INF)
            e = jnp.exp2(s - jnp.max(s, axis=-1, keepdims=True))
            l = jnp.sum(e, axis=-1, keepdims=True)
            p = e * jnp.where(t_col >= CMP_BLOCK - 1, 1.0 / l, 0.0)
            o_cmp = jnp.dot(p.astype(BF16), vc_ref[0, c, 0:n_cols, 0:LANES], preferred_element_type=F32)
            for g in range(B_GROUP):
                c0 = (c * B_GROUP + g) * HEAD_DIM
                ocmp_ref[:, c0:c0 + HEAD_DIM] = o_cmp[g * Q_TILE:(g + 1) * Q_TILE, :HEAD_DIM]

            psum = p[0:Q_TILE] + p[Q_TILE:2 * Q_TILE] + p[2 * Q_TILE:3 * Q_TILE] + p[3 * Q_TILE:]
            ps_hi = psum.astype(BF16)
            ps_lo = (psum - ps_hi.astype(F32)).astype(BF16)
            pslc = (lax.dot_general(wt, ps_hi, NT_DIMS, preferred_element_type=F32)
                    + lax.dot_general(wt, ps_lo, NT_DIMS, preferred_element_type=F32))
            score = jnp.where(candidate, pslc, jnp.where(forced, -jnp.inf, NEG_INF))
            sel_t = forced.astype(F32)
            for _ in range(SLC_TOP_N - SLC_FORCED):
                best = jnp.max(score, axis=0, keepdims=True)
                first = jnp.min(jnp.where(score == best, blk, N_BLK_LANES), axis=0, keepdims=True)
                hit = blk == first
                score = jnp.where(hit, -jnp.inf, score)
                sel_t = jnp.where(hit, 1.0, sel_t)
            if n_blk < N_BLK_LANES:
                sel_t = jnp.concatenate([sel_t, never], axis=0)
            bias_ref[c] = jnp.where(sel_t.T > 0.0, 0.0, NEG_INF).astype(BF16)

    n_blk_all = n_cmp // (SLC_BLOCK // CMP_STRIDE)
    tiles_per_variant = pl.num_programs(1) // SELECT_VARIANTS
    for v in range(1, SELECT_VARIANTS + 1):
        pl.when(qt // tiles_per_variant == v - 1)(functools.partial(
            chains, n_cmp * v // SELECT_VARIANTS, n_blk_all * v // SELECT_VARIANTS))


def _nsa_select_call(q_hm, cmp_aug, wt):
    _, t, _ = q_hm.shape
    n_cmp = cmp_aug.shape[2]
    kvs = KV_PER_SELECT_STEP
    assert (t // Q_TILE) % SELECT_VARIANTS == 0 and (t // SLC_BLOCK) % (8 * SELECT_VARIANTS) == 0
    return pl.pallas_call(
        _nsa_select_kernel, grid=(B_KV_HEADS // kvs, t // Q_TILE),
        in_specs=[pl.BlockSpec((kvs * B_GROUP // 2, Q_TILE, LANES), lambda p, i: (p, i, 0)),
                  pl.BlockSpec((1, kvs, n_cmp, 2 * LANES), lambda p, i: (0, p, 0, 0)),
                  pl.BlockSpec((1, kvs, n_cmp, 2 * LANES), lambda p, i: (1, p, 0, 0)),
                  pl.BlockSpec((N_BLK_LANES, n_cmp), lambda p, i: (0, 0))],
        out_specs=[pl.BlockSpec((Q_TILE, kvs * B_GROUP * HEAD_DIM), lambda p, i: (i, p)),
                   pl.BlockSpec((kvs, Q_TILE, N_BLK_LANES), lambda p, i: (p, i, 0))],
        out_shape=[jax.ShapeDtypeStruct((t, D_B), F32),
                   jax.ShapeDtypeStruct((B_KV_HEADS, t, N_BLK_LANES), BF16)],
        scratch_shapes=[pltpu.VMEM((kvs, B_GROUP * Q_TILE, 2 * LANES), BF16)], name="nsa_select",
        compiler_params=_cparams(2, VMEM_LIMIT))(q_hm, cmp_aug, cmp_aug, wt)


def _nsa_attend_kernel(q_ref, bias_ref, ocmp_ref, gate_ref, gx_ref, ksv_ref, kwv_ref, vs_ref, vw_ref, o_ref,
                       lhs_w_ref, lhs_s_ref, m_ref, acc_ref, *, key_tile, tiles_per_group):
    q0 = pl.program_id(1) * ATT_TILE
    rows = B_GROUP * ATT_TILE
    chains = range(KV_PER_ATTEND_STEP)
    t_col = q0 + lax.broadcasted_iota(jnp.int32, (rows, 1), 0) % ATT_TILE

    for c in chains:
        for g in range(B_GROUP):
            r = slice(g * ATT_TILE, (g + 1) * ATT_TILE)
            qp = q_ref[c * (B_GROUP // 2) + g // 2]
            qhi = qp[:, (g % 2) * HEAD_DIM:(g % 2 + 1) * HEAD_DIM].astype(BF16)
            lhs_s_ref[c, r, 0:HEAD_DIM] = qhi
            lhs_s_ref[c, r, LANES:] = bias_ref[c]
            lhs_w_ref[c, r, 0:HEAD_DIM] = qhi
        lhs_s_ref[c, :, HEAD_DIM:LANES] = jnp.zeros((rows, LANES - HEAD_DIM), BF16)
        lhs_w_ref[c, :, HEAD_DIM:] = jnp.zeros((rows, 2 * LANES - HEAD_DIM), BF16)

    for c in chains:
        m_ref[c] = jnp.full((rows, LANES), NEG_INF, F32)
        acc_ref[c] = jnp.zeros((rows, LANES), F32)

    def tile(ti, causal):
        start = pl.multiple_of(ti * key_tile, key_tile)
        for c in chains:
            st = lax.dot_general(lhs_s_ref[c], ksv_ref[0, c, pl.ds(start, key_tile), :], NT_DIMS,
                                 preferred_element_type=F32)
            if causal:
                kp = start + lax.broadcasted_iota(jnp.int32, (rows, key_tile), 1)
                st = jnp.where(kp <= t_col, st, NEG_INF)
            m_run = m_ref[c]
            m_new = jnp.maximum(m_run, jnp.max(st, axis=-1, keepdims=True))
            pt = jnp.exp2(st - jnp.concatenate([m_new] * (key_tile // LANES), axis=1))
            pv = jnp.dot(pt.astype(BF16), vs_ref[0, c, pl.ds(start, key_tile), :], preferred_element_type=F32)
            acc_ref[c] = acc_ref[c] * jnp.exp2(m_run - m_new) + pv
            m_ref[c] = m_new

    def group(gi, _):
        for u in range(tiles_per_group):
            tile(gi * tiles_per_group + u, False)
        return 0

    def single(ti, _):
        tile(ti, False)
        return 0

    n_full = q0 // key_tile
    n_groups = n_full // tiles_per_group
    lax.fori_loop(0, n_groups, group, 0)
    lax.fori_loop(n_groups * tiles_per_group, n_full, single, 0)
    tile(n_full, True)

    slab = WINDOW + ATT_TILE
    ws = pl.multiple_of(jnp.maximum(q0 - WINDOW, 0), ATT_TILE)
    kpos = ws + lax.broadcasted_iota(jnp.int32, (rows, slab), 1)
    in_window = lax.bitcast_convert_type(t_col - kpos, jnp.uint32) < jnp.uint32(WINDOW)
    lane = lax.broadcasted_iota(jnp.int32, (ATT_TILE, LANES), 1)
    per_branch = B_GROUP * HEAD_DIM

    def head_pair(acc, k):
        a_even = acc[2 * k * ATT_TILE:(2 * k + 1) * ATT_TILE]
        a_odd = acc[(2 * k + 1) * ATT_TILE:(2 * k + 2) * ATT_TILE]
        num = jnp.where(lane < HEAD_DIM, a_even, pltpu.roll(a_odd, HEAD_DIM, 1))
        den = jnp.where(lane < HEAD_DIM, pltpu.roll(a_even, HEAD_DIM, 1), a_odd)
        return num * (1.0 / den)

    gates = gate_ref[...]
    g_hi = gates.astype(BF16)
    g_lo = (gates - g_hi.astype(F32)).astype(BF16)

    for c in chains:
        acc_s = acc_ref[c]
        s = lax.dot_general(lhs_w_ref[c], kwv_ref[0, c, pl.ds(ws, slab), :], NT_DIMS,
                            preferred_element_type=F32)
        s = jnp.where(in_window, s, NEG_INF)
        e = jnp.exp2(s - jnp.max(s, axis=-1, keepdims=True))
        acc_w = jnp.dot(e.astype(BF16), vw_ref[0, c, pl.ds(ws, slab), :], preferred_element_type=F32)

        gx = (jnp.dot(g_hi, gx_ref[c], preferred_element_type=F32)
              + jnp.dot(g_lo, gx_ref[c], preferred_element_type=F32))
        for k in range(B_GROUP // 2):
            c0 = c * B_GROUP * HEAD_DIM + k * LANES
            g0 = k * LANES
            og = (gx[:, g0:g0 + LANES] * ocmp_ref[:, c0:c0 + LANES]
                  + gx[:, per_branch + g0:per_branch + g0 + LANES] * head_pair(acc_s, k)
                  + gx[:, 2 * per_branch + g0:2 * per_branch + g0 + LANES] * head_pair(acc_w, k))
            o_ref[:, c0:c0 + LANES] = og.astype(o_ref.dtype)


def _nsa_attend_call(q_hm, bias, ocmp, gates, kv_aug, v_aug, *, key_tile=512, tiles_per_group=2):
    _, t, _ = q_hm.shape
    assert t % key_tile == 0 and key_tile % ATT_TILE == 0 and t % ATT_TILE == 0
    rows = B_GROUP * ATT_TILE
    kvs = KV_PER_ATTEND_STEP
    wide = kvs * B_GROUP * HEAD_DIM
    resident = dict(pipeline_mode=pl.Buffered(1))
    return pl.pallas_call(
        functools.partial(_nsa_attend_kernel, key_tile=key_tile, tiles_per_group=tiles_per_group),
        grid=(B_KV_HEADS // kvs, t // ATT_TILE),
        in_specs=[pl.BlockSpec((kvs * B_GROUP // 2, ATT_TILE, LANES), lambda p, i: (p, i, 0)),
                  pl.BlockSpec((kvs, ATT_TILE, N_BLK_LANES), lambda p, i: (p, i, 0)),
                  pl.BlockSpec((ATT_TILE, wide), lambda p, i: (i, p)),
                  pl.BlockSpec((ATT_TILE, GATE_LANES), lambda p, i: (i, 0)),
                  pl.BlockSpec((kvs, GATE_LANES, 3 * B_GROUP * HEAD_DIM), lambda p, i: (p, 0, 0)),
                  pl.BlockSpec((1, kvs, t, 2 * LANES), lambda p, i: (0, p, 0, 0), **resident),
                  pl.BlockSpec((1, kvs, t, 2 * LANES), lambda p, i: (1, p, 0, 0), **resident),
                  pl.BlockSpec((1, kvs, t, LANES), lambda p, i: (0, p, 0, 0), **resident),
                  pl.BlockSpec((1, kvs, t, LANES), lambda p, i: (1, p, 0, 0), **resident)],
        out_specs=pl.BlockSpec((ATT_TILE, wide), lambda p, i: (i, p)),
        out_shape=jax.ShapeDtypeStruct((t, D_B), BF16),
        scratch_shapes=[pltpu.VMEM((kvs, rows, 2 * LANES), BF16), pltpu.VMEM((kvs, rows, 2 * LANES), BF16),
                        pltpu.VMEM((kvs, rows, LANES), F32), pltpu.VMEM((kvs, rows, LANES), F32)],
        name="nsa_attend",
        compiler_params=_cparams(2, VMEM_LIMIT))(q_hm, bias, ocmp, gates, _gate_expand_matrix(),
                                                 kv_aug, kv_aug, v_aug, v_aug)


def _post_kernel(oa_ref, ob_ref, sg_ref, x_ref, wa_ref, wb_ref, wo_ref, g_ref, wg_ref, wu_ref, wd_ref,
                 o_ref, wg16_ref, wu16_ref, wd16_ref):
    ya = jnp.dot(oa_ref[...], wa_ref[...], preferred_element_type=F32)
    yb = jnp.dot(ob_ref[...], wb_ref[...], preferred_element_type=F32)
    merged = sg_ref[0].astype(F32) * ya + sg_ref[1].astype(F32) * yb
    mix = jnp.dot(merged.astype(BF16), wo_ref[...], preferred_element_type=F32)
    y = mix * lax.rsqrt(jnp.mean(mix * mix, axis=-1, keepdims=True) + EPS)
    o_ref[...] = x_ref[...] + y * g_ref[...]
    wg16_ref[...] = wg_ref[...].astype(BF16)
    wu16_ref[...] = wu_ref[...].astype(BF16)
    wd16_ref[...] = wd_ref[...].astype(BF16)


def _post_call(oa, ob, sg, x2, wa, wb, wo, g, wg, wu, wd, tm=256):
    t, d = x2.shape
    steps = t // tm
    dff = wg.shape[1]
    assert d % (steps * 16) == 0 and dff % (steps * 16) == 0
    const = dict(pipeline_mode=pl.Buffered(1))
    return pl.pallas_call(
        _post_kernel, grid=(steps,),
        in_specs=[pl.BlockSpec((tm, D_A), lambda i: (i, 0)),
                  pl.BlockSpec((tm, D_B), lambda i: (i, 0)),
                  pl.BlockSpec((2, tm, d), lambda i: (0, i, 0)),
                  pl.BlockSpec((tm, d), lambda i: (i, 0)),
                  pl.BlockSpec((D_A, d), lambda i: (0, 0), **const),
                  pl.BlockSpec((D_B, d), lambda i: (0, 0), **const),
                  pl.BlockSpec((d, d), lambda i: (0, 0), **const),
                  pl.BlockSpec((1, d), lambda i: (0, 0)),
                  pl.BlockSpec((d // steps, dff), lambda i: (i, 0)),
                  pl.BlockSpec((d // steps, dff), lambda i: (i, 0)),
                  pl.BlockSpec((dff // steps, d), lambda i: (i, 0))],
        out_specs=[pl.BlockSpec((tm, d), lambda i: (i, 0)),
                   pl.BlockSpec((d // steps, dff), lambda i: (i, 0)),
                   pl.BlockSpec((d // steps, dff), lambda i: (i, 0)),
                   pl.BlockSpec((dff // steps, d), lambda i: (i, 0))],
        out_shape=[jax.ShapeDtypeStruct((t, d), F32), jax.ShapeDtypeStruct((d, dff), BF16),
                   jax.ShapeDtypeStruct((d, dff), BF16), jax.ShapeDtypeStruct((dff, d), BF16)],
        name="mix_out_residual",
        compiler_params=_cparams(1, VMEM_LIMIT))(oa, ob, sg, x2, wa, wb, wo, g, wg, wu, wd)


def _ffn_kernel(h_ref, g1_ref, wg_ref, wu_ref, wd_ref, g2_ref, o_ref, hn_ref, acc_ref):
    f = pl.program_id(1)

    @pl.when(f == 0)
    def _():
        h = h_ref[...]
        y = h * lax.rsqrt(jnp.mean(h * h, axis=-1, keepdims=True) + EPS)
        hn_ref[...] = (y * g1_ref[...]).astype(BF16)
        acc_ref[...] = jnp.zeros_like(acc_ref)

    hn = hn_ref[...]
    a = jnp.dot(hn, wg_ref[...], preferred_element_type=F32)
    u = jnp.dot(hn, wu_ref[...], preferred_element_type=F32)
    z = (a * jax.nn.sigmoid(a) * u).astype(BF16)
    acc_ref[...] += jnp.dot(z, wd_ref[...], preferred_element_type=F32)

    @pl.when(f == pl.num_programs(1) - 1)
    def _():
        ff = acc_ref[...]
        y = ff * lax.rsqrt(jnp.mean(ff * ff, axis=-1, keepdims=True) + EPS)
        o_ref[...] = h_ref[...] + y * g2_ref[...]


def _ffn_call(h1, g1, wg, wu, wd, g2, tm=512, tf=512):
    t, d = h1.shape
    dff = wg.shape[1]
    return pl.pallas_call(
        _ffn_kernel, grid=(t // tm, dff // tf),
        in_specs=[pl.BlockSpec((tm, d), lambda i, f: (i, 0)),
                  pl.BlockSpec((1, d), lambda i, f: (0, 0)),
                  pl.BlockSpec((d, tf), lambda i, f: (0, f)),
                  pl.BlockSpec((d, tf), lambda i, f: (0, f)),
                  pl.BlockSpec((tf, d), lambda i, f: (f, 0)),
                  pl.BlockSpec((1, d), lambda i, f: (0, 0))],
        out_specs=pl.BlockSpec((tm, d), lambda i, f: (i, 0)),
        out_shape=jax.ShapeDtypeStruct((t, d), F32),
        scratch_shapes=[pltpu.VMEM((tm, d), BF16), pltpu.VMEM((tm, d), F32)], name="swiglu_ffn",
        compiler_params=_cparams(2, VMEM_LIMIT))(h1, g1, wg, wu, wd, g2)


def _rope_tables(pos, period):
    half = ROPE_DIM // 2
    inv = jnp.float32(ROPE_THETA) ** (-jnp.arange(half, dtype=F32) * 2.0 / ROPE_DIM)
    r = np.arange(LANES) % period
    pick = np.zeros((half, LANES), np.float32)
    pick[r % half, np.arange(LANES)] = 1.0
    inv_lane = jnp.sum(inv[:, None] * pick, axis=0)
    ang = pos.astype(F32)[:, None] * inv_lane[None, :]
    cos, sin = jnp.cos(ang), jnp.sin(ang)
    lo = jnp.asarray((r < half)[None, :])
    hi = jnp.asarray(((r >= half) & (r < ROPE_DIM))[None, :])
    c = jnp.where(lo | hi, cos, 1.0)
    sa = jnp.where(lo, -sin, 0.0)
    sb = jnp.where(hi, sin, 0.0)
    return jnp.stack([c, sa, sb])


def _gate_expand_matrix():
    e = np.zeros((B_KV_HEADS, GATE_LANES, 3 * B_GROUP * HEAD_DIM), np.float32)
    for kvh in range(B_KV_HEADS):
        for g in range(B_GROUP):
            for b in range(3):
                c0 = (b * B_GROUP + g) * HEAD_DIM
                e[kvh, 3 * (kvh * B_GROUP + g) + b, c0:c0 + HEAD_DIM] = 1.0
    return jnp.asarray(e, BF16)


def _slc_weight_matrix(n_cmp_pad):
    ratio = SLC_BLOCK // CMP_STRIDE
    w = np.zeros((N_BLK_LANES, n_cmp_pad), np.float32)
    for j in range(N_BLK_LANES):
        for o, wv in enumerate(SLC_WEIGHTS):
            n = ratio * j + o - 1
            if 0 <= n < n_cmp_pad - 1:
                w[j, n] = wv
    return jnp.asarray(w, BF16)


def kernel(x, pre_mix_g, w_in, lb_table, a_norm_g, cmp_pe_k, cmp_w1_k, cmp_b1_k, cmp_w2_k, cmp_b2_k, cmp_pe_v, cmp_w1_v, cmp_b1_v, cmp_w2_v, cmp_b2_v, w_proj_a, w_proj_b, w_out, post_mix_g, pre_ffn_g, w_gate, w_up, w_down, post_ffn_g):
    bsz, t, d = x.shape
    assert bsz == 1 and d == D_MODEL and WINDOW + ATT_TILE <= t <= N_BLK_LANES * SLC_BLOCK
    x2 = x.reshape(t, d)
    n_cmp_pad = t // CMP_STRIDE
    assert n_cmp_pad % LANES == 0

    lb = jnp.cumsum(jax.nn.softmax(lb_table.astype(F32), axis=0), axis=0)[0].reshape(1, D_A)
    w = jnp.swapaxes(w_in[0], 0, 1).astype(BF16)
    o_q, o_kc, o_ks, o_g, o_m = 4 * D_A, 4 * D_A + D_B, 4 * D_A + D_B + 2 * KV_DIM, 4 * D_A + D_B + 6 * KV_DIM, 4 * D_A + D_B + 6 * KV_DIM + 3 * B_HEADS
    w_kv = (w[o_ks:o_g].reshape(2, 2, B_KV_HEADS, HEAD_DIM, d).transpose(0, 2, 1, 3, 4)
            .reshape(4 * KV_DIM, d))
    w_g = jnp.pad(w[o_g:o_m], ((0, GATE_LANES - 3 * B_HEADS), (0, 0)))
    w_m = w[o_m:]

    pos = jnp.arange(t)
    tab_pos = _rope_tables(pos, HEAD_DIM)
    end_pos = jnp.arange(n_cmp_pad) * CMP_STRIDE + (CMP_BLOCK - 1)
    ident = jnp.stack([jnp.ones((n_cmp_pad, LANES), F32), jnp.zeros((n_cmp_pad, LANES), F32),
                       jnp.zeros((n_cmp_pad, LANES), F32)])
    tab_cmp = jnp.concatenate([_rope_tables(end_pos, LANES), ident], axis=0)

    tm = 1024 if t % 1024 == 0 else 256
    proj4, xn = _proj_call(
        "proj_hgrn", x2, w, w_rows=(0, o_q), tm=tm, tn=D_A, epilogue=_ep_plain,
        out_shape=jax.ShapeDtypeStruct((4, t, D_A), F32),
        out_specs=pl.BlockSpec((1, tm, D_A), lambda i, j: (j, i, 0)),
        norm_gain=pre_mix_g[0].reshape(1, d))
    q_hm = _proj_call(
        "proj_q", xn, w, w_rows=(o_q, D_B), tm=tm, tn=512, epilogue=_ep_q,
        out_shape=jax.ShapeDtypeStruct((B_HEADS // 2, t, LANES), F32),
        out_specs=pl.BlockSpec((512 // LANES, tm, LANES), lambda i, j: (j, i, 0)),
        extra=(tab_pos,), extra_specs=(pl.BlockSpec((3, tm, LANES), lambda i, j: (0, i, 0)),))
    cmp_in = _proj_call(
        "proj_cmp", xn, w, w_rows=(o_kc, 2 * KV_DIM), tm=tm, tn=2 * KV_DIM, epilogue=_ep_cmp,
        out_shape=jax.ShapeDtypeStruct((2, B_KV_HEADS, t, HEAD_DIM), F32),
        out_specs=pl.BlockSpec((2, B_KV_HEADS, tm, HEAD_DIM), lambda i, j: (0, 0, i, 0)))
    kv_aug, v_aug = _proj_call(
        "proj_kv", xn, w_kv, tm=tm, tn=2 * KV_DIM, epilogue=functools.partial(_ep_kv, tm=tm),
        out_shape=[jax.ShapeDtypeStruct((2, B_KV_HEADS, t, 2 * LANES), BF16),
                   jax.ShapeDtypeStruct((2, B_KV_HEADS, t, LANES), BF16)],
        out_specs=[pl.BlockSpec((1, B_KV_HEADS, tm, 2 * LANES), lambda i, j: (j, 0, i, 0)),
                   pl.BlockSpec((1, B_KV_HEADS, tm, LANES), lambda i, j: (j, 0, i, 0))],
        extra=(tab_pos,), extra_specs=(pl.BlockSpec((3, tm, LANES), lambda i, j: (0, i, 0)),))
    gates = _proj_call(
        "proj_gate", xn, w_g, tm=tm, tn=GATE_LANES, epilogue=_ep_gate,
        out_shape=jax.ShapeDtypeStruct((t, GATE_LANES), F32),
        out_specs=pl.BlockSpec((tm, GATE_LANES), lambda i, j: (i, 0)))
    sg, wa16, wb16, wo16 = _proj_call(
        "proj_merge", xn, w_m, tm=tm, tn=1024, epilogue=_ep_sigmoid,
        out_shape=jax.ShapeDtypeStruct((2, t, d), BF16),
        out_specs=pl.BlockSpec((1, tm, 1024), lambda i, j: (j // 2, i, j % 2)),
        to_bf16=(w_proj_a[0], w_proj_b[0], w_out[0]))

    oa = _hgrn_call(proj4, lb, a_norm_g[0].reshape(1, D_A))

    half = CMP_BLOCK // 2
    x16 = cmp_in.reshape(2, B_KV_HEADS, n_cmp_pad, CMP_STRIDE * HEAD_DIM)
    pe2 = jnp.stack([cmp_pe_k[0], cmp_pe_v[0]]).reshape(2, 2, half * HEAD_DIM)
    w1 = jnp.stack([cmp_w1_k[0], cmp_w1_v[0]]).astype(BF16)
    b1 = jnp.stack([cmp_b1_k[0], cmp_b1_v[0]]).reshape(2, 1, CMP_HIDDEN)
    w2p = jnp.pad(jnp.stack([cmp_w2_k[0], cmp_w2_v[0]]), ((0, 0), (0, 0), (0, LANES - HEAD_DIM))).astype(BF16)
    b2p = jnp.pad(jnp.stack([cmp_b2_k[0], cmp_b2_v[0]]), ((0, 0), (0, LANES - HEAD_DIM))).reshape(2, 1, LANES)
    cmp_aug = _compress_call(x16, pe2, w1, b1, w2p, b2p, tab_cmp)
    ocmp, bias = _nsa_select_call(q_hm, cmp_aug, _slc_weight_matrix(n_cmp_pad))
    ob = _nsa_attend_call(q_hm, bias, ocmp, gates, kv_aug, v_aug)

    h1, wg16, wu16, wd16 = _post_call(oa, ob, sg, x2, wa16, wb16, wo16, post_mix_g[0].reshape(1, d),
                                      w_gate[0], w_up[0], w_down[0])
    out = _ffn_call(h1, pre_ffn_g[0].reshape(1, d), wg16, wu16, wd16, post_ffn_g[0].reshape(1, d))
    return out.reshape(bsz, t, d)
```
